```python
import math
import jax, jax.numpy as jnp
from jax import lax
import numpy as np

D_MODEL = 1024
BATCH = 8
SEQ = 8192
DEPTH = 1

D_MIX = D_MODEL
D_POOL = D_MIX // 2
POOL_WINDOWS = (2, 4, 8, 16)
N_POOL_GROUPS = len(POOL_WINDOWS)
POOL_GROUP = D_POOL // N_POOL_GROUPS
D_GMLP = D_MIX - D_POOL
N_GMLP_HEADS = 8
GMLP_HEAD = D_GMLP // N_GMLP_HEADS
CHUNK = 128
D_FF = 2816
N_SUB = 3
N_MOD = 3
EPS = 1e-6
HALF_STEP = 0.5

kernel_name = "hybrid_pool_gmlp_macaron_adaln"


def rms_norm(x, g):
    xf = x.astype(jnp.float32)
    r = lax.rsqrt(jnp.mean(xf * xf, axis=-1, keepdims=True) + EPS)
    return (xf * r).astype(x.dtype) * g


def layer_norm(x, g, b):
    xf = x.astype(jnp.float32)
    mu = jnp.mean(xf, axis=-1, keepdims=True)
    var = jnp.mean(jnp.square(xf - mu), axis=-1, keepdims=True)
    return ((xf - mu) * lax.rsqrt(var + EPS)).astype(x.dtype) * g + b


def modulate(h, shift, scale):
    return h * (1.0 + scale[:, None, :]) + shift[:, None, :]


def swiglu(h, w_in, w_out):
    gu = h @ w_in
    g, u = jnp.split(gu, 2, axis=-1)
    return (jax.nn.silu(g) * u) @ w_out


def causal_multiscale_pool(xp, w_pool, pool_scale):
    B, S, _ = xp.shape
    xg = xp.reshape(B, S, N_POOL_GROUPS, POOL_GROUP)
    xf = xg.astype(jnp.float32)
    cs = jnp.cumsum(xf, axis=1)
    pos = jnp.arange(S, dtype=jnp.int32)
    outs = []
    for i, w in enumerate(POOL_WINDOWS):
        c_i = cs[:, :, i]
        c_prev = jnp.pad(c_i[:, :-w], ((0, 0), (w, 0), (0, 0)))
        cnt = jnp.minimum(pos + 1, w).astype(jnp.float32)[None, :, None]
        outs.append((c_i - c_prev) / cnt - xf[:, :, i])
    pooled = jnp.stack(outs, axis=2).astype(xp.dtype)
    mixed = jnp.einsum('bsgc,gcd->bsgd', pooled, w_pool)
    return mixed.reshape(B, S, D_POOL) * pool_scale


def chunked_spatial_gating(zb, ln_g, ln_b, w_spatial, b_spatial):
    B, S, _ = zb.shape
    z = jax.nn.gelu(zb)
    u, v = jnp.split(z, 2, axis=-1)
    v = layer_norm(v, ln_g, ln_b)
    vc = v.reshape(B, S // CHUNK, CHUNK, N_GMLP_HEADS, GMLP_HEAD)
    mask = jnp.tril(jnp.ones((CHUNK, CHUNK), dtype=bool))
    ws = jnp.where(mask[None], w_spatial, jnp.zeros((), w_spatial.dtype))
    sv = jnp.einsum('hts,bnshc->bnthc', ws, vc)
    sv = sv + jnp.transpose(b_spatial)[None, None, :, :, None]
    return u * sv.reshape(B, S, D_GMLP)


def _fwd_setup_inputs(seed: int = 0) -> dict:
    key = jax.random.key(seed)
    ks = jax.random.split(key, 24)
    f32 = jnp.float32
    L, D = DEPTH, D_MODEL

    def nrm(k, shape, fan_in):
        return jax.random.normal(k, shape, f32) * (fan_in ** -0.5)

    def gain(k, shape):
        return 1.0 + 0.05 * jax.random.normal(k, shape, f32)

    return {
        "x": jax.random.normal(ks[0], (BATCH, SEQ, D), f32),
        "c": jax.random.normal(ks[1], (BATCH, D), f32),
        "w_ada": nrm(ks[2], (L, D, N_SUB * N_MOD * D), D),
        "b_ada": 0.02 * jax.random.normal(ks[3], (L, N_SUB * N_MOD * D), f32),
        "norm_ffn1_g": gain(ks[4], (L, D)),
        "ffn1_w_in": nrm(ks[5], (L, D, 2 * D_FF), D),
        "ffn1_w_out": nrm(ks[6], (L, D_FF, D), D_FF),
        "norm_mix_g": gain(ks[7], (L, D)),
        "w_mix_in": nrm(ks[8], (L, D, D_POOL + 2 * D_GMLP), D),
        "w_pool": nrm(ks[9], (L, N_POOL_GROUPS, POOL_GROUP, POOL_GROUP), POOL_GROUP),
        "pool_scale": gain(ks[10], (L, D_POOL)),
        "gmlp_ln_g": gain(ks[11], (L, D_GMLP)),
        "gmlp_ln_b": 0.02 * jax.random.normal(ks[12], (L, D_GMLP), f32),
        "w_spatial": nrm(ks[13], (L, N_GMLP_HEADS, CHUNK, CHUNK), CHUNK),
        "b_spatial": gain(ks[14], (L, N_GMLP_HEADS, CHUNK)),
        "w_mix_out": nrm(ks[15], (L, D_MIX, D), D_MIX),
        "norm_ffn2_g": gain(ks[16], (L, D)),
        "ffn2_w_in": nrm(ks[17], (L, D, 2 * D_FF), D),
        "ffn2_w_out": nrm(ks[18], (L, D_FF, D), D_FF),
        "norm_final_g": gain(ks[19], (D,)),
    }


def _fwd_reference(x, c, w_ada, b_ada, norm_ffn1_g, ffn1_w_in, ffn1_w_out,
              norm_mix_g, w_mix_in, w_pool, pool_scale, gmlp_ln_g, gmlp_ln_b,
              w_spatial, b_spatial, w_mix_out, norm_ffn2_g, ffn2_w_in, ffn2_w_out,
              norm_final_g):
    B = x.shape[0]
    c_act = jax.nn.silu(c)
    for l in range(DEPTH):
        mod = (c_act @ w_ada[l] + b_ada[l]).reshape(B, N_SUB, N_MOD, D_MODEL)

        h = modulate(rms_norm(x, norm_ffn1_g[l]), mod[:, 0, 0], mod[:, 0, 1])
        x = x + HALF_STEP * mod[:, 0, 2][:, None, :] * swiglu(h, ffn1_w_in[l], ffn1_w_out[l])

        h = modulate(rms_norm(x, norm_mix_g[l]), mod[:, 1, 0], mod[:, 1, 1])
        proj = h @ w_mix_in[l]
        xa = proj[..., :D_POOL]
        zb = proj[..., D_POOL:]
        ya = causal_multiscale_pool(xa, w_pool[l], pool_scale[l])
        yb = chunked_spatial_gating(zb, gmlp_ln_g[l], gmlp_ln_b[l], w_spatial[l], b_spatial[l])
        mix = jnp.concatenate([ya, yb], axis=-1) @ w_mix_out[l]
        x = x + mod[:, 1, 2][:, None, :] * mix

        h = modulate(rms_norm(x, norm_ffn2_g[l]), mod[:, 2, 0], mod[:, 2, 1])
        x = x + HALF_STEP * mod[:, 2, 2][:, None, :] * swiglu(h, ffn2_w_in[l], ffn2_w_out[l])
    return rms_norm(x, norm_final_g)


import jax as _jax
import jax.numpy as _jnp

TWIN_FORMAT = 'train_step'
FWD_PARAMS = ['x', 'c', 'w_ada', 'b_ada', 'norm_ffn1_g', 'ffn1_w_in', 'ffn1_w_out', 'norm_mix_g', 'w_mix_in', 'w_pool', 'pool_scale', 'gmlp_ln_g', 'gmlp_ln_b', 'w_spatial', 'b_spatial', 'w_mix_out', 'norm_ffn2_g', 'ffn2_w_in', 'ffn2_w_out', 'norm_final_g']
TWIN_WEIGHTS = ['w_ada', 'b_ada', 'norm_ffn1_g', 'ffn1_w_in', 'ffn1_w_out', 'norm_mix_g', 'w_mix_in', 'w_pool', 'pool_scale', 'gmlp_ln_g', 'gmlp_ln_b', 'w_spatial', 'b_spatial', 'w_mix_out', 'norm_ffn2_g', 'ffn2_w_in', 'ffn2_w_out', 'norm_final_g']
TWIN_DIFF_INPUT = 'x'
TWIN_INPUTS = ['x', 'c', 'w_ada', 'b_ada', 'norm_ffn1_g', 'ffn1_w_in', 'ffn1_w_out', 'norm_mix_g', 'w_mix_in', 'w_pool', 'pool_scale', 'gmlp_ln_g', 'gmlp_ln_b', 'w_spatial', 'b_spatial', 'w_mix_out', 'norm_ffn2_g', 'ffn2_w_in', 'ffn2_w_out', 'norm_final_g', 'loss_target', 'm_w_ada', 'm_b_ada', 'm_norm_ffn1_g', 'm_ffn1_w_in', 'm_ffn1_w_out', 'm_norm_mix_g', 'm_w_mix_in', 'm_w_pool', 'm_pool_scale', 'm_gmlp_ln_g', 'm_gmlp_ln_b', 'm_w_spatial', 'm_b_spatial', 'm_w_mix_out', 'm_norm_ffn2_g', 'm_ffn2_w_in', 'm_ffn2_w_out', 'm_norm_final_g', 'v_w_ada', 'v_b_ada', 'v_norm_ffn1_g', 'v_ffn1_w_in', 'v_ffn1_w_out', 'v_norm_mix_g', 'v_w_mix_in', 'v_w_pool', 'v_pool_scale', 'v_gmlp_ln_g', 'v_gmlp_ln_b', 'v_w_spatial', 'v_b_spatial', 'v_w_mix_out', 'v_norm_ffn2_g', 'v_ffn2_w_in', 'v_ffn2_w_out', 'v_norm_final_g']
TWIN_OUTPUTS = ['loss', 'grad_x', 'grad_w_ada', 'grad_b_ada', 'grad_norm_ffn1_g', 'grad_ffn1_w_in', 'grad_ffn1_w_out', 'grad_norm_mix_g', 'grad_w_mix_in', 'grad_w_pool', 'grad_pool_scale', 'grad_gmlp_ln_g', 'grad_gmlp_ln_b', 'grad_w_spatial', 'grad_b_spatial', 'grad_w_mix_out', 'grad_norm_ffn2_g', 'grad_ffn2_w_in', 'grad_ffn2_w_out', 'grad_norm_final_g', 'delta_w_ada', 'delta_b_ada', 'delta_norm_ffn1_g', 'delta_ffn1_w_in', 'delta_ffn1_w_out', 'delta_norm_mix_g', 'delta_w_mix_in', 'delta_w_pool', 'delta_pool_scale', 'delta_gmlp_ln_g', 'delta_gmlp_ln_b', 'delta_w_spatial', 'delta_b_spatial', 'delta_w_mix_out', 'delta_norm_ffn2_g', 'delta_ffn2_w_in', 'delta_ffn2_w_out', 'delta_norm_final_g', 'new_m_w_ada', 'new_m_b_ada', 'new_m_norm_ffn1_g', 'new_m_ffn1_w_in', 'new_m_ffn1_w_out', 'new_m_norm_mix_g', 'new_m_w_mix_in', 'new_m_w_pool', 'new_m_pool_scale', 'new_m_gmlp_ln_g', 'new_m_gmlp_ln_b', 'new_m_w_spatial', 'new_m_b_spatial', 'new_m_w_mix_out', 'new_m_norm_ffn2_g', 'new_m_ffn2_w_in', 'new_m_ffn2_w_out', 'new_m_norm_final_g', 'new_v_w_ada', 'new_v_b_ada', 'new_v_norm_ffn1_g', 'new_v_ffn1_w_in', 'new_v_ffn1_w_out', 'new_v_norm_mix_g', 'new_v_w_mix_in', 'new_v_w_pool', 'new_v_pool_scale', 'new_v_gmlp_ln_g', 'new_v_gmlp_ln_b', 'new_v_w_spatial', 'new_v_b_spatial', 'new_v_w_mix_out', 'new_v_norm_ffn2_g', 'new_v_ffn2_w_in', 'new_v_ffn2_w_out', 'new_v_norm_final_g']
TWIN_LEAF_KINDS = {'loss': 'loss', 'grad_x': 'grad_x', 'grad_w_ada': 'grad_w', 'grad_b_ada': 'grad_w', 'grad_norm_ffn1_g': 'grad_w', 'grad_ffn1_w_in': 'grad_w', 'grad_ffn1_w_out': 'grad_w', 'grad_norm_mix_g': 'grad_w', 'grad_w_mix_in': 'grad_w', 'grad_w_pool': 'grad_w', 'grad_pool_scale': 'grad_w', 'grad_gmlp_ln_g': 'grad_w', 'grad_gmlp_ln_b': 'grad_w', 'grad_w_spatial': 'grad_w', 'grad_b_spatial': 'grad_w', 'grad_w_mix_out': 'grad_w', 'grad_norm_ffn2_g': 'grad_w', 'grad_ffn2_w_in': 'grad_w', 'grad_ffn2_w_out': 'grad_w', 'grad_norm_final_g': 'grad_w', 'delta_w_ada': 'delta_w', 'delta_b_ada': 'delta_w', 'delta_norm_ffn1_g': 'delta_w', 'delta_ffn1_w_in': 'delta_w', 'delta_ffn1_w_out': 'delta_w', 'delta_norm_mix_g': 'delta_w', 'delta_w_mix_in': 'delta_w', 'delta_w_pool': 'delta_w', 'delta_pool_scale': 'delta_w', 'delta_gmlp_ln_g': 'delta_w', 'delta_gmlp_ln_b': 'delta_w', 'delta_w_spatial': 'delta_w', 'delta_b_spatial': 'delta_w', 'delta_w_mix_out': 'delta_w', 'delta_norm_ffn2_g': 'delta_w', 'delta_ffn2_w_in': 'delta_w', 'delta_ffn2_w_out': 'delta_w', 'delta_norm_final_g': 'delta_w', 'new_m_w_ada': 'new_m', 'new_m_b_ada': 'new_m', 'new_m_norm_ffn1_g': 'new_m', 'new_m_ffn1_w_in': 'new_m', 'new_m_ffn1_w_out': 'new_m', 'new_m_norm_mix_g': 'new_m', 'new_m_w_mix_in': 'new_m', 'new_m_w_pool': 'new_m', 'new_m_pool_scale': 'new_m', 'new_m_gmlp_ln_g': 'new_m', 'new_m_gmlp_ln_b': 'new_m', 'new_m_w_spatial': 'new_m', 'new_m_b_spatial': 'new_m', 'new_m_w_mix_out': 'new_m', 'new_m_norm_ffn2_g': 'new_m', 'new_m_ffn2_w_in': 'new_m', 'new_m_ffn2_w_out': 'new_m', 'new_m_norm_final_g': 'new_m', 'new_v_w_ada': 'new_v', 'new_v_b_ada': 'new_v', 'new_v_norm_ffn1_g': 'new_v', 'new_v_ffn1_w_in': 'new_v', 'new_v_ffn1_w_out': 'new_v', 'new_v_norm_mix_g': 'new_v', 'new_v_w_mix_in': 'new_v', 'new_v_w_pool': 'new_v', 'new_v_pool_scale': 'new_v', 'new_v_gmlp_ln_g': 'new_v', 'new_v_gmlp_ln_b': 'new_v', 'new_v_w_spatial': 'new_v', 'new_v_b_spatial': 'new_v', 'new_v_w_mix_out': 'new_v', 'new_v_norm_ffn2_g': 'new_v', 'new_v_ffn2_w_in': 'new_v', 'new_v_ffn2_w_out': 'new_v', 'new_v_norm_final_g': 'new_v'}


def _forward(args):
    return _fwd_reference(*[args[k] for k in FWD_PARAMS])


def _output_shape():
    def fwd():
        inp = _fwd_setup_inputs(0)
        return _fwd_reference(*[inp[k] for k in FWD_PARAMS])
    out = _jax.eval_shape(fwd)
    return out.shape, out.dtype

N_MICROBATCH = 1
ADAM_LR = 0.001
ADAM_B1 = 0.9
ADAM_B2 = 0.999
ADAM_EPS = 1e-08
ADAM_WD = 0.01
ADAM_STEP = 10
PER_EXAMPLE_BATCH_AXIS = {'x': 0, 'c': 0, 'loss_target': 0}
SHARED_INPUTS = []
_WEIGHT_DTYPES = {'w_ada': _jnp.float32, 'b_ada': _jnp.float32, 'norm_ffn1_g': _jnp.float32, 'ffn1_w_in': _jnp.float32, 'ffn1_w_out': _jnp.float32, 'norm_mix_g': _jnp.float32, 'w_mix_in': _jnp.float32, 'w_pool': _jnp.float32, 'pool_scale': _jnp.float32, 'gmlp_ln_g': _jnp.float32, 'gmlp_ln_b': _jnp.float32, 'w_spatial': _jnp.float32, 'b_spatial': _jnp.float32, 'w_mix_out': _jnp.float32, 'norm_ffn2_g': _jnp.float32, 'ffn2_w_in': _jnp.float32, 'ffn2_w_out': _jnp.float32, 'norm_final_g': _jnp.float32}
MOMENT_SCALE = {'w_ada': 1.551393e-01, 'b_ada': 3.428792e-01, 'norm_ffn1_g': 1.083142e-01, 'ffn1_w_in': 4.945744e-02, 'ffn1_w_out': 8.063170e-02, 'norm_mix_g': 1.502260e-01, 'w_mix_in': 1.308661e-01, 'w_pool': 1.370074e-01, 'pool_scale': 1.342625e-01, 'gmlp_ln_g': 8.159200e-02, 'gmlp_ln_b': 7.986779e-02, 'w_spatial': 5.646030e-02, 'b_spatial': 9.225559e-02, 'w_mix_out': 1.795330e-01, 'norm_ffn2_g': 8.588930e-02, 'ffn2_w_in': 4.148517e-02, 'ffn2_w_out': 6.870152e-02, 'norm_final_g': 6.467535e+01}


def _to_microbatches(a, axis):
    t = _jnp.moveaxis(a, axis, 0)
    t = t.reshape((N_MICROBATCH, t.shape[0] // N_MICROBATCH) + t.shape[1:])
    return _jnp.moveaxis(t, 1, axis + 1)


def setup_inputs(seed: int = 0) -> dict:
    inp = _fwd_setup_inputs(seed)
    key = _jax.random.fold_in(_jax.random.key(seed), 7919)
    shape, _ = _output_shape()
    out = dict(inp)
    out["loss_target"] = _jax.random.normal(_jax.random.fold_in(key, 0), shape, _jnp.float32)
    for i, name in enumerate(TWIN_WEIGHTS):
        w = inp[name].astype(_jnp.float32)
        if MOMENT_SCALE is None:
            s = _jnp.sqrt(_jnp.mean(_jnp.square(w)) + 1e-30)
        else:
            s = MOMENT_SCALE[name]
        km, kv = _jax.random.split(_jax.random.fold_in(key, i + 1))
        out[name] = w
        out["m_" + name] = s * _jax.random.normal(km, w.shape, _jnp.float32)
        out["v_" + name] = (s * s) * _jax.random.uniform(kv, w.shape, _jnp.float32, 0.5, 1.5)
    if N_MICROBATCH > 1:
        for name, axis in PER_EXAMPLE_BATCH_AXIS.items():
            out[name] = _to_microbatches(out[name], axis)
    return {'x': out['x'], 'c': out['c'], 'w_ada': out['w_ada'], 'b_ada': out['b_ada'], 'norm_ffn1_g': out['norm_ffn1_g'], 'ffn1_w_in': out['ffn1_w_in'], 'ffn1_w_out': out['ffn1_w_out'], 'norm_mix_g': out['norm_mix_g'], 'w_mix_in': out['w_mix_in'], 'w_pool': out['w_pool'], 'pool_scale': out['pool_scale'], 'gmlp_ln_g': out['gmlp_ln_g'], 'gmlp_ln_b': out['gmlp_ln_b'], 'w_spatial': out['w_spatial'], 'b_spatial': out['b_spatial'], 'w_mix_out': out['w_mix_out'], 'norm_ffn2_g': out['norm_ffn2_g'], 'ffn2_w_in': out['ffn2_w_in'], 'ffn2_w_out': out['ffn2_w_out'], 'norm_final_g': out['norm_final_g'], 'loss_target': out['loss_target'], 'm_w_ada': out['m_w_ada'], 'm_b_ada': out['m_b_ada'], 'm_norm_ffn1_g': out['m_norm_ffn1_g'], 'm_ffn1_w_in': out['m_ffn1_w_in'], 'm_ffn1_w_out': out['m_ffn1_w_out'], 'm_norm_mix_g': out['m_norm_mix_g'], 'm_w_mix_in': out['m_w_mix_in'], 'm_w_pool': out['m_w_pool'], 'm_pool_scale': out['m_pool_scale'], 'm_gmlp_ln_g': out['m_gmlp_ln_g'], 'm_gmlp_ln_b': out['m_gmlp_ln_b'], 'm_w_spatial': out['m_w_spatial'], 'm_b_spatial': out['m_b_spatial'], 'm_w_mix_out': out['m_w_mix_out'], 'm_norm_ffn2_g': out['m_norm_ffn2_g'], 'm_ffn2_w_in': out['m_ffn2_w_in'], 'm_ffn2_w_out': out['m_ffn2_w_out'], 'm_norm_final_g': out['m_norm_final_g'], 'v_w_ada': out['v_w_ada'], 'v_b_ada': out['v_b_ada'], 'v_norm_ffn1_g': out['v_norm_ffn1_g'], 'v_ffn1_w_in': out['v_ffn1_w_in'], 'v_ffn1_w_out': out['v_ffn1_w_out'], 'v_norm_mix_g': out['v_norm_mix_g'], 'v_w_mix_in': out['v_w_mix_in'], 'v_w_pool': out['v_w_pool'], 'v_pool_scale': out['v_pool_scale'], 'v_gmlp_ln_g': out['v_gmlp_ln_g'], 'v_gmlp_ln_b': out['v_gmlp_ln_b'], 'v_w_spatial': out['v_w_spatial'], 'v_b_spatial': out['v_b_spatial'], 'v_w_mix_out': out['v_w_mix_out'], 'v_norm_ffn2_g': out['v_norm_ffn2_g'], 'v_ffn2_w_in': out['v_ffn2_w_in'], 'v_ffn2_w_out': out['v_ffn2_w_out'], 'v_norm_final_g': out['v_norm_final_g']}


def _loss(weights, diff, rest, loss_target):
    with _jax.named_scope("forward"):
        args = {**rest, TWIN_DIFF_INPUT: diff, **{k: w.astype(_WEIGHT_DTYPES[k]) for k, w in weights.items()}}
        y = _forward(args)
    with _jax.named_scope("loss_head"):
        err = _jnp.square(y.astype(_jnp.float32) - loss_target)
        return 0.5 * _jnp.sum(_jnp.mean(err, axis=-1)) if err.ndim else 0.5 * err


def _adamw(w, g, m, v):
    m = ADAM_B1 * m + (1.0 - ADAM_B1) * g
    v = ADAM_B2 * v + (1.0 - ADAM_B2) * _jnp.square(g)
    m_hat = m / (1.0 - ADAM_B1 ** ADAM_STEP)
    v_hat = v / (1.0 - ADAM_B2 ** ADAM_STEP)
    delta = -ADAM_LR * (m_hat / (_jnp.sqrt(v_hat) + ADAM_EPS) + ADAM_WD * w)
    return delta, m, v


def reference(x, c, w_ada, b_ada, norm_ffn1_g, ffn1_w_in, ffn1_w_out, norm_mix_g, w_mix_in, w_pool, pool_scale, gmlp_ln_g, gmlp_ln_b, w_spatial, b_spatial, w_mix_out, norm_ffn2_g, ffn2_w_in, ffn2_w_out, norm_final_g, loss_target, m_w_ada, m_b_ada, m_norm_ffn1_g, m_ffn1_w_in, m_ffn1_w_out, m_norm_mix_g, m_w_mix_in, m_w_pool, m_pool_scale, m_gmlp_ln_g, m_gmlp_ln_b, m_w_spatial, m_b_spatial, m_w_mix_out, m_norm_ffn2_g, m_ffn2_w_in, m_ffn2_w_out, m_norm_final_g, v_w_ada, v_b_ada, v_norm_ffn1_g, v_ffn1_w_in, v_ffn1_w_out, v_norm_mix_g, v_w_mix_in, v_w_pool, v_pool_scale, v_gmlp_ln_g, v_gmlp_ln_b, v_w_spatial, v_b_spatial, v_w_mix_out, v_norm_ffn2_g, v_ffn2_w_in, v_ffn2_w_out, v_norm_final_g):
    given = dict(x=x, c=c, w_ada=w_ada, b_ada=b_ada, norm_ffn1_g=norm_ffn1_g, ffn1_w_in=ffn1_w_in, ffn1_w_out=ffn1_w_out, norm_mix_g=norm_mix_g, w_mix_in=w_mix_in, w_pool=w_pool, pool_scale=pool_scale, gmlp_ln_g=gmlp_ln_g, gmlp_ln_b=gmlp_ln_b, w_spatial=w_spatial, b_spatial=b_spatial, w_mix_out=w_mix_out, norm_ffn2_g=norm_ffn2_g, ffn2_w_in=ffn2_w_in, ffn2_w_out=ffn2_w_out, norm_final_g=norm_final_g, loss_target=loss_target, m_w_ada=m_w_ada, m_b_ada=m_b_ada, m_norm_ffn1_g=m_norm_ffn1_g, m_ffn1_w_in=m_ffn1_w_in, m_ffn1_w_out=m_ffn1_w_out, m_norm_mix_g=m_norm_mix_g, m_w_mix_in=m_w_mix_in, m_w_pool=m_w_pool, m_pool_scale=m_pool_scale, m_gmlp_ln_g=m_gmlp_ln_g, m_gmlp_ln_b=m_gmlp_ln_b, m_w_spatial=m_w_spatial, m_b_spatial=m_b_spatial, m_w_mix_out=m_w_mix_out, m_norm_ffn2_g=m_norm_ffn2_g, m_ffn2_w_in=m_ffn2_w_in, m_ffn2_w_out=m_ffn2_w_out, m_norm_final_g=m_norm_final_g, v_w_ada=v_w_ada, v_b_ada=v_b_ada, v_norm_ffn1_g=v_norm_ffn1_g, v_ffn1_w_in=v_ffn1_w_in, v_ffn1_w_out=v_ffn1_w_out, v_norm_mix_g=v_norm_mix_g, v_w_mix_in=v_w_mix_in, v_w_pool=v_w_pool, v_pool_scale=v_pool_scale, v_gmlp_ln_g=v_gmlp_ln_g, v_gmlp_ln_b=v_gmlp_ln_b, v_w_spatial=v_w_spatial, v_b_spatial=v_b_spatial, v_w_mix_out=v_w_mix_out, v_norm_ffn2_g=v_norm_ffn2_g, v_ffn2_w_in=v_ffn2_w_in, v_ffn2_w_out=v_ffn2_w_out, v_norm_final_g=v_norm_final_g)
    weights = {n: given[n] for n in TWIN_WEIGHTS}
    shared = {n: given[n] for n in SHARED_INPUTS}
    per_example = {n: given[n] for n in ['x', 'c']}
    grad_fn = _jax.value_and_grad(_loss, argnums=(0, 1))

    def one_microbatch(ex, loss_target):
        ex = dict(ex)
        diff = ex.pop(TWIN_DIFF_INPUT)
        return grad_fn(weights, diff, {**shared, **ex}, loss_target)

    if N_MICROBATCH == 1:
        loss, (grad_w, grad_x) = one_microbatch(per_example, given["loss_target"])
    else:
        def body(carry, xs):
            loss_sum, grad_sum = carry
            l_k, (gw_k, gx_k) = one_microbatch(xs[0], xs[1])
            with _jax.named_scope("update"):
                return (loss_sum + l_k, _jax.tree.map(_jnp.add, grad_sum, gw_k)), gx_k

        init = (_jnp.zeros((), _jnp.float32), _jax.tree.map(_jnp.zeros_like, weights))
        (loss, grad_w), grad_x = _jax.lax.scan(body, init, (per_example, given["loss_target"]))
    with _jax.named_scope("update"):
        delta_w, new_m, new_v = {}, {}, {}
        for n in TWIN_WEIGHTS:
            delta_w[n], new_m[n], new_v[n] = _adamw(weights[n], grad_w[n], given["m_" + n], given["v_" + n])
    return (loss, grad_x, *[grad_w[n] for n in TWIN_WEIGHTS], *[delta_w[n] for n in TWIN_WEIGHTS],
            *[new_m[n] for n in TWIN_WEIGHTS], *[new_v[n] for n in TWIN_WEIGHTS])
```

```python
import functools
import math

import jax
import jax.numpy as jnp
from jax import lax
from jax.experimental import pallas as pl
from jax.experimental.pallas import tpu as pltpu

D = 1024
F = 2816
DP = 512
DG = 512
DPROJ = DP + 2 * DG
CHUNK = 128
WINDOWS = (2, 4, 8, 16)
HALO = 16
NDEV = 8
T_FFN = 512
T_MIX = 256
EPS = 1e-6
LR, B1, B2, AEPS, WD, STEP = 0.001, 0.9, 0.999, 1e-08, 0.01, 10
BC1 = 1.0 - B1 ** STEP
BC2 = 1.0 - B2 ** STEP
GELU_C = math.sqrt(2.0 / math.pi)
GELU_A = 0.044715

BF = jnp.bfloat16
F32 = jnp.float32
MESH = pl.DeviceIdType.MESH
HBM = pl.BlockSpec(memory_space=pltpu.HBM)
VMEM = pl.BlockSpec(memory_space=pltpu.VMEM)

NT = (((1,), (1,)), ((), ()))
TN = (((0,), (0,)), ((), ()))


def _dot(a, b):
    return jnp.dot(a, b, preferred_element_type=F32)


def _dot_nt(a, b):
    return lax.dot_general(a, b, NT, preferred_element_type=F32)


def _dot_tn(a, b):
    return lax.dot_general(a, b, TN, preferred_element_type=F32)


def _cparams(vmem_mb, sem=None):
    kw = dict(vmem_limit_bytes=vmem_mb * 1024 * 1024)
    if sem is not None:
        kw["dimension_semantics"] = sem
    return pltpu.CompilerParams(**kw)


def _position():
    return lax.axis_index("x"), lax.axis_index("y"), lax.axis_index("c")


def _slot(p):
    return 4 * p[0] + 2 * p[1] + p[2]


def _flip(me, d):
    x, y, c = me
    return (1 - x if d & 4 else x, 1 - y if d & 2 else y, 1 - c if d & 1 else c)


def _remote(src, dst, send_sem, recv_sem, to):
    return pltpu.make_async_remote_copy(src_ref=src, dst_ref=dst, send_sem=send_sem, recv_sem=recv_sem,
                                        device_id=to, device_id_type=MESH)


def _rms_mod(x, gn, shift, scale):
    ms = jnp.mean(x * x, axis=-1, keepdims=True)
    r = lax.rsqrt(ms + EPS)
    xhat = x * r
    n = xhat * gn
    h = n * (1.0 + scale) + shift
    return r, xhat, n, h


def _rms_mod_bwd(dh, dres, r, xhat, n, gn, scale):
    dshift = jnp.sum(dh, axis=0, keepdims=True)
    dscale = jnp.sum(dh * n, axis=0, keepdims=True)
    dn = dh * (1.0 + scale)
    dgn = jnp.sum(dn * xhat, axis=0, keepdims=True)
    dxhat = dn * gn
    dx = dres + r * (dxhat - xhat * jnp.mean(dxhat * xhat, axis=-1, keepdims=True))
    return dx, dshift, dscale, dgn


def _rows3(a, b, c, width):
    row = lax.broadcasted_iota(jnp.int32, (8, width), 0)
    z = jnp.zeros((8, width), F32)
    return jnp.where(row == 0, a, z) + jnp.where(row == 1, b, z) + jnp.where(row == 2, c, z)


def _gelu(x):
    t = jnp.tanh(GELU_C * (x + GELU_A * x * x * x))
    return 0.5 * x * (1.0 + t), t


def _gelu_grad(x, t):
    return 0.5 * (1.0 + t) + 0.5 * x * (1.0 - t * t) * GELU_C * (1.0 + 3.0 * GELU_A * x * x)


def _adamw(w, g, m, v):
    m = B1 * m + (1.0 - B1) * g
    v = B2 * v + (1.0 - B2) * (g * g)
    m_hat = m / BC1
    v_hat = v / BC2
    delta = -LR * (m_hat / (jnp.sqrt(v_hat) + AEPS) + WD * w)
    return delta, m, v


def _all_gather(shards, name):
    n = len(shards)

    def body(*refs):
        xs, outs = refs[:n], refs[n:2 * n]
        send_sems, recv_sems, local_sems = refs[2 * n:]
        me = _position()
        x, y, c = me
        sibling = (x, y, 1 - c)
        chips = [(1 - x, y), (x, 1 - y), (1 - x, 1 - y)]

        def copy(a, k, block, to, src=None):
            dst = outs[a].at[_slot(block)]
            return _remote(dst if src is None else src, dst, send_sems.at[a, k], recv_sems.at[a, k], to)

        mine = [pltpu.make_async_copy(xs[a], outs[a].at[_slot(me)], local_sems.at[a]) for a in range(n)]
        for cp in mine:
            cp.start()
        first = []
        for a in range(n):
            first.append(copy(a, 0, me, sibling, src=xs[a]))
            for j, chip in enumerate(chips):
                first.append(copy(a, 1 + j, me, (*chip, c), src=xs[a]))
        for cp in first:
            cp.start()
        passed = []
        for j, chip in enumerate(chips):
            for a in range(n):
                copy(a, 1 + j, (*chip, c), me).wait_recv()
                cp = copy(a, 4 + j, (*chip, c), sibling)
                cp.start()
                passed.append(cp)
        for a in range(n):
            copy(a, 0, sibling, me).wait_recv()
        for j, chip in enumerate(chips):
            for a in range(n):
                copy(a, 4 + j, (*chip, 1 - c), me).wait_recv()
        for cp in first + passed:
            cp.wait_send()
        for cp in mine:
            cp.wait()

    return pl.pallas_call(
        body, name=name,
        out_shape=[jax.ShapeDtypeStruct((NDEV,) + s.shape, s.dtype) for s in shards],
        in_specs=[HBM] * n, out_specs=[HBM] * n,
        scratch_shapes=[pltpu.SemaphoreType.DMA((n, 7)), pltpu.SemaphoreType.DMA((n, 7)),
                        pltpu.SemaphoreType.DMA((n,))],
    )(*shards)


def _exchange_slots(bufs, name):
    n = len(bufs)

    def body(*refs):
        xs, outs = refs[:n], refs[n:2 * n]
        send_sems, recv_sems, local_sems = refs[2 * n:]
        me = _position()
        mine = [pltpu.make_async_copy(xs[a].at[_slot(me)], outs[a].at[_slot(me)], local_sems.at[a])
                for a in range(n)]
        for cp in mine:
            cp.start()
        copies = []
        for a in range(n):
            for d in range(1, NDEV):
                to = _flip(me, d)
                copies.append(_remote(xs[a].at[_slot(to)], outs[a].at[_slot(me)],
                                      send_sems.at[a, d - 1], recv_sems.at[a, d - 1], to))
        for cp in copies:
            cp.start()
        for cp in copies:
            cp.wait()
        for cp in mine:
            cp.wait()

    return pl.pallas_call(
        body, name=name,
        out_shape=[jax.ShapeDtypeStruct(b.shape, b.dtype) for b in bufs],
        in_specs=[HBM] * n, out_specs=[HBM] * n,
        scratch_shapes=[pltpu.SemaphoreType.DMA((n, 7)), pltpu.SemaphoreType.DMA((n, 7)),
                        pltpu.SemaphoreType.DMA((n,))],
    )(*bufs)


def _ada_forward(c8, w_ada, b8):
    wc = w_ada.shape[1]

    def body(c8_ref, w_ref, b8_ref, mod_ref, cact_ref, call_ref, mall_ref, send_sems, recv_sems):
        me = _position()
        my = _slot(me)
        row = lax.broadcasted_iota(jnp.int32, (8, 1), 0)
        call_ref[my] = c8_ref[...]
        sends = []
        for d in range(1, NDEV):
            to = _flip(me, d)
            sends.append(_remote(c8_ref, call_ref.at[my], send_sems.at[0, d - 1], recv_sems.at[0, d - 1], to))
        for cp in sends:
            cp.start()
        for cp in sends:
            cp.wait()
        c_all = jnp.zeros((8, D), F32)
        for k in range(NDEV):
            c_all = c_all + jnp.where(row == k, call_ref[k], 0.0)
        cact = c_all * jax.nn.sigmoid(c_all)
        cact_ref[...] = cact
        part = _dot(cact.astype(BF), w_ref[...].astype(BF))
        mall_ref[my] = part
        sends = []
        for d in range(1, NDEV):
            to = _flip(me, d)
            sends.append(_remote(mall_ref.at[my], mall_ref.at[my], send_sems.at[1, d - 1], recv_sems.at[1, d - 1], to))
        for cp in sends:
            cp.start()
        for cp in sends:
            cp.wait()
        out = jnp.zeros((8, wc), F32)
        for k in range(NDEV):
            piece = jnp.sum(jnp.where(row == my, mall_ref[k], 0.0), axis=0, keepdims=True)
            out = out + jnp.where(row == k, piece, 0.0)
        mod_ref[...] = out + b8_ref[...]

    return pl.pallas_call(
        body, name="ada_forward",
        out_shape=[jax.ShapeDtypeStruct((8, wc), F32), jax.ShapeDtypeStruct((8, D), F32)],
        in_specs=[VMEM, VMEM, VMEM], out_specs=[VMEM, VMEM],
        scratch_shapes=[pltpu.VMEM((NDEV, 8, D), F32), pltpu.VMEM((NDEV, 8, wc), F32),
                        pltpu.SemaphoreType.DMA((2, 7)), pltpu.SemaphoreType.DMA((2, 7))],
        compiler_params=_cparams(32),
    )(c8, w_ada, b8)


def _small_reduce_update(gpack, wpack, mpack, vpack, dmod_rep, cact_all):
    rows = gpack.shape[0]
    rs = rows // NDEV
    wc = dmod_rep.shape[2]

    def body(g_ref, w_ref, m_ref, v_ref, dm_ref, cact_ref,
             gsum_ref, delta_ref, newm_ref, newv_ref, gada_ref,
             rs_ref, ag_ref, dw_ref, send_sems, recv_sems):
        me = _position()
        my = _slot(me)
        row = lax.broadcasted_iota(jnp.int32, (8, 1), 0)
        mine = pl.ds(pl.multiple_of(my * rs, 8), rs)
        rs_ref[my] = g_ref[mine, :]
        dw_ref[my] = dm_ref[my]
        first = []
        for d in range(1, NDEV):
            to = _flip(me, d)
            theirs = pl.ds(pl.multiple_of(_slot(to) * rs, 8), rs)
            first.append(_remote(g_ref.at[theirs, :], rs_ref.at[my], send_sems.at[0, d - 1], recv_sems.at[0, d - 1], to))
            first.append(_remote(dm_ref.at[_slot(to)], dw_ref.at[my], send_sems.at[1, d - 1], recv_sems.at[1, d - 1], to))
        for cp in first:
            cp.start()
        for cp in first:
            cp.wait()
        red = rs_ref[0]
        for k in range(1, NDEV):
            red = red + rs_ref[k]
        ag_ref[my] = red
        second = []
        for d in range(1, NDEV):
            to = _flip(me, d)
            second.append(_remote(ag_ref.at[my], ag_ref.at[my], send_sems.at[2, d - 1], recv_sems.at[2, d - 1], to))
        for cp in second:
            cp.start()
        dmine = jnp.zeros((8, wc), F32)
        for k in range(NDEV):
            dmine = dmine + jnp.where(row == k, dw_ref[k], 0.0)
        gada_ref[...] = _dot_tn(cact_ref[...].astype(BF), dmine.astype(BF))
        for cp in second:
            cp.wait()
        for k in range(NDEV):
            sl = slice(k * rs, (k + 1) * rs)
            g = ag_ref[k]
            gsum_ref[sl, :] = g
            delta, m, v = _adamw(w_ref[sl, :], g, m_ref[sl, :], v_ref[sl, :])
            delta_ref[sl, :] = delta
            newm_ref[sl, :] = m
            newv_ref[sl, :] = v

    pack = jax.ShapeDtypeStruct(gpack.shape, F32)
    return pl.pallas_call(
        body, name="small_reduce_update",
        out_shape=[pack, pack, pack, pack, jax.ShapeDtypeStruct((D, wc), F32)],
        in_specs=[VMEM] * 6, out_specs=[VMEM] * 5,
        scratch_shapes=[pltpu.VMEM((NDEV, rs, 128), F32), pltpu.VMEM((NDEV, rs, 128), F32),
                        pltpu.VMEM((NDEV, 8, wc), F32),
                        pltpu.SemaphoreType.DMA((3, 7)), pltpu.SemaphoreType.DMA((3, 7))],
        compiler_params=_cparams(48),
    )(gpack, wpack, mpack, vpack, dmod_rep, cact_all)


def _sum_adamw(recv, w, m, v, tr, name):
    R, C = w.shape

    def body(r_ref, w_ref, m_ref, v_ref, g_ref, d_ref, nm_ref, nv_ref):
        g = r_ref[0].astype(F32)
        for k in range(1, NDEV):
            g = g + r_ref[k].astype(F32)
        g_ref[...] = g
        d_ref[...], nm_ref[...], nv_ref[...] = _adamw(w_ref[...], g, m_ref[...], v_ref[...])

    blk = pl.BlockSpec((tr, C), lambda i: (i, 0))
    out = jax.ShapeDtypeStruct((R, C), F32)
    return pl.pallas_call(
        body, name=name, grid=(R // tr,), out_shape=[out] * 4,
        in_specs=[pl.BlockSpec((NDEV, tr, C), lambda i: (0, i, 0)), blk, blk, blk], out_specs=[blk] * 4,
        compiler_params=_cparams(48, ("arbitrary",)),
    )(recv, w, m, v)


def _plain_adamw(g, w, m, v, tr, name):
    R, C = w.shape

    def body(g_ref, w_ref, m_ref, v_ref, d_ref, nm_ref, nv_ref):
        d_ref[...], nm_ref[...], nv_ref[...] = _adamw(w_ref[...], g_ref[...], m_ref[...], v_ref[...])

    blk = pl.BlockSpec((tr, C), lambda i: (i, 0))
    out = jax.ShapeDtypeStruct((R, C), F32)
    return pl.pallas_call(
        body, name=name, grid=(R // tr,), out_shape=[out] * 3,
        in_specs=[blk] * 4, out_specs=[blk] * 3,
        compiler_params=_cparams(48, ("arbitrary",)),
    )(g, w, m, v)


FC = F // 2


def _ffn_fwd(x, mod, gn, w_in, w_out, sub, name):
    S = x.shape[0]
    T = min(T_FFN, S)
    nJ = F // FC

    def body(x_ref, mod_ref, gn_ref, wg_ref, wu_ref, wo_ref, xo_ref, gu_ref, h_scr, acc_scr):
        j = pl.program_id(1)

        @pl.when(j == 0)
        def _():
            _, _, _, h = _rms_mod(x_ref[...], gn_ref[...], mod_ref[3 * sub:3 * sub + 1, :],
                                  mod_ref[3 * sub + 1:3 * sub + 2, :])
            h_scr[...] = h.astype(BF)
            acc_scr[...] = jnp.zeros_like(acc_scr)

        h = h_scr[...]
        g = _dot(h, wg_ref[0])
        u = _dot(h, wu_ref[0])
        gu_ref[0] = g.astype(BF)
        gu_ref[1] = u.astype(BF)
        a = (g * jax.nn.sigmoid(g) * u).astype(BF)
        acc_scr[...] += _dot(a, wo_ref[...])

        @pl.when(j == nJ - 1)
        def _():
            xo_ref[...] = x_ref[...] + (0.5 * mod_ref[3 * sub + 2:3 * sub + 3, :]) * acc_scr[...]

    return pl.pallas_call(
        body, name=name, grid=(S // T, nJ),
        out_shape=[jax.ShapeDtypeStruct((S, D), F32), jax.ShapeDtypeStruct((2, S, F), BF)],
        in_specs=[pl.BlockSpec((T, D), lambda i, j: (i, 0)),
                  pl.BlockSpec((9, D), lambda i, j: (0, 0)),
                  pl.BlockSpec((1, D), lambda i, j: (0, 0)),
                  pl.BlockSpec((1, D, FC), lambda i, j: (0, 0, j)),
                  pl.BlockSpec((1, D, FC), lambda i, j: (1, 0, j)),
                  pl.BlockSpec((FC, D), lambda i, j: (j, 0))],
        out_specs=[pl.BlockSpec((T, D), lambda i, j: (i, 0)),
                   pl.BlockSpec((2, T, FC), lambda i, j: (0, i, j))],
        scratch_shapes=[pltpu.VMEM((T, D), BF), pltpu.VMEM((T, D), F32)],
        compiler_params=_cparams(56, ("arbitrary", "arbitrary")),
    )(x, mod, gn, w_in, w_in, w_out)


def _ffn_bwd_hidden(dx, mod, gu, w_out, sub, name):
    S = dx.shape[0]
    T = min(T_FFN, S)
    nS, nJ = S // T, F // FC

    def body(dx_ref, mod_ref, gu_ref, wo_ref, dgu_ref, gw_ref, dgate_ref):
        i = pl.program_id(1)
        gate = mod_ref[3 * sub + 2:3 * sub + 3, :]
        dx = dx_ref[...]
        da = _dot_nt((dx * (0.5 * gate)).astype(BF), wo_ref[...])
        g = gu_ref[0].astype(F32)
        u = gu_ref[1].astype(F32)
        sg = jax.nn.sigmoid(g)
        s = g * sg
        dgu_ref[0] = (da * u * (sg * (1.0 + g * (1.0 - sg)))).astype(BF)
        dgu_ref[1] = (da * s).astype(BF)
        contrib = _dot_tn((s * u).astype(BF), dx.astype(BF))

        @pl.when(i == 0)
        def _():
            gw_ref[...] = contrib

        @pl.when(i > 0)
        def _():
            gw_ref[...] += contrib

        @pl.when(i == nS - 1)
        def _():
            acc = gw_ref[...]
            dgate = 0.5 * jnp.sum(acc * wo_ref[...].astype(F32), axis=0, keepdims=True)
            dgate_ref[...] = jnp.broadcast_to(dgate, (8, D))
            gw_ref[...] = acc * (0.5 * gate)

    return pl.pallas_call(
        body, name=name, grid=(nJ, nS),
        out_shape=[jax.ShapeDtypeStruct((2, S, F), BF), jax.ShapeDtypeStruct((F, D), F32),
                   jax.ShapeDtypeStruct((8 * nJ, D), F32)],
        in_specs=[pl.BlockSpec((T, D), lambda j, i: (i, 0)),
                  pl.BlockSpec((9, D), lambda j, i: (0, 0)),
                  pl.BlockSpec((2, T, FC), lambda j, i: (0, i, j)),
                  pl.BlockSpec((FC, D), lambda j, i: (j, 0))],
        out_specs=[pl.BlockSpec((2, T, FC), lambda j, i: (0, i, j)),
                   pl.BlockSpec((FC, D), lambda j, i: (j, 0)),
                   pl.BlockSpec((8, D), lambda j, i: (j, 0))],
        compiler_params=_cparams(56, ("arbitrary", "arbitrary")),
    )(dx, mod, gu, w_out)


def _ffn_bwd_input(dgu, w_in, x, dx, mod, gn, sub, name):
    S = x.shape[0]
    T = min(T_FFN, S)
    nJ = F // FC

    def body(dgu_ref, wg_ref, wu_ref, x_ref, dx_ref, mod_ref, gn_ref, dxin_ref, h_ref, st_ref, acc_scr):
        i, j = pl.program_id(0), pl.program_id(1)

        @pl.when(j == 0)
        def _():
            acc_scr[...] = jnp.zeros_like(acc_scr)

        acc_scr[...] += _dot_nt(dgu_ref[0], wg_ref[0]) + _dot_nt(dgu_ref[1], wu_ref[0])

        @pl.when(j == nJ - 1)
        def _():
            gn = gn_ref[...]
            scale = mod_ref[3 * sub + 1:3 * sub + 2, :]
            r, xhat, n, h = _rms_mod(x_ref[...], gn, mod_ref[3 * sub:3 * sub + 1, :], scale)
            h_ref[...] = h.astype(BF)
            dxin, dshift, dscale, dgn = _rms_mod_bwd(acc_scr[...], dx_ref[...], r, xhat, n, gn, scale)
            dxin_ref[...] = dxin
            upd = _rows3(dshift, dscale, dgn, D)

            @pl.when(i == 0)
            def _():
                st_ref[...] = upd

            @pl.when(i > 0)
            def _():
                st_ref[...] += upd

    return pl.pallas_call(
        body, name=name, grid=(S // T, nJ),
        out_shape=[jax.ShapeDtypeStruct((S, D), F32), jax.ShapeDtypeStruct((S, D), BF),
                   jax.ShapeDtypeStruct((8, D), F32)],
        in_specs=[pl.BlockSpec((2, T, FC), lambda i, j: (0, i, j)),
                  pl.BlockSpec((1, D, FC), lambda i, j: (0, 0, j)),
                  pl.BlockSpec((1, D, FC), lambda i, j: (1, 0, j)),
                  pl.BlockSpec((T, D), lambda i, j: (i, 0)),
                  pl.BlockSpec((T, D), lambda i, j: (i, 0)),
                  pl.BlockSpec((9, D), lambda i, j: (0, 0)),
                  pl.BlockSpec((1, D), lambda i, j: (0, 0))],
        out_specs=[pl.BlockSpec((T, D), lambda i, j: (i, 0)),
                   pl.BlockSpec((T, D), lambda i, j: (i, 0)),
                   pl.BlockSpec((8, D), lambda i, j: (0, 0))],
        scratch_shapes=[pltpu.VMEM((T, D), F32)],
        compiler_params=_cparams(56, ("arbitrary", "arbitrary")),
    )(dgu, w_in, w_in, x, dx, mod, gn)


def _ffn_bwd_win(h, dgu, name):
    S = h.shape[0]
    T = min(T_FFN, S)
    nS, nJ = S // T, F // FC

    def body(h_ref, dgu_ref, out_ref):
        i = pl.program_id(2)
        contrib = _dot_tn(h_ref[...], dgu_ref[0])

        @pl.when(i == 0)
        def _():
            out_ref[0] = contrib

        @pl.when(i > 0)
        def _():
            out_ref[0] += contrib

    return pl.pallas_call(
        body, name=name, grid=(2, nJ, nS),
        out_shape=jax.ShapeDtypeStruct((2, D, F), F32),
        in_specs=[pl.BlockSpec((T, D), lambda p, j, i: (i, 0)),
                  pl.BlockSpec((1, T, FC), lambda p, j, i: (p, i, j))],
        out_specs=pl.BlockSpec((1, D, FC), lambda p, j, i: (p, 0, j)),
        compiler_params=_cparams(56, ("arbitrary", "arbitrary", "arbitrary")),
    )(h, dgu)


def _pool_counts(pos0, T):
    pos = pos0 + lax.broadcasted_iota(jnp.int32, (T, 1), 0)
    return [jnp.minimum(pos + 1, w).astype(F32) for w in WINDOWS]


def _pool_fwd(xa, halo, ext_scr, cnts, T):
    ext_scr[0:HALO, :] = halo
    ext_scr[HALO:HALO + T, :] = xa
    out = []
    for gi, w in enumerate(WINDOWS):
        cols = slice(128 * gi, 128 * gi + 128)
        acc = xa[:, cols]
        for k in range(1, w):
            acc = acc + ext_scr[HALO - k:HALO - k + T, cols]
        out.append(acc / cnts[gi] - xa[:, cols])
    return out


def _sgu_fwd(vnb, ws_ref, sv_scr, T):
    lane = lax.broadcasted_iota(jnp.int32, (CHUNK, 128), 1)
    for n in range(T // CHUNK):
        rows = slice(n * CHUNK, (n + 1) * CHUNK)
        for b in range(DG // 128):
            cols = slice(128 * b, 128 * b + 128)
            vb = vnb[rows, cols]
            sv_scr[rows, cols] = jnp.where(lane < 64, _dot(ws_ref[2 * b], vb), _dot(ws_ref[2 * b + 1], vb))


def _mix_fwd(x, mod, gn, wmi, wmo, wp, ps, lg, lb, ws, bias, name):
    S = x.shape[0]
    T = min(T_MIX, S)

    def body(x_ref, mod_ref, gn_ref, wmi_ref, wmo_ref, wp_ref, ps_ref, lg_ref, lb_ref, ws_ref, bias_ref,
             xo_ref, carry_scr, ext_scr, sv_scr, ycat_scr):
        i = pl.program_id(0)

        @pl.when(i == 0)
        def _():
            carry_scr[...] = jnp.zeros_like(carry_scr)

        x = x_ref[...]
        _, _, _, h = _rms_mod(x, gn_ref[...], mod_ref[3:4, :], mod_ref[4:5, :])
        proj = _dot(h.astype(BF), wmi_ref[...])
        xa = proj[:, 0:DP]
        p = _pool_fwd(xa, carry_scr[...], ext_scr, _pool_counts(i * T, T), T)
        carry_scr[...] = xa[T - HALO:T, :]
        for gi in range(4):
            cols = slice(128 * gi, 128 * gi + 128)
            ycat_scr[:, cols] = (_dot(p[gi].astype(BF), wp_ref[gi]) * ps_ref[:, cols]).astype(BF)
        u, _ = _gelu(proj[:, DP:DP + DG])
        v, _ = _gelu(proj[:, DP + DG:DPROJ])
        mu = jnp.mean(v, axis=-1, keepdims=True)
        vc = v - mu
        rstd = lax.rsqrt(jnp.mean(vc * vc, axis=-1, keepdims=True) + EPS)
        vn = vc * rstd * lg_ref[...] + lb_ref[...]
        _sgu_fwd(vn.astype(BF), ws_ref, sv_scr, T)
        for n in range(T // CHUNK):
            rows = slice(n * CHUNK, (n + 1) * CHUNK)
            ycat_scr[rows, DP:D] = (u[rows, :] * (sv_scr[rows, :] + bias_ref[...])).astype(BF)
        xo_ref[...] = x + mod_ref[5:6, :] * _dot(ycat_scr[...], wmo_ref[...])

    full = lambda shape: pl.BlockSpec(shape, lambda i: (0,) * len(shape))
    return pl.pallas_call(
        body, name=name, grid=(S // T,),
        out_shape=jax.ShapeDtypeStruct((S, D), F32),
        in_specs=[pl.BlockSpec((T, D), lambda i: (i, 0)), full((9, D)), full((1, D)), full((D, DPROJ)), full((D, D)),
                  full((4, 128, 128)), full((1, DP)), full((1, DG)), full((1, DG)), full((8, CHUNK, CHUNK)),
                  full((CHUNK, DG))],
        out_specs=pl.BlockSpec((T, D), lambda i: (i, 0)),
        scratch_shapes=[pltpu.VMEM((HALO, DP), F32), pltpu.VMEM((T + HALO, DP), F32), pltpu.VMEM((T, DG), F32),
                        pltpu.VMEM((T, D), BF)],
        compiler_params=_cparams(48, ("arbitrary",)),
    )(x, mod, gn, wmi, wmo, wp, ps, lg, lb, ws, bias)


def _mix_bwd(x, dxo, mod, gn, wmi, wmo, wp, ps, lg, lb, ws, bias, name):
    S = x.shape[0]
    T = min(T_MIX, S)
    nS = S // T
    hb = T // HALO

    def body(x_ref, xh_ref, dxo_ref, mod_ref, gn_ref, wmi_ref, wmo_ref, wp_ref, ps_ref, lg_ref, lb_ref, ws_ref,
             bias_ref, dxi_ref, gwmi_ref, gwmo_ref, gwp_ref, gws_ref, st_ref, vec_ref, dbias_ref,
             carry_scr, ext_scr, qext_scr, sv_scr, dvn_scr, ycat_scr, dproj_scr):
        i = pl.program_id(0)
        t = nS - 1 - i
        gn = gn_ref[...]
        shift, scale, gate = mod_ref[3:4, :], mod_ref[4:5, :], mod_ref[5:6, :]

        @pl.when(i == 0)
        def _():
            carry_scr[...] = jnp.zeros_like(carry_scr)
            gwmi_ref[...] = jnp.zeros_like(gwmi_ref)
            gwmo_ref[...] = jnp.zeros_like(gwmo_ref)
            gwp_ref[...] = jnp.zeros_like(gwp_ref)
            gws_ref[...] = jnp.zeros_like(gws_ref)
            st_ref[...] = jnp.zeros_like(st_ref)
            vec_ref[...] = jnp.zeros_like(vec_ref)
            dbias_ref[...] = jnp.zeros_like(dbias_ref)

        x = x_ref[...]
        dxo = dxo_ref[...]
        r, xhat, n, h = _rms_mod(x, gn, shift, scale)
        hbf = h.astype(BF)
        proj = _dot(hbf, wmi_ref[...])
        xa = proj[:, 0:DP]
        zu = proj[:, DP:DP + DG]
        zv = proj[:, DP + DG:DPROJ]
        _, _, _, hh = _rms_mod(xh_ref[...], gn, shift, scale)
        halo = _dot(hh.astype(BF), wmi_ref[:, 0:DP])
        halo = jnp.where(t == 0, 0.0, halo)
        cnts = _pool_counts(t * T, T)
        p = _pool_fwd(xa, halo, ext_scr, cnts, T)
        m = []
        for gi in range(4):
            cols = slice(128 * gi, 128 * gi + 128)
            m.append(_dot(p[gi].astype(BF), wp_ref[gi]))
            ycat_scr[:, cols] = (m[gi] * ps_ref[:, cols]).astype(BF)
        u, tu = _gelu(zu)
        v, tv = _gelu(zv)
        mu = jnp.mean(v, axis=-1, keepdims=True)
        vc = v - mu
        rstd = lax.rsqrt(jnp.mean(vc * vc, axis=-1, keepdims=True) + EPS)
        vhat = vc * rstd
        lg = lg_ref[...]
        vnb = (vhat * lg + lb_ref[...]).astype(BF)
        _sgu_fwd(vnb, ws_ref, sv_scr, T)
        for nck in range(T // CHUNK):
            rows = slice(nck * CHUNK, (nck + 1) * CHUNK)
            sv_scr[rows, :] = sv_scr[rows, :] + bias_ref[...]
        sv = sv_scr[...]
        ycat_scr[:, DP:D] = (u * sv).astype(BF)

        gwmo_ref[...] += _dot_tn(ycat_scr[...], dxo.astype(BF))
        dyc = _dot_nt((dxo * gate).astype(BF), wmo_ref[...])
        dya = dyc[:, 0:DP]
        dyb = dyc[:, DP:D]

        dps = []
        dp = []
        for gi in range(4):
            cols = slice(128 * gi, 128 * gi + 128)
            dps.append(jnp.sum(dya[:, cols] * m[gi], axis=0, keepdims=True))
            dm = (dya[:, cols] * ps_ref[:, cols]).astype(BF)
            gwp_ref[gi] += _dot_tn(p[gi].astype(BF), dm)
            dp.append(_dot_nt(dm, wp_ref[gi]))
            qext_scr[0:T, cols] = dp[gi] / cnts[gi]
        qext_scr[T:T + HALO, :] = carry_scr[...]
        for gi, w in enumerate(WINDOWS):
            cols = slice(128 * gi, 128 * gi + 128)
            acc = qext_scr[0:T, cols]
            for k in range(1, w):
                acc = acc + qext_scr[k:k + T, cols]
            dproj_scr[:, cols] = (acc - dp[gi]).astype(BF)
        carry_scr[...] = qext_scr[0:HALO, :]

        du = dyb * sv
        dsv = dyb * u
        lane = lax.broadcasted_iota(jnp.int32, (CHUNK, 128), 1)
        dbias = jnp.zeros((CHUNK, DG), F32)
        for nck in range(T // CHUNK):
            rows = slice(nck * CHUNK, (nck + 1) * CHUNK)
            dbias = dbias + dsv[rows, :]
            for b in range(DG // 128):
                cols = slice(128 * b, 128 * b + 128)
                dsvb = dsv[rows, cols]
                vb = vnb[rows, cols]
                gws_ref[2 * b] += _dot_nt(jnp.where(lane < 64, dsvb, 0.0).astype(BF), vb)
                gws_ref[2 * b + 1] += _dot_nt(jnp.where(lane < 64, 0.0, dsvb).astype(BF), vb)
                dsvbb = dsvb.astype(BF)
                dvn_scr[rows, cols] = jnp.where(lane < 64, _dot_tn(ws_ref[2 * b], dsvbb),
                                                _dot_tn(ws_ref[2 * b + 1], dsvbb))
        dbias_ref[...] += dbias
        dvn = dvn_scr[...]
        dlg = jnp.sum(dvn * vhat, axis=0, keepdims=True)
        dlb = jnp.sum(dvn, axis=0, keepdims=True)
        dvhat = dvn * lg
        dv = rstd * (dvhat - jnp.mean(dvhat, axis=-1, keepdims=True)
                     - vhat * jnp.mean(dvhat * vhat, axis=-1, keepdims=True))
        dproj_scr[:, DP:DP + DG] = (du * _gelu_grad(zu, tu)).astype(BF)
        dproj_scr[:, DP + DG:DPROJ] = (dv * _gelu_grad(zv, tv)).astype(BF)
        vec_ref[...] += _rows3(jnp.concatenate(dps, axis=1), dlg, dlb, DP)

        dproj = dproj_scr[...]
        gwmi_ref[...] += _dot_tn(hbf, dproj)
        dh = _dot_nt(dproj, wmi_ref[...])
        dxi, dshift, dscale, dgn = _rms_mod_bwd(dh, dxo, r, xhat, n, gn, scale)
        dxi_ref[...] = dxi
        st_ref[...] += _rows3(dshift, dscale, dgn, D)

        @pl.when(i == nS - 1)
        def _():
            acc = gwmo_ref[...]
            dgate = jnp.sum(acc * wmo_ref[...].astype(F32), axis=0, keepdims=True)
            row = lax.broadcasted_iota(jnp.int32, (8, D), 0)
            st_ref[...] += jnp.where(row == 3, dgate, 0.0)
            gwmo_ref[...] = acc * gate
            tt = lax.broadcasted_iota(jnp.int32, (CHUNK, CHUNK), 0)
            ss = lax.broadcasted_iota(jnp.int32, (CHUNK, CHUNK), 1)
            for hd in range(8):
                gws_ref[hd] = jnp.where(tt >= ss, gws_ref[hd], 0.0)

    full = lambda shape: pl.BlockSpec(shape, lambda i: (0,) * len(shape))
    return pl.pallas_call(
        body, name=name, grid=(nS,),
        out_shape=[jax.ShapeDtypeStruct((S, D), F32), jax.ShapeDtypeStruct((D, DPROJ), F32),
                   jax.ShapeDtypeStruct((D, D), F32), jax.ShapeDtypeStruct((4, 128, 128), F32),
                   jax.ShapeDtypeStruct((8, CHUNK, CHUNK), F32), jax.ShapeDtypeStruct((8, D), F32),
                   jax.ShapeDtypeStruct((8, DP), F32), jax.ShapeDtypeStruct((CHUNK, DG), F32)],
        in_specs=[pl.BlockSpec((T, D), lambda i: (nS - 1 - i, 0)),
                  pl.BlockSpec((HALO, D), lambda i: (jnp.maximum((nS - 1 - i) * hb - 1, 0), 0)),
                  pl.BlockSpec((T, D), lambda i: (nS - 1 - i, 0)),
                  full((9, D)), full((1, D)), full((D, DPROJ)), full((D, D)),
                  full((4, 128, 128)), full((1, DP)), full((1, DG)), full((1, DG)), full((8, CHUNK, CHUNK)),
                  full((CHUNK, DG))],
        out_specs=[pl.BlockSpec((T, D), lambda i: (nS - 1 - i, 0)), full((D, DPROJ)), full((D, D)),
                   full((4, 128, 128)), full((8, CHUNK, CHUNK)), full((8, D)), full((8, DP)), full((CHUNK, DG))],
        scratch_shapes=[pltpu.VMEM((HALO, DP), F32), pltpu.VMEM((T + HALO, DP), F32),
                        pltpu.VMEM((T + HALO, DP), F32), pltpu.VMEM((T, DG), F32), pltpu.VMEM((T, DG), F32),
                        pltpu.VMEM((T, D), BF), pltpu.VMEM((T, DPROJ), BF)],
        compiler_params=_cparams(56, ("arbitrary",)),
    )(x, x, dxo, mod, gn, wmi, wmo, wp, ps, lg, lb, ws, bias)


def _head_sums(dbias):
    def body(d_ref, o_ref):
        ch = lax.broadcasted_iota(jnp.int32, (DG, 128), 0)
        hd = lax.broadcasted_iota(jnp.int32, (DG, 128), 1)
        sel = jnp.where(ch // 64 == hd, 1.0, 0.0).astype(F32)
        o_ref[...] = jnp.dot(d_ref[...], sel, preferred_element_type=F32, precision=lax.Precision.HIGHEST)

    return pl.pallas_call(body, name="head_sums", out_shape=jax.ShapeDtypeStruct((CHUNK, 128), F32),
                          in_specs=[VMEM], out_specs=VMEM)(dbias)


def _loss_and_grad(x, gf, target):
    S = x.shape[0]
    T = min(T_FFN, S)

    def body(x_ref, gf_ref, t_ref, dx_ref, st_ref):
        i = pl.program_id(0)
        x = x_ref[...]
        gf = gf_ref[...]
        r = lax.rsqrt(jnp.mean(x * x, axis=-1, keepdims=True) + EPS)
        xhat = x * r
        e = xhat * gf - t_ref[...]
        loss = 0.5 * jnp.sum(jnp.sum(e * e, axis=-1, keepdims=True), axis=0, keepdims=True) / D
        dy = e / D
        dgf = jnp.sum(dy * xhat, axis=0, keepdims=True)
        dxhat = dy * gf
        dx_ref[...] = r * (dxhat - xhat * jnp.mean(dxhat * xhat, axis=-1, keepdims=True))
        upd = _rows3(dgf, jnp.broadcast_to(loss, (1, D)), jnp.zeros((1, D), F32), D)

        @pl.when(i == 0)
        def _():
            st_ref[...] = upd

        @pl.when(i > 0)
        def _():
            st_ref[...] += upd

    return pl.pallas_call(
        body, name="loss_and_grad", grid=(S // T,),
        out_shape=[jax.ShapeDtypeStruct((S, D), F32), jax.ShapeDtypeStruct((8, D), F32)],
        in_specs=[pl.BlockSpec((T, D), lambda i: (i, 0)), pl.BlockSpec((1, D), lambda i: (0, 0)),
                  pl.BlockSpec((T, D), lambda i: (i, 0))],
        out_specs=[pl.BlockSpec((T, D), lambda i: (i, 0)), pl.BlockSpec((8, D), lambda i: (0, 0))],
        compiler_params=_cparams(48, ("arbitrary",)),
    )(x, gf, target)


def _win_from_slots(g):
    return g.reshape(2, 4, D, F // 4).transpose(0, 2, 1, 3).reshape(2, D, F)


def _win_to_slots(w):
    return w.reshape(2, D, 4, F // 4).transpose(0, 2, 1, 3).reshape(NDEV, D, F // 4)


def _wmi_from_slots(g):
    return g.transpose(1, 0, 2).reshape(D, DPROJ)


def _wmi_to_slots(w):
    return w.reshape(D, NDEV, DPROJ // NDEV).transpose(1, 0, 2)


SMALL = ("w_spatial", "w_pool", "norm_ffn1_g", "norm_mix_g", "norm_ffn2_g", "norm_final_g",
         "pool_scale", "gmlp_ln_g", "gmlp_ln_b", "b_spatial", "b_ada")
PACK_ROWS = 1664


def _pack(parts):
    flat = [parts[k].reshape(-1, 128) for k in SMALL]
    used = sum(f.shape[0] for f in flat)
    return jnp.concatenate(flat + [jnp.zeros((PACK_ROWS - used, 128), F32)], axis=0)


def _unpack(pack, shapes):
    out, at = {}, 0
    for k in SMALL:
        rows = math.prod(shapes[k]) // 128
        out[k] = pack[at:at + rows].reshape(shapes[k])
        at += rows
    return out


def kernel(x, c, w_ada, b_ada, norm_ffn1_g, ffn1_w_in, ffn1_w_out, norm_mix_g, w_mix_in, w_pool, pool_scale, gmlp_ln_g, gmlp_ln_b, w_spatial, b_spatial, w_mix_out, norm_ffn2_g, ffn2_w_in, ffn2_w_out, norm_final_g, loss_target, m_w_ada, m_b_ada, m_norm_ffn1_g, m_ffn1_w_in, m_ffn1_w_out, m_norm_mix_g, m_w_mix_in, m_w_pool, m_pool_scale, m_gmlp_ln_g, m_gmlp_ln_b, m_w_spatial, m_b_spatial, m_w_mix_out, m_norm_ffn2_g, m_ffn2_w_in, m_ffn2_w_out, m_norm_final_g, v_w_ada, v_b_ada, v_norm_ffn1_g, v_ffn1_w_in, v_ffn1_w_out, v_norm_mix_g, v_w_mix_in, v_w_pool, v_pool_scale, v_gmlp_ln_g, v_gmlp_ln_b, v_w_spatial, v_b_spatial, v_w_mix_out, v_norm_ffn2_g, v_ffn2_w_in, v_ffn2_w_out, v_norm_final_g):
    weights = dict(w_ada=w_ada, b_ada=b_ada, norm_ffn1_g=norm_ffn1_g, ffn1_w_in=ffn1_w_in, ffn1_w_out=ffn1_w_out,
                   norm_mix_g=norm_mix_g, w_mix_in=w_mix_in, w_pool=w_pool, pool_scale=pool_scale,
                   gmlp_ln_g=gmlp_ln_g, gmlp_ln_b=gmlp_ln_b, w_spatial=w_spatial, b_spatial=b_spatial,
                   w_mix_out=w_mix_out, norm_ffn2_g=norm_ffn2_g, ffn2_w_in=ffn2_w_in, ffn2_w_out=ffn2_w_out,
                   norm_final_g=norm_final_g)
    mom1 = dict(w_ada=m_w_ada, b_ada=m_b_ada, norm_ffn1_g=m_norm_ffn1_g, ffn1_w_in=m_ffn1_w_in,
                ffn1_w_out=m_ffn1_w_out, norm_mix_g=m_norm_mix_g, w_mix_in=m_w_mix_in, w_pool=m_w_pool,
                pool_scale=m_pool_scale, gmlp_ln_g=m_gmlp_ln_g, gmlp_ln_b=m_gmlp_ln_b, w_spatial=m_w_spatial,
                b_spatial=m_b_spatial, w_mix_out=m_w_mix_out, norm_ffn2_g=m_norm_ffn2_g, ffn2_w_in=m_ffn2_w_in,
                ffn2_w_out=m_ffn2_w_out, norm_final_g=m_norm_final_g)
    mom2 = dict(w_ada=v_w_ada, b_ada=v_b_ada, norm_ffn1_g=v_norm_ffn1_g, ffn1_w_in=v_ffn1_w_in,
                ffn1_w_out=v_ffn1_w_out, norm_mix_g=v_norm_mix_g, w_mix_in=v_w_mix_in, w_pool=v_w_pool,
                pool_scale=v_pool_scale, gmlp_ln_g=v_gmlp_ln_g, gmlp_ln_b=v_gmlp_ln_b, w_spatial=v_w_spatial,
                b_spatial=v_b_spatial, w_mix_out=v_w_mix_out, norm_ffn2_g=v_norm_ffn2_g, ffn2_w_in=v_ffn2_w_in,
                ffn2_w_out=v_ffn2_w_out, norm_final_g=v_norm_final_g)
    order = list(weights)
    shapes = {k: weights[k].shape for k in order}

    xs = x[0]
    target = loss_target[0]
    big = ("ffn1_w_in", "ffn1_w_out", "w_mix_in", "w_mix_out", "ffn2_w_in", "ffn2_w_out")

    gathered = _all_gather([weights[k][0].astype(BF) for k in big], "gather_weights")
    w1_in = _win_from_slots(gathered[0])
    w1_out = gathered[1].reshape(F, D)
    wmi = _wmi_from_slots(gathered[2])
    wmo = gathered[3].reshape(D, D)
    w2_in = _win_from_slots(gathered[4])
    w2_out = gathered[5].reshape(F, D)

    wc = w_ada.shape[2]
    modp, cact_all = _ada_forward(jnp.broadcast_to(c, (8, D)), w_ada[0], b_ada.reshape(NDEV, wc))
    mod = modp.reshape(9, D)

    tril = jnp.tril(jnp.ones((CHUNK, CHUNK), dtype=bool))
    ws_b = jnp.where(tril[None], w_spatial[0], 0.0).astype(BF)
    wp_b = w_pool[0].astype(BF)
    bias = jnp.repeat(b_spatial[0].T, DG // 8, axis=1)
    mix_args = (wmi, wmo, wp_b, pool_scale, gmlp_ln_g, gmlp_ln_b, ws_b, bias)

    x1, gu1 = _ffn_fwd(xs, mod, norm_ffn1_g, w1_in, w1_out, 0, "ffn1_fwd")
    x2 = _mix_fwd(x1, mod, norm_mix_g, *mix_args, "mix_fwd")
    x3, gu3 = _ffn_fwd(x2, mod, norm_ffn2_g, w2_in, w2_out, 2, "ffn2_fwd")
    dx3, st_f = _loss_and_grad(x3, norm_final_g.reshape(1, D), target)

    dgu3, g_w2_out, dgate3 = _ffn_bwd_hidden(dx3, mod, gu3, w2_out, 2, "ffn2_bwd_hidden")
    dx2, h3, st3 = _ffn_bwd_input(dgu3, w2_in, x2, dx3, mod, norm_ffn2_g, 2, "ffn2_bwd_input")
    g_w2_in = _ffn_bwd_win(h3, dgu3, "ffn2_bwd_win")
    dx1, g_wmi, g_wmo, g_wp, g_ws, st2, vec2, dbias = _mix_bwd(x1, dx2, mod, norm_mix_g, *mix_args, "mix_bwd")
    dgu1, g_w1_out, dgate1 = _ffn_bwd_hidden(dx1, mod, gu1, w1_out, 0, "ffn1_bwd_hidden")
    dx0, h1, st1 = _ffn_bwd_input(dgu1, w1_in, xs, dx1, mod, norm_ffn1_g, 0, "ffn1_bwd_input")
    g_w1_in = _ffn_bwd_win(h1, dgu1, "ffn1_bwd_win")
    g_bs = _head_sums(dbias)[:, 0:8].T

    sendbufs = [_win_to_slots(g_w1_in).astype(BF), g_w1_out.reshape(NDEV, F // NDEV, D).astype(BF),
                _wmi_to_slots(g_wmi).astype(BF), g_wmo.reshape(NDEV, D // NDEV, D).astype(BF),
                _win_to_slots(g_w2_in).astype(BF), g_w2_out.reshape(NDEV, F // NDEV, D).astype(BF)]
    received = _exchange_slots(sendbufs, "scatter_gradients")
    tiles = dict(ffn1_w_in=256, ffn1_w_out=176, w_mix_in=512, w_mix_out=128, ffn2_w_in=256, ffn2_w_out=176)
    grads, deltas, new_m, new_v = {}, {}, {}, {}
    for k, recv in zip(big, received):
        g, dl, nm, nv = _sum_adamw(recv, weights[k][0], mom1[k][0], mom2[k][0], tiles[k], "update_" + k)
        grads[k], deltas[k], new_m[k], new_v[k] = g[None], dl[None], nm[None], nv[None]

    dgate1s = dgate1[0:1] + dgate1[8:9]
    dgate3s = dgate3[0:1] + dgate3[8:9]
    dmod = jnp.concatenate([st1[0:2], dgate1s, st2[0:2], st2[3:4], st3[0:2], dgate3s], axis=0)
    partial = dict(w_spatial=g_ws, w_pool=g_wp, norm_ffn1_g=st1[2], norm_mix_g=st2[2], norm_ffn2_g=st3[2],
                   norm_final_g=st_f[0], pool_scale=vec2[0], gmlp_ln_g=vec2[1], gmlp_ln_b=vec2[2], b_spatial=g_bs,
                   b_ada=dmod)
    dmod_rep = jnp.broadcast_to(dmod.reshape(NDEV, 1, wc), (NDEV, 8, wc))
    gsum, dpack, mpack, vpack, g_ada = _small_reduce_update(
        _pack(partial), _pack(weights), _pack(mom1), _pack(mom2), dmod_rep, cact_all)
    for name, pack in (("g", gsum), ("d", dpack), ("m", mpack), ("v", vpack)):
        dst = dict(g=grads, d=deltas, m=new_m, v=new_v)[name]
        dst.update(_unpack(pack, shapes))
    dl, nm, nv = _plain_adamw(g_ada, w_ada[0], m_w_ada[0], v_w_ada[0], 256, "update_w_ada")
    grads["w_ada"], deltas["w_ada"], new_m["w_ada"], new_v["w_ada"] = g_ada[None], dl[None], nm[None], nv[None]

    loss = lax.psum(st_f[1, 0], ("x", "y", "c"))
    return (loss, dx0[None], *[grads[k] for k in order], *[deltas[k] for k in order],
            *[new_m[k] for k in order], *[new_v[k] for k in order])
```

```python
import math

import jax
import jax.numpy as jnp
from jax import lax
from jax.experimental import pallas as pl
from jax.experimental.pallas import tpu as pltpu

D = 1024
F = 2816
DP = 512
DG = 512
DPROJ = DP + 2 * DG
CHUNK = 128
WINDOWS = (2, 4, 8, 16)
HALO = 16
NDEV = 8
T_FFN = 512
T_MIX = 256
EPS = 1e-6
LR, B1, B2, AEPS, WD, STEP = 0.001, 0.9, 0.999, 1e-08, 0.01, 10
BC1 = 1.0 - B1 ** STEP
BC2 = 1.0 - B2 ** STEP
GELU_C = math.sqrt(2.0 / math.pi)
GELU_A = 0.044715

BF = jnp.bfloat16
F32 = jnp.float32
MESH = pl.DeviceIdType.MESH
HBM = pl.BlockSpec(memory_space=pltpu.HBM)
VMEM = pl.BlockSpec(memory_space=pltpu.VMEM)

NT = (((1,), (1,)), ((), ()))
TN = (((0,), (0,)), ((), ()))


def _dot(a, b):
    return jnp.dot(a, b, preferred_element_type=F32)


def _dot_nt(a, b):
    return lax.dot_general(a, b, NT, preferred_element_type=F32)


def _dot_tn(a, b):
    return lax.dot_general(a, b, TN, preferred_element_type=F32)


def _cparams(vmem_mb, sem=None):
    kw = dict(vmem_limit_bytes=vmem_mb * 1024 * 1024)
    if sem is not None:
        kw["dimension_semantics"] = sem
    return pltpu.CompilerParams(**kw)


def _position():
    return lax.axis_index("x"), lax.axis_index("y"), lax.axis_index("c")


def _slot(p):
    return 4 * p[0] + 2 * p[1] + p[2]


def _flip(me, d):
    x, y, c = me
    return (1 - x if d & 4 else x, 1 - y if d & 2 else y, 1 - c if d & 1 else c)


def _remote(src, dst, send_sem, recv_sem, to):
    return pltpu.make_async_remote_copy(src_ref=src, dst_ref=dst, send_sem=send_sem, recv_sem=recv_sem,
                                        device_id=to, device_id_type=MESH)


def _comm_sems(n):
    return [pltpu.SemaphoreType.DMA((n, 7)), pltpu.SemaphoreType.DMA((n, 7)), pltpu.SemaphoreType.DMA((n,))]


def _gather_phase(phase, xs, outs, sems):
    send_sems, recv_sems, local_sems = sems
    n = len(xs)
    me = _position()
    x, y, c = me
    sibling = (x, y, 1 - c)
    chips = [(1 - x, y), (x, 1 - y), (1 - x, 1 - y)]

    def copy(a, k, block, to, src=None):
        dst = outs[a].at[_slot(block)]
        return _remote(dst if src is None else src, dst, send_sems.at[a, k], recv_sems.at[a, k], to)

    def mine(a):
        return pltpu.make_async_copy(xs[a], outs[a].at[_slot(me)], local_sems.at[a])

    def first(a):
        return [copy(a, 0, me, sibling, src=xs[a])] + [copy(a, 1 + j, me, (*chip, c), src=xs[a])
                                                       for j, chip in enumerate(chips)]

    def passed(a, j):
        return copy(a, 4 + j, (*chips[j], c), sibling)

    if phase == "start":
        for a in range(n):
            mine(a).start()
            for cp in first(a):
                cp.start()
    elif phase == "forward":
        for j, chip in enumerate(chips):
            for a in range(n):
                copy(a, 1 + j, (*chip, c), me).wait_recv()
                passed(a, j).start()
    else:
        for a in range(n):
            copy(a, 0, sibling, me).wait_recv()
        for j, chip in enumerate(chips):
            for a in range(n):
                copy(a, 4 + j, (*chip, 1 - c), me).wait_recv()
        for a in range(n):
            for cp in first(a) + [passed(a, j) for j in range(3)]:
                cp.wait_send()
            mine(a).wait()


def _exchange_phase(phase, xs, outs, sems):
    send_sems, recv_sems, local_sems = sems
    me = _position()
    for a in range(len(xs)):
        copies = [pltpu.make_async_copy(xs[a].at[_slot(me)], outs[a].at[_slot(me)], local_sems.at[a])]
        for d in range(1, NDEV):
            to = _flip(me, d)
            copies.append(_remote(xs[a].at[_slot(to)], outs[a].at[_slot(me)],
                                  send_sems.at[a, d - 1], recv_sems.at[a, d - 1], to))
        for cp in copies:
            if phase == "start":
                cp.start()
            else:
                cp.wait()


def _like(bufs):
    return [jax.ShapeDtypeStruct(b.shape, b.dtype) for b in bufs]


def _rms_mod(x, gn, shift, scale):
    ms = jnp.mean(x * x, axis=-1, keepdims=True)
    r = lax.rsqrt(ms + EPS)
    xhat = x * r
    n = xhat * gn
    h = n * (1.0 + scale) + shift
    return r, xhat, n, h


def _rms_mod_bwd(dh, dres, r, xhat, n, gn, scale):
    dshift = jnp.sum(dh, axis=0, keepdims=True)
    dscale = jnp.sum(dh * n, axis=0, keepdims=True)
    dn = dh * (1.0 + scale)
    dgn = jnp.sum(dn * xhat, axis=0, keepdims=True)
    dxhat = dn * gn
    dx = dres + r * (dxhat - xhat * jnp.mean(dxhat * xhat, axis=-1, keepdims=True))
    return dx, dshift, dscale, dgn


def _rows3(a, b, c, width):
    row = lax.broadcasted_iota(jnp.int32, (8, width), 0)
    z = jnp.zeros((8, width), F32)
    return jnp.where(row == 0, a, z) + jnp.where(row == 1, b, z) + jnp.where(row == 2, c, z)


def _gelu(x):
    t = jnp.tanh(GELU_C * (x + GELU_A * x * x * x))
    return 0.5 * x * (1.0 + t), t


def _gelu_grad(x, t):
    return 0.5 * (1.0 + t) + 0.5 * x * (1.0 - t * t) * GELU_C * (1.0 + 3.0 * GELU_A * x * x)


def _adamw(w, g, m, v):
    m = B1 * m + (1.0 - B1) * g
    v = B2 * v + (1.0 - B2) * (g * g)
    m_hat = m / BC1
    v_hat = v / BC2
    delta = -LR * (m_hat / (jnp.sqrt(v_hat) + AEPS) + WD * w)
    return delta, m, v


def _all_gather(shards, name):
    n = len(shards)

    def body(*refs):
        xs, outs, sems = refs[:n], refs[n:2 * n], refs[2 * n:]
        for phase in ("start", "forward", "finish"):
            _gather_phase(phase, xs, outs, sems)

    return pl.pallas_call(
        body, name=name,
        out_shape=[jax.ShapeDtypeStruct((NDEV,) + s.shape, s.dtype) for s in shards],
        in_specs=[HBM] * n, out_specs=[HBM] * n, scratch_shapes=_comm_sems(n),
    )(*shards)


def _ada_forward(c8, w_ada, b8):
    wc = w_ada.shape[1]

    def body(c8_ref, w_ref, b8_ref, mod_ref, cact_ref, call_ref, mall_ref, send_sems, recv_sems):
        me = _position()
        my = _slot(me)
        row = lax.broadcasted_iota(jnp.int32, (8, 1), 0)
        call_ref[my] = c8_ref[...]
        sends = []
        for d in range(1, NDEV):
            to = _flip(me, d)
            sends.append(_remote(c8_ref, call_ref.at[my], send_sems.at[0, d - 1], recv_sems.at[0, d - 1], to))
        for cp in sends:
            cp.start()
        for cp in sends:
            cp.wait()
        c_all = jnp.zeros((8, D), F32)
        for k in range(NDEV):
            c_all = c_all + jnp.where(row == k, call_ref[k], 0.0)
        cact = c_all * jax.nn.sigmoid(c_all)
        cact_ref[...] = cact
        part = _dot(cact.astype(BF), w_ref[...].astype(BF))
        mall_ref[my] = part
        sends = []
        for d in range(1, NDEV):
            to = _flip(me, d)
            sends.append(_remote(mall_ref.at[my], mall_ref.at[my], send_sems.at[1, d - 1], recv_sems.at[1, d - 1], to))
        for cp in sends:
            cp.start()
        for cp in sends:
            cp.wait()
        out = jnp.zeros((8, wc), F32)
        for k in range(NDEV):
            piece = jnp.sum(jnp.where(row == my, mall_ref[k], 0.0), axis=0, keepdims=True)
            out = out + jnp.where(row == k, piece, 0.0)
        mod_ref[...] = out + b8_ref[...]

    return pl.pallas_call(
        body, name="ada_forward",
        out_shape=[jax.ShapeDtypeStruct((8, wc), F32), jax.ShapeDtypeStruct((8, D), F32)],
        in_specs=[VMEM, VMEM, VMEM], out_specs=[VMEM, VMEM],
        scratch_shapes=[pltpu.VMEM((NDEV, 8, D), F32), pltpu.VMEM((NDEV, 8, wc), F32),
                        pltpu.SemaphoreType.DMA((2, 7)), pltpu.SemaphoreType.DMA((2, 7))],
        compiler_params=_cparams(32),
    )(c8, w_ada, b8)


MATS = ("w_spatial", "w_pool", "b_spatial")
VECS = ("norm_ffn1_g", "norm_mix_g", "norm_ffn2_g", "norm_final_g", "pool_scale", "gmlp_ln_g", "gmlp_ln_b", "b_ada")
VEC_WIDTH = dict(norm_ffn1_g=D, norm_mix_g=D, norm_ffn2_g=D, norm_final_g=D, pool_scale=DP, gmlp_ln_g=DG,
                 gmlp_ln_b=DG, b_ada=9 * D)
MAT_ROWS = 1600
MAT_SLICE = MAT_ROWS // NDEV
VEC_LANES = sum(VEC_WIDTH.values()) + 128
DMOD_AT = VEC_LANES - 128 - 9 * D
SMALL = MATS + VECS


def _small_reduce_update(g_ws, g_wp, dbias, st1, st2, st3, st_f, vec2, dgate1, dgate3, cact_all, params):
    wc = 9 * D // NDEV
    flat = [a for k in SMALL for a in params[k]]
    n_in = 11 + len(flat)

    def body(*refs):
        (g_ws_ref, g_wp_ref, dbias_ref, st1_ref, st2_ref, st3_ref, stf_ref, vec2_ref, dg1_ref, dg3_ref,
         cact_ref) = refs[:11]
        p_refs = refs[11:n_in]
        o_refs = refs[n_in:n_in + 4 * len(SMALL)]
        gada_ref, loss_ref = refs[n_in + 4 * len(SMALL):n_in + 4 * len(SMALL) + 2]
        pack_ref, rs_ref, ag_ref, tot_ref, rv_ref, dmp_ref, dw_ref, send_sems, recv_sems = refs[n_in + 4 * len(SMALL) + 2:]
        me = _position()
        my = _slot(me)

        pack_ref[0:1024, :] = g_ws_ref[...].reshape(1024, 128)
        pack_ref[1024:1536, :] = g_wp_ref[...].reshape(512, 128)
        ch = lax.broadcasted_iota(jnp.int32, (DG, 128), 0)
        hd = lax.broadcasted_iota(jnp.int32, (DG, 128), 1)
        sel = jnp.where(ch // 64 == hd, 1.0, 0.0).astype(F32)
        heads = jnp.dot(dbias_ref[...], sel, preferred_element_type=F32, precision=lax.Precision.HIGHEST)
        pack_ref[1536:1544, :] = heads.T[0:8, :]
        pack_ref[1544:MAT_ROWS, :] = jnp.zeros((MAT_ROWS - 1544, 128), F32)
        dgate1 = dg1_ref[0:1, :] + dg1_ref[8:9, :]
        dgate3 = dg3_ref[0:1, :] + dg3_ref[8:9, :]
        row = jnp.concatenate(
            [st1_ref[2:3, :], st2_ref[2:3, :], st3_ref[2:3, :], stf_ref[0:1, :],
             vec2_ref[0:1, :], vec2_ref[1:2, :], vec2_ref[2:3, :],
             st1_ref[0:1, :], st1_ref[1:2, :], dgate1, st2_ref[0:1, :], st2_ref[1:2, :], st2_ref[3:4, :],
             st3_ref[0:1, :], st3_ref[1:2, :], dgate3, stf_ref[1:2, 0:128]], axis=1)
        rv_ref[my] = row
        for k in range(NDEV):
            dmp_ref[k] = row[:, DMOD_AT + wc * k:DMOD_AT + wc * (k + 1)]
        dw_ref[my] = dmp_ref[my]
        rs_ref[my] = pack_ref[pl.ds(pl.multiple_of(my * MAT_SLICE, 8), MAT_SLICE), :]

        first = []
        for d in range(1, NDEV):
            to = _flip(me, d)
            theirs = pl.ds(pl.multiple_of(_slot(to) * MAT_SLICE, 8), MAT_SLICE)
            first.append(_remote(pack_ref.at[theirs, :], rs_ref.at[my], send_sems.at[0, d - 1], recv_sems.at[0, d - 1], to))
            first.append(_remote(dmp_ref.at[_slot(to)], dw_ref.at[my], send_sems.at[1, d - 1], recv_sems.at[1, d - 1], to))
            first.append(_remote(rv_ref.at[my], rv_ref.at[my], send_sems.at[2, d - 1], recv_sems.at[2, d - 1], to))
        for cp in first:
            cp.start()
        for cp in first:
            cp.wait()
        red = rs_ref[0]
        for k in range(1, NDEV):
            red = red + rs_ref[k]
        ag_ref[my] = red
        second = []
        for d in range(1, NDEV):
            to = _flip(me, d)
            second.append(_remote(ag_ref.at[my], ag_ref.at[my], send_sems.at[3, d - 1], recv_sems.at[3, d - 1], to))
        for cp in second:
            cp.start()

        rsum = rv_ref[0]
        for k in range(1, NDEV):
            rsum = rsum + rv_ref[k]
        loss_ref[...] = rsum[:, VEC_LANES - 128:VEC_LANES]
        r8 = lax.broadcasted_iota(jnp.int32, (8, 1), 0)
        dmine = jnp.zeros((8, wc), F32)
        for k in range(NDEV):
            dmine = dmine + jnp.where(r8 == k, dw_ref[k], 0.0)
        gada_ref[...] = _dot_tn(cact_ref[...].astype(BF), dmine.astype(BF))

        for cp in second:
            cp.wait()
        for k in range(NDEV):
            tot_ref[k * MAT_SLICE:(k + 1) * MAT_SLICE, :] = ag_ref[k]

        def update(idx, g):
            w_ref, m_ref, v_ref = p_refs[3 * idx:3 * idx + 3]
            g_out, d_out, m_out, v_out = o_refs[4 * idx:4 * idx + 4]
            g = g.reshape(w_ref.shape)
            g_out[...] = g
            d_out[...], m_out[...], v_out[...] = _adamw(w_ref[...], g, m_ref[...], v_ref[...])

        update(0, tot_ref[0:1024, :])
        update(1, tot_ref[1024:1536, :])
        update(2, tot_ref[1536:1544, :])
        at = 0
        for idx, k in enumerate(VECS):
            update(3 + idx, rsum[:, at:at + VEC_WIDTH[k]])
            at += VEC_WIDTH[k]

    outs = []
    for k in SMALL:
        outs += [jax.ShapeDtypeStruct(params[k][0].shape, F32)] * 4
    outs += [jax.ShapeDtypeStruct((D, wc), F32), jax.ShapeDtypeStruct((1, 128), F32)]
    res = pl.pallas_call(
        body, name="small_reduce_update", out_shape=outs,
        in_specs=[VMEM] * n_in, out_specs=[VMEM] * len(outs),
        scratch_shapes=[pltpu.VMEM((MAT_ROWS, 128), F32), pltpu.VMEM((NDEV, MAT_SLICE, 128), F32),
                        pltpu.VMEM((NDEV, MAT_SLICE, 128), F32), pltpu.VMEM((MAT_ROWS, 128), F32),
                        pltpu.VMEM((NDEV, 1, VEC_LANES), F32), pltpu.VMEM((NDEV, 1, wc), F32),
                        pltpu.VMEM((NDEV, 1, wc), F32),
                        pltpu.SemaphoreType.DMA((4, 7)), pltpu.SemaphoreType.DMA((4, 7))],
        compiler_params=_cparams(48),
    )(g_ws, g_wp, dbias, st1, st2, st3, st_f, vec2, dgate1, dgate3, cact_all, *flat)
    upd = {k: tuple(res[4 * i:4 * i + 4]) for i, k in enumerate(SMALL)}
    return upd, res[-2], res[-1]


def _sum_adamw(recv, w, m, v, tr, name):
    R, C = w.shape

    def body(r_ref, w_ref, m_ref, v_ref, g_ref, d_ref, nm_ref, nv_ref):
        g = r_ref[0].astype(F32)
        for k in range(1, NDEV):
            g = g + r_ref[k].astype(F32)
        g_ref[...] = g
        d_ref[...], nm_ref[...], nv_ref[...] = _adamw(w_ref[...], g, m_ref[...], v_ref[...])

    blk = pl.BlockSpec((tr, C), lambda i: (i, 0))
    out = jax.ShapeDtypeStruct((R, C), F32)
    return pl.pallas_call(
        body, name=name, grid=(R // tr,), out_shape=[out] * 4,
        in_specs=[pl.BlockSpec((NDEV, tr, C), lambda i: (0, i, 0)), blk, blk, blk], out_specs=[blk] * 4,
        compiler_params=_cparams(48, ("arbitrary",)),
    )(recv, w, m, v)


def _plain_adamw(g, w, m, v, tr, name):
    R, C = w.shape

    def body(g_ref, w_ref, m_ref, v_ref, d_ref, nm_ref, nv_ref):
        d_ref[...], nm_ref[...], nv_ref[...] = _adamw(w_ref[...], g_ref[...], m_ref[...], v_ref[...])

    blk = pl.BlockSpec((tr, C), lambda i: (i, 0))
    out = jax.ShapeDtypeStruct((R, C), F32)
    return pl.pallas_call(
        body, name=name, grid=(R // tr,), out_shape=[out] * 3,
        in_specs=[blk] * 4, out_specs=[blk] * 3,
        compiler_params=_cparams(48, ("arbitrary",)),
    )(g, w, m, v)


FC = F // 2


def _ffn_fwd(x, mod, gn, w_in, w_out, sub, name, gather=()):
    S = x.shape[0]
    T = min(T_FFN, S)
    nS, nJ = S // T, F // FC
    ng = len(gather)
    forward_step = (3 * nS) // 4

    def body(*refs):
        x_ref, mod_ref, gn_ref, wg_ref, wu_ref, wo_ref = refs[:6]
        shards = refs[6:6 + ng]
        xo_ref, gu_ref, h_ref = refs[6 + ng:9 + ng]
        gathered = refs[9 + ng:9 + 2 * ng]
        acc_scr = refs[9 + 2 * ng]
        sems = refs[10 + 2 * ng:]
        i, j = pl.program_id(0), pl.program_id(1)

        if ng:
            @pl.when((i == 0) & (j == 0))
            def _():
                _gather_phase("start", shards, gathered, sems)

            @pl.when((i == forward_step) & (j == 0))
            def _():
                _gather_phase("forward", shards, gathered, sems)

        @pl.when(j == 0)
        def _():
            _, _, _, h = _rms_mod(x_ref[...], gn_ref[...], mod_ref[3 * sub:3 * sub + 1, :],
                                  mod_ref[3 * sub + 1:3 * sub + 2, :])
            h_ref[...] = h.astype(BF)
            acc_scr[...] = jnp.zeros_like(acc_scr)

        h = h_ref[...]
        g = _dot(h, wg_ref[0])
        u = _dot(h, wu_ref[0])
        gu_ref[0] = g.astype(BF)
        gu_ref[1] = u.astype(BF)
        a = (g * jax.nn.sigmoid(g) * u).astype(BF)
        acc_scr[...] += _dot(a, wo_ref[...])

        @pl.when(j == nJ - 1)
        def _():
            xo_ref[...] = x_ref[...] + (0.5 * mod_ref[3 * sub + 2:3 * sub + 3, :]) * acc_scr[...]

        if ng:
            @pl.when((i == nS - 1) & (j == nJ - 1))
            def _():
                _gather_phase("finish", shards, gathered, sems)

    return pl.pallas_call(
        body, name=name, grid=(nS, nJ),
        out_shape=[jax.ShapeDtypeStruct((S, D), F32), jax.ShapeDtypeStruct((2, S, F), BF),
                   jax.ShapeDtypeStruct((S, D), BF)]
                  + [jax.ShapeDtypeStruct((NDEV,) + s.shape, s.dtype) for s in gather],
        in_specs=[pl.BlockSpec((T, D), lambda i, j: (i, 0)),
                  pl.BlockSpec((9, D), lambda i, j: (0, 0)),
                  pl.BlockSpec((1, D), lambda i, j: (0, 0)),
                  pl.BlockSpec((1, D, FC), lambda i, j: (0, 0, j)),
                  pl.BlockSpec((1, D, FC), lambda i, j: (1, 0, j)),
                  pl.BlockSpec((FC, D), lambda i, j: (j, 0))] + [HBM] * ng,
        out_specs=[pl.BlockSpec((T, D), lambda i, j: (i, 0)),
                   pl.BlockSpec((2, T, FC), lambda i, j: (0, i, j)),
                   pl.BlockSpec((T, D), lambda i, j: (i, 0))] + [HBM] * ng,
        scratch_shapes=[pltpu.VMEM((T, D), F32)] + (_comm_sems(ng) if ng else []),
        compiler_params=_cparams(56, ("arbitrary", "arbitrary")),
    )(x, mod, gn, w_in, w_in, w_out, *gather)


def _ffn_bwd_hidden(dx, mod, gu, w_out, sub, name, exchange=()):
    S = dx.shape[0]
    T = min(T_FFN, S)
    nS, nJ = S // T, F // FC
    ne = len(exchange)

    def body(*refs):
        dx_ref, mod_ref, gu_ref, wo_ref = refs[:4]
        sendbufs = refs[4:4 + ne]
        dgu_ref, gw_ref, dgate_ref = refs[4 + ne:7 + ne]
        recvbufs = refs[7 + ne:7 + 2 * ne]
        sems = refs[7 + 2 * ne:]
        j, i = pl.program_id(0), pl.program_id(1)

        if ne:
            @pl.when((i == 0) & (j == 0))
            def _():
                _exchange_phase("start", sendbufs, recvbufs, sems)

        gate = mod_ref[3 * sub + 2:3 * sub + 3, :]
        dx = dx_ref[...]
        da = _dot_nt((dx * (0.5 * gate)).astype(BF), wo_ref[...])
        g = gu_ref[0].astype(F32)
        u = gu_ref[1].astype(F32)
        sg = jax.nn.sigmoid(g)
        s = g * sg
        dgu_ref[0] = (da * u * (sg * (1.0 + g * (1.0 - sg)))).astype(BF)
        dgu_ref[1] = (da * s).astype(BF)
        contrib = _dot_tn((s * u).astype(BF), dx.astype(BF))

        @pl.when(i == 0)
        def _():
            gw_ref[...] = contrib

        @pl.when(i > 0)
        def _():
            gw_ref[...] += contrib

        @pl.when(i == nS - 1)
        def _():
            acc = gw_ref[...]
            dgate = 0.5 * jnp.sum(acc * wo_ref[...].astype(F32), axis=0, keepdims=True)
            dgate_ref[...] = jnp.broadcast_to(dgate, (8, D))
            gw_ref[...] = acc * (0.5 * gate)

        if ne:
            @pl.when((i == nS - 1) & (j == nJ - 1))
            def _():
                _exchange_phase("wait", sendbufs, recvbufs, sems)

    return pl.pallas_call(
        body, name=name, grid=(nJ, nS),
        out_shape=[jax.ShapeDtypeStruct((2, S, F), BF), jax.ShapeDtypeStruct((F, D), F32),
                   jax.ShapeDtypeStruct((8 * nJ, D), F32)] + _like(exchange),
        in_specs=[pl.BlockSpec((T, D), lambda j, i: (i, 0)),
                  pl.BlockSpec((9, D), lambda j, i: (0, 0)),
                  pl.BlockSpec((2, T, FC), lambda j, i: (0, i, j)),
                  pl.BlockSpec((FC, D), lambda j, i: (j, 0))] + [HBM] * ne,
        out_specs=[pl.BlockSpec((2, T, FC), lambda j, i: (0, i, j)),
                   pl.BlockSpec((FC, D), lambda j, i: (j, 0)),
                   pl.BlockSpec((8, D), lambda j, i: (j, 0))] + [HBM] * ne,
        scratch_shapes=_comm_sems(ne) if ne else [],
        compiler_params=_cparams(56, ("arbitrary", "arbitrary")),
    )(dx, mod, gu, w_out, *exchange)


def _ffn_bwd_input(dgu, w_in, x, dx, mod, gn, sub, name, exchange=()):
    S = x.shape[0]
    T = min(T_FFN, S)
    nS, nJ = S // T, F // FC
    ne = len(exchange)

    def body(*refs):
        dgu_ref, wg_ref, wu_ref, x_ref, dx_ref, mod_ref, gn_ref = refs[:7]
        sendbufs = refs[7:7 + ne]
        dxin_ref, st_ref = refs[7 + ne:9 + ne]
        recvbufs = refs[9 + ne:9 + 2 * ne]
        acc_scr = refs[9 + 2 * ne]
        sems = refs[10 + 2 * ne:]
        i, j = pl.program_id(0), pl.program_id(1)

        if ne:
            @pl.when((i == 0) & (j == 0))
            def _():
                _exchange_phase("start", sendbufs, recvbufs, sems)

        @pl.when(j == 0)
        def _():
            acc_scr[...] = jnp.zeros_like(acc_scr)

        acc_scr[...] += _dot_nt(dgu_ref[0], wg_ref[0]) + _dot_nt(dgu_ref[1], wu_ref[0])

        @pl.when(j == nJ - 1)
        def _():
            gn = gn_ref[...]
            scale = mod_ref[3 * sub + 1:3 * sub + 2, :]
            r, xhat, n, _ = _rms_mod(x_ref[...], gn, mod_ref[3 * sub:3 * sub + 1, :], scale)
            dxin, dshift, dscale, dgn = _rms_mod_bwd(acc_scr[...], dx_ref[...], r, xhat, n, gn, scale)
            dxin_ref[...] = dxin
            upd = _rows3(dshift, dscale, dgn, D)

            @pl.when(i == 0)
            def _():
                st_ref[...] = upd

            @pl.when(i > 0)
            def _():
                st_ref[...] += upd

        if ne:
            @pl.when((i == nS - 1) & (j == nJ - 1))
            def _():
                _exchange_phase("wait", sendbufs, recvbufs, sems)

    return pl.pallas_call(
        body, name=name, grid=(nS, nJ),
        out_shape=[jax.ShapeDtypeStruct((S, D), F32), jax.ShapeDtypeStruct((8, D), F32)] + _like(exchange),
        in_specs=[pl.BlockSpec((2, T, FC), lambda i, j: (0, i, j)),
                  pl.BlockSpec((1, D, FC), lambda i, j: (0, 0, j)),
                  pl.BlockSpec((1, D, FC), lambda i, j: (1, 0, j)),
                  pl.BlockSpec((T, D), lambda i, j: (i, 0)),
                  pl.BlockSpec((T, D), lambda i, j: (i, 0)),
                  pl.BlockSpec((9, D), lambda i, j: (0, 0)),
                  pl.BlockSpec((1, D), lambda i, j: (0, 0))] + [HBM] * ne,
        out_specs=[pl.BlockSpec((T, D), lambda i, j: (i, 0)),
                   pl.BlockSpec((8, D), lambda i, j: (0, 0))] + [HBM] * ne,
        scratch_shapes=[pltpu.VMEM((T, D), F32)] + (_comm_sems(ne) if ne else []),
        compiler_params=_cparams(56, ("arbitrary", "arbitrary")),
    )(dgu, w_in, w_in, x, dx, mod, gn, *exchange)


def _ffn_bwd_win(h, dgu, name, exchange=()):
    S = h.shape[0]
    T = min(T_FFN, S)
    nS, nJ = S // T, F // FC
    ne = len(exchange)

    def body(*refs):
        h_ref, dgu_ref = refs[:2]
        sendbufs = refs[2:2 + ne]
        out_ref = refs[2 + ne]
        recvbufs = refs[3 + ne:3 + 2 * ne]
        sems = refs[3 + 2 * ne:]
        p, j, i = pl.program_id(0), pl.program_id(1), pl.program_id(2)

        if ne:
            @pl.when((p == 0) & (j == 0) & (i == 0))
            def _():
                _exchange_phase("start", sendbufs, recvbufs, sems)

        contrib = _dot_tn(h_ref[...], dgu_ref[0])

        @pl.when(i == 0)
        def _():
            out_ref[0] = contrib

        @pl.when(i > 0)
        def _():
            out_ref[0] += contrib

        if ne:
            @pl.when((p == 1) & (j == nJ - 1) & (i == nS - 1))
            def _():
                _exchange_phase("wait", sendbufs, recvbufs, sems)

    res = pl.pallas_call(
        body, name=name, grid=(2, nJ, nS),
        out_shape=[jax.ShapeDtypeStruct((2, D, F), F32)] + _like(exchange),
        in_specs=[pl.BlockSpec((T, D), lambda p, j, i: (i, 0)),
                  pl.BlockSpec((1, T, FC), lambda p, j, i: (p, i, j))] + [HBM] * ne,
        out_specs=[pl.BlockSpec((1, D, FC), lambda p, j, i: (p, 0, j))] + [HBM] * ne,
        scratch_shapes=_comm_sems(ne) if ne else [],
        compiler_params=_cparams(56, ("arbitrary", "arbitrary", "arbitrary")),
    )(h, dgu, *exchange)
    return res


def _pool_counts(pos0, T):
    pos = pos0 + lax.broadcasted_iota(jnp.int32, (T, 1), 0)
    return [jnp.minimum(pos + 1, w).astype(F32) for w in WINDOWS]


def _pool_fwd(xa, halo, ext_scr, cnts, T):
    ext_scr[0:HALO, :] = halo
    ext_scr[HALO:HALO + T, :] = xa
    out = []
    for gi, w in enumerate(WINDOWS):
        cols = slice(128 * gi, 128 * gi + 128)
        acc = xa[:, cols]
        for k in range(1, w):
            acc = acc + ext_scr[HALO - k:HALO - k + T, cols]
        out.append(acc / cnts[gi] - xa[:, cols])
    return out


def _sgu_fwd(vnb, ws_ref, sv_scr, T):
    lane = lax.broadcasted_iota(jnp.int32, (CHUNK, 128), 1)
    for n in range(T // CHUNK):
        rows = slice(n * CHUNK, (n + 1) * CHUNK)
        for b in range(DG // 128):
            cols = slice(128 * b, 128 * b + 128)
            vb = vnb[rows, cols]
            sv_scr[rows, cols] = jnp.where(lane < 64, _dot(ws_ref[2 * b], vb), _dot(ws_ref[2 * b + 1], vb))


def _mix_fwd(x, mod, gn, wmi, wmo, wp, ps, lg, lb, ws, bias, name):
    S = x.shape[0]
    T = min(T_MIX, S)

    def body(x_ref, mod_ref, gn_ref, wmi_ref, wmo_ref, wp_ref, ps_ref, lg_ref, lb_ref, ws_ref, bias_ref,
             xo_ref, carry_scr, ext_scr, sv_scr, ycat_scr):
        i = pl.program_id(0)

        @pl.when(i == 0)
        def _():
            carry_scr[...] = jnp.zeros_like(carry_scr)

        x = x_ref[...]
        _, _, _, h = _rms_mod(x, gn_ref[...], mod_ref[3:4, :], mod_ref[4:5, :])
        proj = _dot(h.astype(BF), wmi_ref[...])
        xa = proj[:, 0:DP]
        p = _pool_fwd(xa, carry_scr[...], ext_scr, _pool_counts(i * T, T), T)
        carry_scr[...] = xa[T - HALO:T, :]
        for gi in range(4):
            cols = slice(128 * gi, 128 * gi + 128)
            ycat_scr[:, cols] = (_dot(p[gi].astype(BF), wp_ref[gi]) * ps_ref[:, cols]).astype(BF)
        u, _ = _gelu(proj[:, DP:DP + DG])
        v, _ = _gelu(proj[:, DP + DG:DPROJ])
        mu = jnp.mean(v, axis=-1, keepdims=True)
        vc = v - mu
        rstd = lax.rsqrt(jnp.mean(vc * vc, axis=-1, keepdims=True) + EPS)
        vn = vc * rstd * lg_ref[...] + lb_ref[...]
        _sgu_fwd(vn.astype(BF), ws_ref, sv_scr, T)
        for n in range(T // CHUNK):
            rows = slice(n * CHUNK, (n + 1) * CHUNK)
            ycat_scr[rows, DP:D] = (u[rows, :] * (sv_scr[rows, :] + bias_ref[...])).astype(BF)
        xo_ref[...] = x + mod_ref[5:6, :] * _dot(ycat_scr[...], wmo_ref[...])

    full = lambda shape: pl.BlockSpec(shape, lambda i: (0,) * len(shape))
    return pl.pallas_call(
        body, name=name, grid=(S // T,),
        out_shape=jax.ShapeDtypeStruct((S, D), F32),
        in_specs=[pl.BlockSpec((T, D), lambda i: (i, 0)), full((9, D)), full((1, D)), full((D, DPROJ)), full((D, D)),
                  full((4, 128, 128)), full((1, DP)), full((1, DG)), full((1, DG)), full((8, CHUNK, CHUNK)),
                  full((CHUNK, DG))],
        out_specs=pl.BlockSpec((T, D), lambda i: (i, 0)),
        scratch_shapes=[pltpu.VMEM((HALO, DP), F32), pltpu.VMEM((T + HALO, DP), F32), pltpu.VMEM((T, DG), F32),
                        pltpu.VMEM((T, D), BF)],
        compiler_params=_cparams(48, ("arbitrary",)),
    )(x, mod, gn, wmi, wmo, wp, ps, lg, lb, ws, bias)


def _mix_bwd(x, dxo, mod, gn, wmi, wmo, wp, ps, lg, lb, ws, bias, name, exchange=()):
    S = x.shape[0]
    T = min(T_MIX, S)
    nS = S // T
    hb = T // HALO
    ne = len(exchange)

    def body(*refs):
        (x_ref, xh_ref, dxo_ref, mod_ref, gn_ref, wmi_ref, wmo_ref, wp_ref, ps_ref, lg_ref, lb_ref, ws_ref,
         bias_ref) = refs[:13]
        sendbufs = refs[13:13 + ne]
        dxi_ref, gwmi_ref, gwmo_ref, gwp_ref, gws_ref, st_ref, vec_ref, dbias_ref = refs[13 + ne:21 + ne]
        recvbufs = refs[21 + ne:21 + 2 * ne]
        carry_scr, ext_scr, qext_scr, sv_scr, dvn_scr, ycat_scr, dproj_scr = refs[21 + 2 * ne:28 + 2 * ne]
        sems = refs[28 + 2 * ne:]
        i = pl.program_id(0)
        t = nS - 1 - i
        gn = gn_ref[...]
        shift, scale, gate = mod_ref[3:4, :], mod_ref[4:5, :], mod_ref[5:6, :]

        @pl.when(i == 0)
        def _():
            if ne:
                _exchange_phase("start", sendbufs, recvbufs, sems)
            carry_scr[...] = jnp.zeros_like(carry_scr)
            gwmi_ref[...] = jnp.zeros_like(gwmi_ref)
            gwmo_ref[...] = jnp.zeros_like(gwmo_ref)
            gwp_ref[...] = jnp.zeros_like(gwp_ref)
            gws_ref[...] = jnp.zeros_like(gws_ref)
            st_ref[...] = jnp.zeros_like(st_ref)
            vec_ref[...] = jnp.zeros_like(vec_ref)
            dbias_ref[...] = jnp.zeros_like(dbias_ref)

        x = x_ref[...]
        dxo = dxo_ref[...]
        r, xhat, n, h = _rms_mod(x, gn, shift, scale)
        hbf = h.astype(BF)
        proj = _dot(hbf, wmi_ref[...])
        xa = proj[:, 0:DP]
        zu = proj[:, DP:DP + DG]
        zv = proj[:, DP + DG:DPROJ]
        _, _, _, hh = _rms_mod(xh_ref[...], gn, shift, scale)
        halo = _dot(hh.astype(BF), wmi_ref[:, 0:DP])
        halo = jnp.where(t == 0, 0.0, halo)
        cnts = _pool_counts(t * T, T)
        p = _pool_fwd(xa, halo, ext_scr, cnts, T)
        m = []
        for gi in range(4):
            cols = slice(128 * gi, 128 * gi + 128)
            m.append(_dot(p[gi].astype(BF), wp_ref[gi]))
            ycat_scr[:, cols] = (m[gi] * ps_ref[:, cols]).astype(BF)
        u, tu = _gelu(zu)
        v, tv = _gelu(zv)
        mu = jnp.mean(v, axis=-1, keepdims=True)
        vc = v - mu
        rstd = lax.rsqrt(jnp.mean(vc * vc, axis=-1, keepdims=True) + EPS)
        vhat = vc * rstd
        lg = lg_ref[...]
        vnb = (vhat * lg + lb_ref[...]).astype(BF)
        _sgu_fwd(vnb, ws_ref, sv_scr, T)
        for nck in range(T // CHUNK):
            rows = slice(nck * CHUNK, (nck + 1) * CHUNK)
            sv_scr[rows, :] = sv_scr[rows, :] + bias_ref[...]
        sv = sv_scr[...]
        ycat_scr[:, DP:D] = (u * sv).astype(BF)

        gwmo_ref[...] += _dot_tn(ycat_scr[...], dxo.astype(BF))
        dyc = _dot_nt((dxo * gate).astype(BF), wmo_ref[...])
        dya = dyc[:, 0:DP]
        dyb = dyc[:, DP:D]

        dps = []
        dp = []
        for gi in range(4):
            cols = slice(128 * gi, 128 * gi + 128)
            dps.append(jnp.sum(dya[:, cols] * m[gi], axis=0, keepdims=True))
            dm = (dya[:, cols] * ps_ref[:, cols]).astype(BF)
            gwp_ref[gi] += _dot_tn(p[gi].astype(BF), dm)
            dp.append(_dot_nt(dm, wp_ref[gi]))
            qext_scr[0:T, cols] = dp[gi] / cnts[gi]
        qext_scr[T:T + HALO, :] = carry_scr[...]
        for gi, w in enumerate(WINDOWS):
            cols = slice(128 * gi, 128 * gi + 128)
            acc = qext_scr[0:T, cols]
            for k in range(1, w):
                acc = acc + qext_scr[k:k + T, cols]
            dproj_scr[:, cols] = (acc - dp[gi]).astype(BF)
        carry_scr[...] = qext_scr[0:HALO, :]

        du = dyb * sv
        dsv = dyb * u
        lane = lax.broadcasted_iota(jnp.int32, (CHUNK, 128), 1)
        dbias = jnp.zeros((CHUNK, DG), F32)
        for nck in range(T // CHUNK):
            rows = slice(nck * CHUNK, (nck + 1) * CHUNK)
            dbias = dbias + dsv[rows, :]
            for b in range(DG // 128):
                cols = slice(128 * b, 128 * b + 128)
                dsvb = dsv[rows, cols]
                vb = vnb[rows, cols]
                gws_ref[2 * b] += _dot_nt(jnp.where(lane < 64, dsvb, 0.0).astype(BF), vb)
                gws_ref[2 * b + 1] += _dot_nt(jnp.where(lane < 64, 0.0, dsvb).astype(BF), vb)
                dsvbb = dsvb.astype(BF)
                dvn_scr[rows, cols] = jnp.where(lane < 64, _dot_tn(ws_ref[2 * b], dsvbb),
                                                _dot_tn(ws_ref[2 * b + 1], dsvbb))
        dbias_ref[...] += dbias
        dvn = dvn_scr[...]
        dlg = jnp.sum(dvn * vhat, axis=0, keepdims=True)
        dlb = jnp.sum(dvn, axis=0, keepdims=True)
        dvhat = dvn * lg
        dv = rstd * (dvhat - jnp.mean(dvhat, axis=-1, keepdims=True)
                     - vhat * jnp.mean(dvhat * vhat, axis=-1, keepdims=True))
        dproj_scr[:, DP:DP + DG] = (du * _gelu_grad(zu, tu)).astype(BF)
        dproj_scr[:, DP + DG:DPROJ] = (dv * _gelu_grad(zv, tv)).astype(BF)
        vec_ref[...] += _rows3(jnp.concatenate(dps, axis=1), dlg, dlb, DP)

        dproj = dproj_scr[...]
        gwmi_ref[...] += _dot_tn(hbf, dproj)
        dh = _dot_nt(dproj, wmi_ref[...])
        dxi, dshift, dscale, dgn = _rms_mod_bwd(dh, dxo, r, xhat, n, gn, scale)
        dxi_ref[...] = dxi
        st_ref[...] += _rows3(dshift, dscale, dgn, D)

        @pl.when(i == nS - 1)
        def _():
            acc = gwmo_ref[...]
            dgate = jnp.sum(acc * wmo_ref[...].astype(F32), axis=0, keepdims=True)
            row = lax.broadcasted_iota(jnp.int32, (8, D), 0)
            st_ref[...] += jnp.where(row == 3, dgate, 0.0)
            gwmo_ref[...] = acc * gate
            tt = lax.broadcasted_iota(jnp.int32, (CHUNK, CHUNK), 0)
            ss = lax.broadcasted_iota(jnp.int32, (CHUNK, CHUNK), 1)
            for hd in range(8):
                gws_ref[hd] = jnp.where(tt >= ss, gws_ref[hd], 0.0)
            if ne:
                _exchange_phase("wait", sendbufs, recvbufs, sems)

    full = lambda shape: pl.BlockSpec(shape, lambda i: (0,) * len(shape))
    return pl.pallas_call(
        body, name=name, grid=(nS,),
        out_shape=[jax.ShapeDtypeStruct((S, D), F32), jax.ShapeDtypeStruct((D, DPROJ), F32),
                   jax.ShapeDtypeStruct((D, D), F32), jax.ShapeDtypeStruct((4, 128, 128), F32),
                   jax.ShapeDtypeStruct((8, CHUNK, CHUNK), F32), jax.ShapeDtypeStruct((8, D), F32),
                   jax.ShapeDtypeStruct((8, DP), F32), jax.ShapeDtypeStruct((CHUNK, DG), F32)] + _like(exchange),
        in_specs=[pl.BlockSpec((T, D), lambda i: (nS - 1 - i, 0)),
                  pl.BlockSpec((HALO, D), lambda i: (jnp.maximum((nS - 1 - i) * hb - 1, 0), 0)),
                  pl.BlockSpec((T, D), lambda i: (nS - 1 - i, 0)),
                  full((9, D)), full((1, D)), full((D, DPROJ)), full((D, D)),
                  full((4, 128, 128)), full((1, DP)), full((1, DG)), full((1, DG)), full((8, CHUNK, CHUNK)),
                  full((CHUNK, DG))] + [HBM] * ne,
        out_specs=[pl.BlockSpec((T, D), lambda i: (nS - 1 - i, 0)), full((D, DPROJ)), full((D, D)),
                   full((4, 128, 128)), full((8, CHUNK, CHUNK)), full((8, D)), full((8, DP)), full((CHUNK, DG))]
                  + [HBM] * ne,
        scratch_shapes=[pltpu.VMEM((HALO, DP), F32), pltpu.VMEM((T + HALO, DP), F32),
                        pltpu.VMEM((T + HALO, DP), F32), pltpu.VMEM((T, DG), F32), pltpu.VMEM((T, DG), F32),
                        pltpu.VMEM((T, D), BF), pltpu.VMEM((T, DPROJ), BF)] + (_comm_sems(ne) if ne else []),
        compiler_params=_cparams(56, ("arbitrary",)),
    )(x, x, dxo, mod, gn, wmi, wmo, wp, ps, lg, lb, ws, bias, *exchange)


def _loss_and_grad(x, gf, target):
    S = x.shape[0]
    T = min(T_FFN, S)

    def body(x_ref, gf_ref, t_ref, dx_ref, st_ref):
        i = pl.program_id(0)
        x = x_ref[...]
        gf = gf_ref[...]
        r = lax.rsqrt(jnp.mean(x * x, axis=-1, keepdims=True) + EPS)
        xhat = x * r
        e = xhat * gf - t_ref[...]
        loss = 0.5 * jnp.sum(jnp.sum(e * e, axis=-1, keepdims=True), axis=0, keepdims=True) / D
        dy = e / D
        dgf = jnp.sum(dy * xhat, axis=0, keepdims=True)
        dxhat = dy * gf
        dx_ref[...] = r * (dxhat - xhat * jnp.mean(dxhat * xhat, axis=-1, keepdims=True))
        upd = _rows3(dgf, jnp.broadcast_to(loss, (1, D)), jnp.zeros((1, D), F32), D)

        @pl.when(i == 0)
        def _():
            st_ref[...] = upd

        @pl.when(i > 0)
        def _():
            st_ref[...] += upd

    return pl.pallas_call(
        body, name="loss_and_grad", grid=(S // T,),
        out_shape=[jax.ShapeDtypeStruct((S, D), F32), jax.ShapeDtypeStruct((8, D), F32)],
        in_specs=[pl.BlockSpec((T, D), lambda i: (i, 0)), pl.BlockSpec((1, D), lambda i: (0, 0)),
                  pl.BlockSpec((T, D), lambda i: (i, 0))],
        out_specs=[pl.BlockSpec((T, D), lambda i: (i, 0)), pl.BlockSpec((8, D), lambda i: (0, 0))],
        compiler_params=_cparams(48, ("arbitrary",)),
    )(x, gf, target)


def _win_from_slots(g):
    return g.reshape(2, 4, D, F // 4).transpose(0, 2, 1, 3).reshape(2, D, F)


def _win_to_slots(w):
    return w.reshape(2, D, 4, F // 4).transpose(0, 2, 1, 3).reshape(NDEV, D, F // 4)


def _wmi_from_slots(g):
    return g.transpose(1, 0, 2).reshape(D, DPROJ)


def _wmi_to_slots(w):
    return w.reshape(D, NDEV, DPROJ // NDEV).transpose(1, 0, 2)


def kernel(x, c, w_ada, b_ada, norm_ffn1_g, ffn1_w_in, ffn1_w_out, norm_mix_g, w_mix_in, w_pool, pool_scale, gmlp_ln_g, gmlp_ln_b, w_spatial, b_spatial, w_mix_out, norm_ffn2_g, ffn2_w_in, ffn2_w_out, norm_final_g, loss_target, m_w_ada, m_b_ada, m_norm_ffn1_g, m_ffn1_w_in, m_ffn1_w_out, m_norm_mix_g, m_w_mix_in, m_w_pool, m_pool_scale, m_gmlp_ln_g, m_gmlp_ln_b, m_w_spatial, m_b_spatial, m_w_mix_out, m_norm_ffn2_g, m_ffn2_w_in, m_ffn2_w_out, m_norm_final_g, v_w_ada, v_b_ada, v_norm_ffn1_g, v_ffn1_w_in, v_ffn1_w_out, v_norm_mix_g, v_w_mix_in, v_w_pool, v_pool_scale, v_gmlp_ln_g, v_gmlp_ln_b, v_w_spatial, v_b_spatial, v_w_mix_out, v_norm_ffn2_g, v_ffn2_w_in, v_ffn2_w_out, v_norm_final_g):
    weights = dict(w_ada=w_ada, b_ada=b_ada, norm_ffn1_g=norm_ffn1_g, ffn1_w_in=ffn1_w_in, ffn1_w_out=ffn1_w_out,
                   norm_mix_g=norm_mix_g, w_mix_in=w_mix_in, w_pool=w_pool, pool_scale=pool_scale,
                   gmlp_ln_g=gmlp_ln_g, gmlp_ln_b=gmlp_ln_b, w_spatial=w_spatial, b_spatial=b_spatial,
                   w_mix_out=w_mix_out, norm_ffn2_g=norm_ffn2_g, ffn2_w_in=ffn2_w_in, ffn2_w_out=ffn2_w_out,
                   norm_final_g=norm_final_g)
    mom1 = dict(w_ada=m_w_ada, b_ada=m_b_ada, norm_ffn1_g=m_norm_ffn1_g, ffn1_w_in=m_ffn1_w_in,
                ffn1_w_out=m_ffn1_w_out, norm_mix_g=m_norm_mix_g, w_mix_in=m_w_mix_in, w_pool=m_w_pool,
                pool_scale=m_pool_scale, gmlp_ln_g=m_gmlp_ln_g, gmlp_ln_b=m_gmlp_ln_b, w_spatial=m_w_spatial,
                b_spatial=m_b_spatial, w_mix_out=m_w_mix_out, norm_ffn2_g=m_norm_ffn2_g, ffn2_w_in=m_ffn2_w_in,
                ffn2_w_out=m_ffn2_w_out, norm_final_g=m_norm_final_g)
    mom2 = dict(w_ada=v_w_ada, b_ada=v_b_ada, norm_ffn1_g=v_norm_ffn1_g, ffn1_w_in=v_ffn1_w_in,
                ffn1_w_out=v_ffn1_w_out, norm_mix_g=v_norm_mix_g, w_mix_in=v_w_mix_in, w_pool=v_w_pool,
                pool_scale=v_pool_scale, gmlp_ln_g=v_gmlp_ln_g, gmlp_ln_b=v_gmlp_ln_b, w_spatial=v_w_spatial,
                b_spatial=v_b_spatial, w_mix_out=v_w_mix_out, norm_ffn2_g=v_norm_ffn2_g, ffn2_w_in=v_ffn2_w_in,
                ffn2_w_out=v_ffn2_w_out, norm_final_g=v_norm_final_g)
    order = list(weights)
    xs = x[0]
    target = loss_target[0]
    shard = lambda k: weights[k][0].astype(BF)
    wc = w_ada.shape[2]

    modp, cact_all = _ada_forward(jnp.broadcast_to(c, (8, D)), w_ada[0], b_ada.reshape(NDEV, wc))
    mod = modp.reshape(9, D)
    g_w1_in, g_w1_out = _all_gather([shard("ffn1_w_in"), shard("ffn1_w_out")], "gather_ffn1")
    w1_in = _win_from_slots(g_w1_in)
    w1_out = g_w1_out.reshape(F, D)

    x1, gu1, h1, g_wmi, g_wmo, g_w2_out, g_w2_in = _ffn_fwd(
        xs, mod, norm_ffn1_g, w1_in, w1_out, 0, "ffn1_fwd",
        gather=[shard("w_mix_in"), shard("w_mix_out"), shard("ffn2_w_out"), shard("ffn2_w_in")])
    wmi = _wmi_from_slots(g_wmi)
    wmo = g_wmo.reshape(D, D)
    w2_in = _win_from_slots(g_w2_in)
    w2_out = g_w2_out.reshape(F, D)
    tril = jnp.tril(jnp.ones((CHUNK, CHUNK), dtype=bool))
    ws_b = jnp.where(tril[None], w_spatial[0], 0.0).astype(BF)
    wp_b = w_pool[0].astype(BF)
    bias = jnp.repeat(b_spatial[0].T, DG // 8, axis=1)
    mix_args = (wmi, wmo, wp_b, pool_scale, gmlp_ln_g, gmlp_ln_b, ws_b, bias)
    x2 = _mix_fwd(x1, mod, norm_mix_g, *mix_args, "mix_fwd")
    x3, gu3, h3 = _ffn_fwd(x2, mod, norm_ffn2_g, w2_in, w2_out, 2, "ffn2_fwd")
    dx3, st_f = _loss_and_grad(x3, norm_final_g.reshape(1, D), target)

    dgu3, d_w2_out, dgate3 = _ffn_bwd_hidden(dx3, mod, gu3, w2_out, 2, "ffn2_bwd_hidden")
    d_w2_in = _ffn_bwd_win(h3, dgu3, "ffn2_bwd_win")[0]
    dx2, st3 = _ffn_bwd_input(dgu3, w2_in, x2, dx3, mod, norm_ffn2_g, 2, "ffn2_bwd_input")
    dx1, d_wmi, d_wmo, d_wp, d_ws, st2, vec2, dbias, r_w2_in, r_w2_out = _mix_bwd(
        x1, dx2, mod, norm_mix_g, *mix_args, "mix_bwd",
        exchange=[_win_to_slots(d_w2_in).astype(BF), d_w2_out.reshape(NDEV, F // NDEV, D).astype(BF)])
    dgu1, d_w1_out, dgate1, r_wmi, r_wmo = _ffn_bwd_hidden(
        dx1, mod, gu1, w1_out, 0, "ffn1_bwd_hidden",
        exchange=[_wmi_to_slots(d_wmi).astype(BF), d_wmo.reshape(NDEV, D // NDEV, D).astype(BF)])
    d_w1_in, r_w1_out = _ffn_bwd_win(h1, dgu1, "ffn1_bwd_win",
                                     exchange=[d_w1_out.reshape(NDEV, F // NDEV, D).astype(BF)])
    dx0, st1, r_w1_in = _ffn_bwd_input(dgu1, w1_in, xs, dx1, mod, norm_ffn1_g, 0, "ffn1_bwd_input",
                                       exchange=[_win_to_slots(d_w1_in).astype(BF)])

    received = dict(ffn1_w_in=r_w1_in, ffn1_w_out=r_w1_out, w_mix_in=r_wmi, w_mix_out=r_wmo,
                    ffn2_w_in=r_w2_in, ffn2_w_out=r_w2_out)
    tiles = dict(ffn1_w_in=256, ffn1_w_out=176, w_mix_in=512, w_mix_out=128, ffn2_w_in=256, ffn2_w_out=176)
    result = {}
    for k, recv in received.items():
        g, dl, nm, nv = _sum_adamw(recv, weights[k][0], mom1[k][0], mom2[k][0], tiles[k], "update_" + k)
        result[k] = (g[None], dl[None], nm[None], nv[None])
    row = lambda a: a.reshape(1, D)
    params = {k: (weights[k], mom1[k], mom2[k]) for k in SMALL}
    params["norm_final_g"] = (row(norm_final_g), row(m_norm_final_g), row(v_norm_final_g))
    small, g_ada, loss_row = _small_reduce_update(d_ws, d_wp, dbias, st1, st2, st3, st_f, vec2, dgate1, dgate3,
                                                  cact_all, params)
    result.update(small)
    result["norm_final_g"] = tuple(a.reshape(D) for a in small["norm_final_g"])
    dl, nm, nv = _plain_adamw(g_ada, w_ada[0], m_w_ada[0], v_w_ada[0], 256, "update_w_ada")
    result["w_ada"] = (g_ada[None], dl[None], nm[None], nv[None])

    return (loss_row[0, 0], dx0[None], *[result[k][0] for k in order], *[result[k][1] for k in order],
            *[result[k][2] for k in order], *[result[k][3] for k in order])
```

```python
import math

import jax
import jax.numpy as jnp
from jax import lax
from jax.experimental import pallas as pl
from jax.experimental.pallas import tpu as pltpu

D = 1024
F = 2816
DP = 512
DG = 512
DPROJ = DP + 2 * DG
CHUNK = 128
WINDOWS = (2, 4, 8, 16)
HALO = 16
NDEV = 8
T_FFN = 512
T_MIX = 256
T_WIN = 2048
EPS = 1e-6
LR, B1, B2, AEPS, WD, STEP = 0.001, 0.9, 0.999, 1e-08, 0.01, 10
BC1 = 1.0 - B1 ** STEP
BC2 = 1.0 - B2 ** STEP
GELU_C = math.sqrt(2.0 / math.pi)
GELU_A = 0.044715

BF = jnp.bfloat16
F32 = jnp.float32
MESH = pl.DeviceIdType.MESH
HBM = pl.BlockSpec(memory_space=pltpu.HBM)
VMEM = pl.BlockSpec(memory_space=pltpu.VMEM)

NT = (((1,), (1,)), ((), ()))
TN = (((0,), (0,)), ((), ()))


def _dot(a, b):
    return jnp.dot(a, b, preferred_element_type=F32)


def _dot_nt(a, b):
    return lax.dot_general(a, b, NT, preferred_element_type=F32)


def _dot_tn(a, b):
    return lax.dot_general(a, b, TN, preferred_element_type=F32)


def _cparams(vmem_mb, sem=None):
    kw = dict(vmem_limit_bytes=vmem_mb * 1024 * 1024)
    if sem is not None:
        kw["dimension_semantics"] = sem
    return pltpu.CompilerParams(**kw)


def _position():
    return lax.axis_index("x"), lax.axis_index("y"), lax.axis_index("c")


def _slot(p):
    return 4 * p[0] + 2 * p[1] + p[2]


def _flip(me, d):
    x, y, c = me
    return (1 - x if d & 4 else x, 1 - y if d & 2 else y, 1 - c if d & 1 else c)


def _remote(src, dst, send_sem, recv_sem, to):
    return pltpu.make_async_remote_copy(src_ref=src, dst_ref=dst, send_sem=send_sem, recv_sem=recv_sem,
                                        device_id=to, device_id_type=MESH)


def _comm_sems(n):
    return [pltpu.SemaphoreType.DMA((n, 7)), pltpu.SemaphoreType.DMA((n, 7)), pltpu.SemaphoreType.DMA((n,))]


def _gather_phase(phase, xs, outs, sems):
    send_sems, recv_sems, local_sems = sems
    n = len(xs)
    me = _position()
    x, y, c = me
    sibling = (x, y, 1 - c)
    chips = [(1 - x, y), (x, 1 - y), (1 - x, 1 - y)]

    def copy(a, k, block, to, src=None):
        dst = outs[a].at[_slot(block)]
        return _remote(dst if src is None else src, dst, send_sems.at[a, k], recv_sems.at[a, k], to)

    def mine(a):
        return pltpu.make_async_copy(xs[a], outs[a].at[_slot(me)], local_sems.at[a])

    def first(a):
        return [copy(a, 0, me, sibling, src=xs[a])] + [copy(a, 1 + j, me, (*chip, c), src=xs[a])
                                                       for j, chip in enumerate(chips)]

    def passed(a, j):
        return copy(a, 4 + j, (*chips[j], c), sibling)

    if phase == "start":
        for a in range(n):
            mine(a).start()
            for cp in first(a):
                cp.start()
    elif phase == "forward":
        for j, chip in enumerate(chips):
            for a in range(n):
                copy(a, 1 + j, (*chip, c), me).wait_recv()
                passed(a, j).start()
    else:
        for a in range(n):
            copy(a, 0, sibling, me).wait_recv()
        for j, chip in enumerate(chips):
            for a in range(n):
                copy(a, 4 + j, (*chip, 1 - c), me).wait_recv()
        for a in range(n):
            for cp in first(a) + [passed(a, j) for j in range(3)]:
                cp.wait_send()
            mine(a).wait()


def _exchange_phase(phase, xs, outs, sems):
    send_sems, recv_sems, local_sems = sems
    me = _position()
    for a in range(len(xs)):
        copies = [pltpu.make_async_copy(xs[a].at[_slot(me)], outs[a].at[_slot(me)], local_sems.at[a])]
        for d in range(1, NDEV):
            to = _flip(me, d)
            copies.append(_remote(xs[a].at[_slot(to)], outs[a].at[_slot(me)],
                                  send_sems.at[a, d - 1], recv_sems.at[a, d - 1], to))
        for cp in copies:
            if phase == "start":
                cp.start()
            else:
                cp.wait()


def _like(bufs):
    return [jax.ShapeDtypeStruct(b.shape, b.dtype) for b in bufs]


def _rms_mod(x, gn, shift, scale):
    ms = jnp.mean(x * x, axis=-1, keepdims=True)
    r = lax.rsqrt(ms + EPS)
    xhat = x * r
    n = xhat * gn
    h = n * (1.0 + scale) + shift
    return r, xhat, n, h


def _rms_mod_bwd(dh, dres, r, xhat, n, gn, scale):
    dshift = jnp.sum(dh, axis=0, keepdims=True)
    dscale = jnp.sum(dh * n, axis=0, keepdims=True)
    dn = dh * (1.0 + scale)
    dgn = jnp.sum(dn * xhat, axis=0, keepdims=True)
    dxhat = dn * gn
    dx = dres + r * (dxhat - xhat * jnp.mean(dxhat * xhat, axis=-1, keepdims=True))
    return dx, dshift, dscale, dgn


def _final_norm_loss(x, gf, target):
    r = lax.rsqrt(jnp.mean(x * x, axis=-1, keepdims=True) + EPS)
    xhat = x * r
    e = xhat * gf - target
    part = 0.5 * jnp.sum(jnp.sum(e * e, axis=-1, keepdims=True), axis=0, keepdims=True) / D
    dy = e / D
    dgf = jnp.sum(dy * xhat, axis=0, keepdims=True)
    dxhat = dy * gf
    dx = r * (dxhat - xhat * jnp.mean(dxhat * xhat, axis=-1, keepdims=True))
    return dx, dgf, part


def _rows3(a, b, c, width):
    row = lax.broadcasted_iota(jnp.int32, (8, width), 0)
    z = jnp.zeros((8, width), F32)
    return jnp.where(row == 0, a, z) + jnp.where(row == 1, b, z) + jnp.where(row == 2, c, z)


def _gelu(x):
    t = jnp.tanh(GELU_C * (x + GELU_A * x * x * x))
    return 0.5 * x * (1.0 + t), t


def _gelu_grad(x, t):
    return 0.5 * (1.0 + t) + 0.5 * x * (1.0 - t * t) * GELU_C * (1.0 + 3.0 * GELU_A * x * x)


def _adamw(w, g, m, v):
    m = B1 * m + (1.0 - B1) * g
    v = B2 * v + (1.0 - B2) * (g * g)
    m_hat = m / BC1
    v_hat = v / BC2
    delta = -LR * (m_hat / (jnp.sqrt(v_hat) + AEPS) + WD * w)
    return delta, m, v


def _cast_shards(shards):
    n = len(shards)

    def body(*refs):
        for a in range(n):
            refs[n + a][...] = refs[a][...].astype(BF)

    return pl.pallas_call(
        body, name="cast_shards", out_shape=[jax.ShapeDtypeStruct(s.shape, BF) for s in shards],
        in_specs=[VMEM] * n, out_specs=[VMEM] * n, compiler_params=_cparams(40),
    )(*shards)


def _all_gather(shards, name):
    n = len(shards)

    def body(*refs):
        xs, outs, sems = refs[:n], refs[n:2 * n], refs[2 * n:]
        for phase in ("start", "forward", "finish"):
            _gather_phase(phase, xs, outs, sems)

    return pl.pallas_call(
        body, name=name,
        out_shape=[jax.ShapeDtypeStruct((NDEV,) + s.shape, s.dtype) for s in shards],
        in_specs=[HBM] * n, out_specs=[HBM] * n, scratch_shapes=_comm_sems(n),
    )(*shards)


def _ada_forward(c8, w_ada, b8):
    wc = w_ada.shape[1]

    def body(c8_ref, w_ref, b8_ref, mod_ref, cact_ref, call_ref, mall_ref, send_sems, recv_sems):
        me = _position()
        my = _slot(me)
        row = lax.broadcasted_iota(jnp.int32, (8, 1), 0)
        call_ref[my] = c8_ref[...]
        sends = []
        for d in range(1, NDEV):
            to = _flip(me, d)
            sends.append(_remote(c8_ref, call_ref.at[my], send_sems.at[0, d - 1], recv_sems.at[0, d - 1], to))
        for cp in sends:
            cp.start()
        for cp in sends:
            cp.wait()
        c_all = jnp.zeros((8, D), F32)
        for k in range(NDEV):
            c_all = c_all + jnp.where(row == k, call_ref[k], 0.0)
        cact = c_all * jax.nn.sigmoid(c_all)
        cact_ref[...] = cact
        part = _dot(cact.astype(BF), w_ref[...].astype(BF))
        mall_ref[my] = part
        sends = []
        for d in range(1, NDEV):
            to = _flip(me, d)
            sends.append(_remote(mall_ref.at[my], mall_ref.at[my], send_sems.at[1, d - 1], recv_sems.at[1, d - 1], to))
        for cp in sends:
            cp.start()
        for cp in sends:
            cp.wait()
        out = jnp.zeros((8, wc), F32)
        for k in range(NDEV):
            piece = jnp.sum(jnp.where(row == my, mall_ref[k], 0.0), axis=0, keepdims=True)
            out = out + jnp.where(row == k, piece, 0.0)
        mod_ref[...] = out + b8_ref[...]

    return pl.pallas_call(
        body, name="ada_forward",
        out_shape=[jax.ShapeDtypeStruct((8, wc), F32), jax.ShapeDtypeStruct((8, D), F32)],
        in_specs=[VMEM, VMEM, VMEM], out_specs=[VMEM, VMEM],
        scratch_shapes=[pltpu.VMEM((NDEV, 8, D), F32), pltpu.VMEM((NDEV, 8, wc), F32),
                        pltpu.SemaphoreType.DMA((2, 7)), pltpu.SemaphoreType.DMA((2, 7))],
        compiler_params=_cparams(32),
    )(c8, w_ada, b8)


MATS = ("w_spatial", "w_pool", "b_spatial")
VECS = ("norm_ffn1_g", "norm_mix_g", "norm_ffn2_g", "norm_final_g", "pool_scale", "gmlp_ln_g", "gmlp_ln_b", "b_ada")
VEC_WIDTH = dict(norm_ffn1_g=D, norm_mix_g=D, norm_ffn2_g=D, norm_final_g=D, pool_scale=DP, gmlp_ln_g=DG,
                 gmlp_ln_b=DG, b_ada=9 * D)
MAT_ROWS = 1600
MAT_SLICE = MAT_ROWS // NDEV
VEC_LANES = sum(VEC_WIDTH.values()) + 128
DMOD_AT = VEC_LANES - 128 - 9 * D
SMALL = MATS + VECS


def _small_reduce(g_ws, g_wp, dbias, st1, st2, st3, st_f, vec2, dgate1, dgate3):
    wc = 9 * D // NDEV

    def body(g_ws_ref, g_wp_ref, dbias_ref, st1_ref, st2_ref, st3_ref, stf_ref, vec2_ref, dg1_ref, dg3_ref,
             tot_ref, rsum_ref, dmine_ref,
             pack_ref, rs_ref, ag_ref, rv_ref, dmp_ref, dw_ref, send_sems, recv_sems):
        me = _position()
        my = _slot(me)

        pack_ref[0:1024, :] = g_ws_ref[...].reshape(1024, 128)
        pack_ref[1024:1536, :] = g_wp_ref[...].reshape(512, 128)
        ch = lax.broadcasted_iota(jnp.int32, (DG, 128), 0)
        hd = lax.broadcasted_iota(jnp.int32, (DG, 128), 1)
        sel = jnp.where(ch // 64 == hd, 1.0, 0.0).astype(F32)
        heads = jnp.dot(dbias_ref[...], sel, preferred_element_type=F32, precision=lax.Precision.HIGHEST)
        pack_ref[1536:1544, :] = heads.T[0:8, :]
        pack_ref[1544:MAT_ROWS, :] = jnp.zeros((MAT_ROWS - 1544, 128), F32)
        dgate1 = dg1_ref[0:1, :] + dg1_ref[8:9, :]
        dgate3 = dg3_ref[0:1, :] + dg3_ref[8:9, :]
        row = jnp.concatenate(
            [st1_ref[2:3, :], st2_ref[2:3, :], st3_ref[2:3, :], stf_ref[0:1, :],
             vec2_ref[0:1, :], vec2_ref[1:2, :], vec2_ref[2:3, :],
             st1_ref[0:1, :], st1_ref[1:2, :], dgate1, st2_ref[0:1, :], st2_ref[1:2, :], st2_ref[3:4, :],
             st3_ref[0:1, :], st3_ref[1:2, :], dgate3, stf_ref[1:2, 0:128]], axis=1)
        rv_ref[my] = row
        for k in range(NDEV):
            dmp_ref[k] = row[:, DMOD_AT + wc * k:DMOD_AT + wc * (k + 1)]
        dw_ref[my] = dmp_ref[my]
        rs_ref[my] = pack_ref[pl.ds(pl.multiple_of(my * MAT_SLICE, 8), MAT_SLICE), :]

        first = []
        for d in range(1, NDEV):
            to = _flip(me, d)
            theirs = pl.ds(pl.multiple_of(_slot(to) * MAT_SLICE, 8), MAT_SLICE)
            first.append(_remote(pack_ref.at[theirs, :], rs_ref.at[my], send_sems.at[0, d - 1], recv_sems.at[0, d - 1], to))
            first.append(_remote(dmp_ref.at[_slot(to)], dw_ref.at[my], send_sems.at[1, d - 1], recv_sems.at[1, d - 1], to))
            first.append(_remote(rv_ref.at[my], rv_ref.at[my], send_sems.at[2, d - 1], recv_sems.at[2, d - 1], to))
        for cp in first:
            cp.start()
        for cp in first:
            cp.wait()
        red = rs_ref[0]
        for k in range(1, NDEV):
            red = red + rs_ref[k]
        ag_ref[my] = red
        second = []
        for d in range(1, NDEV):
            to = _flip(me, d)
            second.append(_remote(ag_ref.at[my], ag_ref.at[my], send_sems.at[3, d - 1], recv_sems.at[3, d - 1], to))
        for cp in second:
            cp.start()

        rsum = rv_ref[0]
        for k in range(1, NDEV):
            rsum = rsum + rv_ref[k]
        rsum_ref[...] = rsum
        r8 = lax.broadcasted_iota(jnp.int32, (8, 1), 0)
        dmine = jnp.zeros((8, wc), F32)
        for k in range(NDEV):
            dmine = dmine + jnp.where(r8 == k, dw_ref[k], 0.0)
        dmine_ref[...] = dmine

        for cp in second:
            cp.wait()
        for k in range(NDEV):
            tot_ref[k * MAT_SLICE:(k + 1) * MAT_SLICE, :] = ag_ref[k]

    return pl.pallas_call(
        body, name="small_reduce",
        out_shape=[jax.ShapeDtypeStruct((MAT_ROWS, 128), F32), jax.ShapeDtypeStruct((1, VEC_LANES), F32),
                   jax.ShapeDtypeStruct((8, wc), F32)],
        in_specs=[VMEM] * 10, out_specs=[VMEM] * 3,
        scratch_shapes=[pltpu.VMEM((MAT_ROWS, 128), F32), pltpu.VMEM((NDEV, MAT_SLICE, 128), F32),
                        pltpu.VMEM((NDEV, MAT_SLICE, 128), F32),
                        pltpu.VMEM((NDEV, 1, VEC_LANES), F32), pltpu.VMEM((NDEV, 1, wc), F32),
                        pltpu.VMEM((NDEV, 1, wc), F32),
                        pltpu.SemaphoreType.DMA((4, 7)), pltpu.SemaphoreType.DMA((4, 7))],
        compiler_params=_cparams(32),
    )(g_ws, g_wp, dbias, st1, st2, st3, st_f, vec2, dgate1, dgate3)


def _small_update(tot, rsum, params):
    flat = [a for k in SMALL for a in params[k]]
    n_in = 2 + len(flat)

    def body(*refs):
        tot_ref, rsum_ref = refs[:2]
        p_refs = refs[2:n_in]
        o_refs = refs[n_in:n_in + 4 * len(SMALL)]
        loss_ref = refs[n_in + 4 * len(SMALL)]
        loss_ref[...] = rsum_ref[:, VEC_LANES - 128:VEC_LANES]

        def update(idx, g):
            w_ref, m_ref, v_ref = p_refs[3 * idx:3 * idx + 3]
            g_out, d_out, m_out, v_out = o_refs[4 * idx:4 * idx + 4]
            g = g.reshape(w_ref.shape)
            g_out[...] = g
            d_out[...], m_out[...], v_out[...] = _adamw(w_ref[...], g, m_ref[...], v_ref[...])

        update(0, tot_ref[0:1024, :])
        update(1, tot_ref[1024:1536, :])
        update(2, tot_ref[1536:1544, :])
        at = 0
        for idx, k in enumerate(VECS):
            update(3 + idx, rsum_ref[:, at:at + VEC_WIDTH[k]])
            at += VEC_WIDTH[k]

    outs = []
    for k in SMALL:
        outs += [jax.ShapeDtypeStruct(params[k][0].shape, F32)] * 4
    outs += [jax.ShapeDtypeStruct((1, 128), F32)]
    res = pl.pallas_call(
        body, name="small_update", out_shape=outs,
        in_specs=[VMEM] * n_in, out_specs=[VMEM] * len(outs), compiler_params=_cparams(32),
    )(tot, rsum, *flat)
    return {k: tuple(res[4 * i:4 * i + 4]) for i, k in enumerate(SMALL)}, res[-1]


def _sum_adamw(recv, w, m, v, tr, name):
    R, C = w.shape

    def body(r_ref, w_ref, m_ref, v_ref, g_ref, d_ref, nm_ref, nv_ref):
        g = r_ref[0].astype(F32)
        for k in range(1, NDEV):
            g = g + r_ref[k].astype(F32)
        g_ref[...] = g
        d_ref[...], nm_ref[...], nv_ref[...] = _adamw(w_ref[...], g, m_ref[...], v_ref[...])

    blk = pl.BlockSpec((tr, C), lambda i: (i, 0))
    out = jax.ShapeDtypeStruct((R, C), F32)
    return pl.pallas_call(
        body, name=name, grid=(R // tr,), out_shape=[out] * 4,
        in_specs=[pl.BlockSpec((NDEV, tr, C), lambda i: (0, i, 0)), blk, blk, blk], out_specs=[blk] * 4,
        compiler_params=_cparams(48, ("arbitrary",)),
    )(recv, w, m, v)


def _ada_update(cact_all, dmine, w, m, v, tr):
    R, C = w.shape

    def body(c_ref, dm_ref, w_ref, m_ref, v_ref, g_ref, d_ref, nm_ref, nv_ref):
        g = _dot_tn(c_ref[...].astype(BF), dm_ref[...].astype(BF))
        g_ref[...] = g
        d_ref[...], nm_ref[...], nv_ref[...] = _adamw(w_ref[...], g, m_ref[...], v_ref[...])

    blk = pl.BlockSpec((tr, C), lambda i: (i, 0))
    out = jax.ShapeDtypeStruct((R, C), F32)
    return pl.pallas_call(
        body, name="update_w_ada", grid=(R // tr,), out_shape=[out] * 4,
        in_specs=[pl.BlockSpec((8, tr), lambda i: (0, i)), pl.BlockSpec((8, C), lambda i: (0, 0)), blk, blk, blk],
        out_specs=[blk] * 4,
        compiler_params=_cparams(48, ("arbitrary",)),
    )(cact_all, dmine, w, m, v)


FC = F // 2


def _ffn_fwd(x, mod, gn, w_in_t, w_out, sub, name, gather=(), loss=None):
    S = x.shape[0]
    T = min(T_FFN, S)
    nS, nJ = S // T, F // FC
    ng = len(gather)
    nl = 2 if loss else 0
    forward_step = (3 * nS) // 4

    def body(*refs):
        x_ref, mod_ref, gn_ref, wg_ref, wu_ref, wo_ref = refs[:6]
        gf_ref, t_ref = refs[6:6 + nl] if loss else (None, None)
        shards = refs[6 + nl:6 + nl + ng]
        at = 6 + nl + ng
        xo_ref, gu_ref, h_ref = refs[at:at + 3]
        gathered = refs[at + 3:at + 3 + ng]
        at += 3 + ng
        st_ref = refs[at] if loss else None
        at += nl // 2
        acc_scr = refs[at]
        sems = refs[at + 1:]
        i, j = pl.program_id(0), pl.program_id(1)

        if ng:
            @pl.when((i == 0) & (j == 0))
            def _():
                _gather_phase("start", shards, gathered, sems)

            @pl.when((i == forward_step) & (j == 0))
            def _():
                _gather_phase("forward", shards, gathered, sems)

        @pl.when(j == 0)
        def _():
            _, _, _, h = _rms_mod(x_ref[...], gn_ref[...], mod_ref[3 * sub:3 * sub + 1, :],
                                  mod_ref[3 * sub + 1:3 * sub + 2, :])
            h_ref[...] = h.astype(BF)
            acc_scr[...] = jnp.zeros_like(acc_scr)

        h = h_ref[...]
        g = _dot_nt(h, wg_ref[0])
        u = _dot_nt(h, wu_ref[0])
        gu_ref[0] = g.astype(BF)
        gu_ref[1] = u.astype(BF)
        a = (g * jax.nn.sigmoid(g) * u).astype(BF)
        acc_scr[...] += _dot(a, wo_ref[...])

        @pl.when(j == nJ - 1)
        def _():
            xo = x_ref[...] + (0.5 * mod_ref[3 * sub + 2:3 * sub + 3, :]) * acc_scr[...]
            if not loss:
                xo_ref[...] = xo
            else:
                dx, dgf, part = _final_norm_loss(xo, gf_ref[...], t_ref[...])
                xo_ref[...] = dx
                upd = _rows3(dgf, jnp.broadcast_to(part, (1, D)), jnp.zeros((1, D), F32), D)

                @pl.when(i == 0)
                def _():
                    st_ref[...] = upd

                @pl.when(i > 0)
                def _():
                    st_ref[...] += upd

        if ng:
            @pl.when((i == nS - 1) & (j == nJ - 1))
            def _():
                _gather_phase("finish", shards, gathered, sems)

    tile = pl.BlockSpec((T, D), lambda i, j: (i, 0))
    return pl.pallas_call(
        body, name=name, grid=(nS, nJ),
        out_shape=[jax.ShapeDtypeStruct((S, D), F32), jax.ShapeDtypeStruct((2, S, F), BF),
                   jax.ShapeDtypeStruct((S, D), BF)]
                  + [jax.ShapeDtypeStruct((NDEV,) + s.shape, s.dtype) for s in gather]
                  + ([jax.ShapeDtypeStruct((8, D), F32)] if loss else []),
        in_specs=[tile,
                  pl.BlockSpec((9, D), lambda i, j: (0, 0)),
                  pl.BlockSpec((1, D), lambda i, j: (0, 0)),
                  pl.BlockSpec((1, FC, D), lambda i, j: (0, j, 0)),
                  pl.BlockSpec((1, FC, D), lambda i, j: (1, j, 0)),
                  pl.BlockSpec((FC, D), lambda i, j: (j, 0))]
                 + ([pl.BlockSpec((1, D), lambda i, j: (0, 0)), tile] if loss else []) + [HBM] * ng,
        out_specs=[tile, pl.BlockSpec((2, T, FC), lambda i, j: (0, i, j)), tile] + [HBM] * ng
                  + ([pl.BlockSpec((8, D), lambda i, j: (0, 0))] if loss else []),
        scratch_shapes=[pltpu.VMEM((T, D), F32)] + (_comm_sems(ng) if ng else []),
        compiler_params=_cparams(56, ("arbitrary", "arbitrary")),
    )(x, mod, gn, w_in_t, w_in_t, w_out, *(loss or ()), *gather)


def _ffn_bwd_hidden(dx, mod, gu, w_out, sub, name, exchange=()):
    S = dx.shape[0]
    T = min(T_FFN, S)
    nS, nJ = S // T, F // FC
    ne = len(exchange)

    def body(*refs):
        dx_ref, mod_ref, gu_ref, wo_ref = refs[:4]
        sendbufs = refs[4:4 + ne]
        dgu_ref, gw_ref, dgate_ref = refs[4 + ne:7 + ne]
        recvbufs = refs[7 + ne:7 + 2 * ne]
        acc_scr = refs[7 + 2 * ne]
        sems = refs[8 + 2 * ne:]
        j, i = pl.program_id(0), pl.program_id(1)

        if ne:
            @pl.when((i == 0) & (j == 0))
            def _():
                _exchange_phase("start", sendbufs, recvbufs, sems)

        gate = mod_ref[3 * sub + 2:3 * sub + 3, :]
        dx = dx_ref[...]
        da = _dot_nt((dx * (0.5 * gate)).astype(BF), wo_ref[...])
        g = gu_ref[0].astype(F32)
        u = gu_ref[1].astype(F32)
        sg = jax.nn.sigmoid(g)
        s = g * sg
        dgu_ref[0] = (da * u * (sg * (1.0 + g * (1.0 - sg)))).astype(BF)
        dgu_ref[1] = (da * s).astype(BF)
        contrib = _dot_tn((s * u).astype(BF), dx.astype(BF))

        @pl.when(i == 0)
        def _():
            acc_scr[...] = contrib

        @pl.when(i > 0)
        def _():
            acc_scr[...] += contrib

        @pl.when(i == nS - 1)
        def _():
            acc = acc_scr[...]
            dgate = 0.5 * jnp.sum(acc * wo_ref[...].astype(F32), axis=0, keepdims=True)
            dgate_ref[...] = jnp.broadcast_to(dgate, (8, D))
            gw_ref[...] = (acc * (0.5 * gate)).astype(BF)

        if ne:
            @pl.when((i == nS - 1) & (j == nJ - 1))
            def _():
                _exchange_phase("wait", sendbufs, recvbufs, sems)

    return pl.pallas_call(
        body, name=name, grid=(nJ, nS),
        out_shape=[jax.ShapeDtypeStruct((2, S, F), BF), jax.ShapeDtypeStruct((F, D), BF),
                   jax.ShapeDtypeStruct((8 * nJ, D), F32)] + _like(exchange),
        in_specs=[pl.BlockSpec((T, D), lambda j, i: (i, 0)),
                  pl.BlockSpec((9, D), lambda j, i: (0, 0)),
                  pl.BlockSpec((2, T, FC), lambda j, i: (0, i, j)),
                  pl.BlockSpec((FC, D), lambda j, i: (j, 0))] + [HBM] * ne,
        out_specs=[pl.BlockSpec((2, T, FC), lambda j, i: (0, i, j)),
                   pl.BlockSpec((FC, D), lambda j, i: (j, 0)),
                   pl.BlockSpec((8, D), lambda j, i: (j, 0))] + [HBM] * ne,
        scratch_shapes=[pltpu.VMEM((FC, D), F32)] + (_comm_sems(ne) if ne else []),
        compiler_params=_cparams(56, ("arbitrary", "arbitrary")),
    )(dx, mod, gu, w_out, *exchange)


def _ffn_bwd_input(dgu, w_in_t, x, dx, mod, gn, sub, name, exchange=()):
    S = x.shape[0]
    T = min(T_FFN, S)
    nS, nJ = S // T, F // FC
    ne = len(exchange)

    def body(*refs):
        dgu_ref, wg_ref, wu_ref, x_ref, dx_ref, mod_ref, gn_ref = refs[:7]
        sendbufs = refs[7:7 + ne]
        dxin_ref, st_ref = refs[7 + ne:9 + ne]
        recvbufs = refs[9 + ne:9 + 2 * ne]
        acc_scr = refs[9 + 2 * ne]
        sems = refs[10 + 2 * ne:]
        i, j = pl.program_id(0), pl.program_id(1)

        if ne:
            @pl.when((i == 0) & (j == 0))
            def _():
                _exchange_phase("start", sendbufs, recvbufs, sems)

        @pl.when(j == 0)
        def _():
            acc_scr[...] = jnp.zeros_like(acc_scr)

        acc_scr[...] += _dot(dgu_ref[0], wg_ref[0]) + _dot(dgu_ref[1], wu_ref[0])

        @pl.when(j == nJ - 1)
        def _():
            gn = gn_ref[...]
            scale = mod_ref[3 * sub + 1:3 * sub + 2, :]
            r, xhat, n, _ = _rms_mod(x_ref[...], gn, mod_ref[3 * sub:3 * sub + 1, :], scale)
            dxin, dshift, dscale, dgn = _rms_mod_bwd(acc_scr[...], dx_ref[...], r, xhat, n, gn, scale)
            dxin_ref[...] = dxin
            upd = _rows3(dshift, dscale, dgn, D)

            @pl.when(i == 0)
            def _():
                st_ref[...] = upd

            @pl.when(i > 0)
            def _():
                st_ref[...] += upd

        if ne:
            @pl.when((i == nS - 1) & (j == nJ - 1))
            def _():
                _exchange_phase("wait", sendbufs, recvbufs, sems)

    return pl.pallas_call(
        body, name=name, grid=(nS, nJ),
        out_shape=[jax.ShapeDtypeStruct((S, D), F32), jax.ShapeDtypeStruct((8, D), F32)] + _like(exchange),
        in_specs=[pl.BlockSpec((2, T, FC), lambda i, j: (0, i, j)),
                  pl.BlockSpec((1, FC, D), lambda i, j: (0, j, 0)),
                  pl.BlockSpec((1, FC, D), lambda i, j: (1, j, 0)),
                  pl.BlockSpec((T, D), lambda i, j: (i, 0)),
                  pl.BlockSpec((T, D), lambda i, j: (i, 0)),
                  pl.BlockSpec((9, D), lambda i, j: (0, 0)),
                  pl.BlockSpec((1, D), lambda i, j: (0, 0))] + [HBM] * ne,
        out_specs=[pl.BlockSpec((T, D), lambda i, j: (i, 0)),
                   pl.BlockSpec((8, D), lambda i, j: (0, 0))] + [HBM] * ne,
        scratch_shapes=[pltpu.VMEM((T, D), F32)] + (_comm_sems(ne) if ne else []),
        compiler_params=_cparams(56, ("arbitrary", "arbitrary")),
    )(dgu, w_in_t, w_in_t, x, dx, mod, gn, *exchange)


def _ffn_bwd_win(h, dgu, name, exchange=()):
    S = h.shape[0]
    T = min(T_WIN, S)
    nS, nJ = S // T, F // FC
    ne = len(exchange)

    def body(*refs):
        h_ref, dgu_ref = refs[:2]
        sendbufs = refs[2:2 + ne]
        out_ref = refs[2 + ne]
        recvbufs = refs[3 + ne:3 + 2 * ne]
        acc_scr = refs[3 + 2 * ne]
        sems = refs[4 + 2 * ne:]
        p, j, i = pl.program_id(0), pl.program_id(1), pl.program_id(2)

        if ne:
            @pl.when((p == 0) & (j == 0) & (i == 0))
            def _():
                _exchange_phase("start", sendbufs, recvbufs, sems)

        contrib = _dot_tn(dgu_ref[0], h_ref[...])

        @pl.when(i == 0)
        def _():
            acc_scr[...] = contrib

        @pl.when(i > 0)
        def _():
            acc_scr[...] += contrib

        @pl.when(i == nS - 1)
        def _():
            out_ref[0] = acc_scr[...].astype(BF)

        if ne:
            @pl.when((p == 1) & (j == nJ - 1) & (i == nS - 1))
            def _():
                _exchange_phase("wait", sendbufs, recvbufs, sems)

    return pl.pallas_call(
        body, name=name, grid=(2, nJ, nS),
        out_shape=[jax.ShapeDtypeStruct((2, F, D), BF)] + _like(exchange),
        in_specs=[pl.BlockSpec((T, D), lambda p, j, i: (i, 0)),
                  pl.BlockSpec((1, T, FC), lambda p, j, i: (p, i, j))] + [HBM] * ne,
        out_specs=[pl.BlockSpec((1, FC, D), lambda p, j, i: (p, j, 0))] + [HBM] * ne,
        scratch_shapes=[pltpu.VMEM((FC, D), F32)] + (_comm_sems(ne) if ne else []),
        compiler_params=_cparams(56, ("arbitrary", "arbitrary", "arbitrary")),
    )(h, dgu, *exchange)


def _pool_counts(pos0, T):
    pos = pos0 + lax.broadcasted_iota(jnp.int32, (T, 1), 0)
    return [jnp.minimum(pos + 1, w).astype(F32) for w in WINDOWS]


def _pool_fwd(xa, halo, ext_scr, cnts, T):
    ext_scr[0:HALO, :] = halo
    ext_scr[HALO:HALO + T, :] = xa
    out = []
    for gi, w in enumerate(WINDOWS):
        cols = slice(128 * gi, 128 * gi + 128)
        acc = xa[:, cols]
        for k in range(1, w):
            acc = acc + ext_scr[HALO - k:HALO - k + T, cols]
        out.append(acc / cnts[gi] - xa[:, cols])
    return out


def _sgu_fwd(vnb, ws_ref, sv_scr, T):
    lane = lax.broadcasted_iota(jnp.int32, (CHUNK, 128), 1)
    for n in range(T // CHUNK):
        rows = slice(n * CHUNK, (n + 1) * CHUNK)
        for b in range(DG // 128):
            cols = slice(128 * b, 128 * b + 128)
            vb = vnb[rows, cols]
            sv_scr[rows, cols] = jnp.where(lane < 64, _dot(ws_ref[2 * b], vb), _dot(ws_ref[2 * b + 1], vb))


def _mix_fwd(x, mod, gn, wmi, wmo, wp, ps, lg, lb, ws, bias, name):
    S = x.shape[0]
    T = min(T_MIX, S)

    def body(x_ref, mod_ref, gn_ref, wmi_ref, wmo_ref, wp_ref, ps_ref, lg_ref, lb_ref, ws_ref, bias_ref,
             xo_ref, carry_scr, ext_scr, sv_scr, ycat_scr):
        i = pl.program_id(0)

        @pl.when(i == 0)
        def _():
            carry_scr[...] = jnp.zeros_like(carry_scr)

        x = x_ref[...]
        _, _, _, h = _rms_mod(x, gn_ref[...], mod_ref[3:4, :], mod_ref[4:5, :])
        proj = _dot_nt(h.astype(BF), wmi_ref[...])
        xa = proj[:, 0:DP]
        p = _pool_fwd(xa, carry_scr[...], ext_scr, _pool_counts(i * T, T), T)
        carry_scr[...] = xa[T - HALO:T, :]
        for gi in range(4):
            cols = slice(128 * gi, 128 * gi + 128)
            ycat_scr[:, cols] = (_dot(p[gi].astype(BF), wp_ref[gi]) * ps_ref[:, cols]).astype(BF)
        u, _ = _gelu(proj[:, DP:DP + DG])
        v, _ = _gelu(proj[:, DP + DG:DPROJ])
        mu = jnp.mean(v, axis=-1, keepdims=True)
        vc = v - mu
        rstd = lax.rsqrt(jnp.mean(vc * vc, axis=-1, keepdims=True) + EPS)
        vn = vc * rstd * lg_ref[...] + lb_ref[...]
        _sgu_fwd(vn.astype(BF), ws_ref, sv_scr, T)
        for n in range(T // CHUNK):
            rows = slice(n * CHUNK, (n + 1) * CHUNK)
            ycat_scr[rows, DP:D] = (u[rows, :] * (sv_scr[rows, :] + bias_ref[...])).astype(BF)
        xo_ref[...] = x + mod_ref[5:6, :] * _dot(ycat_scr[...], wmo_ref[...])

    full = lambda shape: pl.BlockSpec(shape, lambda i: (0,) * len(shape))
    return pl.pallas_call(
        body, name=name, grid=(S // T,),
        out_shape=jax.ShapeDtypeStruct((S, D), F32),
        in_specs=[pl.BlockSpec((T, D), lambda i: (i, 0)), full((9, D)), full((1, D)), full((DPROJ, D)), full((D, D)),
                  full((4, 128, 128)), full((1, DP)), full((1, DG)), full((1, DG)), full((8, CHUNK, CHUNK)),
                  full((CHUNK, DG))],
        out_specs=pl.BlockSpec((T, D), lambda i: (i, 0)),
        scratch_shapes=[pltpu.VMEM((HALO, DP), F32), pltpu.VMEM((T + HALO, DP), F32), pltpu.VMEM((T, DG), F32),
                        pltpu.VMEM((T, D), BF)],
        compiler_params=_cparams(48, ("arbitrary",)),
    )(x, mod, gn, wmi, wmo, wp, ps, lg, lb, ws, bias)


def _mix_bwd(x, dxo, mod, gn, wmi, wmo, wp, ps, lg, lb, ws, bias, name, exchange=()):
    S = x.shape[0]
    T = min(T_MIX, S)
    nS = S // T
    hb = T // HALO
    ne = len(exchange)

    def body(*refs):
        (x_ref, xh_ref, dxo_ref, mod_ref, gn_ref, wmi_ref, wmo_ref, wp_ref, ps_ref, lg_ref, lb_ref, ws_ref,
         bias_ref) = refs[:13]
        sendbufs = refs[13:13 + ne]
        dxi_ref, gwmi_out, gwmo_out, gwp_ref, gws_ref, st_ref, vec_ref, dbias_ref = refs[13 + ne:21 + ne]
        recvbufs = refs[21 + ne:21 + 2 * ne]
        (carry_scr, ext_scr, qext_scr, sv_scr, dvn_scr, ycat_scr, dproj_scr, gwmi_ref,
         gwmo_ref) = refs[21 + 2 * ne:30 + 2 * ne]
        sems = refs[30 + 2 * ne:]
        i = pl.program_id(0)
        t = nS - 1 - i
        gn = gn_ref[...]
        shift, scale, gate = mod_ref[3:4, :], mod_ref[4:5, :], mod_ref[5:6, :]

        @pl.when(i == 0)
        def _():
            if ne:
                _exchange_phase("start", sendbufs, recvbufs, sems)
            carry_scr[...] = jnp.zeros_like(carry_scr)
            gwmi_ref[...] = jnp.zeros_like(gwmi_ref)
            gwmo_ref[...] = jnp.zeros_like(gwmo_ref)
            gwp_ref[...] = jnp.zeros_like(gwp_ref)
            gws_ref[...] = jnp.zeros_like(gws_ref)
            st_ref[...] = jnp.zeros_like(st_ref)
            vec_ref[...] = jnp.zeros_like(vec_ref)
            dbias_ref[...] = jnp.zeros_like(dbias_ref)

        x = x_ref[...]
        dxo = dxo_ref[...]
        r, xhat, n, h = _rms_mod(x, gn, shift, scale)
        hbf = h.astype(BF)
        proj = _dot_nt(hbf, wmi_ref[...])
        xa = proj[:, 0:DP]
        zu = proj[:, DP:DP + DG]
        zv = proj[:, DP + DG:DPROJ]
        _, _, _, hh = _rms_mod(xh_ref[...], gn, shift, scale)
        halo = _dot_nt(hh.astype(BF), wmi_ref[0:DP, :])
        halo = jnp.where(t == 0, 0.0, halo)
        cnts = _pool_counts(t * T, T)
        p = _pool_fwd(xa, halo, ext_scr, cnts, T)
        m = []
        for gi in range(4):
            cols = slice(128 * gi, 128 * gi + 128)
            m.append(_dot(p[gi].astype(BF), wp_ref[gi]))
            ycat_scr[:, cols] = (m[gi] * ps_ref[:, cols]).astype(BF)
        u, tu = _gelu(zu)
        v, tv = _gelu(zv)
        mu = jnp.mean(v, axis=-1, keepdims=True)
        vc = v - mu
        rstd = lax.rsqrt(jnp.mean(vc * vc, axis=-1, keepdims=True) + EPS)
        vhat = vc * rstd
        lg = lg_ref[...]
        vnb = (vhat * lg + lb_ref[...]).astype(BF)
        _sgu_fwd(vnb, ws_ref, sv_scr, T)
        for nck in range(T // CHUNK):
            rows = slice(nck * CHUNK, (nck + 1) * CHUNK)
            sv_scr[rows, :] = sv_scr[rows, :] + bias_ref[...]
        sv = sv_scr[...]
        ycat_scr[:, DP:D] = (u * sv).astype(BF)

        gwmo_ref[...] += _dot_tn(ycat_scr[...], dxo.astype(BF))
        dyc = _dot_nt((dxo * gate).astype(BF), wmo_ref[...])
        dya = dyc[:, 0:DP]
        dyb = dyc[:, DP:D]

        dps = []
        dp = []
        for gi in range(4):
            cols = slice(128 * gi, 128 * gi + 128)
            dps.append(jnp.sum(dya[:, cols] * m[gi], axis=0, keepdims=True))
            dm = (dya[:, cols] * ps_ref[:, cols]).astype(BF)
            gwp_ref[gi] += _dot_tn(p[gi].astype(BF), dm)
            dp.append(_dot_nt(dm, wp_ref[gi]))
            qext_scr[0:T, cols] = dp[gi] / cnts[gi]
        qext_scr[T:T + HALO, :] = carry_scr[...]
        for gi, w in enumerate(WINDOWS):
            cols = slice(128 * gi, 128 * gi + 128)
            acc = qext_scr[0:T, cols]
            for k in range(1, w):
                acc = acc + qext_scr[k:k + T, cols]
            dproj_scr[:, cols] = (acc - dp[gi]).astype(BF)
        carry_scr[...] = qext_scr[0:HALO, :]

        du = dyb * sv
        dsv = dyb * u
        lane = lax.broadcasted_iota(jnp.int32, (CHUNK, 128), 1)
        dbias = jnp.zeros((CHUNK, DG), F32)
        for nck in range(T // CHUNK):
            rows = slice(nck * CHUNK, (nck + 1) * CHUNK)
            dbias = dbias + dsv[rows, :]
            for b in range(DG // 128):
                cols = slice(128 * b, 128 * b + 128)
                dsvb = dsv[rows, cols]
                vb = vnb[rows, cols]
                gws_ref[2 * b] += _dot_nt(jnp.where(lane < 64, dsvb, 0.0).astype(BF), vb)
                gws_ref[2 * b + 1] += _dot_nt(jnp.where(lane < 64, 0.0, dsvb).astype(BF), vb)
                dsvbb = dsvb.astype(BF)
                dvn_scr[rows, cols] = jnp.where(lane < 64, _dot_tn(ws_ref[2 * b], dsvbb),
                                                _dot_tn(ws_ref[2 * b + 1], dsvbb))
        dbias_ref[...] += dbias
        dvn = dvn_scr[...]
        dlg = jnp.sum(dvn * vhat, axis=0, keepdims=True)
        dlb = jnp.sum(dvn, axis=0, keepdims=True)
        dvhat = dvn * lg
        dv = rstd * (dvhat - jnp.mean(dvhat, axis=-1, keepdims=True)
                     - vhat * jnp.mean(dvhat * vhat, axis=-1, keepdims=True))
        dproj_scr[:, DP:DP + DG] = (du * _gelu_grad(zu, tu)).astype(BF)
        dproj_scr[:, DP + DG:DPROJ] = (dv * _gelu_grad(zv, tv)).astype(BF)
        vec_ref[...] += _rows3(jnp.concatenate(dps, axis=1), dlg, dlb, DP)

        dproj = dproj_scr[...]
        gwmi_ref[...] += _dot_tn(dproj, hbf)
        dh = _dot(dproj, wmi_ref[...])
        dxi, dshift, dscale, dgn = _rms_mod_bwd(dh, dxo, r, xhat, n, gn, scale)
        dxi_ref[...] = dxi
        st_ref[...] += _rows3(dshift, dscale, dgn, D)

        @pl.when(i == nS - 1)
        def _():
            acc = gwmo_ref[...]
            dgate = jnp.sum(acc * wmo_ref[...].astype(F32), axis=0, keepdims=True)
            row = lax.broadcasted_iota(jnp.int32, (8, D), 0)
            st_ref[...] += jnp.where(row == 3, dgate, 0.0)
            gwmo_out[...] = (acc * gate).astype(BF)
            gwmi_out[...] = gwmi_ref[...].astype(BF)
            tt = lax.broadcasted_iota(jnp.int32, (CHUNK, CHUNK), 0)
            ss = lax.broadcasted_iota(jnp.int32, (CHUNK, CHUNK), 1)
            for hd in range(8):
                gws_ref[hd] = jnp.where(tt >= ss, gws_ref[hd], 0.0)
            if ne:
                _exchange_phase("wait", sendbufs, recvbufs, sems)

    full = lambda shape: pl.BlockSpec(shape, lambda i: (0,) * len(shape))
    return pl.pallas_call(
        body, name=name, grid=(nS,),
        out_shape=[jax.ShapeDtypeStruct((S, D), F32), jax.ShapeDtypeStruct((DPROJ, D), BF),
                   jax.ShapeDtypeStruct((D, D), BF), jax.ShapeDtypeStruct((4, 128, 128), F32),
                   jax.ShapeDtypeStruct((8, CHUNK, CHUNK), F32), jax.ShapeDtypeStruct((8, D), F32),
                   jax.ShapeDtypeStruct((8, DP), F32), jax.ShapeDtypeStruct((CHUNK, DG), F32)] + _like(exchange),
        in_specs=[pl.BlockSpec((T, D), lambda i: (nS - 1 - i, 0)),
                  pl.BlockSpec((HALO, D), lambda i: (jnp.maximum((nS - 1 - i) * hb - 1, 0), 0)),
                  pl.BlockSpec((T, D), lambda i: (nS - 1 - i, 0)),
                  full((9, D)), full((1, D)), full((DPROJ, D)), full((D, D)),
                  full((4, 128, 128)), full((1, DP)), full((1, DG)), full((1, DG)), full((8, CHUNK, CHUNK)),
                  full((CHUNK, DG))] + [HBM] * ne,
        out_specs=[pl.BlockSpec((T, D), lambda i: (nS - 1 - i, 0)), full((DPROJ, D)), full((D, D)),
                   full((4, 128, 128)), full((8, CHUNK, CHUNK)), full((8, D)), full((8, DP)), full((CHUNK, DG))]
                  + [HBM] * ne,
        scratch_shapes=[pltpu.VMEM((HALO, DP), F32), pltpu.VMEM((T + HALO, DP), F32),
                        pltpu.VMEM((T + HALO, DP), F32), pltpu.VMEM((T, DG), F32), pltpu.VMEM((T, DG), F32),
                        pltpu.VMEM((T, D), BF), pltpu.VMEM((T, DPROJ), BF), pltpu.VMEM((DPROJ, D), F32),
                        pltpu.VMEM((D, D), F32)] + (_comm_sems(ne) if ne else []),
        compiler_params=_cparams(56, ("arbitrary",)),
    )(x, x, dxo, mod, gn, wmi, wmo, wp, ps, lg, lb, ws, bias, *exchange)


def kernel(x, c, w_ada, b_ada, norm_ffn1_g, ffn1_w_in, ffn1_w_out, norm_mix_g, w_mix_in, w_pool, pool_scale, gmlp_ln_g, gmlp_ln_b, w_spatial, b_spatial, w_mix_out, norm_ffn2_g, ffn2_w_in, ffn2_w_out, norm_final_g, loss_target, m_w_ada, m_b_ada, m_norm_ffn1_g, m_ffn1_w_in, m_ffn1_w_out, m_norm_mix_g, m_w_mix_in, m_w_pool, m_pool_scale, m_gmlp_ln_g, m_gmlp_ln_b, m_w_spatial, m_b_spatial, m_w_mix_out, m_norm_ffn2_g, m_ffn2_w_in, m_ffn2_w_out, m_norm_final_g, v_w_ada, v_b_ada, v_norm_ffn1_g, v_ffn1_w_in, v_ffn1_w_out, v_norm_mix_g, v_w_mix_in, v_w_pool, v_pool_scale, v_gmlp_ln_g, v_gmlp_ln_b, v_w_spatial, v_b_spatial, v_w_mix_out, v_norm_ffn2_g, v_ffn2_w_in, v_ffn2_w_out, v_norm_final_g):
    weights = dict(w_ada=w_ada, b_ada=b_ada, norm_ffn1_g=norm_ffn1_g, ffn1_w_in=ffn1_w_in, ffn1_w_out=ffn1_w_out,
                   norm_mix_g=norm_mix_g, w_mix_in=w_mix_in, w_pool=w_pool, pool_scale=pool_scale,
                   gmlp_ln_g=gmlp_ln_g, gmlp_ln_b=gmlp_ln_b, w_spatial=w_spatial, b_spatial=b_spatial,
                   w_mix_out=w_mix_out, norm_ffn2_g=norm_ffn2_g, ffn2_w_in=ffn2_w_in, ffn2_w_out=ffn2_w_out,
                   norm_final_g=norm_final_g)
    mom1 = dict(w_ada=m_w_ada, b_ada=m_b_ada, norm_ffn1_g=m_norm_ffn1_g, ffn1_w_in=m_ffn1_w_in,
                ffn1_w_out=m_ffn1_w_out, norm_mix_g=m_norm_mix_g, w_mix_in=m_w_mix_in, w_pool=m_w_pool,
                pool_scale=m_pool_scale, gmlp_ln_g=m_gmlp_ln_g, gmlp_ln_b=m_gmlp_ln_b, w_spatial=m_w_spatial,
                b_spatial=m_b_spatial, w_mix_out=m_w_mix_out, norm_ffn2_g=m_norm_ffn2_g, ffn2_w_in=m_ffn2_w_in,
                ffn2_w_out=m_ffn2_w_out, norm_final_g=m_norm_final_g)
    mom2 = dict(w_ada=v_w_ada, b_ada=v_b_ada, norm_ffn1_g=v_norm_ffn1_g, ffn1_w_in=v_ffn1_w_in,
                ffn1_w_out=v_ffn1_w_out, norm_mix_g=v_norm_mix_g, w_mix_in=v_w_mix_in, w_pool=v_w_pool,
                pool_scale=v_pool_scale, gmlp_ln_g=v_gmlp_ln_g, gmlp_ln_b=v_gmlp_ln_b, w_spatial=v_w_spatial,
                b_spatial=v_b_spatial, w_mix_out=v_w_mix_out, norm_ffn2_g=v_norm_ffn2_g, ffn2_w_in=v_ffn2_w_in,
                ffn2_w_out=v_ffn2_w_out, norm_final_g=v_norm_final_g)
    order = list(weights)
    xs = x[0]
    target = loss_target[0]
    transposed = ("ffn1_w_in", "w_mix_in", "ffn2_w_in")
    big = ("ffn1_w_in", "ffn1_w_out", "w_mix_in", "w_mix_out", "ffn2_w_in", "ffn2_w_out")
    local = lambda a, k: a[0].T if k in transposed else a[0]
    wc = w_ada.shape[2]

    shard = dict(zip(big, _cast_shards([local(weights[k], k) for k in big])))
    modp, cact_all = _ada_forward(jnp.broadcast_to(c, (8, D)), w_ada[0], b_ada.reshape(NDEV, wc))
    mod = modp.reshape(9, D)
    g_w1_in, g_w1_out = _all_gather([shard["ffn1_w_in"], shard["ffn1_w_out"]], "gather_ffn1")
    w1_in = g_w1_in.reshape(2, F, D)
    w1_out = g_w1_out.reshape(F, D)

    x1, gu1, h1, g_wmi, g_wmo, g_w2_out, g_w2_in = _ffn_fwd(
        xs, mod, norm_ffn1_g, w1_in, w1_out, 0, "ffn1_fwd",
        gather=[shard["w_mix_in"], shard["w_mix_out"], shard["ffn2_w_out"], shard["ffn2_w_in"]])
    wmi = g_wmi.reshape(DPROJ, D)
    wmo = g_wmo.reshape(D, D)
    w2_in = g_w2_in.reshape(2, F, D)
    w2_out = g_w2_out.reshape(F, D)
    tril = jnp.tril(jnp.ones((CHUNK, CHUNK), dtype=bool))
    ws_b = jnp.where(tril[None], w_spatial[0], 0.0).astype(BF)
    wp_b = w_pool[0].astype(BF)
    bias = jnp.repeat(b_spatial[0].T, DG // 8, axis=1)
    mix_args = (wmi, wmo, wp_b, pool_scale, gmlp_ln_g, gmlp_ln_b, ws_b, bias)
    x2 = _mix_fwd(x1, mod, norm_mix_g, *mix_args, "mix_fwd")
    dx3, gu3, h3, st_f = _ffn_fwd(x2, mod, norm_ffn2_g, w2_in, w2_out, 2, "ffn2_fwd",
                                  loss=(norm_final_g.reshape(1, D), target))

    slots = lambda a: a.reshape(NDEV, a.size // (NDEV * D), D)
    dgu3, d_w2_out, dgate3 = _ffn_bwd_hidden(dx3, mod, gu3, w2_out, 2, "ffn2_bwd_hidden")
    d_w2_in = _ffn_bwd_win(h3, dgu3, "ffn2_bwd_win")[0]
    dx2, st3 = _ffn_bwd_input(dgu3, w2_in, x2, dx3, mod, norm_ffn2_g, 2, "ffn2_bwd_input")
    dx1, d_wmi, d_wmo, d_wp, d_ws, st2, vec2, dbias, r_w2_in, r_w2_out = _mix_bwd(
        x1, dx2, mod, norm_mix_g, *mix_args, "mix_bwd", exchange=[slots(d_w2_in), slots(d_w2_out)])
    dgu1, d_w1_out, dgate1, r_wmi, r_wmo = _ffn_bwd_hidden(
        dx1, mod, gu1, w1_out, 0, "ffn1_bwd_hidden", exchange=[slots(d_wmi), slots(d_wmo)])
    d_w1_in, r_w1_out = _ffn_bwd_win(h1, dgu1, "ffn1_bwd_win", exchange=[slots(d_w1_out)])
    dx0, st1, r_w1_in = _ffn_bwd_input(dgu1, w1_in, xs, dx1, mod, norm_ffn1_g, 0, "ffn1_bwd_input",
                                       exchange=[slots(d_w1_in)])

    received = dict(ffn1_w_in=r_w1_in, ffn1_w_out=r_w1_out, w_mix_in=r_wmi, w_mix_out=r_wmo,
                    ffn2_w_in=r_w2_in, ffn2_w_out=r_w2_out)
    tiles = dict(ffn1_w_in=176, ffn1_w_out=176, w_mix_in=96, w_mix_out=128, ffn2_w_in=176, ffn2_w_out=176)
    result = {}
    for k, recv in received.items():
        res = _sum_adamw(recv, local(weights[k], k), local(mom1[k], k), local(mom2[k], k), tiles[k], "update_" + k)
        result[k] = tuple((a.T if k in transposed else a)[None] for a in res)
    row = lambda a: a.reshape(1, D)
    params = {k: (weights[k], mom1[k], mom2[k]) for k in SMALL}
    params["norm_final_g"] = (row(norm_final_g), row(m_norm_final_g), row(v_norm_final_g))
    tot, rsum, dmine = _small_reduce(d_ws, d_wp, dbias, st1, st2, st3, st_f, vec2, dgate1, dgate3)
    small, loss_row = _small_update(tot, rsum, params)
    result.update(small)
    result["norm_final_g"] = tuple(a.reshape(D) for a in small["norm_final_g"])
    result["w_ada"] = tuple(a[None] for a in _ada_update(cact_all, dmine, w_ada[0], m_w_ada[0], v_w_ada[0], 256))

    return (loss_row[0, 0], dx0[None], *[result[k][0] for k in order], *[result[k][1] for k in order],
            *[result[k][2] for k in order], *[result[k][3] for k in order])
```

```python
import math

import jax
import jax.numpy as jnp
from jax import lax
from jax.experimental import pallas as pl
from jax.experimental.pallas import tpu as pltpu

D = 1024
F = 2816
DP = 512
DG = 512
DPROJ = DP + 2 * DG
CHUNK = 128
WINDOWS = (2, 4, 8, 16)
HALO = 16
NDEV = 8
T_FFN = 512
T_MIX = 256
T_WIN = 2048
EPS = 1e-6
LR, B1, B2, AEPS, WD, STEP = 0.001, 0.9, 0.999, 1e-08, 0.01, 10
BC1 = 1.0 - B1 ** STEP
BC2 = 1.0 - B2 ** STEP
GELU_C = math.sqrt(2.0 / math.pi)
GELU_A = 0.044715

BF = jnp.bfloat16
F32 = jnp.float32
MESH = pl.DeviceIdType.MESH
HBM = pl.BlockSpec(memory_space=pltpu.HBM)
VMEM = pl.BlockSpec(memory_space=pltpu.VMEM)

NT = (((1,), (1,)), ((), ()))
TN = (((0,), (0,)), ((), ()))


def _dot(a, b):
    return jnp.dot(a, b, preferred_element_type=F32)


def _dot_nt(a, b):
    return lax.dot_general(a, b, NT, preferred_element_type=F32)


def _dot_tn(a, b):
    return lax.dot_general(a, b, TN, preferred_element_type=F32)


def _cparams(vmem_mb, sem=None):
    kw = dict(vmem_limit_bytes=vmem_mb * 1024 * 1024)
    if sem is not None:
        kw["dimension_semantics"] = sem
    return pltpu.CompilerParams(**kw)


def _position():
    return lax.axis_index("x"), lax.axis_index("y"), lax.axis_index("c")


def _slot(p):
    return 4 * p[0] + 2 * p[1] + p[2]


def _flip(me, d):
    x, y, c = me
    return (1 - x if d & 4 else x, 1 - y if d & 2 else y, 1 - c if d & 1 else c)


def _remote(src, dst, send_sem, recv_sem, to):
    return pltpu.make_async_remote_copy(src_ref=src, dst_ref=dst, send_sem=send_sem, recv_sem=recv_sem,
                                        device_id=to, device_id_type=MESH)


def _comm_sems(n):
    return [pltpu.SemaphoreType.DMA((n, 7)), pltpu.SemaphoreType.DMA((n, 7)), pltpu.SemaphoreType.DMA((n,))]


def _gather_phase(phase, xs, outs, sems):
    send_sems, recv_sems, local_sems = sems
    n = len(xs)
    me = _position()
    x, y, c = me
    sibling = (x, y, 1 - c)
    chips = [(1 - x, y), (x, 1 - y), (1 - x, 1 - y)]

    def copy(a, k, block, to, src=None):
        dst = outs[a].at[_slot(block)]
        return _remote(dst if src is None else src, dst, send_sems.at[a, k], recv_sems.at[a, k], to)

    def mine(a):
        return pltpu.make_async_copy(xs[a], outs[a].at[_slot(me)], local_sems.at[a])

    def first(a):
        return [copy(a, 0, me, sibling, src=xs[a])] + [copy(a, 1 + j, me, (*chip, c), src=xs[a])
                                                       for j, chip in enumerate(chips)]

    def passed(a, j):
        return copy(a, 4 + j, (*chips[j], c), sibling)

    if phase == "start":
        for a in range(n):
            mine(a).start()
            for cp in first(a):
                cp.start()
    elif phase == "forward":
        for j, chip in enumerate(chips):
            for a in range(n):
                copy(a, 1 + j, (*chip, c), me).wait_recv()
                passed(a, j).start()
    else:
        for a in range(n):
            copy(a, 0, sibling, me).wait_recv()
        for j, chip in enumerate(chips):
            for a in range(n):
                copy(a, 4 + j, (*chip, 1 - c), me).wait_recv()
        for a in range(n):
            for cp in first(a) + [passed(a, j) for j in range(3)]:
                cp.wait_send()
            mine(a).wait()


def _exchange_phase(phase, xs, outs, sems):
    send_sems, recv_sems, local_sems = sems
    me = _position()
    for a in range(len(xs)):
        copies = [pltpu.make_async_copy(xs[a].at[_slot(me)], outs[a].at[_slot(me)], local_sems.at[a])]
        for d in range(1, NDEV):
            to = _flip(me, d)
            copies.append(_remote(xs[a].at[_slot(to)], outs[a].at[_slot(me)],
                                  send_sems.at[a, d - 1], recv_sems.at[a, d - 1], to))
        for cp in copies:
            if phase == "start":
                cp.start()
            else:
                cp.wait()


def _like(bufs):
    return [jax.ShapeDtypeStruct(b.shape, b.dtype) for b in bufs]


def _rms_mod(x, gn, shift, scale):
    ms = jnp.mean(x * x, axis=-1, keepdims=True)
    r = lax.rsqrt(ms + EPS)
    xhat = x * r
    n = xhat * gn
    h = n * (1.0 + scale) + shift
    return r, xhat, n, h


def _rms_mod_bwd(dh, dres, r, xhat, n, gn, scale):
    dshift = jnp.sum(dh, axis=0, keepdims=True)
    dscale = jnp.sum(dh * n, axis=0, keepdims=True)
    dn = dh * (1.0 + scale)
    dgn = jnp.sum(dn * xhat, axis=0, keepdims=True)
    dxhat = dn * gn
    dx = dres + r * (dxhat - xhat * jnp.mean(dxhat * xhat, axis=-1, keepdims=True))
    return dx, dshift, dscale, dgn


def _final_norm_loss(x, gf, target):
    r = lax.rsqrt(jnp.mean(x * x, axis=-1, keepdims=True) + EPS)
    xhat = x * r
    e = xhat * gf - target
    part = 0.5 * jnp.sum(jnp.sum(e * e, axis=-1, keepdims=True), axis=0, keepdims=True) / D
    dy = e / D
    dgf = jnp.sum(dy * xhat, axis=0, keepdims=True)
    dxhat = dy * gf
    dx = r * (dxhat - xhat * jnp.mean(dxhat * xhat, axis=-1, keepdims=True))
    return dx, dgf, part


def _rows3(a, b, c, width):
    row = lax.broadcasted_iota(jnp.int32, (8, width), 0)
    z = jnp.zeros((8, width), F32)
    return jnp.where(row == 0, a, z) + jnp.where(row == 1, b, z) + jnp.where(row == 2, c, z)


def _sigmoid(x):
    return 0.5 * jnp.tanh(0.5 * x) + 0.5


def _gelu(x):
    t = jnp.tanh(GELU_C * (x + GELU_A * x * x * x))
    return 0.5 * x * (1.0 + t), t


def _gelu_grad(x, t):
    return 0.5 * (1.0 + t) + 0.5 * x * (1.0 - t * t) * GELU_C * (1.0 + 3.0 * GELU_A * x * x)


def _adamw(w, g, m, v):
    m = B1 * m + (1.0 - B1) * g
    v = B2 * v + (1.0 - B2) * (g * g)
    m_hat = m / BC1
    v_hat = v / BC2
    delta = -LR * (m_hat / (jnp.sqrt(v_hat) + AEPS) + WD * w)
    return delta, m, v


def _cast_shards(shards):
    n = len(shards)

    def body(*refs):
        for a in range(n):
            refs[n + a][...] = refs[a][...].astype(BF)

    return pl.pallas_call(
        body, name="cast_shards", out_shape=[jax.ShapeDtypeStruct(s.shape, BF) for s in shards],
        in_specs=[VMEM] * n, out_specs=[VMEM] * n, compiler_params=_cparams(40),
    )(*shards)


def _all_gather(shards, name):
    n = len(shards)

    def body(*refs):
        xs, outs, sems = refs[:n], refs[n:2 * n], refs[2 * n:]
        for phase in ("start", "forward", "finish"):
            _gather_phase(phase, xs, outs, sems)

    return pl.pallas_call(
        body, name=name,
        out_shape=[jax.ShapeDtypeStruct((NDEV,) + s.shape, s.dtype) for s in shards],
        in_specs=[HBM] * n, out_specs=[HBM] * n, scratch_shapes=_comm_sems(n),
    )(*shards)


def _ada_forward(c8, w_ada, b8):
    wc = w_ada.shape[1]

    def body(c8_ref, w_ref, b8_ref, mod_ref, cact_ref, call_ref, mall_ref, send_sems, recv_sems):
        me = _position()
        my = _slot(me)
        row = lax.broadcasted_iota(jnp.int32, (8, 1), 0)
        call_ref[my] = c8_ref[...]
        sends = []
        for d in range(1, NDEV):
            to = _flip(me, d)
            sends.append(_remote(c8_ref, call_ref.at[my], send_sems.at[0, d - 1], recv_sems.at[0, d - 1], to))
        for cp in sends:
            cp.start()
        for cp in sends:
            cp.wait()
        c_all = jnp.zeros((8, D), F32)
        for k in range(NDEV):
            c_all = c_all + jnp.where(row == k, call_ref[k], 0.0)
        cact = c_all * jax.nn.sigmoid(c_all)
        cact_ref[...] = cact
        part = _dot(cact.astype(BF), w_ref[...].astype(BF))
        mall_ref[my] = part
        sends = []
        for d in range(1, NDEV):
            to = _flip(me, d)
            sends.append(_remote(mall_ref.at[my], mall_ref.at[my], send_sems.at[1, d - 1], recv_sems.at[1, d - 1], to))
        for cp in sends:
            cp.start()
        for cp in sends:
            cp.wait()
        out = jnp.zeros((8, wc), F32)
        for k in range(NDEV):
            piece = jnp.sum(jnp.where(row == my, mall_ref[k], 0.0), axis=0, keepdims=True)
            out = out + jnp.where(row == k, piece, 0.0)
        mod_ref[...] = out + b8_ref[...]

    return pl.pallas_call(
        body, name="ada_forward",
        out_shape=[jax.ShapeDtypeStruct((8, wc), F32), jax.ShapeDtypeStruct((8, D), F32)],
        in_specs=[VMEM, VMEM, VMEM], out_specs=[VMEM, VMEM],
        scratch_shapes=[pltpu.VMEM((NDEV, 8, D), F32), pltpu.VMEM((NDEV, 8, wc), F32),
                        pltpu.SemaphoreType.DMA((2, 7)), pltpu.SemaphoreType.DMA((2, 7))],
        compiler_params=_cparams(32),
    )(c8, w_ada, b8)


MATS = ("w_spatial", "w_pool", "b_spatial")
VECS = ("norm_ffn1_g", "norm_mix_g", "norm_ffn2_g", "norm_final_g", "pool_scale", "gmlp_ln_g", "gmlp_ln_b", "b_ada")
VEC_WIDTH = dict(norm_ffn1_g=D, norm_mix_g=D, norm_ffn2_g=D, norm_final_g=D, pool_scale=DP, gmlp_ln_g=DG,
                 gmlp_ln_b=DG, b_ada=9 * D)
MAT_ROWS = 1600
MAT_SLICE = MAT_ROWS // NDEV
VEC_LANES = sum(VEC_WIDTH.values()) + 128
DMOD_AT = VEC_LANES - 128 - 9 * D
SMALL = MATS + VECS


def _small_reduce(g_ws, g_wp, dbias, st1, st2, st3, st_f, vec2, dgate1, dgate3):
    wc = 9 * D // NDEV

    def body(g_ws_ref, g_wp_ref, dbias_ref, st1_ref, st2_ref, st3_ref, stf_ref, vec2_ref, dg1_ref, dg3_ref,
             tot_ref, rsum_ref, dmine_ref,
             pack_ref, rs_ref, ag_ref, rv_ref, dmp_ref, dw_ref, send_sems, recv_sems):
        me = _position()
        my = _slot(me)

        pack_ref[0:1024, :] = g_ws_ref[...].reshape(1024, 128)
        pack_ref[1024:1536, :] = g_wp_ref[...].reshape(512, 128)
        ch = lax.broadcasted_iota(jnp.int32, (DG, 128), 0)
        hd = lax.broadcasted_iota(jnp.int32, (DG, 128), 1)
        sel = jnp.where(ch // 64 == hd, 1.0, 0.0).astype(F32)
        heads = jnp.dot(dbias_ref[...], sel, preferred_element_type=F32, precision=lax.Precision.HIGHEST)
        pack_ref[1536:1544, :] = heads.T[0:8, :]
        pack_ref[1544:MAT_ROWS, :] = jnp.zeros((MAT_ROWS - 1544, 128), F32)
        dgate1 = dg1_ref[0:1, :] + dg1_ref[8:9, :]
        dgate3 = dg3_ref[0:1, :] + dg3_ref[8:9, :]
        row = jnp.concatenate(
            [st1_ref[2:3, :], st2_ref[2:3, :], st3_ref[2:3, :], stf_ref[0:1, :],
             vec2_ref[0:1, :], vec2_ref[1:2, :], vec2_ref[2:3, :],
             st1_ref[0:1, :], st1_ref[1:2, :], dgate1, st2_ref[0:1, :], st2_ref[1:2, :], st2_ref[3:4, :],
             st3_ref[0:1, :], st3_ref[1:2, :], dgate3, stf_ref[1:2, 0:128]], axis=1)
        rv_ref[my] = row
        for k in range(NDEV):
            dmp_ref[k] = row[:, DMOD_AT + wc * k:DMOD_AT + wc * (k + 1)]
        dw_ref[my] = dmp_ref[my]
        rs_ref[my] = pack_ref[pl.ds(pl.multiple_of(my * MAT_SLICE, 8), MAT_SLICE), :]

        first = []
        for d in range(1, NDEV):
            to = _flip(me, d)
            theirs = pl.ds(pl.multiple_of(_slot(to) * MAT_SLICE, 8), MAT_SLICE)
            first.append(_remote(pack_ref.at[theirs, :], rs_ref.at[my], send_sems.at[0, d - 1], recv_sems.at[0, d - 1], to))
            first.append(_remote(dmp_ref.at[_slot(to)], dw_ref.at[my], send_sems.at[1, d - 1], recv_sems.at[1, d - 1], to))
            first.append(_remote(rv_ref.at[my], rv_ref.at[my], send_sems.at[2, d - 1], recv_sems.at[2, d - 1], to))
        for cp in first:
            cp.start()
        for cp in first:
            cp.wait()
        red = rs_ref[0]
        for k in range(1, NDEV):
            red = red + rs_ref[k]
        ag_ref[my] = red
        second = []
        for d in range(1, NDEV):
            to = _flip(me, d)
            second.append(_remote(ag_ref.at[my], ag_ref.at[my], send_sems.at[3, d - 1], recv_sems.at[3, d - 1], to))
        for cp in second:
            cp.start()

        rsum = rv_ref[0]
        for k in range(1, NDEV):
            rsum = rsum + rv_ref[k]
        rsum_ref[...] = rsum
        r8 = lax.broadcasted_iota(jnp.int32, (8, 1), 0)
        dmine = jnp.zeros((8, wc), F32)
        for k in range(NDEV):
            dmine = dmine + jnp.where(r8 == k, dw_ref[k], 0.0)
        dmine_ref[...] = dmine

        for cp in second:
            cp.wait()
        for k in range(NDEV):
            tot_ref[k * MAT_SLICE:(k + 1) * MAT_SLICE, :] = ag_ref[k]

    return pl.pallas_call(
        body, name="small_reduce",
        out_shape=[jax.ShapeDtypeStruct((MAT_ROWS, 128), F32), jax.ShapeDtypeStruct((1, VEC_LANES), F32),
                   jax.ShapeDtypeStruct((8, wc), F32)],
        in_specs=[VMEM] * 10, out_specs=[VMEM] * 3,
        scratch_shapes=[pltpu.VMEM((MAT_ROWS, 128), F32), pltpu.VMEM((NDEV, MAT_SLICE, 128), F32),
                        pltpu.VMEM((NDEV, MAT_SLICE, 128), F32),
                        pltpu.VMEM((NDEV, 1, VEC_LANES), F32), pltpu.VMEM((NDEV, 1, wc), F32),
                        pltpu.VMEM((NDEV, 1, wc), F32),
                        pltpu.SemaphoreType.DMA((4, 7)), pltpu.SemaphoreType.DMA((4, 7))],
        compiler_params=_cparams(32),
    )(g_ws, g_wp, dbias, st1, st2, st3, st_f, vec2, dgate1, dgate3)


def _small_update(tot, rsum, params):
    flat = [a for k in SMALL for a in params[k]]
    n_in = 2 + len(flat)

    def body(*refs):
        tot_ref, rsum_ref = refs[:2]
        p_refs = refs[2:n_in]
        o_refs = refs[n_in:n_in + 4 * len(SMALL)]
        loss_ref = refs[n_in + 4 * len(SMALL)]
        loss_ref[...] = rsum_ref[:, VEC_LANES - 128:VEC_LANES]

        def update(idx, g):
            w_ref, m_ref, v_ref = p_refs[3 * idx:3 * idx + 3]
            g_out, d_out, m_out, v_out = o_refs[4 * idx:4 * idx + 4]
            g = g.reshape(w_ref.shape)
            g_out[...] = g
            d_out[...], m_out[...], v_out[...] = _adamw(w_ref[...], g, m_ref[...], v_ref[...])

        update(0, tot_ref[0:1024, :])
        update(1, tot_ref[1024:1536, :])
        update(2, tot_ref[1536:1544, :])
        at = 0
        for idx, k in enumerate(VECS):
            update(3 + idx, rsum_ref[:, at:at + VEC_WIDTH[k]])
            at += VEC_WIDTH[k]

    outs = []
    for k in SMALL:
        outs += [jax.ShapeDtypeStruct(params[k][0].shape, F32)] * 4
    outs += [jax.ShapeDtypeStruct((1, 128), F32)]
    res = pl.pallas_call(
        body, name="small_update", out_shape=outs,
        in_specs=[VMEM] * n_in, out_specs=[VMEM] * len(outs), compiler_params=_cparams(32),
    )(tot, rsum, *flat)
    return {k: tuple(res[4 * i:4 * i + 4]) for i, k in enumerate(SMALL)}, res[-1]


def _sum_adamw(recv, w, m, v, tr, name):
    R, C = w.shape

    def body(r_ref, w_ref, m_ref, v_ref, g_ref, d_ref, nm_ref, nv_ref):
        g = r_ref[0].astype(F32)
        for k in range(1, NDEV):
            g = g + r_ref[k].astype(F32)
        g_ref[...] = g
        d_ref[...], nm_ref[...], nv_ref[...] = _adamw(w_ref[...], g, m_ref[...], v_ref[...])

    blk = pl.BlockSpec((tr, C), lambda i: (i, 0))
    out = jax.ShapeDtypeStruct((R, C), F32)
    return pl.pallas_call(
        body, name=name, grid=(R // tr,), out_shape=[out] * 4,
        in_specs=[pl.BlockSpec((NDEV, tr, C), lambda i: (0, i, 0)), blk, blk, blk], out_specs=[blk] * 4,
        compiler_params=_cparams(48, ("arbitrary",)),
    )(recv, w, m, v)


def _ada_update(cact_all, dmine, w, m, v, tr):
    R, C = w.shape

    def body(c_ref, dm_ref, w_ref, m_ref, v_ref, g_ref, d_ref, nm_ref, nv_ref):
        g = _dot_tn(c_ref[...].astype(BF), dm_ref[...].astype(BF))
        g_ref[...] = g
        d_ref[...], nm_ref[...], nv_ref[...] = _adamw(w_ref[...], g, m_ref[...], v_ref[...])

    blk = pl.BlockSpec((tr, C), lambda i: (i, 0))
    out = jax.ShapeDtypeStruct((R, C), F32)
    return pl.pallas_call(
        body, name="update_w_ada", grid=(R // tr,), out_shape=[out] * 4,
        in_specs=[pl.BlockSpec((8, tr), lambda i: (0, i)), pl.BlockSpec((8, C), lambda i: (0, 0)), blk, blk, blk],
        out_specs=[blk] * 4,
        compiler_params=_cparams(48, ("arbitrary",)),
    )(cact_all, dmine, w, m, v)


FC = F // 2


def _ffn_fwd(x, mod, gn, w_in_t, w_out, sub, name, gather=(), loss=None):
    S = x.shape[0]
    T = min(T_FFN, S)
    nS, nJ = S // T, F // FC
    ng = len(gather)
    nl = 2 if loss else 0
    forward_step = (3 * nS) // 4

    def body(*refs):
        x_ref, mod_ref, gn_ref, wg_ref, wu_ref, wo_ref = refs[:6]
        gf_ref, t_ref = refs[6:6 + nl] if loss else (None, None)
        shards = refs[6 + nl:6 + nl + ng]
        at = 6 + nl + ng
        xo_ref, gu_ref, h_ref = refs[at:at + 3]
        gathered = refs[at + 3:at + 3 + ng]
        at += 3 + ng
        st_ref = refs[at] if loss else None
        at += nl // 2
        acc_scr = refs[at]
        sems = refs[at + 1:]
        i, j = pl.program_id(0), pl.program_id(1)

        if ng:
            @pl.when((i == 0) & (j == 0))
            def _():
                _gather_phase("start", shards, gathered, sems)

            @pl.when((i == forward_step) & (j == 0))
            def _():
                _gather_phase("forward", shards, gathered, sems)

        @pl.when(j == 0)
        def _():
            _, _, _, h = _rms_mod(x_ref[...], gn_ref[...], mod_ref[3 * sub:3 * sub + 1, :],
                                  mod_ref[3 * sub + 1:3 * sub + 2, :])
            h_ref[...] = h.astype(BF)
            acc_scr[...] = jnp.zeros_like(acc_scr)

        h = h_ref[...]
        g = _dot_nt(h, wg_ref[0])
        u = _dot_nt(h, wu_ref[0])
        gu_ref[0] = g.astype(BF)
        gu_ref[1] = u.astype(BF)
        a = (g * _sigmoid(g) * u).astype(BF)
        acc_scr[...] += _dot(a, wo_ref[...])

        @pl.when(j == nJ - 1)
        def _():
            xo = x_ref[...] + (0.5 * mod_ref[3 * sub + 2:3 * sub + 3, :]) * acc_scr[...]
            if not loss:
                xo_ref[...] = xo
            else:
                dx, dgf, part = _final_norm_loss(xo, gf_ref[...], t_ref[...])
                xo_ref[...] = dx
                upd = _rows3(dgf, jnp.broadcast_to(part, (1, D)), jnp.zeros((1, D), F32), D)

                @pl.when(i == 0)
                def _():
                    st_ref[...] = upd

                @pl.when(i > 0)
                def _():
                    st_ref[...] += upd

        if ng:
            @pl.when((i == nS - 1) & (j == nJ - 1))
            def _():
                _gather_phase("finish", shards, gathered, sems)

    tile = pl.BlockSpec((T, D), lambda i, j: (i, 0))
    return pl.pallas_call(
        body, name=name, grid=(nS, nJ),
        out_shape=[jax.ShapeDtypeStruct((S, D), F32), jax.ShapeDtypeStruct((2, S, F), BF),
                   jax.ShapeDtypeStruct((S, D), BF)]
                  + [jax.ShapeDtypeStruct((NDEV,) + s.shape, s.dtype) for s in gather]
                  + ([jax.ShapeDtypeStruct((8, D), F32)] if loss else []),
        in_specs=[tile,
                  pl.BlockSpec((9, D), lambda i, j: (0, 0)),
                  pl.BlockSpec((1, D), lambda i, j: (0, 0)),
                  pl.BlockSpec((1, FC, D), lambda i, j: (0, j, 0)),
                  pl.BlockSpec((1, FC, D), lambda i, j: (1, j, 0)),
                  pl.BlockSpec((FC, D), lambda i, j: (j, 0))]
                 + ([pl.BlockSpec((1, D), lambda i, j: (0, 0)), tile] if loss else []) + [HBM] * ng,
        out_specs=[tile, pl.BlockSpec((2, T, FC), lambda i, j: (0, i, j)), tile] + [HBM] * ng
                  + ([pl.BlockSpec((8, D), lambda i, j: (0, 0))] if loss else []),
        scratch_shapes=[pltpu.VMEM((T, D), F32)] + (_comm_sems(ng) if ng else []),
        compiler_params=_cparams(56, ("arbitrary", "arbitrary")),
    )(x, mod, gn, w_in_t, w_in_t, w_out, *(loss or ()), *gather)


def _ffn_bwd_hidden(dx, mod, gu, w_out, sub, name, exchange=()):
    S = dx.shape[0]
    T = min(T_FFN, S)
    nS, nJ = S // T, F // FC
    ne = len(exchange)

    def body(*refs):
        dx_ref, mod_ref, gu_ref, wo_ref = refs[:4]
        sendbufs = refs[4:4 + ne]
        dgu_ref, gw_ref, dgate_ref = refs[4 + ne:7 + ne]
        recvbufs = refs[7 + ne:7 + 2 * ne]
        acc_scr = refs[7 + 2 * ne]
        sems = refs[8 + 2 * ne:]
        j, i = pl.program_id(0), pl.program_id(1)

        if ne:
            @pl.when((i == 0) & (j == 0))
            def _():
                _exchange_phase("start", sendbufs, recvbufs, sems)

        gate = mod_ref[3 * sub + 2:3 * sub + 3, :]
        dx = dx_ref[...]
        da = _dot_nt((dx * (0.5 * gate)).astype(BF), wo_ref[...])
        g = gu_ref[0].astype(F32)
        u = gu_ref[1].astype(F32)
        sg = _sigmoid(g)
        s = g * sg
        dgu_ref[0] = (da * u * (sg * (1.0 + g * (1.0 - sg)))).astype(BF)
        dgu_ref[1] = (da * s).astype(BF)
        contrib = _dot_tn((s * u).astype(BF), dx.astype(BF))

        @pl.when(i == 0)
        def _():
            acc_scr[...] = contrib

        @pl.when(i > 0)
        def _():
            acc_scr[...] += contrib

        @pl.when(i == nS - 1)
        def _():
            acc = acc_scr[...]
            dgate = 0.5 * jnp.sum(acc * wo_ref[...].astype(F32), axis=0, keepdims=True)
            dgate_ref[...] = jnp.broadcast_to(dgate, (8, D))
            gw_ref[...] = (acc * (0.5 * gate)).astype(BF)

        if ne:
            @pl.when((i == nS - 1) & (j == nJ - 1))
            def _():
                _exchange_phase("wait", sendbufs, recvbufs, sems)

    return pl.pallas_call(
        body, name=name, grid=(nJ, nS),
        out_shape=[jax.ShapeDtypeStruct((2, S, F), BF), jax.ShapeDtypeStruct((F, D), BF),
                   jax.ShapeDtypeStruct((8 * nJ, D), F32)] + _like(exchange),
        in_specs=[pl.BlockSpec((T, D), lambda j, i: (i, 0)),
                  pl.BlockSpec((9, D), lambda j, i: (0, 0)),
                  pl.BlockSpec((2, T, FC), lambda j, i: (0, i, j)),
                  pl.BlockSpec((FC, D), lambda j, i: (j, 0))] + [HBM] * ne,
        out_specs=[pl.BlockSpec((2, T, FC), lambda j, i: (0, i, j)),
                   pl.BlockSpec((FC, D), lambda j, i: (j, 0)),
                   pl.BlockSpec((8, D), lambda j, i: (j, 0))] + [HBM] * ne,
        scratch_shapes=[pltpu.VMEM((FC, D), F32)] + (_comm_sems(ne) if ne else []),
        compiler_params=_cparams(56, ("arbitrary", "arbitrary")),
    )(dx, mod, gu, w_out, *exchange)


def _ffn_bwd_input(dgu, w_in_t, x, dx, mod, gn, sub, name, exchange=()):
    S = x.shape[0]
    T = min(T_FFN, S)
    nS = S // T
    ne = len(exchange)
    NC = 256
    chunks = [slice(k * NC, (k + 1) * NC) for k in range(D // NC)]

    def body(*refs):
        dgu_ref, w_ref, x_ref, dx_ref, mod_ref, gn_ref = refs[:6]
        sendbufs = refs[6:6 + ne]
        dxin_ref, st_ref = refs[6 + ne:8 + ne]
        recvbufs = refs[8 + ne:8 + 2 * ne]
        dxh_scr = refs[8 + 2 * ne]
        sems = refs[9 + 2 * ne:]
        i = pl.program_id(0)

        if ne:
            @pl.when(i == 0)
            def _():
                _exchange_phase("start", sendbufs, recvbufs, sems)

        gn = gn_ref[...]
        scale = mod_ref[3 * sub + 1:3 * sub + 2, :]
        r, xhat, n, _ = _rms_mod(x_ref[...], gn, mod_ref[3 * sub:3 * sub + 1, :], scale)
        dg = dgu_ref[0]
        du = dgu_ref[1]
        rowsum = jnp.zeros((T, 1), F32)
        dshift, dscale, dgn = [], [], []
        for cols in chunks:
            dh = _dot(dg, w_ref[0, :, cols]) + _dot(du, w_ref[1, :, cols])
            dshift.append(jnp.sum(dh, axis=0, keepdims=True))
            dscale.append(jnp.sum(dh * n[:, cols], axis=0, keepdims=True))
            dn = dh * (1.0 + scale[:, cols])
            dgn.append(jnp.sum(dn * xhat[:, cols], axis=0, keepdims=True))
            dxhat = dn * gn[:, cols]
            rowsum = rowsum + jnp.sum(dxhat * xhat[:, cols], axis=-1, keepdims=True)
            dxh_scr[:, cols] = dxhat
        dxin_ref[...] = dx_ref[...] + r * (dxh_scr[...] - xhat * (rowsum / D))
        cat = lambda parts: jnp.concatenate(parts, axis=1)
        upd = _rows3(cat(dshift), cat(dscale), cat(dgn), D)

        @pl.when(i == 0)
        def _():
            st_ref[...] = upd

        @pl.when(i > 0)
        def _():
            st_ref[...] += upd

        if ne:
            @pl.when(i == nS - 1)
            def _():
                _exchange_phase("wait", sendbufs, recvbufs, sems)

    tile = pl.BlockSpec((T, D), lambda i: (i, 0))
    return pl.pallas_call(
        body, name=name, grid=(nS,),
        out_shape=[jax.ShapeDtypeStruct((S, D), F32), jax.ShapeDtypeStruct((8, D), F32)] + _like(exchange),
        in_specs=[pl.BlockSpec((2, T, F), lambda i: (0, i, 0)),
                  pl.BlockSpec((2, F, D), lambda i: (0, 0, 0), pipeline_mode=pl.Buffered(1)),
                  tile, tile,
                  pl.BlockSpec((9, D), lambda i: (0, 0)),
                  pl.BlockSpec((1, D), lambda i: (0, 0))] + [HBM] * ne,
        out_specs=[tile, pl.BlockSpec((8, D), lambda i: (0, 0))] + [HBM] * ne,
        scratch_shapes=[pltpu.VMEM((T, D), F32)] + (_comm_sems(ne) if ne else []),
        compiler_params=_cparams(60, ("arbitrary",)),
    )(dgu, w_in_t, x, dx, mod, gn, *exchange)


def _ffn_bwd_win(h, dgu, name, exchange=()):
    S = h.shape[0]
    T = min(T_WIN, S)
    nS, nJ = S // T, F // FC
    ne = len(exchange)

    def body(*refs):
        h_ref, dgu_ref = refs[:2]
        sendbufs = refs[2:2 + ne]
        out_ref = refs[2 + ne]
        recvbufs = refs[3 + ne:3 + 2 * ne]
        acc_scr = refs[3 + 2 * ne]
        sems = refs[4 + 2 * ne:]
        p, j, i = pl.program_id(0), pl.program_id(1), pl.program_id(2)

        if ne:
            @pl.when((p == 0) & (j == 0) & (i == 0))
            def _():
                _exchange_phase("start", sendbufs, recvbufs, sems)

        contrib = _dot_tn(dgu_ref[0], h_ref[...])

        @pl.when(i == 0)
        def _():
            acc_scr[...] = contrib

        @pl.when(i > 0)
        def _():
            acc_scr[...] += contrib

        @pl.when(i == nS - 1)
        def _():
            out_ref[0] = acc_scr[...].astype(BF)

        if ne:
            @pl.when((p == 1) & (j == nJ - 1) & (i == nS - 1))
            def _():
                _exchange_phase("wait", sendbufs, recvbufs, sems)

    return pl.pallas_call(
        body, name=name, grid=(2, nJ, nS),
        out_shape=[jax.ShapeDtypeStruct((2, F, D), BF)] + _like(exchange),
        in_specs=[pl.BlockSpec((T, D), lambda p, j, i: (i, 0)),
                  pl.BlockSpec((1, T, FC), lambda p, j, i: (p, i, j))] + [HBM] * ne,
        out_specs=[pl.BlockSpec((1, FC, D), lambda p, j, i: (p, j, 0))] + [HBM] * ne,
        scratch_shapes=[pltpu.VMEM((FC, D), F32)] + (_comm_sems(ne) if ne else []),
        compiler_params=_cparams(56, ("arbitrary", "arbitrary", "arbitrary")),
    )(h, dgu, *exchange)


def _pool_counts(pos0, T):
    pos = pos0 + lax.broadcasted_iota(jnp.int32, (T, 1), 0)
    return [jnp.minimum(pos + 1, w).astype(F32) for w in WINDOWS]


def _pool_fwd(xa, halo, ext_scr, cnts, T):
    ext_scr[0:HALO, :] = halo
    ext_scr[HALO:HALO + T, :] = xa
    out = []
    for gi, w in enumerate(WINDOWS):
        cols = slice(128 * gi, 128 * gi + 128)
        acc = xa[:, cols]
        for k in range(1, w):
            acc = acc + ext_scr[HALO - k:HALO - k + T, cols]
        out.append(acc / cnts[gi] - xa[:, cols])
    return out


def _sgu_fwd(vnb, ws_ref, sv_scr, T):
    lane = lax.broadcasted_iota(jnp.int32, (CHUNK, 128), 1)
    for n in range(T // CHUNK):
        rows = slice(n * CHUNK, (n + 1) * CHUNK)
        for b in range(DG // 128):
            cols = slice(128 * b, 128 * b + 128)
            vb = vnb[rows, cols]
            sv_scr[rows, cols] = jnp.where(lane < 64, _dot(ws_ref[2 * b], vb), _dot(ws_ref[2 * b + 1], vb))


def _mix_fwd(x, mod, gn, wmi, wmo, wp, ps, lg, lb, ws, bias, name):
    S = x.shape[0]
    T = min(T_MIX, S)

    def body(x_ref, mod_ref, gn_ref, wmi_ref, wmo_ref, wp_ref, ps_ref, lg_ref, lb_ref, ws_ref, bias_ref,
             xo_ref, carry_scr, ext_scr, sv_scr, ycat_scr):
        i = pl.program_id(0)

        @pl.when(i == 0)
        def _():
            carry_scr[...] = jnp.zeros_like(carry_scr)

        x = x_ref[...]
        _, _, _, h = _rms_mod(x, gn_ref[...], mod_ref[3:4, :], mod_ref[4:5, :])
        proj = _dot_nt(h.astype(BF), wmi_ref[...])
        xa = proj[:, 0:DP]
        p = _pool_fwd(xa, carry_scr[...], ext_scr, _pool_counts(i * T, T), T)
        carry_scr[...] = xa[T - HALO:T, :]
        for gi in range(4):
            cols = slice(128 * gi, 128 * gi + 128)
            ycat_scr[:, cols] = (_dot(p[gi].astype(BF), wp_ref[gi]) * ps_ref[:, cols]).astype(BF)
        u, _ = _gelu(proj[:, DP:DP + DG])
        v, _ = _gelu(proj[:, DP + DG:DPROJ])
        mu = jnp.mean(v, axis=-1, keepdims=True)
        vc = v - mu
        rstd = lax.rsqrt(jnp.mean(vc * vc, axis=-1, keepdims=True) + EPS)
        vn = vc * rstd * lg_ref[...] + lb_ref[...]
        _sgu_fwd(vn.astype(BF), ws_ref, sv_scr, T)
        for n in range(T // CHUNK):
            rows = slice(n * CHUNK, (n + 1) * CHUNK)
            ycat_scr[rows, DP:D] = (u[rows, :] * (sv_scr[rows, :] + bias_ref[...])).astype(BF)
        xo_ref[...] = x + mod_ref[5:6, :] * _dot(ycat_scr[...], wmo_ref[...])

    full = lambda shape: pl.BlockSpec(shape, lambda i: (0,) * len(shape))
    return pl.pallas_call(
        body, name=name, grid=(S // T,),
        out_shape=jax.ShapeDtypeStruct((S, D), F32),
        in_specs=[pl.BlockSpec((T, D), lambda i: (i, 0)), full((9, D)), full((1, D)), full((DPROJ, D)), full((D, D)),
                  full((4, 128, 128)), full((1, DP)), full((1, DG)), full((1, DG)), full((8, CHUNK, CHUNK)),
                  full((CHUNK, DG))],
        out_specs=pl.BlockSpec((T, D), lambda i: (i, 0)),
        scratch_shapes=[pltpu.VMEM((HALO, DP), F32), pltpu.VMEM((T + HALO, DP), F32), pltpu.VMEM((T, DG), F32),
                        pltpu.VMEM((T, D), BF)],
        compiler_params=_cparams(48, ("arbitrary",)),
    )(x, mod, gn, wmi, wmo, wp, ps, lg, lb, ws, bias)


def _mix_bwd(x, dxo, mod, gn, wmi, wmo, wp, ps, lg, lb, ws, bias, name, exchange=()):
    S = x.shape[0]
    T = min(T_MIX, S)
    nS = S // T
    hb = T // HALO
    ne = len(exchange)

    def body(*refs):
        (x_ref, xh_ref, dxo_ref, mod_ref, gn_ref, wmi_ref, wmo_ref, wp_ref, ps_ref, lg_ref, lb_ref, ws_ref,
         bias_ref) = refs[:13]
        sendbufs = refs[13:13 + ne]
        dxi_ref, gwmi_out, gwmo_out, gwp_ref, gws_ref, st_ref, vec_ref, dbias_ref = refs[13 + ne:21 + ne]
        recvbufs = refs[21 + ne:21 + 2 * ne]
        (carry_scr, ext_scr, qext_scr, sv_scr, dvn_scr, ycat_scr, dproj_scr, gwmi_ref,
         gwmo_ref) = refs[21 + 2 * ne:30 + 2 * ne]
        sems = refs[30 + 2 * ne:]
        i = pl.program_id(0)
        t = nS - 1 - i
        gn = gn_ref[...]
        shift, scale, gate = mod_ref[3:4, :], mod_ref[4:5, :], mod_ref[5:6, :]

        @pl.when(i == 0)
        def _():
            if ne:
                _exchange_phase("start", sendbufs, recvbufs, sems)
            carry_scr[...] = jnp.zeros_like(carry_scr)
            gwmi_ref[...] = jnp.zeros_like(gwmi_ref)
            gwmo_ref[...] = jnp.zeros_like(gwmo_ref)
            gwp_ref[...] = jnp.zeros_like(gwp_ref)
            gws_ref[...] = jnp.zeros_like(gws_ref)
            st_ref[...] = jnp.zeros_like(st_ref)
            vec_ref[...] = jnp.zeros_like(vec_ref)
            dbias_ref[...] = jnp.zeros_like(dbias_ref)

        x = x_ref[...]
        dxo = dxo_ref[...]
        r, xhat, n, h = _rms_mod(x, gn, shift, scale)
        hbf = h.astype(BF)
        proj = _dot_nt(hbf, wmi_ref[...])
        xa = proj[:, 0:DP]
        zu = proj[:, DP:DP + DG]
        zv = proj[:, DP + DG:DPROJ]
        _, _, _, hh = _rms_mod(xh_ref[...], gn, shift, scale)
        halo = _dot_nt(hh.astype(BF), wmi_ref[0:DP, :])
        halo = jnp.where(t == 0, 0.0, halo)
        cnts = _pool_counts(t * T, T)
        p = _pool_fwd(xa, halo, ext_scr, cnts, T)
        m = []
        for gi in range(4):
            cols = slice(128 * gi, 128 * gi + 128)
            m.append(_dot(p[gi].astype(BF), wp_ref[gi]))
            ycat_scr[:, cols] = (m[gi] * ps_ref[:, cols]).astype(BF)
        u, tu = _gelu(zu)
        v, tv = _gelu(zv)
        mu = jnp.mean(v, axis=-1, keepdims=True)
        vc = v - mu
        rstd = lax.rsqrt(jnp.mean(vc * vc, axis=-1, keepdims=True) + EPS)
        vhat = vc * rstd
        lg = lg_ref[...]
        vnb = (vhat * lg + lb_ref[...]).astype(BF)
        _sgu_fwd(vnb, ws_ref, sv_scr, T)
        for nck in range(T // CHUNK):
            rows = slice(nck * CHUNK, (nck + 1) * CHUNK)
            sv_scr[rows, :] = sv_scr[rows, :] + bias_ref[...]
        sv = sv_scr[...]
        ycat_scr[:, DP:D] = (u * sv).astype(BF)

        gwmo_ref[...] += _dot_tn(ycat_scr[...], dxo.astype(BF))
        dyc = _dot_nt((dxo * gate).astype(BF), wmo_ref[...])
        dya = dyc[:, 0:DP]
        dyb = dyc[:, DP:D]

        dps = []
        dp = []
        for gi in range(4):
            cols = slice(128 * gi, 128 * gi + 128)
            dps.append(jnp.sum(dya[:, cols] * m[gi], axis=0, keepdims=True))
            dm = (dya[:, cols] * ps_ref[:, cols]).astype(BF)
            gwp_ref[gi] += _dot_tn(p[gi].astype(BF), dm)
            dp.append(_dot_nt(dm, wp_ref[gi]))
            qext_scr[0:T, cols] = dp[gi] / cnts[gi]
        qext_scr[T:T + HALO, :] = carry_scr[...]
        for gi, w in enumerate(WINDOWS):
            cols = slice(128 * gi, 128 * gi + 128)
            acc = qext_scr[0:T, cols]
            for k in range(1, w):
                acc = acc + qext_scr[k:k + T, cols]
            dproj_scr[:, cols] = (acc - dp[gi]).astype(BF)
        carry_scr[...] = qext_scr[0:HALO, :]

        du = dyb * sv
        dsv = dyb * u
        lane = lax.broadcasted_iota(jnp.int32, (CHUNK, 128), 1)
        dbias = jnp.zeros((CHUNK, DG), F32)
        for nck in range(T // CHUNK):
            rows = slice(nck * CHUNK, (nck + 1) * CHUNK)
            dbias = dbias + dsv[rows, :]
            for b in range(DG // 128):
                cols = slice(128 * b, 128 * b + 128)
                dsvb = dsv[rows, cols]
                vb = vnb[rows, cols]
                gws_ref[2 * b] += _dot_nt(jnp.where(lane < 64, dsvb, 0.0).astype(BF), vb)
                gws_ref[2 * b + 1] += _dot_nt(jnp.where(lane < 64, 0.0, dsvb).astype(BF), vb)
                dsvbb = dsvb.astype(BF)
                dvn_scr[rows, cols] = jnp.where(lane < 64, _dot_tn(ws_ref[2 * b], dsvbb),
                                                _dot_tn(ws_ref[2 * b + 1], dsvbb))
        dbias_ref[...] += dbias
        dvn = dvn_scr[...]
        dlg = jnp.sum(dvn * vhat, axis=0, keepdims=True)
        dlb = jnp.sum(dvn, axis=0, keepdims=True)
        dvhat = dvn * lg
        dv = rstd * (dvhat - jnp.mean(dvhat, axis=-1, keepdims=True)
                     - vhat * jnp.mean(dvhat * vhat, axis=-1, keepdims=True))
        dproj_scr[:, DP:DP + DG] = (du * _gelu_grad(zu, tu)).astype(BF)
        dproj_scr[:, DP + DG:DPROJ] = (dv * _gelu_grad(zv, tv)).astype(BF)
        vec_ref[...] += _rows3(jnp.concatenate(dps, axis=1), dlg, dlb, DP)

        dproj = dproj_scr[...]
        gwmi_ref[...] += _dot_tn(dproj, hbf)
        dh = _dot(dproj, wmi_ref[...])
        dxi, dshift, dscale, dgn = _rms_mod_bwd(dh, dxo, r, xhat, n, gn, scale)
        dxi_ref[...] = dxi
        st_ref[...] += _rows3(dshift, dscale, dgn, D)

        @pl.when(i == nS - 1)
        def _():
            acc = gwmo_ref[...]
            dgate = jnp.sum(acc * wmo_ref[...].astype(F32), axis=0, keepdims=True)
            row = lax.broadcasted_iota(jnp.int32, (8, D), 0)
            st_ref[...] += jnp.where(row == 3, dgate, 0.0)
            gwmo_out[...] = (acc * gate).astype(BF)
            gwmi_out[...] = gwmi_ref[...].astype(BF)
            tt = lax.broadcasted_iota(jnp.int32, (CHUNK, CHUNK), 0)
            ss = lax.broadcasted_iota(jnp.int32, (CHUNK, CHUNK), 1)
            for hd in range(8):
                gws_ref[hd] = jnp.where(tt >= ss, gws_ref[hd], 0.0)
            if ne:
                _exchange_phase("wait", sendbufs, recvbufs, sems)

    full = lambda shape: pl.BlockSpec(shape, lambda i: (0,) * len(shape))
    return pl.pallas_call(
        body, name=name, grid=(nS,),
        out_shape=[jax.ShapeDtypeStruct((S, D), F32), jax.ShapeDtypeStruct((DPROJ, D), BF),
                   jax.ShapeDtypeStruct((D, D), BF), jax.ShapeDtypeStruct((4, 128, 128), F32),
                   jax.ShapeDtypeStruct((8, CHUNK, CHUNK), F32), jax.ShapeDtypeStruct((8, D), F32),
                   jax.ShapeDtypeStruct((8, DP), F32), jax.ShapeDtypeStruct((CHUNK, DG), F32)] + _like(exchange),
        in_specs=[pl.BlockSpec((T, D), lambda i: (nS - 1 - i, 0)),
                  pl.BlockSpec((HALO, D), lambda i: (jnp.maximum((nS - 1 - i) * hb - 1, 0), 0)),
                  pl.BlockSpec((T, D), lambda i: (nS - 1 - i, 0)),
                  full((9, D)), full((1, D)), full((DPROJ, D)), full((D, D)),
                  full((4, 128, 128)), full((1, DP)), full((1, DG)), full((1, DG)), full((8, CHUNK, CHUNK)),
                  full((CHUNK, DG))] + [HBM] * ne,
        out_specs=[pl.BlockSpec((T, D), lambda i: (nS - 1 - i, 0)), full((DPROJ, D)), full((D, D)),
                   full((4, 128, 128)), full((8, CHUNK, CHUNK)), full((8, D)), full((8, DP)), full((CHUNK, DG))]
                  + [HBM] * ne,
        scratch_shapes=[pltpu.VMEM((HALO, DP), F32), pltpu.VMEM((T + HALO, DP), F32),
                        pltpu.VMEM((T + HALO, DP), F32), pltpu.VMEM((T, DG), F32), pltpu.VMEM((T, DG), F32),
                        pltpu.VMEM((T, D), BF), pltpu.VMEM((T, DPROJ), BF), pltpu.VMEM((DPROJ, D), F32),
                        pltpu.VMEM((D, D), F32)] + (_comm_sems(ne) if ne else []),
        compiler_params=_cparams(56, ("arbitrary",)),
    )(x, x, dxo, mod, gn, wmi, wmo, wp, ps, lg, lb, ws, bias, *exchange)


def kernel(x, c, w_ada, b_ada, norm_ffn1_g, ffn1_w_in, ffn1_w_out, norm_mix_g, w_mix_in, w_pool, pool_scale, gmlp_ln_g, gmlp_ln_b, w_spatial, b_spatial, w_mix_out, norm_ffn2_g, ffn2_w_in, ffn2_w_out, norm_final_g, loss_target, m_w_ada, m_b_ada, m_norm_ffn1_g, m_ffn1_w_in, m_ffn1_w_out, m_norm_mix_g, m_w_mix_in, m_w_pool, m_pool_scale, m_gmlp_ln_g, m_gmlp_ln_b, m_w_spatial, m_b_spatial, m_w_mix_out, m_norm_ffn2_g, m_ffn2_w_in, m_ffn2_w_out, m_norm_final_g, v_w_ada, v_b_ada, v_norm_ffn1_g, v_ffn1_w_in, v_ffn1_w_out, v_norm_mix_g, v_w_mix_in, v_w_pool, v_pool_scale, v_gmlp_ln_g, v_gmlp_ln_b, v_w_spatial, v_b_spatial, v_w_mix_out, v_norm_ffn2_g, v_ffn2_w_in, v_ffn2_w_out, v_norm_final_g):
    weights = dict(w_ada=w_ada, b_ada=b_ada, norm_ffn1_g=norm_ffn1_g, ffn1_w_in=ffn1_w_in, ffn1_w_out=ffn1_w_out,
                   norm_mix_g=norm_mix_g, w_mix_in=w_mix_in, w_pool=w_pool, pool_scale=pool_scale,
                   gmlp_ln_g=gmlp_ln_g, gmlp_ln_b=gmlp_ln_b, w_spatial=w_spatial, b_spatial=b_spatial,
                   w_mix_out=w_mix_out, norm_ffn2_g=norm_ffn2_g, ffn2_w_in=ffn2_w_in, ffn2_w_out=ffn2_w_out,
                   norm_final_g=norm_final_g)
    mom1 = dict(w_ada=m_w_ada, b_ada=m_b_ada, norm_ffn1_g=m_norm_ffn1_g, ffn1_w_in=m_ffn1_w_in,
                ffn1_w_out=m_ffn1_w_out, norm_mix_g=m_norm_mix_g, w_mix_in=m_w_mix_in, w_pool=m_w_pool,
                pool_scale=m_pool_scale, gmlp_ln_g=m_gmlp_ln_g, gmlp_ln_b=m_gmlp_ln_b, w_spatial=m_w_spatial,
                b_spatial=m_b_spatial, w_mix_out=m_w_mix_out, norm_ffn2_g=m_norm_ffn2_g, ffn2_w_in=m_ffn2_w_in,
                ffn2_w_out=m_ffn2_w_out, norm_final_g=m_norm_final_g)
    mom2 = dict(w_ada=v_w_ada, b_ada=v_b_ada, norm_ffn1_g=v_norm_ffn1_g, ffn1_w_in=v_ffn1_w_in,
                ffn1_w_out=v_ffn1_w_out, norm_mix_g=v_norm_mix_g, w_mix_in=v_w_mix_in, w_pool=v_w_pool,
                pool_scale=v_pool_scale, gmlp_ln_g=v_gmlp_ln_g, gmlp_ln_b=v_gmlp_ln_b, w_spatial=v_w_spatial,
                b_spatial=v_b_spatial, w_mix_out=v_w_mix_out, norm_ffn2_g=v_norm_ffn2_g, ffn2_w_in=v_ffn2_w_in,
                ffn2_w_out=v_ffn2_w_out, norm_final_g=v_norm_final_g)
    order = list(weights)
    xs = x[0]
    target = loss_target[0]
    transposed = ("ffn1_w_in", "w_mix_in", "ffn2_w_in")
    big = ("ffn1_w_in", "ffn1_w_out", "w_mix_in", "w_mix_out", "ffn2_w_in", "ffn2_w_out")
    local = lambda a, k: a[0].T if k in transposed else a[0]
    wc = w_ada.shape[2]

    shard = dict(zip(big, _cast_shards([local(weights[k], k) for k in big])))
    modp, cact_all = _ada_forward(jnp.broadcast_to(c, (8, D)), w_ada[0], b_ada.reshape(NDEV, wc))
    mod = modp.reshape(9, D)
    g_w1_in, g_w1_out = _all_gather([shard["ffn1_w_in"], shard["ffn1_w_out"]], "gather_ffn1")
    w1_in = g_w1_in.reshape(2, F, D)
    w1_out = g_w1_out.reshape(F, D)

    x1, gu1, h1, g_wmi, g_wmo, g_w2_out, g_w2_in = _ffn_fwd(
        xs, mod, norm_ffn1_g, w1_in, w1_out, 0, "ffn1_fwd",
        gather=[shard["w_mix_in"], shard["w_mix_out"], shard["ffn2_w_out"], shard["ffn2_w_in"]])
    wmi = g_wmi.reshape(DPROJ, D)
    wmo = g_wmo.reshape(D, D)
    w2_in = g_w2_in.reshape(2, F, D)
    w2_out = g_w2_out.reshape(F, D)
    tril = jnp.tril(jnp.ones((CHUNK, CHUNK), dtype=bool))
    ws_b = jnp.where(tril[None], w_spatial[0], 0.0).astype(BF)
    wp_b = w_pool[0].astype(BF)
    bias = jnp.repeat(b_spatial[0].T, DG // 8, axis=1)
    mix_args = (wmi, wmo, wp_b, pool_scale, gmlp_ln_g, gmlp_ln_b, ws_b, bias)
    x2 = _mix_fwd(x1, mod, norm_mix_g, *mix_args, "mix_fwd")
    dx3, gu3, h3, st_f = _ffn_fwd(x2, mod, norm_ffn2_g, w2_in, w2_out, 2, "ffn2_fwd",
                                  loss=(norm_final_g.reshape(1, D), target))

    slots = lambda a: a.reshape(NDEV, a.size // (NDEV * D), D)
    dgu3, d_w2_out, dgate3 = _ffn_bwd_hidden(dx3, mod, gu3, w2_out, 2, "ffn2_bwd_hidden")
    d_w2_in = _ffn_bwd_win(h3, dgu3, "ffn2_bwd_win")[0]
    dx2, st3 = _ffn_bwd_input(dgu3, w2_in, x2, dx3, mod, norm_ffn2_g, 2, "ffn2_bwd_input")
    dx1, d_wmi, d_wmo, d_wp, d_ws, st2, vec2, dbias, r_w2_in, r_w2_out = _mix_bwd(
        x1, dx2, mod, norm_mix_g, *mix_args, "mix_bwd", exchange=[slots(d_w2_in), slots(d_w2_out)])
    dgu1, d_w1_out, dgate1, r_wmi, r_wmo = _ffn_bwd_hidden(
        dx1, mod, gu1, w1_out, 0, "ffn1_bwd_hidden", exchange=[slots(d_wmi), slots(d_wmo)])
    d_w1_in, r_w1_out = _ffn_bwd_win(h1, dgu1, "ffn1_bwd_win", exchange=[slots(d_w1_out)])
    dx0, st1, r_w1_in = _ffn_bwd_input(dgu1, w1_in, xs, dx1, mod, norm_ffn1_g, 0, "ffn1_bwd_input",
                                       exchange=[slots(d_w1_in)])

    received = dict(ffn1_w_in=r_w1_in, ffn1_w_out=r_w1_out, w_mix_in=r_wmi, w_mix_out=r_wmo,
                    ffn2_w_in=r_w2_in, ffn2_w_out=r_w2_out)
    tiles = dict(ffn1_w_in=176, ffn1_w_out=176, w_mix_in=96, w_mix_out=128, ffn2_w_in=176, ffn2_w_out=176)
    result = {}
    for k, recv in received.items():
        res = _sum_adamw(recv, local(weights[k], k), local(mom1[k], k), local(mom2[k], k), tiles[k], "update_" + k)
        result[k] = tuple((a.T if k in transposed else a)[None] for a in res)
    row = lambda a: a.reshape(1, D)
    params = {k: (weights[k], mom1[k], mom2[k]) for k in SMALL}
    params["norm_final_g"] = (row(norm_final_g), row(m_norm_final_g), row(v_norm_final_g))
    tot, rsum, dmine = _small_reduce(d_ws, d_wp, dbias, st1, st2, st3, st_f, vec2, dgate1, dgate3)
    small, loss_row = _small_update(tot, rsum, params)
    result.update(small)
    result["norm_final_g"] = tuple(a.reshape(D) for a in small["norm_final_g"])
    result["w_ada"] = tuple(a[None] for a in _ada_update(cact_all, dmine, w_ada[0], m_w_ada[0], v_w_ada[0], 256))

    return (loss_row[0, 0], dx0[None], *[result[k][0] for k in order], *[result[k][1] for k in order],
            *[result[k][2] for k in order], *[result[k][3] for k in order])
```

```python
import math

import jax
import jax.numpy as jnp
from jax import lax
from jax.experimental import pallas as pl
from jax.experimental.pallas import tpu as pltpu

D = 1024
F = 2816
DP = 512
DG = 512
DPROJ = DP + 2 * DG
CHUNK = 128
WINDOWS = (2, 4, 8, 16)
HALO = 16
NDEV = 8
T_FFN = 512
T_MIX = 256
T_WIN = 2048
EPS = 1e-6
LR, B1, B2, AEPS, WD, STEP = 0.001, 0.9, 0.999, 1e-08, 0.01, 10
BC1 = 1.0 - B1 ** STEP
BC2 = 1.0 - B2 ** STEP
GELU_C = math.sqrt(2.0 / math.pi)
GELU_A = 0.044715

BF = jnp.bfloat16
F32 = jnp.float32
MESH = pl.DeviceIdType.MESH
HBM = pl.BlockSpec(memory_space=pltpu.HBM)


def _whole(a):
    return pl.BlockSpec(a.shape, lambda i: (0,) * len(a.shape))


NT = (((1,), (1,)), ((), ()))
TN = (((0,), (0,)), ((), ()))


def _dot(a, b):
    return jnp.dot(a, b, preferred_element_type=F32)


def _dot_nt(a, b):
    return lax.dot_general(a, b, NT, preferred_element_type=F32)


def _dot_tn(a, b):
    return lax.dot_general(a, b, TN, preferred_element_type=F32)


def _cparams(vmem_mb, sem=None):
    kw = dict(vmem_limit_bytes=vmem_mb * 1024 * 1024)
    if sem is not None:
        kw["dimension_semantics"] = sem
    return pltpu.CompilerParams(**kw)


def _position():
    return lax.axis_index("x"), lax.axis_index("y"), lax.axis_index("c")


def _slot(p):
    return 4 * p[0] + 2 * p[1] + p[2]


def _flip(me, d):
    x, y, c = me
    return (1 - x if d & 4 else x, 1 - y if d & 2 else y, 1 - c if d & 1 else c)


def _remote(src, dst, send_sem, recv_sem, to):
    return pltpu.make_async_remote_copy(src_ref=src, dst_ref=dst, send_sem=send_sem, recv_sem=recv_sem,
                                        device_id=to, device_id_type=MESH)


def _comm_sems(n):
    return [pltpu.SemaphoreType.DMA((n, 7)), pltpu.SemaphoreType.DMA((n, 7)), pltpu.SemaphoreType.DMA((n,))]


def _gather_phase(phase, xs, outs, sems):
    send_sems, recv_sems, local_sems = sems
    n = len(xs)
    me = _position()
    x, y, c = me
    sibling = (x, y, 1 - c)
    xn, yn, diag = (1 - x, y), (x, 1 - y), (1 - x, 1 - y)
    relay_from = (x + c * (1 - 2 * x), y + (1 - c) * (1 - 2 * y))
    relay_to = (x + (1 - c) * (1 - 2 * x), y + c * (1 - 2 * y))

    def copy(a, k, block, to, src=None):
        dst = outs[a].at[_slot(block)]
        return _remote(dst if src is None else src, dst, send_sems.at[a, k], recv_sems.at[a, k], to)

    def mine(a):
        return pltpu.make_async_copy(xs[a], outs[a].at[_slot(me)], local_sems.at[a])

    def first(a):
        return [copy(a, 0, me, sibling, src=xs[a]), copy(a, 1, me, (*xn, c), src=xs[a]),
                copy(a, 2, me, (*yn, c), src=xs[a])]

    def second(a):
        return [copy(a, 3, (*relay_from, c), (*relay_to, c)), copy(a, 4, (*xn, c), sibling),
                copy(a, 5, (*yn, c), sibling)]

    def third(a):
        return copy(a, 6, (*diag, c), sibling)

    if phase == "start":
        for a in range(n):
            mine(a).start()
            for cp in first(a):
                cp.start()
    elif phase == "forward":
        for a in range(n):
            copy(a, 1, (*xn, c), me).wait_recv()
            copy(a, 2, (*yn, c), me).wait_recv()
            for cp in second(a):
                cp.start()
    else:
        for a in range(n):
            copy(a, 3, (*diag, c), me).wait_recv()
            third(a).start()
        for a in range(n):
            copy(a, 0, sibling, me).wait_recv()
            for k, chip in ((4, xn), (5, yn), (6, diag)):
                copy(a, k, (*chip, 1 - c), me).wait_recv()
        for a in range(n):
            for cp in first(a) + second(a) + [third(a)]:
                cp.wait_send()
            mine(a).wait()


def _exchange_phase(phase, xs, outs, sems):
    send_sems, recv_sems, local_sems = sems
    me = _position()
    for a in range(len(xs)):
        copies = [pltpu.make_async_copy(xs[a].at[_slot(me)], outs[a].at[_slot(me)], local_sems.at[a])]
        for d in range(1, NDEV):
            to = _flip(me, d)
            copies.append(_remote(xs[a].at[_slot(to)], outs[a].at[_slot(me)],
                                  send_sems.at[a, d - 1], recv_sems.at[a, d - 1], to))
        for cp in copies:
            if phase == "start":
                cp.start()
            else:
                cp.wait()


def _like(bufs):
    return [jax.ShapeDtypeStruct(b.shape, b.dtype) for b in bufs]


def _rms_mod(x, gn, shift, scale):
    ms = jnp.mean(x * x, axis=-1, keepdims=True)
    r = lax.rsqrt(ms + EPS)
    xhat = x * r
    n = xhat * gn
    h = n * (1.0 + scale) + shift
    return r, xhat, n, h


def _rms_mod_bwd(dh, dres, r, xhat, n, gn, scale):
    dshift = jnp.sum(dh, axis=0, keepdims=True)
    dscale = jnp.sum(dh * n, axis=0, keepdims=True)
    dn = dh * (1.0 + scale)
    dgn = jnp.sum(dn * xhat, axis=0, keepdims=True)
    dxhat = dn * gn
    dx = dres + r * (dxhat - xhat * jnp.mean(dxhat * xhat, axis=-1, keepdims=True))
    return dx, dshift, dscale, dgn


def _final_norm_loss(x, gf, target):
    r = lax.rsqrt(jnp.mean(x * x, axis=-1, keepdims=True) + EPS)
    xhat = x * r
    e = xhat * gf - target
    part = 0.5 * jnp.sum(jnp.sum(e * e, axis=-1, keepdims=True), axis=0, keepdims=True) / D
    dy = e / D
    dgf = jnp.sum(dy * xhat, axis=0, keepdims=True)
    dxhat = dy * gf
    dx = r * (dxhat - xhat * jnp.mean(dxhat * xhat, axis=-1, keepdims=True))
    return dx, dgf, part


def _rows3(a, b, c, width):
    row = lax.broadcasted_iota(jnp.int32, (8, width), 0)
    z = jnp.zeros((8, width), F32)
    return jnp.where(row == 0, a, z) + jnp.where(row == 1, b, z) + jnp.where(row == 2, c, z)


def _sigmoid(x):
    return 0.5 * jnp.tanh(0.5 * x) + 0.5


def _gelu(x):
    t = jnp.tanh(GELU_C * (x + GELU_A * x * x * x))
    return 0.5 * x * (1.0 + t), t


def _gelu_grad(x, t):
    return 0.5 * (1.0 + t) + 0.5 * x * (1.0 - t * t) * GELU_C * (1.0 + 3.0 * GELU_A * x * x)


def _adamw(w, g, m, v):
    m = B1 * m + (1.0 - B1) * g
    v = B2 * v + (1.0 - B2) * (g * g)
    m_hat = m / BC1
    v_hat = v / BC2
    delta = -LR * (m_hat / (jnp.sqrt(v_hat) + AEPS) + WD * w)
    return delta, m, v


def _cast_shards(shards):
    n = len(shards)

    def body(*refs):
        for a in range(n):
            refs[n + a][...] = refs[a][...].astype(BF)

    outs = [jax.ShapeDtypeStruct(s.shape, BF) for s in shards]
    return pl.pallas_call(
        body, name="cast_shards", grid=(1,), out_shape=outs,
        in_specs=[_whole(s) for s in shards], out_specs=[_whole(s) for s in outs],
        compiler_params=_cparams(48, ("arbitrary",)),
    )(*shards)


def _all_gather(shards, name):
    n = len(shards)

    def body(*refs):
        xs, outs, sems = refs[:n], refs[n:2 * n], refs[2 * n:]
        for phase in ("start", "forward", "finish"):
            _gather_phase(phase, xs, outs, sems)

    return pl.pallas_call(
        body, name=name,
        out_shape=[jax.ShapeDtypeStruct((NDEV,) + s.shape, s.dtype) for s in shards],
        in_specs=[HBM] * n, out_specs=[HBM] * n, scratch_shapes=_comm_sems(n),
    )(*shards)


def _ada_forward(c8, w_ada, b8):
    wc = w_ada.shape[1]

    def body(c8_ref, w_ref, b8_ref, mod_ref, cact_ref, call_ref, mall_ref, send_sems, recv_sems):
        me = _position()
        my = _slot(me)
        row = lax.broadcasted_iota(jnp.int32, (8, 1), 0)
        call_ref[my] = c8_ref[...]
        sends = []
        for d in range(1, NDEV):
            to = _flip(me, d)
            sends.append(_remote(call_ref.at[my], call_ref.at[my], send_sems.at[0, d - 1], recv_sems.at[0, d - 1], to))
        for cp in sends:
            cp.start()
        for cp in sends:
            cp.wait()
        c_all = jnp.zeros((8, D), F32)
        for k in range(NDEV):
            c_all = c_all + jnp.where(row == k, call_ref[k], 0.0)
        cact = c_all * jax.nn.sigmoid(c_all)
        cact_ref[...] = cact
        part = _dot(cact.astype(BF), w_ref[...].astype(BF))
        mall_ref[my] = part
        sends = []
        for d in range(1, NDEV):
            to = _flip(me, d)
            sends.append(_remote(mall_ref.at[my], mall_ref.at[my], send_sems.at[1, d - 1], recv_sems.at[1, d - 1], to))
        for cp in sends:
            cp.start()
        for cp in sends:
            cp.wait()
        out = jnp.zeros((8, wc), F32)
        for k in range(NDEV):
            piece = jnp.sum(jnp.where(row == my, mall_ref[k], 0.0), axis=0, keepdims=True)
            out = out + jnp.where(row == k, piece, 0.0)
        mod_ref[...] = out + b8_ref[...]

    outs = [jax.ShapeDtypeStruct((8, wc), F32), jax.ShapeDtypeStruct((8, D), F32)]
    return pl.pallas_call(
        body, name="ada_forward", grid=(1,), out_shape=outs,
        in_specs=[_whole(a) for a in (c8, w_ada, b8)], out_specs=[_whole(a) for a in outs],
        scratch_shapes=[pltpu.VMEM((NDEV, 8, D), F32), pltpu.VMEM((NDEV, 8, wc), F32),
                        pltpu.SemaphoreType.DMA((2, 7)), pltpu.SemaphoreType.DMA((2, 7))],
        compiler_params=_cparams(32, ("arbitrary",)),
    )(c8, w_ada, b8)


MATS = ("w_spatial", "w_pool", "b_spatial")
VECS = ("norm_ffn1_g", "norm_mix_g", "norm_ffn2_g", "norm_final_g", "pool_scale", "gmlp_ln_g", "gmlp_ln_b", "b_ada")
VEC_WIDTH = dict(norm_ffn1_g=D, norm_mix_g=D, norm_ffn2_g=D, norm_final_g=D, pool_scale=DP, gmlp_ln_g=DG,
                 gmlp_ln_b=DG, b_ada=9 * D)
MAT_ROWS = 1600
MAT_SLICE = MAT_ROWS // NDEV
VEC_LANES = sum(VEC_WIDTH.values()) + 128
DMOD_AT = VEC_LANES - 128 - 9 * D
SMALL = MATS + VECS


def _small_reduce(g_ws, g_wp, dbias, st1, st1b, st2, st3, st_f, vec2, dgate1, dgate3):
    wc = 9 * D // NDEV

    def body(g_ws_ref, g_wp_ref, dbias_ref, st1_ref, st1b_ref, st2_ref, st3_ref, stf_ref, vec2_ref, dg1_ref, dg3_ref,
             tot_ref, rsum_ref, dmine_ref,
             pack_ref, rs_ref, ag_ref, rv_ref, dmp_ref, dw_ref, send_sems, recv_sems):
        me = _position()
        my = _slot(me)

        pack_ref[0:1024, :] = g_ws_ref[...].reshape(1024, 128)
        pack_ref[1024:1536, :] = g_wp_ref[...].reshape(512, 128)
        ch = lax.broadcasted_iota(jnp.int32, (DG, 128), 0)
        hd = lax.broadcasted_iota(jnp.int32, (DG, 128), 1)
        sel = jnp.where(ch // 64 == hd, 1.0, 0.0).astype(F32)
        heads = jnp.dot(dbias_ref[...], sel, preferred_element_type=F32, precision=lax.Precision.HIGHEST)
        pack_ref[1536:1544, :] = heads.T[0:8, :]
        pack_ref[1544:MAT_ROWS, :] = jnp.zeros((MAT_ROWS - 1544, 128), F32)
        dgate1 = dg1_ref[0:1, :] + dg1_ref[8:9, :]
        dgate3 = dg3_ref[0:1, :] + dg3_ref[8:9, :]
        st1 = [st1_ref[k:k + 1, :] + st1b_ref[k:k + 1, :] for k in range(3)]
        row = jnp.concatenate(
            [st1[2], st2_ref[2:3, :], st3_ref[2:3, :], stf_ref[0:1, :],
             vec2_ref[0:1, :], vec2_ref[1:2, :], vec2_ref[2:3, :],
             st1[0], st1[1], dgate1, st2_ref[0:1, :], st2_ref[1:2, :], st2_ref[3:4, :],
             st3_ref[0:1, :], st3_ref[1:2, :], dgate3, stf_ref[1:2, 0:128]], axis=1)
        rv_ref[my] = row
        for k in range(NDEV):
            dmp_ref[k] = row[:, DMOD_AT + wc * k:DMOD_AT + wc * (k + 1)]
        dw_ref[my] = dmp_ref[my]
        rs_ref[my] = pack_ref[pl.ds(pl.multiple_of(my * MAT_SLICE, 8), MAT_SLICE), :]

        first = []
        for d in range(1, NDEV):
            to = _flip(me, d)
            theirs = pl.ds(pl.multiple_of(_slot(to) * MAT_SLICE, 8), MAT_SLICE)
            first.append(_remote(pack_ref.at[theirs, :], rs_ref.at[my], send_sems.at[0, d - 1], recv_sems.at[0, d - 1], to))
            first.append(_remote(dmp_ref.at[_slot(to)], dw_ref.at[my], send_sems.at[1, d - 1], recv_sems.at[1, d - 1], to))
            first.append(_remote(rv_ref.at[my], rv_ref.at[my], send_sems.at[2, d - 1], recv_sems.at[2, d - 1], to))
        for cp in first:
            cp.start()
        for cp in first:
            cp.wait()
        red = rs_ref[0]
        for k in range(1, NDEV):
            red = red + rs_ref[k]
        ag_ref[my] = red
        second = []
        for d in range(1, NDEV):
            to = _flip(me, d)
            second.append(_remote(ag_ref.at[my], ag_ref.at[my], send_sems.at[3, d - 1], recv_sems.at[3, d - 1], to))
        for cp in second:
            cp.start()

        rsum = rv_ref[0]
        for k in range(1, NDEV):
            rsum = rsum + rv_ref[k]
        rsum_ref[...] = rsum
        r8 = lax.broadcasted_iota(jnp.int32, (8, 1), 0)
        dmine = jnp.zeros((8, wc), F32)
        for k in range(NDEV):
            dmine = dmine + jnp.where(r8 == k, dw_ref[k], 0.0)
        dmine_ref[...] = dmine

        for cp in second:
            cp.wait()
        for k in range(NDEV):
            tot_ref[k * MAT_SLICE:(k + 1) * MAT_SLICE, :] = ag_ref[k]

    ins = (g_ws, g_wp, dbias, st1, st1b, st2, st3, st_f, vec2, dgate1, dgate3)
    outs = [jax.ShapeDtypeStruct((MAT_ROWS, 128), F32), jax.ShapeDtypeStruct((1, VEC_LANES), F32),
            jax.ShapeDtypeStruct((8, wc), F32)]
    return pl.pallas_call(
        body, name="small_reduce", grid=(1,), out_shape=outs,
        in_specs=[_whole(a) for a in ins], out_specs=[_whole(a) for a in outs],
        scratch_shapes=[pltpu.VMEM((MAT_ROWS, 128), F32), pltpu.VMEM((NDEV, MAT_SLICE, 128), F32),
                        pltpu.VMEM((NDEV, MAT_SLICE, 128), F32),
                        pltpu.VMEM((NDEV, 1, VEC_LANES), F32), pltpu.VMEM((NDEV, 1, wc), F32),
                        pltpu.VMEM((NDEV, 1, wc), F32),
                        pltpu.SemaphoreType.DMA((4, 7)), pltpu.SemaphoreType.DMA((4, 7))],
        compiler_params=_cparams(32, ("arbitrary",)),
    )(*ins)


def _small_update(tot, rsum, params):
    flat = [a for k in SMALL for a in params[k]]
    n_in = 2 + len(flat)

    def body(*refs):
        tot_ref, rsum_ref = refs[:2]
        p_refs = refs[2:n_in]
        o_refs = refs[n_in:n_in + 4 * len(SMALL)]
        loss_ref = refs[n_in + 4 * len(SMALL)]
        loss_ref[...] = rsum_ref[:, VEC_LANES - 128:VEC_LANES]

        def update(idx, g):
            w_ref, m_ref, v_ref = p_refs[3 * idx:3 * idx + 3]
            g_out, d_out, m_out, v_out = o_refs[4 * idx:4 * idx + 4]
            g = g.reshape(w_ref.shape)
            g_out[...] = g
            d_out[...], m_out[...], v_out[...] = _adamw(w_ref[...], g, m_ref[...], v_ref[...])

        update(0, tot_ref[0:1024, :])
        update(1, tot_ref[1024:1536, :])
        update(2, tot_ref[1536:1544, :])
        at = 0
        for idx, k in enumerate(VECS):
            update(3 + idx, rsum_ref[:, at:at + VEC_WIDTH[k]])
            at += VEC_WIDTH[k]

    outs = []
    for k in SMALL:
        outs += [jax.ShapeDtypeStruct(params[k][0].shape, F32)] * 4
    outs += [jax.ShapeDtypeStruct((1, 128), F32)]
    res = pl.pallas_call(
        body, name="small_update", grid=(1,), out_shape=outs,
        in_specs=[_whole(a) for a in (tot, rsum, *flat)], out_specs=[_whole(a) for a in outs],
        compiler_params=_cparams(32, ("arbitrary",)),
    )(tot, rsum, *flat)
    return {k: tuple(res[4 * i:4 * i + 4]) for i, k in enumerate(SMALL)}, res[-1]


def _sum_adamw(recv, w, m, v, tr, name):
    R, C = w.shape

    def body(r_ref, w_ref, m_ref, v_ref, g_ref, d_ref, nm_ref, nv_ref):
        g = r_ref[0].astype(F32)
        for k in range(1, NDEV):
            g = g + r_ref[k].astype(F32)
        g_ref[...] = g
        d_ref[...], nm_ref[...], nv_ref[...] = _adamw(w_ref[...], g, m_ref[...], v_ref[...])

    blk = pl.BlockSpec((tr, C), lambda i: (i, 0))
    out = jax.ShapeDtypeStruct((R, C), F32)
    return pl.pallas_call(
        body, name=name, grid=(R // tr,), out_shape=[out] * 4,
        in_specs=[pl.BlockSpec((NDEV, tr, C), lambda i: (0, i, 0)), blk, blk, blk], out_specs=[blk] * 4,
        compiler_params=_cparams(48, ("arbitrary",)),
    )(recv, w, m, v)


def _ada_update(cact_all, dmine, w, m, v, tr):
    R, C = w.shape

    def body(c_ref, dm_ref, w_ref, m_ref, v_ref, g_ref, d_ref, nm_ref, nv_ref):
        g = _dot_tn(c_ref[...].astype(BF), dm_ref[...].astype(BF))
        g_ref[...] = g
        d_ref[...], nm_ref[...], nv_ref[...] = _adamw(w_ref[...], g, m_ref[...], v_ref[...])

    blk = pl.BlockSpec((tr, C), lambda i: (i, 0))
    out = jax.ShapeDtypeStruct((R, C), F32)
    return pl.pallas_call(
        body, name="update_w_ada", grid=(R // tr,), out_shape=[out] * 4,
        in_specs=[pl.BlockSpec((8, tr), lambda i: (0, i)), pl.BlockSpec((8, C), lambda i: (0, 0)), blk, blk, blk],
        out_specs=[blk] * 4,
        compiler_params=_cparams(48, ("arbitrary",)),
    )(cact_all, dmine, w, m, v)


FC = F // 2


def _ffn_fwd(x, mod, gn, w_in_t, w_out, sub, name, gather=(), loss=None):
    S = x.shape[0]
    T = min(T_FFN, S)
    nS, nJ = S // T, F // FC
    ng = len(gather)
    nl = 2 if loss else 0
    forward_step = nS // 2

    def body(*refs):
        x_ref, mod_ref, gn_ref, wg_ref, wu_ref, wo_ref = refs[:6]
        gf_ref, t_ref = refs[6:6 + nl] if loss else (None, None)
        shards = refs[6 + nl:6 + nl + ng]
        at = 6 + nl + ng
        xo_ref, gu_ref, h_ref = refs[at:at + 3]
        gathered = refs[at + 3:at + 3 + ng]
        at += 3 + ng
        st_ref = refs[at] if loss else None
        at += nl // 2
        acc_scr = refs[at]
        sems = refs[at + 1:]
        i, j = pl.program_id(0), pl.program_id(1)

        if ng:
            @pl.when((i == 0) & (j == 0))
            def _():
                _gather_phase("start", shards, gathered, sems)

            @pl.when((i == forward_step) & (j == 0))
            def _():
                _gather_phase("forward", shards, gathered, sems)

        @pl.when(j == 0)
        def _():
            _, _, _, h = _rms_mod(x_ref[...], gn_ref[...], mod_ref[3 * sub:3 * sub + 1, :],
                                  mod_ref[3 * sub + 1:3 * sub + 2, :])
            h_ref[...] = h.astype(BF)
            acc_scr[...] = jnp.zeros_like(acc_scr)

        h = h_ref[...]
        g = _dot_nt(h, wg_ref[0])
        u = _dot_nt(h, wu_ref[0])
        gu_ref[0] = g.astype(BF)
        gu_ref[1] = u.astype(BF)
        a = (g * _sigmoid(g) * u).astype(BF)
        acc_scr[...] += _dot(a, wo_ref[...])

        @pl.when(j == nJ - 1)
        def _():
            xo = x_ref[...] + (0.5 * mod_ref[3 * sub + 2:3 * sub + 3, :]) * acc_scr[...]
            if not loss:
                xo_ref[...] = xo
            else:
                dx, dgf, part = _final_norm_loss(xo, gf_ref[...], t_ref[...])
                xo_ref[...] = dx
                upd = _rows3(dgf, jnp.broadcast_to(part, (1, D)), jnp.zeros((1, D), F32), D)

                @pl.when(i == 0)
                def _():
                    st_ref[...] = upd

                @pl.when(i > 0)
                def _():
                    st_ref[...] += upd

        if ng:
            @pl.when((i == nS - 1) & (j == nJ - 1))
            def _():
                _gather_phase("finish", shards, gathered, sems)

    tile = pl.BlockSpec((T, D), lambda i, j: (i, 0))
    return pl.pallas_call(
        body, name=name, grid=(nS, nJ),
        out_shape=[jax.ShapeDtypeStruct((S, D), F32), jax.ShapeDtypeStruct((2, S, F), BF),
                   jax.ShapeDtypeStruct((S, D), BF)]
                  + [jax.ShapeDtypeStruct((NDEV,) + s.shape, s.dtype) for s in gather]
                  + ([jax.ShapeDtypeStruct((8, D), F32)] if loss else []),
        in_specs=[tile,
                  pl.BlockSpec((9, D), lambda i, j: (0, 0)),
                  pl.BlockSpec((1, D), lambda i, j: (0, 0)),
                  pl.BlockSpec((1, FC, D), lambda i, j: (0, j, 0)),
                  pl.BlockSpec((1, FC, D), lambda i, j: (1, j, 0)),
                  pl.BlockSpec((FC, D), lambda i, j: (j, 0))]
                 + ([pl.BlockSpec((1, D), lambda i, j: (0, 0)), tile] if loss else []) + [HBM] * ng,
        out_specs=[tile, pl.BlockSpec((2, T, FC), lambda i, j: (0, i, j)), tile] + [HBM] * ng
                  + ([pl.BlockSpec((8, D), lambda i, j: (0, 0))] if loss else []),
        scratch_shapes=[pltpu.VMEM((T, D), F32)] + (_comm_sems(ng) if ng else []),
        compiler_params=_cparams(56, ("arbitrary", "arbitrary")),
    )(x, mod, gn, w_in_t, w_in_t, w_out, *(loss or ()), *gather)


def _ffn_bwd_hidden(dx, mod, gu, w_out, sub, name, exchange=()):
    S = dx.shape[0]
    T = min(T_FFN, S)
    nS, nJ = S // T, F // FC
    ne = len(exchange)

    def body(*refs):
        dx_ref, mod_ref, gu_ref, wo_ref = refs[:4]
        sendbufs = refs[4:4 + ne]
        dgu_ref, gw_ref, dgate_ref = refs[4 + ne:7 + ne]
        recvbufs = refs[7 + ne:7 + 2 * ne]
        acc_scr = refs[7 + 2 * ne]
        sems = refs[8 + 2 * ne:]
        j, i = pl.program_id(0), pl.program_id(1)

        if ne:
            @pl.when((i == 0) & (j == 0))
            def _():
                _exchange_phase("start", sendbufs, recvbufs, sems)

        gate = mod_ref[3 * sub + 2:3 * sub + 3, :]
        dx = dx_ref[...]
        da = _dot_nt((dx * (0.5 * gate)).astype(BF), wo_ref[...])
        g = gu_ref[0].astype(F32)
        u = gu_ref[1].astype(F32)
        sg = _sigmoid(g)
        s = g * sg
        dgu_ref[0] = (da * u * (sg * (1.0 + g * (1.0 - sg)))).astype(BF)
        dgu_ref[1] = (da * s).astype(BF)
        contrib = _dot_tn((s * u).astype(BF), dx.astype(BF))

        @pl.when(i == 0)
        def _():
            acc_scr[...] = contrib

        @pl.when(i > 0)
        def _():
            acc_scr[...] += contrib

        @pl.when(i == nS - 1)
        def _():
            acc = acc_scr[...]
            dgate = 0.5 * jnp.sum(acc * wo_ref[...].astype(F32), axis=0, keepdims=True)
            dgate_ref[...] = jnp.broadcast_to(dgate, (8, D))
            gw_ref[...] = (acc * (0.5 * gate)).astype(BF)

        if ne:
            @pl.when((i == nS - 1) & (j == nJ - 1))
            def _():
                _exchange_phase("wait", sendbufs, recvbufs, sems)

    return pl.pallas_call(
        body, name=name, grid=(nJ, nS),
        out_shape=[jax.ShapeDtypeStruct((2, S, F), BF), jax.ShapeDtypeStruct((F, D), BF),
                   jax.ShapeDtypeStruct((8 * nJ, D), F32)] + _like(exchange),
        in_specs=[pl.BlockSpec((T, D), lambda j, i: (i, 0)),
                  pl.BlockSpec((9, D), lambda j, i: (0, 0)),
                  pl.BlockSpec((2, T, FC), lambda j, i: (0, i, j)),
                  pl.BlockSpec((FC, D), lambda j, i: (j, 0))] + [HBM] * ne,
        out_specs=[pl.BlockSpec((2, T, FC), lambda j, i: (0, i, j)),
                   pl.BlockSpec((FC, D), lambda j, i: (j, 0)),
                   pl.BlockSpec((8, D), lambda j, i: (j, 0))] + [HBM] * ne,
        scratch_shapes=[pltpu.VMEM((FC, D), F32)] + (_comm_sems(ne) if ne else []),
        compiler_params=_cparams(56, ("arbitrary", "arbitrary")),
    )(dx, mod, gu, w_out, *exchange)


def _ffn_bwd_input(dgu, w_in_t, x, dx, mod, gn, sub, name, exchange=(), tiles=None, into=None):
    S = x.shape[0]
    T = min(T_FFN, S)
    lo, hi = tiles or (0, S // T)
    nS = hi - lo
    ne = len(exchange)
    ni = 0 if into is None else 1
    NC = 256
    chunks = [slice(k * NC, (k + 1) * NC) for k in range(D // NC)]

    def body(*refs):
        dgu_ref, w_ref, x_ref, dx_ref, mod_ref, gn_ref = refs[:6]
        sendbufs = refs[6 + ni:6 + ni + ne]
        dxin_ref, st_ref = refs[6 + ni + ne:8 + ni + ne]
        recvbufs = refs[8 + ni + ne:8 + ni + 2 * ne]
        dxh_scr = refs[8 + ni + 2 * ne]
        sems = refs[9 + ni + 2 * ne:]
        i = pl.program_id(0)

        if ne:
            @pl.when(i == 0)
            def _():
                _exchange_phase("start", sendbufs, recvbufs, sems)

        gn = gn_ref[...]
        scale = mod_ref[3 * sub + 1:3 * sub + 2, :]
        r, xhat, n, _ = _rms_mod(x_ref[...], gn, mod_ref[3 * sub:3 * sub + 1, :], scale)
        dg = dgu_ref[0]
        du = dgu_ref[1]
        rowsum = jnp.zeros((T, 1), F32)
        dshift, dscale, dgn = [], [], []
        for cols in chunks:
            dh = _dot(dg, w_ref[0, :, cols]) + _dot(du, w_ref[1, :, cols])
            dshift.append(jnp.sum(dh, axis=0, keepdims=True))
            dscale.append(jnp.sum(dh * n[:, cols], axis=0, keepdims=True))
            dn = dh * (1.0 + scale[:, cols])
            dgn.append(jnp.sum(dn * xhat[:, cols], axis=0, keepdims=True))
            dxhat = dn * gn[:, cols]
            rowsum = rowsum + jnp.sum(dxhat * xhat[:, cols], axis=-1, keepdims=True)
            dxh_scr[:, cols] = dxhat
        dxin_ref[...] = dx_ref[...] + r * (dxh_scr[...] - xhat * (rowsum / D))
        cat = lambda parts: jnp.concatenate(parts, axis=1)
        upd = _rows3(cat(dshift), cat(dscale), cat(dgn), D)

        @pl.when(i == 0)
        def _():
            st_ref[...] = upd

        @pl.when(i > 0)
        def _():
            st_ref[...] += upd

        if ne:
            @pl.when(i == nS - 1)
            def _():
                _exchange_phase("wait", sendbufs, recvbufs, sems)

    tile = pl.BlockSpec((T, D), lambda i: (i + lo, 0))
    return pl.pallas_call(
        body, name=name, grid=(nS,),
        out_shape=[jax.ShapeDtypeStruct((S, D), F32), jax.ShapeDtypeStruct((8, D), F32)] + _like(exchange),
        in_specs=[pl.BlockSpec((2, T, F), lambda i: (0, i + lo, 0)),
                  pl.BlockSpec((2, F, D), lambda i: (0, 0, 0), pipeline_mode=pl.Buffered(1)),
                  tile, tile,
                  pl.BlockSpec((9, D), lambda i: (0, 0)),
                  pl.BlockSpec((1, D), lambda i: (0, 0))] + [HBM] * (ni + ne),
        out_specs=[tile, pl.BlockSpec((8, D), lambda i: (0, 0))] + [HBM] * ne,
        input_output_aliases={6: 0} if ni else {},
        scratch_shapes=[pltpu.VMEM((T, D), F32)] + (_comm_sems(ne) if ne else []),
        compiler_params=_cparams(60, ("arbitrary",)),
    )(dgu, w_in_t, x, dx, mod, gn, *(() if into is None else (into,)), *exchange)


def _ffn_bwd_win(h, dgu, name, exchange=()):
    S = h.shape[0]
    T = min(T_WIN, S)
    nS, nJ = S // T, F // FC
    ne = len(exchange)

    def body(*refs):
        h_ref, dgu_ref = refs[:2]
        sendbufs = refs[2:2 + ne]
        out_ref = refs[2 + ne]
        recvbufs = refs[3 + ne:3 + 2 * ne]
        acc_scr = refs[3 + 2 * ne]
        sems = refs[4 + 2 * ne:]
        p, j, i = pl.program_id(0), pl.program_id(1), pl.program_id(2)

        if ne:
            @pl.when((p == 0) & (j == 0) & (i == 0))
            def _():
                _exchange_phase("start", sendbufs, recvbufs, sems)

        contrib = _dot_tn(dgu_ref[0], h_ref[...])

        @pl.when(i == 0)
        def _():
            acc_scr[...] = contrib

        @pl.when(i > 0)
        def _():
            acc_scr[...] += contrib

        @pl.when(i == nS - 1)
        def _():
            out_ref[0] = acc_scr[...].astype(BF)

        if ne:
            @pl.when((p == 1) & (j == nJ - 1) & (i == nS - 1))
            def _():
                _exchange_phase("wait", sendbufs, recvbufs, sems)

    return pl.pallas_call(
        body, name=name, grid=(2, nJ, nS),
        out_shape=[jax.ShapeDtypeStruct((2, F, D), BF)] + _like(exchange),
        in_specs=[pl.BlockSpec((T, D), lambda p, j, i: (i, 0)),
                  pl.BlockSpec((1, T, FC), lambda p, j, i: (p, i, j))] + [HBM] * ne,
        out_specs=[pl.BlockSpec((1, FC, D), lambda p, j, i: (p, j, 0))] + [HBM] * ne,
        scratch_shapes=[pltpu.VMEM((FC, D), F32)] + (_comm_sems(ne) if ne else []),
        compiler_params=_cparams(56, ("arbitrary", "arbitrary", "arbitrary")),
    )(h, dgu, *exchange)


def _pool_counts(pos0, T):
    pos = pos0 + lax.broadcasted_iota(jnp.int32, (T, 1), 0)
    return [jnp.minimum(pos + 1, w).astype(F32) for w in WINDOWS]


def _pool_fwd(xa, halo, ext_scr, cnts, T):
    ext_scr[0:HALO, :] = halo
    ext_scr[HALO:HALO + T, :] = xa
    out = []
    for gi, w in enumerate(WINDOWS):
        cols = slice(128 * gi, 128 * gi + 128)
        acc = xa[:, cols]
        for k in range(1, w):
            acc = acc + ext_scr[HALO - k:HALO - k + T, cols]
        out.append(acc / cnts[gi] - xa[:, cols])
    return out


def _sgu_fwd(vnb, ws_ref, sv_scr, T):
    lane = lax.broadcasted_iota(jnp.int32, (CHUNK, 128), 1)
    for n in range(T // CHUNK):
        rows = slice(n * CHUNK, (n + 1) * CHUNK)
        for b in range(DG // 128):
            cols = slice(128 * b, 128 * b + 128)
            vb = vnb[rows, cols]
            sv_scr[rows, cols] = jnp.where(lane < 64, _dot(ws_ref[2 * b], vb), _dot(ws_ref[2 * b + 1], vb))


def _mix_fwd(x, mod, gn, wmi, wmo, wp, ps, lg, lb, ws, bias, name):
    S = x.shape[0]
    T = min(T_MIX, S)

    def body(x_ref, mod_ref, gn_ref, wmi_ref, wmo_ref, wp_ref, ps_ref, lg_ref, lb_ref, ws_ref, bias_ref,
             xo_ref, carry_scr, ext_scr, sv_scr, ycat_scr):
        i = pl.program_id(0)

        @pl.when(i == 0)
        def _():
            carry_scr[...] = jnp.zeros_like(carry_scr)

        x = x_ref[...]
        _, _, _, h = _rms_mod(x, gn_ref[...], mod_ref[3:4, :], mod_ref[4:5, :])
        proj = _dot_nt(h.astype(BF), wmi_ref[...])
        xa = proj[:, 0:DP]
        p = _pool_fwd(xa, carry_scr[...], ext_scr, _pool_counts(i * T, T), T)
        carry_scr[...] = xa[T - HALO:T, :]
        for gi in range(4):
            cols = slice(128 * gi, 128 * gi + 128)
            ycat_scr[:, cols] = (_dot(p[gi].astype(BF), wp_ref[gi]) * ps_ref[:, cols]).astype(BF)
        u, _ = _gelu(proj[:, DP:DP + DG])
        v, _ = _gelu(proj[:, DP + DG:DPROJ])
        mu = jnp.mean(v, axis=-1, keepdims=True)
        vc = v - mu
        rstd = lax.rsqrt(jnp.mean(vc * vc, axis=-1, keepdims=True) + EPS)
        vn = vc * rstd * lg_ref[...] + lb_ref[...]
        _sgu_fwd(vn.astype(BF), ws_ref, sv_scr, T)
        for n in range(T // CHUNK):
            rows = slice(n * CHUNK, (n + 1) * CHUNK)
            ycat_scr[rows, DP:D] = (u[rows, :] * (sv_scr[rows, :] + bias_ref[...])).astype(BF)
        xo_ref[...] = x + mod_ref[5:6, :] * _dot(ycat_scr[...], wmo_ref[...])

    full = lambda shape: pl.BlockSpec(shape, lambda i: (0,) * len(shape))
    return pl.pallas_call(
        body, name=name, grid=(S // T,),
        out_shape=jax.ShapeDtypeStruct((S, D), F32),
        in_specs=[pl.BlockSpec((T, D), lambda i: (i, 0)), full((9, D)), full((1, D)), full((DPROJ, D)), full((D, D)),
                  full((4, 128, 128)), full((1, DP)), full((1, DG)), full((1, DG)), full((8, CHUNK, CHUNK)),
                  full((CHUNK, DG))],
        out_specs=pl.BlockSpec((T, D), lambda i: (i, 0)),
        scratch_shapes=[pltpu.VMEM((HALO, DP), F32), pltpu.VMEM((T + HALO, DP), F32), pltpu.VMEM((T, DG), F32),
                        pltpu.VMEM((T, D), BF)],
        compiler_params=_cparams(48, ("arbitrary",)),
    )(x, mod, gn, wmi, wmo, wp, ps, lg, lb, ws, bias)


def _mix_bwd(x, dxo, mod, gn, wmi, wmo, wp, ps, lg, lb, ws, bias, name, exchange=()):
    S = x.shape[0]
    T = min(T_MIX, S)
    nS = S // T
    hb = T // HALO
    ne = len(exchange)

    def body(*refs):
        (x_ref, xh_ref, dxo_ref, mod_ref, gn_ref, wmi_ref, wmo_ref, wp_ref, ps_ref, lg_ref, lb_ref, ws_ref,
         bias_ref) = refs[:13]
        sendbufs = refs[13:13 + ne]
        dxi_ref, gwmi_out, gwmo_out, gwp_ref, gws_ref, st_ref, vec_ref, dbias_ref = refs[13 + ne:21 + ne]
        recvbufs = refs[21 + ne:21 + 2 * ne]
        (carry_scr, ext_scr, qext_scr, sv_scr, dvn_scr, ycat_scr, dproj_scr, gwmi_ref,
         gwmo_ref) = refs[21 + 2 * ne:30 + 2 * ne]
        sems = refs[30 + 2 * ne:]
        i = pl.program_id(0)
        t = nS - 1 - i
        gn = gn_ref[...]
        shift, scale, gate = mod_ref[3:4, :], mod_ref[4:5, :], mod_ref[5:6, :]

        @pl.when(i == 0)
        def _():
            if ne:
                _exchange_phase("start", sendbufs, recvbufs, sems)
            carry_scr[...] = jnp.zeros_like(carry_scr)
            gwmi_ref[...] = jnp.zeros_like(gwmi_ref)
            gwmo_ref[...] = jnp.zeros_like(gwmo_ref)
            gwp_ref[...] = jnp.zeros_like(gwp_ref)
            gws_ref[...] = jnp.zeros_like(gws_ref)
            st_ref[...] = jnp.zeros_like(st_ref)
            vec_ref[...] = jnp.zeros_like(vec_ref)
            dbias_ref[...] = jnp.zeros_like(dbias_ref)

        x = x_ref[...]
        dxo = dxo_ref[...]
        r, xhat, n, h = _rms_mod(x, gn, shift, scale)
        hbf = h.astype(BF)
        proj = _dot_nt(hbf, wmi_ref[...])
        xa = proj[:, 0:DP]
        zu = proj[:, DP:DP + DG]
        zv = proj[:, DP + DG:DPROJ]
        _, _, _, hh = _rms_mod(xh_ref[...], gn, shift, scale)
        halo = _dot_nt(hh.astype(BF), wmi_ref[0:DP, :])
        halo = jnp.where(t == 0, 0.0, halo)
        cnts = _pool_counts(t * T, T)
        p = _pool_fwd(xa, halo, ext_scr, cnts, T)
        m = []
        for gi in range(4):
            cols = slice(128 * gi, 128 * gi + 128)
            m.append(_dot(p[gi].astype(BF), wp_ref[gi]))
            ycat_scr[:, cols] = (m[gi] * ps_ref[:, cols]).astype(BF)
        u, tu = _gelu(zu)
        v, tv = _gelu(zv)
        mu = jnp.mean(v, axis=-1, keepdims=True)
        vc = v - mu
        rstd = lax.rsqrt(jnp.mean(vc * vc, axis=-1, keepdims=True) + EPS)
        vhat = vc * rstd
        lg = lg_ref[...]
        vnb = (vhat * lg + lb_ref[...]).astype(BF)
        _sgu_fwd(vnb, ws_ref, sv_scr, T)
        for nck in range(T // CHUNK):
            rows = slice(nck * CHUNK, (nck + 1) * CHUNK)
            sv_scr[rows, :] = sv_scr[rows, :] + bias_ref[...]
        sv = sv_scr[...]
        ycat_scr[:, DP:D] = (u * sv).astype(BF)

        gwmo_ref[...] += _dot_tn(ycat_scr[...], dxo.astype(BF))
        dyc = _dot_nt((dxo * gate).astype(BF), wmo_ref[...])
        dya = dyc[:, 0:DP]
        dyb = dyc[:, DP:D]

        dps = []
        dp = []
        for gi in range(4):
            cols = slice(128 * gi, 128 * gi + 128)
            dps.append(jnp.sum(dya[:, cols] * m[gi], axis=0, keepdims=True))
            dm = (dya[:, cols] * ps_ref[:, cols]).astype(BF)
            gwp_ref[gi] += _dot_tn(p[gi].astype(BF), dm)
            dp.append(_dot_nt(dm, wp_ref[gi]))
            qext_scr[0:T, cols] = dp[gi] / cnts[gi]
        qext_scr[T:T + HALO, :] = carry_scr[...]
        for gi, w in enumerate(WINDOWS):
            cols = slice(128 * gi, 128 * gi + 128)
            acc = qext_scr[0:T, cols]
            for k in range(1, w):
                acc = acc + qext_scr[k:k + T, cols]
            dproj_scr[:, cols] = (acc - dp[gi]).astype(BF)
        carry_scr[...] = qext_scr[0:HALO, :]

        du = dyb * sv
        dsv = dyb * u
        lane = lax.broadcasted_iota(jnp.int32, (CHUNK, 128), 1)
        dbias = jnp.zeros((CHUNK, DG), F32)
        for nck in range(T // CHUNK):
            rows = slice(nck * CHUNK, (nck + 1) * CHUNK)
            dbias = dbias + dsv[rows, :]
            for b in range(DG // 128):
                cols = slice(128 * b, 128 * b + 128)
                dsvb = dsv[rows, cols]
                vb = vnb[rows, cols]
                gws_ref[2 * b] += _dot_nt(jnp.where(lane < 64, dsvb, 0.0).astype(BF), vb)
                gws_ref[2 * b + 1] += _dot_nt(jnp.where(lane < 64, 0.0, dsvb).astype(BF), vb)
                dsvbb = dsvb.astype(BF)
                dvn_scr[rows, cols] = jnp.where(lane < 64, _dot_tn(ws_ref[2 * b], dsvbb),
                                                _dot_tn(ws_ref[2 * b + 1], dsvbb))
        dbias_ref[...] += dbias
        dvn = dvn_scr[...]
        dlg = jnp.sum(dvn * vhat, axis=0, keepdims=True)
        dlb = jnp.sum(dvn, axis=0, keepdims=True)
        dvhat = dvn * lg
        dv = rstd * (dvhat - jnp.mean(dvhat, axis=-1, keepdims=True)
                     - vhat * jnp.mean(dvhat * vhat, axis=-1, keepdims=True))
        dproj_scr[:, DP:DP + DG] = (du * _gelu_grad(zu, tu)).astype(BF)
        dproj_scr[:, DP + DG:DPROJ] = (dv * _gelu_grad(zv, tv)).astype(BF)
        vec_ref[...] += _rows3(jnp.concatenate(dps, axis=1), dlg, dlb, DP)

        dproj = dproj_scr[...]
        gwmi_ref[...] += _dot_tn(dproj, hbf)
        dh = _dot(dproj, wmi_ref[...])
        dxi, dshift, dscale, dgn = _rms_mod_bwd(dh, dxo, r, xhat, n, gn, scale)
        dxi_ref[...] = dxi
        st_ref[...] += _rows3(dshift, dscale, dgn, D)

        @pl.when(i == nS - 1)
        def _():
            acc = gwmo_ref[...]
            dgate = jnp.sum(acc * wmo_ref[...].astype(F32), axis=0, keepdims=True)
            row = lax.broadcasted_iota(jnp.int32, (8, D), 0)
            st_ref[...] += jnp.where(row == 3, dgate, 0.0)
            gwmo_out[...] = (acc * gate).astype(BF)
            gwmi_out[...] = gwmi_ref[...].astype(BF)
            tt = lax.broadcasted_iota(jnp.int32, (CHUNK, CHUNK), 0)
            ss = lax.broadcasted_iota(jnp.int32, (CHUNK, CHUNK), 1)
            for hd in range(8):
                gws_ref[hd] = jnp.where(tt >= ss, gws_ref[hd], 0.0)
            if ne:
                _exchange_phase("wait", sendbufs, recvbufs, sems)

    full = lambda shape: pl.BlockSpec(shape, lambda i: (0,) * len(shape))
    return pl.pallas_call(
        body, name=name, grid=(nS,),
        out_shape=[jax.ShapeDtypeStruct((S, D), F32), jax.ShapeDtypeStruct((DPROJ, D), BF),
                   jax.ShapeDtypeStruct((D, D), BF), jax.ShapeDtypeStruct((4, 128, 128), F32),
                   jax.ShapeDtypeStruct((8, CHUNK, CHUNK), F32), jax.ShapeDtypeStruct((8, D), F32),
                   jax.ShapeDtypeStruct((8, DP), F32), jax.ShapeDtypeStruct((CHUNK, DG), F32)] + _like(exchange),
        in_specs=[pl.BlockSpec((T, D), lambda i: (nS - 1 - i, 0)),
                  pl.BlockSpec((HALO, D), lambda i: (jnp.maximum((nS - 1 - i) * hb - 1, 0), 0)),
                  pl.BlockSpec((T, D), lambda i: (nS - 1 - i, 0)),
                  full((9, D)), full((1, D)), full((DPROJ, D)), full((D, D)),
                  full((4, 128, 128)), full((1, DP)), full((1, DG)), full((1, DG)), full((8, CHUNK, CHUNK)),
                  full((CHUNK, DG))] + [HBM] * ne,
        out_specs=[pl.BlockSpec((T, D), lambda i: (nS - 1 - i, 0)), full((DPROJ, D)), full((D, D)),
                   full((4, 128, 128)), full((8, CHUNK, CHUNK)), full((8, D)), full((8, DP)), full((CHUNK, DG))]
                  + [HBM] * ne,
        scratch_shapes=[pltpu.VMEM((HALO, DP), F32), pltpu.VMEM((T + HALO, DP), F32),
                        pltpu.VMEM((T + HALO, DP), F32), pltpu.VMEM((T, DG), F32), pltpu.VMEM((T, DG), F32),
                        pltpu.VMEM((T, D), BF), pltpu.VMEM((T, DPROJ), BF), pltpu.VMEM((DPROJ, D), F32),
                        pltpu.VMEM((D, D), F32)] + (_comm_sems(ne) if ne else []),
        compiler_params=_cparams(56, ("arbitrary",)),
    )(x, x, dxo, mod, gn, wmi, wmo, wp, ps, lg, lb, ws, bias, *exchange)


def kernel(x, c, w_ada, b_ada, norm_ffn1_g, ffn1_w_in, ffn1_w_out, norm_mix_g, w_mix_in, w_pool, pool_scale, gmlp_ln_g, gmlp_ln_b, w_spatial, b_spatial, w_mix_out, norm_ffn2_g, ffn2_w_in, ffn2_w_out, norm_final_g, loss_target, m_w_ada, m_b_ada, m_norm_ffn1_g, m_ffn1_w_in, m_ffn1_w_out, m_norm_mix_g, m_w_mix_in, m_w_pool, m_pool_scale, m_gmlp_ln_g, m_gmlp_ln_b, m_w_spatial, m_b_spatial, m_w_mix_out, m_norm_ffn2_g, m_ffn2_w_in, m_ffn2_w_out, m_norm_final_g, v_w_ada, v_b_ada, v_norm_ffn1_g, v_ffn1_w_in, v_ffn1_w_out, v_norm_mix_g, v_w_mix_in, v_w_pool, v_pool_scale, v_gmlp_ln_g, v_gmlp_ln_b, v_w_spatial, v_b_spatial, v_w_mix_out, v_norm_ffn2_g, v_ffn2_w_in, v_ffn2_w_out, v_norm_final_g):
    weights = dict(w_ada=w_ada, b_ada=b_ada, norm_ffn1_g=norm_ffn1_g, ffn1_w_in=ffn1_w_in, ffn1_w_out=ffn1_w_out,
                   norm_mix_g=norm_mix_g, w_mix_in=w_mix_in, w_pool=w_pool, pool_scale=pool_scale,
                   gmlp_ln_g=gmlp_ln_g, gmlp_ln_b=gmlp_ln_b, w_spatial=w_spatial, b_spatial=b_spatial,
                   w_mix_out=w_mix_out, norm_ffn2_g=norm_ffn2_g, ffn2_w_in=ffn2_w_in, ffn2_w_out=ffn2_w_out,
                   norm_final_g=norm_final_g)
    mom1 = dict(w_ada=m_w_ada, b_ada=m_b_ada, norm_ffn1_g=m_norm_ffn1_g, ffn1_w_in=m_ffn1_w_in,
                ffn1_w_out=m_ffn1_w_out, norm_mix_g=m_norm_mix_g, w_mix_in=m_w_mix_in, w_pool=m_w_pool,
                pool_scale=m_pool_scale, gmlp_ln_g=m_gmlp_ln_g, gmlp_ln_b=m_gmlp_ln_b, w_spatial=m_w_spatial,
                b_spatial=m_b_spatial, w_mix_out=m_w_mix_out, norm_ffn2_g=m_norm_ffn2_g, ffn2_w_in=m_ffn2_w_in,
                ffn2_w_out=m_ffn2_w_out, norm_final_g=m_norm_final_g)
    mom2 = dict(w_ada=v_w_ada, b_ada=v_b_ada, norm_ffn1_g=v_norm_ffn1_g, ffn1_w_in=v_ffn1_w_in,
                ffn1_w_out=v_ffn1_w_out, norm_mix_g=v_norm_mix_g, w_mix_in=v_w_mix_in, w_pool=v_w_pool,
                pool_scale=v_pool_scale, gmlp_ln_g=v_gmlp_ln_g, gmlp_ln_b=v_gmlp_ln_b, w_spatial=v_w_spatial,
                b_spatial=v_b_spatial, w_mix_out=v_w_mix_out, norm_ffn2_g=v_norm_ffn2_g, ffn2_w_in=v_ffn2_w_in,
                ffn2_w_out=v_ffn2_w_out, norm_final_g=v_norm_final_g)
    order = list(weights)
    xs = x[0]
    target = loss_target[0]
    transposed = ("ffn1_w_in", "w_mix_in", "ffn2_w_in")
    big = ("ffn1_w_in", "ffn1_w_out", "w_mix_in", "w_mix_out", "ffn2_w_in", "ffn2_w_out")
    local = lambda a, k: a[0].T if k in transposed else a[0]
    wc = w_ada.shape[2]

    shard = dict(zip(big, _cast_shards([local(weights[k], k) for k in big])))
    modp, cact_all = _ada_forward(jnp.broadcast_to(c, (8, D)), w_ada[0], b_ada.reshape(NDEV, wc))
    mod = modp.reshape(9, D)
    g_w1_in, g_w1_out = _all_gather([shard["ffn1_w_in"], shard["ffn1_w_out"]], "gather_ffn1")
    w1_in = g_w1_in.reshape(2, F, D)
    w1_out = g_w1_out.reshape(F, D)

    x1, gu1, h1, g_wmi, g_wmo, g_w2_out, g_w2_in = _ffn_fwd(
        xs, mod, norm_ffn1_g, w1_in, w1_out, 0, "ffn1_fwd",
        gather=[shard["w_mix_in"], shard["w_mix_out"], shard["ffn2_w_out"], shard["ffn2_w_in"]])
    wmi = g_wmi.reshape(DPROJ, D)
    wmo = g_wmo.reshape(D, D)
    w2_in = g_w2_in.reshape(2, F, D)
    w2_out = g_w2_out.reshape(F, D)
    tril = jnp.tril(jnp.ones((CHUNK, CHUNK), dtype=bool))
    ws_b = jnp.where(tril[None], w_spatial[0], 0.0).astype(BF)
    wp_b = w_pool[0].astype(BF)
    bias = jnp.repeat(b_spatial[0].T, DG // 8, axis=1)
    mix_args = (wmi, wmo, wp_b, pool_scale, gmlp_ln_g, gmlp_ln_b, ws_b, bias)
    x2 = _mix_fwd(x1, mod, norm_mix_g, *mix_args, "mix_fwd")
    dx3, gu3, h3, st_f = _ffn_fwd(x2, mod, norm_ffn2_g, w2_in, w2_out, 2, "ffn2_fwd",
                                  loss=(norm_final_g.reshape(1, D), target))

    slots = lambda a: a.reshape(NDEV, a.size // (NDEV * D), D)
    dgu3, d_w2_out, dgate3 = _ffn_bwd_hidden(dx3, mod, gu3, w2_out, 2, "ffn2_bwd_hidden")
    d_w2_in = _ffn_bwd_win(h3, dgu3, "ffn2_bwd_win")[0]
    dx2, st3 = _ffn_bwd_input(dgu3, w2_in, x2, dx3, mod, norm_ffn2_g, 2, "ffn2_bwd_input")
    dx1, d_wmi, d_wmo, d_wp, d_ws, st2, vec2, dbias, r_w2_in, r_w2_out = _mix_bwd(
        x1, dx2, mod, norm_mix_g, *mix_args, "mix_bwd", exchange=[slots(d_w2_in), slots(d_w2_out)])
    dgu1, d_w1_out, dgate1, r_wmi, r_wmo = _ffn_bwd_hidden(
        dx1, mod, gu1, w1_out, 0, "ffn1_bwd_hidden", exchange=[slots(d_wmi), slots(d_wmo)])
    d_w1_in, r_w1_out = _ffn_bwd_win(h1, dgu1, "ffn1_bwd_win", exchange=[slots(d_w1_out)])
    n_tiles = xs.shape[0] // min(T_FFN, xs.shape[0])
    cut = max(n_tiles - 2, 1)
    dx0, st1, r_w1_in = _ffn_bwd_input(dgu1, w1_in, xs, dx1, mod, norm_ffn1_g, 0, "ffn1_bwd_input",
                                       exchange=[slots(d_w1_in)], tiles=(0, cut))
    if cut < n_tiles:
        dx0, st1b = _ffn_bwd_input(dgu1, w1_in, xs, dx1, mod, norm_ffn1_g, 0, "ffn1_bwd_input_rest",
                                   tiles=(cut, n_tiles), into=dx0)
    else:
        st1b = jnp.zeros_like(st1)

    received = dict(ffn1_w_in=r_w1_in, ffn1_w_out=r_w1_out, w_mix_in=r_wmi, w_mix_out=r_wmo,
                    ffn2_w_in=r_w2_in, ffn2_w_out=r_w2_out)
    tiles = dict(ffn1_w_in=176, ffn1_w_out=176, w_mix_in=96, w_mix_out=128, ffn2_w_in=176, ffn2_w_out=176)
    result = {}
    for k, recv in received.items():
        res = _sum_adamw(recv, local(weights[k], k), local(mom1[k], k), local(mom2[k], k), tiles[k], "update_" + k)
        result[k] = tuple((a.T if k in transposed else a)[None] for a in res)
    row = lambda a: a.reshape(1, D)
    params = {k: (weights[k], mom1[k], mom2[k]) for k in SMALL}
    params["norm_final_g"] = (row(norm_final_g), row(m_norm_final_g), row(v_norm_final_g))
    tot, rsum, dmine = _small_reduce(d_ws, d_wp, dbias, st1, st1b, st2, st3, st_f, vec2, dgate1, dgate3)
    small, loss_row = _small_update(tot, rsum, params)
    result.update(small)
    result["norm_final_g"] = tuple(a.reshape(D) for a in small["norm_final_g"])
    result["w_ada"] = tuple(a[None] for a in _ada_update(cact_all, dmine, w_ada[0], m_w_ada[0], v_w_ada[0], 256))

    return (loss_row[0, 0], dx0[None], *[result[k][0] for k in order], *[result[k][1] for k in order],
            *[result[k][2] for k in order], *[result[k][3] for k in order])
```

```python
import math

import jax
import jax.numpy as jnp
from jax import lax
from jax.experimental import pallas as pl
from jax.experimental.pallas import tpu as pltpu

D = 1024
F = 2816
DP = 512
DG = 512
DPROJ = DP + 2 * DG
CHUNK = 128
WINDOWS = (2, 4, 8, 16)
HALO = 16
NDEV = 8
T_FFN = 512
T_MIX = 256
T_WIN = 2048
EPS = 1e-6
LR, B1, B2, AEPS, WD, STEP = 0.001, 0.9, 0.999, 1e-08, 0.01, 10
BC1 = 1.0 - B1 ** STEP
BC2 = 1.0 - B2 ** STEP
GELU_C = math.sqrt(2.0 / math.pi)
GELU_A = 0.044715

BF = jnp.bfloat16
F32 = jnp.float32
MESH = pl.DeviceIdType.MESH
HBM = pl.BlockSpec(memory_space=pltpu.HBM)


def _whole(a):
    return pl.BlockSpec(a.shape, lambda i: (0,) * len(a.shape))


NT = (((1,), (1,)), ((), ()))
TN = (((0,), (0,)), ((), ()))


def _dot(a, b):
    return jnp.dot(a, b, preferred_element_type=F32)


def _dot_nt(a, b):
    return lax.dot_general(a, b, NT, preferred_element_type=F32)


def _dot_tn(a, b):
    return lax.dot_general(a, b, TN, preferred_element_type=F32)


def _cparams(vmem_mb, sem=None):
    kw = dict(vmem_limit_bytes=vmem_mb * 1024 * 1024)
    if sem is not None:
        kw["dimension_semantics"] = sem
    return pltpu.CompilerParams(**kw)


def _position():
    return lax.axis_index("x"), lax.axis_index("y"), lax.axis_index("c")


def _slot(p):
    return 4 * p[0] + 2 * p[1] + p[2]


def _flip(me, d):
    x, y, c = me
    return (1 - x if d & 4 else x, 1 - y if d & 2 else y, 1 - c if d & 1 else c)


def _remote(src, dst, send_sem, recv_sem, to):
    return pltpu.make_async_remote_copy(src_ref=src, dst_ref=dst, send_sem=send_sem, recv_sem=recv_sem,
                                        device_id=to, device_id_type=MESH)


def _comm_sems(n):
    return [pltpu.SemaphoreType.DMA((n, 7)), pltpu.SemaphoreType.DMA((n, 7)), pltpu.SemaphoreType.DMA((n,))]


def _gather_phase(phase, xs, outs, sems):
    send_sems, recv_sems, local_sems = sems
    n = len(xs)
    me = _position()
    x, y, c = me
    sibling = (x, y, 1 - c)
    xn, yn, diag = (1 - x, y), (x, 1 - y), (1 - x, 1 - y)
    relay_from = (x + c * (1 - 2 * x), y + (1 - c) * (1 - 2 * y))
    relay_to = (x + (1 - c) * (1 - 2 * x), y + c * (1 - 2 * y))

    def copy(a, k, block, to, src=None):
        dst = outs[a].at[_slot(block)]
        return _remote(dst if src is None else src, dst, send_sems.at[a, k], recv_sems.at[a, k], to)

    def mine(a):
        return pltpu.make_async_copy(xs[a], outs[a].at[_slot(me)], local_sems.at[a])

    def first(a):
        return [copy(a, 0, me, sibling, src=xs[a]), copy(a, 1, me, (*xn, c), src=xs[a]),
                copy(a, 2, me, (*yn, c), src=xs[a])]

    def second(a):
        return [copy(a, 3, (*relay_from, c), (*relay_to, c)), copy(a, 4, (*xn, c), sibling),
                copy(a, 5, (*yn, c), sibling)]

    def third(a):
        return copy(a, 6, (*diag, c), sibling)

    if phase == "start":
        for a in range(n):
            mine(a).start()
            for cp in first(a):
                cp.start()
    elif phase == "forward":
        for a in range(n):
            copy(a, 1, (*xn, c), me).wait_recv()
            copy(a, 2, (*yn, c), me).wait_recv()
            for cp in second(a):
                cp.start()
    else:
        for a in range(n):
            copy(a, 3, (*diag, c), me).wait_recv()
            third(a).start()
        for a in range(n):
            copy(a, 0, sibling, me).wait_recv()
            for k, chip in ((4, xn), (5, yn), (6, diag)):
                copy(a, k, (*chip, 1 - c), me).wait_recv()
        for a in range(n):
            for cp in first(a) + second(a) + [third(a)]:
                cp.wait_send()
            mine(a).wait()


def _exchange_phase(phase, xs, outs, sems):
    send_sems, recv_sems, local_sems = sems
    me = _position()
    for a in range(len(xs)):
        copies = [pltpu.make_async_copy(xs[a].at[_slot(me)], outs[a].at[_slot(me)], local_sems.at[a])]
        for d in range(1, NDEV):
            to = _flip(me, d)
            copies.append(_remote(xs[a].at[_slot(to)], outs[a].at[_slot(me)],
                                  send_sems.at[a, d - 1], recv_sems.at[a, d - 1], to))
        for cp in copies:
            if phase == "start":
                cp.start()
            else:
                cp.wait()


def _like(bufs):
    return [jax.ShapeDtypeStruct(b.shape, b.dtype) for b in bufs]


def _rms_mod(x, gn, shift, scale):
    ms = jnp.mean(x * x, axis=-1, keepdims=True)
    r = lax.rsqrt(ms + EPS)
    xhat = x * r
    n = xhat * gn
    h = n * (1.0 + scale) + shift
    return r, xhat, n, h


def _rms_mod_bwd(dh, dres, r, xhat, n, gn, scale):
    dshift = jnp.sum(dh, axis=0, keepdims=True)
    dscale = jnp.sum(dh * n, axis=0, keepdims=True)
    dn = dh * (1.0 + scale)
    dgn = jnp.sum(dn * xhat, axis=0, keepdims=True)
    dxhat = dn * gn
    dx = dres + r * (dxhat - xhat * jnp.mean(dxhat * xhat, axis=-1, keepdims=True))
    return dx, dshift, dscale, dgn


def _final_norm_loss(x, gf, target):
    r = lax.rsqrt(jnp.mean(x * x, axis=-1, keepdims=True) + EPS)
    xhat = x * r
    e = xhat * gf - target
    part = 0.5 * jnp.sum(jnp.sum(e * e, axis=-1, keepdims=True), axis=0, keepdims=True) / D
    dy = e / D
    dgf = jnp.sum(dy * xhat, axis=0, keepdims=True)
    dxhat = dy * gf
    dx = r * (dxhat - xhat * jnp.mean(dxhat * xhat, axis=-1, keepdims=True))
    return dx, dgf, part


def _rows3(a, b, c, width):
    row = lax.broadcasted_iota(jnp.int32, (8, width), 0)
    z = jnp.zeros((8, width), F32)
    return jnp.where(row == 0, a, z) + jnp.where(row == 1, b, z) + jnp.where(row == 2, c, z)


def _sigmoid(x):
    return 0.5 * jnp.tanh(0.5 * x) + 0.5


def _gelu(x):
    t = jnp.tanh(GELU_C * (x + GELU_A * x * x * x))
    return 0.5 * x * (1.0 + t), t


def _gelu_grad(x, t):
    return 0.5 * (1.0 + t) + 0.5 * x * (1.0 - t * t) * GELU_C * (1.0 + 3.0 * GELU_A * x * x)


def _adamw(w, g, m, v):
    m = B1 * m + (1.0 - B1) * g
    v = B2 * v + (1.0 - B2) * (g * g)
    m_hat = m / BC1
    v_hat = v / BC2
    delta = -LR * (m_hat / (jnp.sqrt(v_hat) + AEPS) + WD * w)
    return delta, m, v


def _cast_shards(shards):
    n = len(shards)

    def body(*refs):
        for a in range(n):
            refs[n + a][...] = refs[a][...].astype(BF)

    outs = [jax.ShapeDtypeStruct(s.shape, BF) for s in shards]
    return pl.pallas_call(
        body, name="cast_shards", grid=(1,), out_shape=outs,
        in_specs=[_whole(s) for s in shards], out_specs=[_whole(s) for s in outs],
        compiler_params=_cparams(48, ("arbitrary",)),
    )(*shards)


def _ada_forward(c8, w_ada, b8, shards):
    wc = w_ada.shape[1]
    n = len(shards)

    def body(*refs):
        c8_ref, w_ref, b8_ref = refs[:3]
        xs = refs[3:3 + n]
        mod_ref, cact_ref = refs[3 + n:5 + n]
        gathered = refs[5 + n:5 + 2 * n]
        call_ref, mall_ref, send_sems, recv_sems = refs[5 + 2 * n:9 + 2 * n]
        gsems = refs[9 + 2 * n:]
        me = _position()
        my = _slot(me)
        row = lax.broadcasted_iota(jnp.int32, (8, 1), 0)
        call_ref[my] = c8_ref[...]
        sends = []
        for d in range(1, NDEV):
            to = _flip(me, d)
            sends.append(_remote(call_ref.at[my], call_ref.at[my], send_sems.at[0, d - 1], recv_sems.at[0, d - 1], to))
        for cp in sends:
            cp.start()
        _gather_phase("start", xs, gathered, gsems)
        for cp in sends:
            cp.wait()
        c_all = jnp.zeros((8, D), F32)
        for k in range(NDEV):
            c_all = c_all + jnp.where(row == k, call_ref[k], 0.0)
        cact = c_all * jax.nn.sigmoid(c_all)
        cact_ref[...] = cact
        part = _dot(cact.astype(BF), w_ref[...].astype(BF))
        mall_ref[my] = part
        sends = []
        for d in range(1, NDEV):
            to = _flip(me, d)
            sends.append(_remote(mall_ref.at[my], mall_ref.at[my], send_sems.at[1, d - 1], recv_sems.at[1, d - 1], to))
        for cp in sends:
            cp.start()
        for cp in sends:
            cp.wait()
        out = jnp.zeros((8, wc), F32)
        for k in range(NDEV):
            piece = jnp.sum(jnp.where(row == my, mall_ref[k], 0.0), axis=0, keepdims=True)
            out = out + jnp.where(row == k, piece, 0.0)
        mod_ref[...] = out + b8_ref[...]
        _gather_phase("forward", xs, gathered, gsems)
        _gather_phase("finish", xs, gathered, gsems)

    outs = [jax.ShapeDtypeStruct((8, wc), F32), jax.ShapeDtypeStruct((8, D), F32)]
    return pl.pallas_call(
        body, name="ada_forward", grid=(1,),
        out_shape=outs + [jax.ShapeDtypeStruct((NDEV,) + s.shape, s.dtype) for s in shards],
        in_specs=[_whole(a) for a in (c8, w_ada, b8)] + [HBM] * n,
        out_specs=[_whole(a) for a in outs] + [HBM] * n,
        scratch_shapes=[pltpu.VMEM((NDEV, 8, D), F32), pltpu.VMEM((NDEV, 8, wc), F32),
                        pltpu.SemaphoreType.DMA((2, 7)), pltpu.SemaphoreType.DMA((2, 7))] + _comm_sems(n),
        compiler_params=_cparams(32, ("arbitrary",)),
    )(c8, w_ada, b8, *shards)


MATS = ("w_spatial", "w_pool", "b_spatial")
VECS = ("norm_ffn1_g", "norm_mix_g", "norm_ffn2_g", "norm_final_g", "pool_scale", "gmlp_ln_g", "gmlp_ln_b", "b_ada")
VEC_WIDTH = dict(norm_ffn1_g=D, norm_mix_g=D, norm_ffn2_g=D, norm_final_g=D, pool_scale=DP, gmlp_ln_g=DG,
                 gmlp_ln_b=DG, b_ada=9 * D)
MAT_ROWS = 1600
MAT_SLICE = MAT_ROWS // NDEV
VEC_LANES = sum(VEC_WIDTH.values()) + 128
DMOD_AT = VEC_LANES - 128 - 9 * D
SMALL = MATS + VECS


def _small_reduce(g_ws, g_wp, dbias, st1, st2, st3, st_f, vec2, dgate1, dgate3):
    wc = 9 * D // NDEV

    def body(g_ws_ref, g_wp_ref, dbias_ref, st1_ref, st2_ref, st3_ref, stf_ref, vec2_ref, dg1_ref, dg3_ref,
             tot_ref, rsum_ref, dmine_ref,
             pack_ref, rs_ref, ag_ref, rv_ref, dmp_ref, dw_ref, send_sems, recv_sems):
        me = _position()
        my = _slot(me)

        pack_ref[0:1024, :] = g_ws_ref[...].reshape(1024, 128)
        pack_ref[1024:1536, :] = g_wp_ref[...].reshape(512, 128)
        ch = lax.broadcasted_iota(jnp.int32, (DG, 128), 0)
        hd = lax.broadcasted_iota(jnp.int32, (DG, 128), 1)
        sel = jnp.where(ch // 64 == hd, 1.0, 0.0).astype(F32)
        heads = jnp.dot(dbias_ref[...], sel, preferred_element_type=F32, precision=lax.Precision.HIGHEST)
        pack_ref[1536:1544, :] = heads.T[0:8, :]
        pack_ref[1544:MAT_ROWS, :] = jnp.zeros((MAT_ROWS - 1544, 128), F32)
        dgate1 = dg1_ref[0:1, :] + dg1_ref[8:9, :]
        dgate3 = dg3_ref[0:1, :] + dg3_ref[8:9, :]
        row = jnp.concatenate(
            [st1_ref[2:3, :], st2_ref[2:3, :], st3_ref[2:3, :], stf_ref[0:1, :],
             vec2_ref[0:1, :], vec2_ref[1:2, :], vec2_ref[2:3, :],
             st1_ref[0:1, :], st1_ref[1:2, :], dgate1, st2_ref[0:1, :], st2_ref[1:2, :], st2_ref[3:4, :],
             st3_ref[0:1, :], st3_ref[1:2, :], dgate3, stf_ref[1:2, 0:128]], axis=1)
        rv_ref[my] = row
        for k in range(NDEV):
            dmp_ref[k] = row[:, DMOD_AT + wc * k:DMOD_AT + wc * (k + 1)]
        dw_ref[my] = dmp_ref[my]
        rs_ref[my] = pack_ref[pl.ds(pl.multiple_of(my * MAT_SLICE, 8), MAT_SLICE), :]

        first = []
        for d in range(1, NDEV):
            to = _flip(me, d)
            theirs = pl.ds(pl.multiple_of(_slot(to) * MAT_SLICE, 8), MAT_SLICE)
            first.append(_remote(pack_ref.at[theirs, :], rs_ref.at[my], send_sems.at[0, d - 1], recv_sems.at[0, d - 1], to))
            first.append(_remote(dmp_ref.at[_slot(to)], dw_ref.at[my], send_sems.at[1, d - 1], recv_sems.at[1, d - 1], to))
            first.append(_remote(rv_ref.at[my], rv_ref.at[my], send_sems.at[2, d - 1], recv_sems.at[2, d - 1], to))
        for cp in first:
            cp.start()
        for cp in first:
            cp.wait()
        red = rs_ref[0]
        for k in range(1, NDEV):
            red = red + rs_ref[k]
        ag_ref[my] = red
        second = []
        for d in range(1, NDEV):
            to = _flip(me, d)
            second.append(_remote(ag_ref.at[my], ag_ref.at[my], send_sems.at[3, d - 1], recv_sems.at[3, d - 1], to))
        for cp in second:
            cp.start()

        rsum = rv_ref[0]
        for k in range(1, NDEV):
            rsum = rsum + rv_ref[k]
        rsum_ref[...] = rsum
        r8 = lax.broadcasted_iota(jnp.int32, (8, 1), 0)
        dmine = jnp.zeros((8, wc), F32)
        for k in range(NDEV):
            dmine = dmine + jnp.where(r8 == k, dw_ref[k], 0.0)
        dmine_ref[...] = dmine

        for cp in second:
            cp.wait()
        for k in range(NDEV):
            tot_ref[k * MAT_SLICE:(k + 1) * MAT_SLICE, :] = ag_ref[k]

    ins = (g_ws, g_wp, dbias, st1, st2, st3, st_f, vec2, dgate1, dgate3)
    outs = [jax.ShapeDtypeStruct((MAT_ROWS, 128), F32), jax.ShapeDtypeStruct((1, VEC_LANES), F32),
            jax.ShapeDtypeStruct((8, wc), F32)]
    return pl.pallas_call(
        body, name="small_reduce", grid=(1,), out_shape=outs,
        in_specs=[_whole(a) for a in ins], out_specs=[_whole(a) for a in outs],
        scratch_shapes=[pltpu.VMEM((MAT_ROWS, 128), F32), pltpu.VMEM((NDEV, MAT_SLICE, 128), F32),
                        pltpu.VMEM((NDEV, MAT_SLICE, 128), F32),
                        pltpu.VMEM((NDEV, 1, VEC_LANES), F32), pltpu.VMEM((NDEV, 1, wc), F32),
                        pltpu.VMEM((NDEV, 1, wc), F32),
                        pltpu.SemaphoreType.DMA((4, 7)), pltpu.SemaphoreType.DMA((4, 7))],
        compiler_params=_cparams(32, ("arbitrary",)),
    )(*ins)


def _small_update(tot, rsum, params):
    flat = [a for k in SMALL for a in params[k]]
    n_in = 2 + len(flat)

    def body(*refs):
        tot_ref, rsum_ref = refs[:2]
        p_hbm = refs[2:n_in]
        o_refs = refs[n_in:n_in + 4 * len(SMALL)]
        loss_ref = refs[n_in + 4 * len(SMALL)]
        p_refs = refs[n_in + 4 * len(SMALL) + 1:-1]
        sem = refs[-1]
        fetch = [pltpu.make_async_copy(p_hbm[k], p_refs[k], sem.at[k]) for k in range(len(flat))]
        for cp in fetch:
            cp.start()
        for cp in fetch:
            cp.wait()
        loss_ref[...] = rsum_ref[:, VEC_LANES - 128:VEC_LANES]

        def update(idx, g):
            w_ref, m_ref, v_ref = p_refs[3 * idx:3 * idx + 3]
            g_out, d_out, m_out, v_out = o_refs[4 * idx:4 * idx + 4]
            g = g.reshape(w_ref.shape)
            g_out[...] = g
            d_out[...], m_out[...], v_out[...] = _adamw(w_ref[...], g, m_ref[...], v_ref[...])

        update(0, tot_ref[0:1024, :])
        update(1, tot_ref[1024:1536, :])
        update(2, tot_ref[1536:1544, :])
        at = 0
        for idx, k in enumerate(VECS):
            update(3 + idx, rsum_ref[:, at:at + VEC_WIDTH[k]])
            at += VEC_WIDTH[k]

    outs = []
    for k in SMALL:
        outs += [jax.ShapeDtypeStruct(params[k][0].shape, F32)] * 4
    outs += [jax.ShapeDtypeStruct((1, 128), F32)]
    res = pl.pallas_call(
        body, name="small_update", grid=(1,), out_shape=outs,
        in_specs=[_whole(tot), _whole(rsum)] + [HBM] * len(flat), out_specs=[_whole(a) for a in outs],
        scratch_shapes=[pltpu.VMEM(a.shape, F32) for a in flat] + [pltpu.SemaphoreType.DMA((len(flat),))],
        compiler_params=_cparams(32, ("arbitrary",)),
    )(tot, rsum, *flat)
    return {k: tuple(res[4 * i:4 * i + 4]) for i, k in enumerate(SMALL)}, res[-1]


def _sum_adamw(recv, w, m, v, tr, name):
    R, C = w.shape

    def body(r_ref, w_ref, m_ref, v_ref, g_ref, d_ref, nm_ref, nv_ref):
        g = r_ref[0].astype(F32)
        for k in range(1, NDEV):
            g = g + r_ref[k].astype(F32)
        g_ref[...] = g
        d_ref[...], nm_ref[...], nv_ref[...] = _adamw(w_ref[...], g, m_ref[...], v_ref[...])

    blk = pl.BlockSpec((tr, C), lambda i: (i, 0))
    out = jax.ShapeDtypeStruct((R, C), F32)
    return pl.pallas_call(
        body, name=name, grid=(R // tr,), out_shape=[out] * 4,
        in_specs=[pl.BlockSpec((NDEV, tr, C), lambda i: (0, i, 0)), blk, blk, blk], out_specs=[blk] * 4,
        compiler_params=_cparams(48, ("arbitrary",)),
    )(recv, w, m, v)


def _ada_update(cact_all, dmine, w, m, v, tr):
    R, C = w.shape

    def body(c_ref, dm_ref, w_ref, m_ref, v_ref, g_ref, d_ref, nm_ref, nv_ref):
        g = _dot_tn(c_ref[...].astype(BF), dm_ref[...].astype(BF))
        g_ref[...] = g
        d_ref[...], nm_ref[...], nv_ref[...] = _adamw(w_ref[...], g, m_ref[...], v_ref[...])

    blk = pl.BlockSpec((tr, C), lambda i: (i, 0))
    out = jax.ShapeDtypeStruct((R, C), F32)
    return pl.pallas_call(
        body, name="update_w_ada", grid=(R // tr,), out_shape=[out] * 4,
        in_specs=[pl.BlockSpec((8, tr), lambda i: (0, i)), pl.BlockSpec((8, C), lambda i: (0, 0)), blk, blk, blk],
        out_specs=[blk] * 4,
        compiler_params=_cparams(48, ("arbitrary",)),
    )(cact_all, dmine, w, m, v)


FC = F // 2


def _ffn_fwd(x, mod, gn, w_in_t, w_out, sub, name, gather=(), loss=None):
    S = x.shape[0]
    T = min(T_FFN, S)
    nS, nJ = S // T, F // FC
    ng = len(gather)
    nl = 2 if loss else 0
    forward_step = nS // 2

    def body(*refs):
        x_ref, mod_ref, gn_ref, wg_ref, wu_ref, wo_ref = refs[:6]
        gf_ref, t_ref = refs[6:6 + nl] if loss else (None, None)
        shards = refs[6 + nl:6 + nl + ng]
        at = 6 + nl + ng
        xo_ref, gu_ref, h_ref = refs[at:at + 3]
        gathered = refs[at + 3:at + 3 + ng]
        at += 3 + ng
        st_ref = refs[at] if loss else None
        at += nl // 2
        acc_scr = refs[at]
        sems = refs[at + 1:]
        i, j = pl.program_id(0), pl.program_id(1)

        if ng:
            @pl.when((i == 0) & (j == 0))
            def _():
                _gather_phase("start", shards, gathered, sems)

            @pl.when((i == forward_step) & (j == 0))
            def _():
                _gather_phase("forward", shards, gathered, sems)

        @pl.when(j == 0)
        def _():
            _, _, _, h = _rms_mod(x_ref[...], gn_ref[...], mod_ref[3 * sub:3 * sub + 1, :],
                                  mod_ref[3 * sub + 1:3 * sub + 2, :])
            h_ref[...] = h.astype(BF)
            acc_scr[...] = jnp.zeros_like(acc_scr)

        h = h_ref[...]
        g = _dot_nt(h, wg_ref[0])
        u = _dot_nt(h, wu_ref[0])
        gu_ref[0] = g.astype(BF)
        gu_ref[1] = u.astype(BF)
        a = (g * _sigmoid(g) * u).astype(BF)
        acc_scr[...] += _dot(a, wo_ref[...])

        @pl.when(j == nJ - 1)
        def _():
            xo = x_ref[...] + (0.5 * mod_ref[3 * sub + 2:3 * sub + 3, :]) * acc_scr[...]
            if not loss:
                xo_ref[...] = xo
            else:
                dx, dgf, part = _final_norm_loss(xo, gf_ref[...], t_ref[...])
                xo_ref[...] = dx
                upd = _rows3(dgf, jnp.broadcast_to(part, (1, D)), jnp.zeros((1, D), F32), D)

                @pl.when(i == 0)
                def _():
                    st_ref[...] = upd

                @pl.when(i > 0)
                def _():
                    st_ref[...] += upd

        if ng:
            @pl.when((i == nS - 1) & (j == nJ - 1))
            def _():
                _gather_phase("finish", shards, gathered, sems)

    tile = pl.BlockSpec((T, D), lambda i, j: (i, 0))
    return pl.pallas_call(
        body, name=name, grid=(nS, nJ),
        out_shape=[jax.ShapeDtypeStruct((S, D), F32), jax.ShapeDtypeStruct((2, S, F), BF),
                   jax.ShapeDtypeStruct((S, D), BF)]
                  + [jax.ShapeDtypeStruct((NDEV,) + s.shape, s.dtype) for s in gather]
                  + ([jax.ShapeDtypeStruct((8, D), F32)] if loss else []),
        in_specs=[tile,
                  pl.BlockSpec((9, D), lambda i, j: (0, 0)),
                  pl.BlockSpec((1, D), lambda i, j: (0, 0)),
                  pl.BlockSpec((1, FC, D), lambda i, j: (0, j, 0)),
                  pl.BlockSpec((1, FC, D), lambda i, j: (1, j, 0)),
                  pl.BlockSpec((FC, D), lambda i, j: (j, 0))]
                 + ([pl.BlockSpec((1, D), lambda i, j: (0, 0)), tile] if loss else []) + [HBM] * ng,
        out_specs=[tile, pl.BlockSpec((2, T, FC), lambda i, j: (0, i, j)), tile] + [HBM] * ng
                  + ([pl.BlockSpec((8, D), lambda i, j: (0, 0))] if loss else []),
        scratch_shapes=[pltpu.VMEM((T, D), F32)] + (_comm_sems(ng) if ng else []),
        compiler_params=_cparams(56, ("arbitrary", "arbitrary")),
    )(x, mod, gn, w_in_t, w_in_t, w_out, *(loss or ()), *gather)


def _ffn_bwd_hidden(dx, mod, gu, w_out, sub, name, exchange=()):
    S = dx.shape[0]
    T = min(T_FFN, S)
    nS, nJ = S // T, F // FC
    ne = len(exchange)

    def body(*refs):
        dx_ref, mod_ref, gu_ref, wo_ref = refs[:4]
        sendbufs = refs[4:4 + ne]
        dgu_ref, gw_ref, dgate_ref = refs[4 + ne:7 + ne]
        recvbufs = refs[7 + ne:7 + 2 * ne]
        acc_scr = refs[7 + 2 * ne]
        sems = refs[8 + 2 * ne:]
        j, i = pl.program_id(0), pl.program_id(1)

        if ne:
            @pl.when((i == 0) & (j == 0))
            def _():
                _exchange_phase("start", sendbufs, recvbufs, sems)

        gate = mod_ref[3 * sub + 2:3 * sub + 3, :]
        dx = dx_ref[...]
        da = _dot_nt((dx * (0.5 * gate)).astype(BF), wo_ref[...])
        g = gu_ref[0].astype(F32)
        u = gu_ref[1].astype(F32)
        sg = _sigmoid(g)
        s = g * sg
        dgu_ref[0] = (da * u * (sg * (1.0 + g * (1.0 - sg)))).astype(BF)
        dgu_ref[1] = (da * s).astype(BF)
        contrib = _dot_tn((s * u).astype(BF), dx.astype(BF))

        @pl.when(i == 0)
        def _():
            acc_scr[...] = contrib

        @pl.when(i > 0)
        def _():
            acc_scr[...] += contrib

        @pl.when(i == nS - 1)
        def _():
            acc = acc_scr[...]
            dgate = 0.5 * jnp.sum(acc * wo_ref[...].astype(F32), axis=0, keepdims=True)
            dgate_ref[...] = jnp.broadcast_to(dgate, (8, D))
            gw_ref[...] = (acc * (0.5 * gate)).astype(BF)

        if ne:
            @pl.when((i == nS - 1) & (j == nJ - 1))
            def _():
                _exchange_phase("wait", sendbufs, recvbufs, sems)

    return pl.pallas_call(
        body, name=name, grid=(nJ, nS),
        out_shape=[jax.ShapeDtypeStruct((2, S, F), BF), jax.ShapeDtypeStruct((F, D), BF),
                   jax.ShapeDtypeStruct((8 * nJ, D), F32)] + _like(exchange),
        in_specs=[pl.BlockSpec((T, D), lambda j, i: (i, 0)),
                  pl.BlockSpec((9, D), lambda j, i: (0, 0)),
                  pl.BlockSpec((2, T, FC), lambda j, i: (0, i, j)),
                  pl.BlockSpec((FC, D), lambda j, i: (j, 0))] + [HBM] * ne,
        out_specs=[pl.BlockSpec((2, T, FC), lambda j, i: (0, i, j)),
                   pl.BlockSpec((FC, D), lambda j, i: (j, 0)),
                   pl.BlockSpec((8, D), lambda j, i: (j, 0))] + [HBM] * ne,
        scratch_shapes=[pltpu.VMEM((FC, D), F32)] + (_comm_sems(ne) if ne else []),
        compiler_params=_cparams(56, ("arbitrary", "arbitrary")),
    )(dx, mod, gu, w_out, *exchange)


def _ffn_bwd_input(dgu, w_in_t, x, dx, mod, gn, sub, name, exchange=()):
    S = x.shape[0]
    T = min(T_FFN, S)
    nS = S // T
    ne = len(exchange)
    NC = 256
    chunks = [slice(k * NC, (k + 1) * NC) for k in range(D // NC)]

    def body(*refs):
        dgu_ref, w_ref, x_ref, dx_ref, mod_ref, gn_ref = refs[:6]
        sendbufs = refs[6:6 + ne]
        dxin_ref, st_ref = refs[6 + ne:8 + ne]
        recvbufs = refs[8 + ne:8 + 2 * ne]
        dxh_scr = refs[8 + 2 * ne]
        sems = refs[9 + 2 * ne:]
        i = pl.program_id(0)

        if ne:
            @pl.when(i == 0)
            def _():
                _exchange_phase("start", sendbufs, recvbufs, sems)

        gn = gn_ref[...]
        scale = mod_ref[3 * sub + 1:3 * sub + 2, :]
        r, xhat, n, _ = _rms_mod(x_ref[...], gn, mod_ref[3 * sub:3 * sub + 1, :], scale)
        dg = dgu_ref[0]
        du = dgu_ref[1]
        rowsum = jnp.zeros((T, 1), F32)
        dshift, dscale, dgn = [], [], []
        for cols in chunks:
            dh = _dot(dg, w_ref[0, :, cols]) + _dot(du, w_ref[1, :, cols])
            dshift.append(jnp.sum(dh, axis=0, keepdims=True))
            dscale.append(jnp.sum(dh * n[:, cols], axis=0, keepdims=True))
            dn = dh * (1.0 + scale[:, cols])
            dgn.append(jnp.sum(dn * xhat[:, cols], axis=0, keepdims=True))
            dxhat = dn * gn[:, cols]
            rowsum = rowsum + jnp.sum(dxhat * xhat[:, cols], axis=-1, keepdims=True)
            dxh_scr[:, cols] = dxhat
        dxin_ref[...] = dx_ref[...] + r * (dxh_scr[...] - xhat * (rowsum / D))
        cat = lambda parts: jnp.concatenate(parts, axis=1)
        upd = _rows3(cat(dshift), cat(dscale), cat(dgn), D)

        @pl.when(i == 0)
        def _():
            st_ref[...] = upd

        @pl.when(i > 0)
        def _():
            st_ref[...] += upd

        if ne:
            @pl.when(i == nS - 1)
            def _():
                _exchange_phase("wait", sendbufs, recvbufs, sems)

    tile = pl.BlockSpec((T, D), lambda i: (i, 0))
    return pl.pallas_call(
        body, name=name, grid=(nS,),
        out_shape=[jax.ShapeDtypeStruct((S, D), F32), jax.ShapeDtypeStruct((8, D), F32)] + _like(exchange),
        in_specs=[pl.BlockSpec((2, T, F), lambda i: (0, i, 0)),
                  pl.BlockSpec((2, F, D), lambda i: (0, 0, 0), pipeline_mode=pl.Buffered(1)),
                  tile, tile,
                  pl.BlockSpec((9, D), lambda i: (0, 0)),
                  pl.BlockSpec((1, D), lambda i: (0, 0))] + [HBM] * ne,
        out_specs=[tile, pl.BlockSpec((8, D), lambda i: (0, 0))] + [HBM] * ne,
        scratch_shapes=[pltpu.VMEM((T, D), F32)] + (_comm_sems(ne) if ne else []),
        compiler_params=_cparams(60, ("arbitrary",)),
    )(dgu, w_in_t, x, dx, mod, gn, *exchange)


def _ffn_bwd_win(h, dgu, name, exchange=()):
    S = h.shape[0]
    T = min(T_WIN, S)
    nS, nJ = S // T, F // FC
    ne = len(exchange)

    def body(*refs):
        h_ref, dgu_ref = refs[:2]
        sendbufs = refs[2:2 + ne]
        out_ref = refs[2 + ne]
        recvbufs = refs[3 + ne:3 + 2 * ne]
        acc_scr = refs[3 + 2 * ne]
        sems = refs[4 + 2 * ne:]
        p, j, i = pl.program_id(0), pl.program_id(1), pl.program_id(2)

        if ne:
            @pl.when((p == 0) & (j == 0) & (i == 0))
            def _():
                _exchange_phase("start", sendbufs, recvbufs, sems)

        contrib = _dot_tn(dgu_ref[0], h_ref[...])

        @pl.when(i == 0)
        def _():
            acc_scr[...] = contrib

        @pl.when(i > 0)
        def _():
            acc_scr[...] += contrib

        @pl.when(i == nS - 1)
        def _():
            out_ref[0] = acc_scr[...].astype(BF)

        if ne:
            @pl.when((p == 1) & (j == nJ - 1) & (i == nS - 1))
            def _():
                _exchange_phase("wait", sendbufs, recvbufs, sems)

    return pl.pallas_call(
        body, name=name, grid=(2, nJ, nS),
        out_shape=[jax.ShapeDtypeStruct((2, F, D), BF)] + _like(exchange),
        in_specs=[pl.BlockSpec((T, D), lambda p, j, i: (i, 0)),
                  pl.BlockSpec((1, T, FC), lambda p, j, i: (p, i, j))] + [HBM] * ne,
        out_specs=[pl.BlockSpec((1, FC, D), lambda p, j, i: (p, j, 0))] + [HBM] * ne,
        scratch_shapes=[pltpu.VMEM((FC, D), F32)] + (_comm_sems(ne) if ne else []),
        compiler_params=_cparams(56, ("arbitrary", "arbitrary", "arbitrary")),
    )(h, dgu, *exchange)


def _pool_counts(pos0, T):
    pos = pos0 + lax.broadcasted_iota(jnp.int32, (T, 1), 0)
    return [jnp.minimum(pos + 1, w).astype(F32) for w in WINDOWS]


def _pool_fwd(xa, halo, ext_scr, cnts, T):
    ext_scr[0:HALO, :] = halo
    ext_scr[HALO:HALO + T, :] = xa
    out = []
    for gi, w in enumerate(WINDOWS):
        cols = slice(128 * gi, 128 * gi + 128)
        acc = xa[:, cols]
        for k in range(1, w):
            acc = acc + ext_scr[HALO - k:HALO - k + T, cols]
        out.append(acc / cnts[gi] - xa[:, cols])
    return out


def _sgu_fwd(vnb, ws_ref, sv_scr, T):
    lane = lax.broadcasted_iota(jnp.int32, (CHUNK, 128), 1)
    for n in range(T // CHUNK):
        rows = slice(n * CHUNK, (n + 1) * CHUNK)
        for b in range(DG // 128):
            cols = slice(128 * b, 128 * b + 128)
            vb = vnb[rows, cols]
            sv_scr[rows, cols] = jnp.where(lane < 64, _dot(ws_ref[2 * b], vb), _dot(ws_ref[2 * b + 1], vb))


def _mix_fwd(x, mod, gn, wmi, wmo, wp, ps, lg, lb, ws, bias, name):
    S = x.shape[0]
    T = min(T_MIX, S)

    def body(x_ref, mod_ref, gn_ref, wmi_ref, wmo_ref, wp_ref, ps_ref, lg_ref, lb_ref, ws_ref, bias_ref,
             xo_ref, carry_scr, ext_scr, sv_scr, ycat_scr):
        i = pl.program_id(0)

        @pl.when(i == 0)
        def _():
            carry_scr[...] = jnp.zeros_like(carry_scr)

        x = x_ref[...]
        _, _, _, h = _rms_mod(x, gn_ref[...], mod_ref[3:4, :], mod_ref[4:5, :])
        proj = _dot_nt(h.astype(BF), wmi_ref[...])
        xa = proj[:, 0:DP]
        p = _pool_fwd(xa, carry_scr[...], ext_scr, _pool_counts(i * T, T), T)
        carry_scr[...] = xa[T - HALO:T, :]
        for gi in range(4):
            cols = slice(128 * gi, 128 * gi + 128)
            ycat_scr[:, cols] = (_dot(p[gi].astype(BF), wp_ref[gi]) * ps_ref[:, cols]).astype(BF)
        u, _ = _gelu(proj[:, DP:DP + DG])
        v, _ = _gelu(proj[:, DP + DG:DPROJ])
        mu = jnp.mean(v, axis=-1, keepdims=True)
        vc = v - mu
        rstd = lax.rsqrt(jnp.mean(vc * vc, axis=-1, keepdims=True) + EPS)
        vn = vc * rstd * lg_ref[...] + lb_ref[...]
        _sgu_fwd(vn.astype(BF), ws_ref, sv_scr, T)
        for n in range(T // CHUNK):
            rows = slice(n * CHUNK, (n + 1) * CHUNK)
            ycat_scr[rows, DP:D] = (u[rows, :] * (sv_scr[rows, :] + bias_ref[...])).astype(BF)
        xo_ref[...] = x + mod_ref[5:6, :] * _dot(ycat_scr[...], wmo_ref[...])

    full = lambda shape: pl.BlockSpec(shape, lambda i: (0,) * len(shape))
    return pl.pallas_call(
        body, name=name, grid=(S // T,),
        out_shape=jax.ShapeDtypeStruct((S, D), F32),
        in_specs=[pl.BlockSpec((T, D), lambda i: (i, 0)), full((9, D)), full((1, D)), full((DPROJ, D)), full((D, D)),
                  full((4, 128, 128)), full((1, DP)), full((1, DG)), full((1, DG)), full((8, CHUNK, CHUNK)),
                  full((CHUNK, DG))],
        out_specs=pl.BlockSpec((T, D), lambda i: (i, 0)),
        scratch_shapes=[pltpu.VMEM((HALO, DP), F32), pltpu.VMEM((T + HALO, DP), F32), pltpu.VMEM((T, DG), F32),
                        pltpu.VMEM((T, D), BF)],
        compiler_params=_cparams(48, ("arbitrary",)),
    )(x, mod, gn, wmi, wmo, wp, ps, lg, lb, ws, bias)


def _mix_bwd(x, dxo, mod, gn, wmi, wmo, wp, ps, lg, lb, ws, bias, name, exchange=()):
    S = x.shape[0]
    T = min(T_MIX, S)
    nS = S // T
    hb = T // HALO
    ne = len(exchange)

    def body(*refs):
        (x_ref, xh_ref, dxo_ref, mod_ref, gn_ref, wmi_ref, wmo_ref, wp_ref, ps_ref, lg_ref, lb_ref, ws_ref,
         bias_ref) = refs[:13]
        sendbufs = refs[13:13 + ne]
        dxi_ref, gwmi_out, gwmo_out, gwp_ref, gws_ref, st_ref, vec_ref, dbias_ref = refs[13 + ne:21 + ne]
        recvbufs = refs[21 + ne:21 + 2 * ne]
        (carry_scr, ext_scr, qext_scr, sv_scr, dvn_scr, ycat_scr, dproj_scr, gwmi_ref,
         gwmo_ref) = refs[21 + 2 * ne:30 + 2 * ne]
        sems = refs[30 + 2 * ne:]
        i = pl.program_id(0)
        t = nS - 1 - i
        gn = gn_ref[...]
        shift, scale, gate = mod_ref[3:4, :], mod_ref[4:5, :], mod_ref[5:6, :]

        @pl.when(i == 0)
        def _():
            if ne:
                _exchange_phase("start", sendbufs, recvbufs, sems)
            carry_scr[...] = jnp.zeros_like(carry_scr)
            gwmi_ref[...] = jnp.zeros_like(gwmi_ref)
            gwmo_ref[...] = jnp.zeros_like(gwmo_ref)
            gwp_ref[...] = jnp.zeros_like(gwp_ref)
            gws_ref[...] = jnp.zeros_like(gws_ref)
            st_ref[...] = jnp.zeros_like(st_ref)
            vec_ref[...] = jnp.zeros_like(vec_ref)
            dbias_ref[...] = jnp.zeros_like(dbias_ref)

        x = x_ref[...]
        dxo = dxo_ref[...]
        r, xhat, n, h = _rms_mod(x, gn, shift, scale)
        hbf = h.astype(BF)
        proj = _dot_nt(hbf, wmi_ref[...])
        xa = proj[:, 0:DP]
        zu = proj[:, DP:DP + DG]
        zv = proj[:, DP + DG:DPROJ]
        _, _, _, hh = _rms_mod(xh_ref[...], gn, shift, scale)
        halo = _dot_nt(hh.astype(BF), wmi_ref[0:DP, :])
        halo = jnp.where(t == 0, 0.0, halo)
        cnts = _pool_counts(t * T, T)
        p = _pool_fwd(xa, halo, ext_scr, cnts, T)
        m = []
        for gi in range(4):
            cols = slice(128 * gi, 128 * gi + 128)
            m.append(_dot(p[gi].astype(BF), wp_ref[gi]))
            ycat_scr[:, cols] = (m[gi] * ps_ref[:, cols]).astype(BF)
        u, tu = _gelu(zu)
        v, tv = _gelu(zv)
        mu = jnp.mean(v, axis=-1, keepdims=True)
        vc = v - mu
        rstd = lax.rsqrt(jnp.mean(vc * vc, axis=-1, keepdims=True) + EPS)
        vhat = vc * rstd
        lg = lg_ref[...]
        vnb = (vhat * lg + lb_ref[...]).astype(BF)
        _sgu_fwd(vnb, ws_ref, sv_scr, T)
        for nck in range(T // CHUNK):
            rows = slice(nck * CHUNK, (nck + 1) * CHUNK)
            sv_scr[rows, :] = sv_scr[rows, :] + bias_ref[...]
        sv = sv_scr[...]
        ycat_scr[:, DP:D] = (u * sv).astype(BF)

        gwmo_ref[...] += _dot_tn(ycat_scr[...], dxo.astype(BF))
        dyc = _dot_nt((dxo * gate).astype(BF), wmo_ref[...])
        dya = dyc[:, 0:DP]
        dyb = dyc[:, DP:D]

        dps = []
        dp = []
        for gi in range(4):
            cols = slice(128 * gi, 128 * gi + 128)
            dps.append(jnp.sum(dya[:, cols] * m[gi], axis=0, keepdims=True))
            dm = (dya[:, cols] * ps_ref[:, cols]).astype(BF)
            gwp_ref[gi] += _dot_tn(p[gi].astype(BF), dm)
            dp.append(_dot_nt(dm, wp_ref[gi]))
            qext_scr[0:T, cols] = dp[gi] / cnts[gi]
        qext_scr[T:T + HALO, :] = carry_scr[...]
        for gi, w in enumerate(WINDOWS):
            cols = slice(128 * gi, 128 * gi + 128)
            acc = qext_scr[0:T, cols]
            for k in range(1, w):
                acc = acc + qext_scr[k:k + T, cols]
            dproj_scr[:, cols] = (acc - dp[gi]).astype(BF)
        carry_scr[...] = qext_scr[0:HALO, :]

        du = dyb * sv
        dsv = dyb * u
        lane = lax.broadcasted_iota(jnp.int32, (CHUNK, 128), 1)
        dbias = jnp.zeros((CHUNK, DG), F32)
        for nck in range(T // CHUNK):
            rows = slice(nck * CHUNK, (nck + 1) * CHUNK)
            dbias = dbias + dsv[rows, :]
            for b in range(DG // 128):
                cols = slice(128 * b, 128 * b + 128)
                dsvb = dsv[rows, cols]
                vb = vnb[rows, cols]
                gws_ref[2 * b] += _dot_nt(jnp.where(lane < 64, dsvb, 0.0).astype(BF), vb)
                gws_ref[2 * b + 1] += _dot_nt(jnp.where(lane < 64, 0.0, dsvb).astype(BF), vb)
                dsvbb = dsvb.astype(BF)
                dvn_scr[rows, cols] = jnp.where(lane < 64, _dot_tn(ws_ref[2 * b], dsvbb),
                                                _dot_tn(ws_ref[2 * b + 1], dsvbb))
        dbias_ref[...] += dbias
        dvn = dvn_scr[...]
        dlg = jnp.sum(dvn * vhat, axis=0, keepdims=True)
        dlb = jnp.sum(dvn, axis=0, keepdims=True)
        dvhat = dvn * lg
        dv = rstd * (dvhat - jnp.mean(dvhat, axis=-1, keepdims=True)
                     - vhat * jnp.mean(dvhat * vhat, axis=-1, keepdims=True))
        dproj_scr[:, DP:DP + DG] = (du * _gelu_grad(zu, tu)).astype(BF)
        dproj_scr[:, DP + DG:DPROJ] = (dv * _gelu_grad(zv, tv)).astype(BF)
        vec_ref[...] += _rows3(jnp.concatenate(dps, axis=1), dlg, dlb, DP)

        dproj = dproj_scr[...]
        gwmi_ref[...] += _dot_tn(dproj, hbf)
        dh = _dot(dproj, wmi_ref[...])
        dxi, dshift, dscale, dgn = _rms_mod_bwd(dh, dxo, r, xhat, n, gn, scale)
        dxi_ref[...] = dxi
        st_ref[...] += _rows3(dshift, dscale, dgn, D)

        @pl.when(i == nS - 1)
        def _():
            acc = gwmo_ref[...]
            dgate = jnp.sum(acc * wmo_ref[...].astype(F32), axis=0, keepdims=True)
            row = lax.broadcasted_iota(jnp.int32, (8, D), 0)
            st_ref[...] += jnp.where(row == 3, dgate, 0.0)
            gwmo_out[...] = (acc * gate).astype(BF)
            gwmi_out[...] = gwmi_ref[...].astype(BF)
            tt = lax.broadcasted_iota(jnp.int32, (CHUNK, CHUNK), 0)
            ss = lax.broadcasted_iota(jnp.int32, (CHUNK, CHUNK), 1)
            for hd in range(8):
                gws_ref[hd] = jnp.where(tt >= ss, gws_ref[hd], 0.0)
            if ne:
                _exchange_phase("wait", sendbufs, recvbufs, sems)

    full = lambda shape: pl.BlockSpec(shape, lambda i: (0,) * len(shape))
    return pl.pallas_call(
        body, name=name, grid=(nS,),
        out_shape=[jax.ShapeDtypeStruct((S, D), F32), jax.ShapeDtypeStruct((DPROJ, D), BF),
                   jax.ShapeDtypeStruct((D, D), BF), jax.ShapeDtypeStruct((4, 128, 128), F32),
                   jax.ShapeDtypeStruct((8, CHUNK, CHUNK), F32), jax.ShapeDtypeStruct((8, D), F32),
                   jax.ShapeDtypeStruct((8, DP), F32), jax.ShapeDtypeStruct((CHUNK, DG), F32)] + _like(exchange),
        in_specs=[pl.BlockSpec((T, D), lambda i: (nS - 1 - i, 0)),
                  pl.BlockSpec((HALO, D), lambda i: (jnp.maximum((nS - 1 - i) * hb - 1, 0), 0)),
                  pl.BlockSpec((T, D), lambda i: (nS - 1 - i, 0)),
                  full((9, D)), full((1, D)), full((DPROJ, D)), full((D, D)),
                  full((4, 128, 128)), full((1, DP)), full((1, DG)), full((1, DG)), full((8, CHUNK, CHUNK)),
                  full((CHUNK, DG))] + [HBM] * ne,
        out_specs=[pl.BlockSpec((T, D), lambda i: (nS - 1 - i, 0)), full((DPROJ, D)), full((D, D)),
                   full((4, 128, 128)), full((8, CHUNK, CHUNK)), full((8, D)), full((8, DP)), full((CHUNK, DG))]
                  + [HBM] * ne,
        scratch_shapes=[pltpu.VMEM((HALO, DP), F32), pltpu.VMEM((T + HALO, DP), F32),
                        pltpu.VMEM((T + HALO, DP), F32), pltpu.VMEM((T, DG), F32), pltpu.VMEM((T, DG), F32),
                        pltpu.VMEM((T, D), BF), pltpu.VMEM((T, DPROJ), BF), pltpu.VMEM((DPROJ, D), F32),
                        pltpu.VMEM((D, D), F32)] + (_comm_sems(ne) if ne else []),
        compiler_params=_cparams(56, ("arbitrary",)),
    )(x, x, dxo, mod, gn, wmi, wmo, wp, ps, lg, lb, ws, bias, *exchange)


def kernel(x, c, w_ada, b_ada, norm_ffn1_g, ffn1_w_in, ffn1_w_out, norm_mix_g, w_mix_in, w_pool, pool_scale, gmlp_ln_g, gmlp_ln_b, w_spatial, b_spatial, w_mix_out, norm_ffn2_g, ffn2_w_in, ffn2_w_out, norm_final_g, loss_target, m_w_ada, m_b_ada, m_norm_ffn1_g, m_ffn1_w_in, m_ffn1_w_out, m_norm_mix_g, m_w_mix_in, m_w_pool, m_pool_scale, m_gmlp_ln_g, m_gmlp_ln_b, m_w_spatial, m_b_spatial, m_w_mix_out, m_norm_ffn2_g, m_ffn2_w_in, m_ffn2_w_out, m_norm_final_g, v_w_ada, v_b_ada, v_norm_ffn1_g, v_ffn1_w_in, v_ffn1_w_out, v_norm_mix_g, v_w_mix_in, v_w_pool, v_pool_scale, v_gmlp_ln_g, v_gmlp_ln_b, v_w_spatial, v_b_spatial, v_w_mix_out, v_norm_ffn2_g, v_ffn2_w_in, v_ffn2_w_out, v_norm_final_g):
    weights = dict(w_ada=w_ada, b_ada=b_ada, norm_ffn1_g=norm_ffn1_g, ffn1_w_in=ffn1_w_in, ffn1_w_out=ffn1_w_out,
                   norm_mix_g=norm_mix_g, w_mix_in=w_mix_in, w_pool=w_pool, pool_scale=pool_scale,
                   gmlp_ln_g=gmlp_ln_g, gmlp_ln_b=gmlp_ln_b, w_spatial=w_spatial, b_spatial=b_spatial,
                   w_mix_out=w_mix_out, norm_ffn2_g=norm_ffn2_g, ffn2_w_in=ffn2_w_in, ffn2_w_out=ffn2_w_out,
                   norm_final_g=norm_final_g)
    mom1 = dict(w_ada=m_w_ada, b_ada=m_b_ada, norm_ffn1_g=m_norm_ffn1_g, ffn1_w_in=m_ffn1_w_in,
                ffn1_w_out=m_ffn1_w_out, norm_mix_g=m_norm_mix_g, w_mix_in=m_w_mix_in, w_pool=m_w_pool,
                pool_scale=m_pool_scale, gmlp_ln_g=m_gmlp_ln_g, gmlp_ln_b=m_gmlp_ln_b, w_spatial=m_w_spatial,
                b_spatial=m_b_spatial, w_mix_out=m_w_mix_out, norm_ffn2_g=m_norm_ffn2_g, ffn2_w_in=m_ffn2_w_in,
                ffn2_w_out=m_ffn2_w_out, norm_final_g=m_norm_final_g)
    mom2 = dict(w_ada=v_w_ada, b_ada=v_b_ada, norm_ffn1_g=v_norm_ffn1_g, ffn1_w_in=v_ffn1_w_in,
                ffn1_w_out=v_ffn1_w_out, norm_mix_g=v_norm_mix_g, w_mix_in=v_w_mix_in, w_pool=v_w_pool,
                pool_scale=v_pool_scale, gmlp_ln_g=v_gmlp_ln_g, gmlp_ln_b=v_gmlp_ln_b, w_spatial=v_w_spatial,
                b_spatial=v_b_spatial, w_mix_out=v_w_mix_out, norm_ffn2_g=v_norm_ffn2_g, ffn2_w_in=v_ffn2_w_in,
                ffn2_w_out=v_ffn2_w_out, norm_final_g=v_norm_final_g)
    order = list(weights)
    xs = x[0]
    target = loss_target[0]
    transposed = ("ffn1_w_in", "w_mix_in", "ffn2_w_in")
    big = ("ffn1_w_in", "ffn1_w_out", "w_mix_in", "w_mix_out", "ffn2_w_in", "ffn2_w_out")
    local = lambda a, k: a[0].T if k in transposed else a[0]
    wc = w_ada.shape[2]

    shard = dict(zip(big, _cast_shards([local(weights[k], k) for k in big])))
    modp, cact_all, g_w1_in, g_w1_out = _ada_forward(jnp.broadcast_to(c, (8, D)), w_ada[0], b_ada.reshape(NDEV, wc),
                                                     [shard["ffn1_w_in"], shard["ffn1_w_out"]])
    mod = modp.reshape(9, D)
    w1_in = g_w1_in.reshape(2, F, D)
    w1_out = g_w1_out.reshape(F, D)

    x1, gu1, h1, g_wmi, g_wmo, g_w2_out, g_w2_in = _ffn_fwd(
        xs, mod, norm_ffn1_g, w1_in, w1_out, 0, "ffn1_fwd",
        gather=[shard["w_mix_in"], shard["w_mix_out"], shard["ffn2_w_out"], shard["ffn2_w_in"]])
    wmi = g_wmi.reshape(DPROJ, D)
    wmo = g_wmo.reshape(D, D)
    w2_in = g_w2_in.reshape(2, F, D)
    w2_out = g_w2_out.reshape(F, D)
    tril = jnp.tril(jnp.ones((CHUNK, CHUNK), dtype=bool))
    ws_b = jnp.where(tril[None], w_spatial[0], 0.0).astype(BF)
    wp_b = w_pool[0].astype(BF)
    bias = jnp.repeat(b_spatial[0].T, DG // 8, axis=1)
    mix_args = (wmi, wmo, wp_b, pool_scale, gmlp_ln_g, gmlp_ln_b, ws_b, bias)
    x2 = _mix_fwd(x1, mod, norm_mix_g, *mix_args, "mix_fwd")
    dx3, gu3, h3, st_f = _ffn_fwd(x2, mod, norm_ffn2_g, w2_in, w2_out, 2, "ffn2_fwd",
                                  loss=(norm_final_g.reshape(1, D), target))

    slots = lambda a: a.reshape(NDEV, a.size // (NDEV * D), D)
    dgu3, d_w2_out, dgate3 = _ffn_bwd_hidden(dx3, mod, gu3, w2_out, 2, "ffn2_bwd_hidden")
    d_w2_in = _ffn_bwd_win(h3, dgu3, "ffn2_bwd_win")[0]
    dx2, st3 = _ffn_bwd_input(dgu3, w2_in, x2, dx3, mod, norm_ffn2_g, 2, "ffn2_bwd_input")
    dx1, d_wmi, d_wmo, d_wp, d_ws, st2, vec2, dbias, r_w2_in, r_w2_out = _mix_bwd(
        x1, dx2, mod, norm_mix_g, *mix_args, "mix_bwd", exchange=[slots(d_w2_in), slots(d_w2_out)])
    dgu1, d_w1_out, dgate1, r_wmi, r_wmo = _ffn_bwd_hidden(
        dx1, mod, gu1, w1_out, 0, "ffn1_bwd_hidden", exchange=[slots(d_wmi), slots(d_wmo)])
    d_w1_in, r_w1_out = _ffn_bwd_win(h1, dgu1, "ffn1_bwd_win", exchange=[slots(d_w1_out)])
    dx0, st1, r_w1_in = _ffn_bwd_input(dgu1, w1_in, xs, dx1, mod, norm_ffn1_g, 0, "ffn1_bwd_input",
                                       exchange=[slots(d_w1_in)])

    received = dict(ffn1_w_in=r_w1_in, ffn1_w_out=r_w1_out, w_mix_in=r_wmi, w_mix_out=r_wmo,
                    ffn2_w_in=r_w2_in, ffn2_w_out=r_w2_out)
    tiles = dict(ffn1_w_in=176, ffn1_w_out=176, w_mix_in=96, w_mix_out=128, ffn2_w_in=176, ffn2_w_out=176)
    result = {}
    for k, recv in received.items():
        res = _sum_adamw(recv, local(weights[k], k), local(mom1[k], k), local(mom2[k], k), tiles[k], "update_" + k)
        result[k] = tuple((a.T if k in transposed else a)[None] for a in res)
    row = lambda a: a.reshape(1, D)
    params = {k: (weights[k], mom1[k], mom2[k]) for k in SMALL}
    params["norm_final_g"] = (row(norm_final_g), row(m_norm_final_g), row(v_norm_final_g))
    tot, rsum, dmine = _small_reduce(d_ws, d_wp, dbias, st1, st2, st3, st_f, vec2, dgate1, dgate3)
    small, loss_row = _small_update(tot, rsum, params)
    result.update(small)
    result["norm_final_g"] = tuple(a.reshape(D) for a in small["norm_final_g"])
    result["w_ada"] = tuple(a[None] for a in _ada_update(cact_all, dmine, w_ada[0], m_w_ada[0], v_w_ada[0], 256))

    return (loss_row[0, 0], dx0[None], *[result[k][0] for k in order], *[result[k][1] for k in order],
            *[result[k][2] for k in order], *[result[k][3] for k in order])
```

```python
import math

import jax
import jax.numpy as jnp
from jax import lax
from jax.experimental import pallas as pl
from jax.experimental.pallas import tpu as pltpu

D = 1024
F = 2816
DP = 512
DG = 512
DPROJ = DP + 2 * DG
CHUNK = 128
WINDOWS = (2, 4, 8, 16)
HALO = 16
NDEV = 8
T_FFN = 512
T_MIX = 256
T_MIX_FWD = 512
T_WIN = 2048
EPS = 1e-6
LR, B1, B2, AEPS, WD, STEP = 0.001, 0.9, 0.999, 1e-08, 0.01, 10
BC1 = 1.0 - B1 ** STEP
BC2 = 1.0 - B2 ** STEP
GELU_C = math.sqrt(2.0 / math.pi)
GELU_A = 0.044715

BF = jnp.bfloat16
F32 = jnp.float32
MESH = pl.DeviceIdType.MESH
HBM = pl.BlockSpec(memory_space=pltpu.HBM)


def _whole(a):
    return pl.BlockSpec(a.shape, lambda i: (0,) * len(a.shape))


NT = (((1,), (1,)), ((), ()))
TN = (((0,), (0,)), ((), ()))


def _dot(a, b):
    return jnp.dot(a, b, preferred_element_type=F32)


def _dot_nt(a, b):
    return lax.dot_general(a, b, NT, preferred_element_type=F32)


def _dot_tn(a, b):
    return lax.dot_general(a, b, TN, preferred_element_type=F32)


def _cparams(vmem_mb, sem=None):
    kw = dict(vmem_limit_bytes=vmem_mb * 1024 * 1024)
    if sem is not None:
        kw["dimension_semantics"] = sem
    return pltpu.CompilerParams(**kw)


def _position():
    return lax.axis_index("x"), lax.axis_index("y"), lax.axis_index("c")


def _slot(p):
    return 4 * p[0] + 2 * p[1] + p[2]


def _flip(me, d):
    x, y, c = me
    return (1 - x if d & 4 else x, 1 - y if d & 2 else y, 1 - c if d & 1 else c)


def _remote(src, dst, send_sem, recv_sem, to):
    return pltpu.make_async_remote_copy(src_ref=src, dst_ref=dst, send_sem=send_sem, recv_sem=recv_sem,
                                        device_id=to, device_id_type=MESH)


def _comm_sems(n):
    return [pltpu.SemaphoreType.DMA((n, 7)), pltpu.SemaphoreType.DMA((n, 7)), pltpu.SemaphoreType.DMA((n,))]


def _gather_phase(phase, xs, outs, sems):
    send_sems, recv_sems, local_sems = sems
    n = len(xs)
    me = _position()
    x, y, c = me
    sibling = (x, y, 1 - c)
    xn, yn, diag = (1 - x, y), (x, 1 - y), (1 - x, 1 - y)
    relay_from = (x + c * (1 - 2 * x), y + (1 - c) * (1 - 2 * y))
    relay_to = (x + (1 - c) * (1 - 2 * x), y + c * (1 - 2 * y))

    def copy(a, k, block, to, src=None):
        dst = outs[a].at[_slot(block)]
        return _remote(dst if src is None else src, dst, send_sems.at[a, k], recv_sems.at[a, k], to)

    def mine(a):
        return pltpu.make_async_copy(xs[a], outs[a].at[_slot(me)], local_sems.at[a])

    def first(a):
        return [copy(a, 0, me, sibling, src=xs[a]), copy(a, 1, me, (*xn, c), src=xs[a]),
                copy(a, 2, me, (*yn, c), src=xs[a])]

    def second(a):
        return [copy(a, 3, (*relay_from, c), (*relay_to, c)), copy(a, 4, (*xn, c), sibling),
                copy(a, 5, (*yn, c), sibling)]

    def third(a):
        return copy(a, 6, (*diag, c), sibling)

    if phase == "start":
        for a in range(n):
            mine(a).start()
            for cp in first(a):
                cp.start()
    elif phase == "forward":
        for a in range(n):
            copy(a, 1, (*xn, c), me).wait_recv()
            copy(a, 2, (*yn, c), me).wait_recv()
            for cp in second(a):
                cp.start()
    else:
        for a in range(n):
            copy(a, 3, (*diag, c), me).wait_recv()
            third(a).start()
        for a in range(n):
            copy(a, 0, sibling, me).wait_recv()
            for k, chip in ((4, xn), (5, yn), (6, diag)):
                copy(a, k, (*chip, 1 - c), me).wait_recv()
        for a in range(n):
            for cp in first(a) + second(a) + [third(a)]:
                cp.wait_send()
            mine(a).wait()


def _exchange_phase(phase, xs, outs, sems):
    send_sems, recv_sems, local_sems = sems
    me = _position()
    for a in range(len(xs)):
        copies = [pltpu.make_async_copy(xs[a].at[_slot(me)], outs[a].at[_slot(me)], local_sems.at[a])]
        for d in range(1, NDEV):
            to = _flip(me, d)
            copies.append(_remote(xs[a].at[_slot(to)], outs[a].at[_slot(me)],
                                  send_sems.at[a, d - 1], recv_sems.at[a, d - 1], to))
        for cp in copies:
            if phase == "start":
                cp.start()
            else:
                cp.wait()


def _like(bufs):
    return [jax.ShapeDtypeStruct(b.shape, b.dtype) for b in bufs]


def _rms_mod(x, gn, shift, scale):
    ms = jnp.mean(x * x, axis=-1, keepdims=True)
    r = lax.rsqrt(ms + EPS)
    xhat = x * r
    n = xhat * gn
    h = n * (1.0 + scale) + shift
    return r, xhat, n, h


def _rms_mod_bwd(dh, dres, r, xhat, n, gn, scale):
    dshift = jnp.sum(dh, axis=0, keepdims=True)
    dscale = jnp.sum(dh * n, axis=0, keepdims=True)
    dn = dh * (1.0 + scale)
    dgn = jnp.sum(dn * xhat, axis=0, keepdims=True)
    dxhat = dn * gn
    dx = dres + r * (dxhat - xhat * jnp.mean(dxhat * xhat, axis=-1, keepdims=True))
    return dx, dshift, dscale, dgn


def _final_norm_loss(x, gf, target):
    r = lax.rsqrt(jnp.mean(x * x, axis=-1, keepdims=True) + EPS)
    xhat = x * r
    e = xhat * gf - target
    part = 0.5 * jnp.sum(jnp.sum(e * e, axis=-1, keepdims=True), axis=0, keepdims=True) / D
    dy = e / D
    dgf = jnp.sum(dy * xhat, axis=0, keepdims=True)
    dxhat = dy * gf
    dx = r * (dxhat - xhat * jnp.mean(dxhat * xhat, axis=-1, keepdims=True))
    return dx, dgf, part


def _rows3(a, b, c, width):
    row = lax.broadcasted_iota(jnp.int32, (8, width), 0)
    z = jnp.zeros((8, width), F32)
    return jnp.where(row == 0, a, z) + jnp.where(row == 1, b, z) + jnp.where(row == 2, c, z)


def _sigmoid(x):
    return 0.5 * jnp.tanh(0.5 * x) + 0.5


def _gelu(x):
    t = jnp.tanh(GELU_C * (x + GELU_A * x * x * x))
    return 0.5 * x * (1.0 + t), t


def _gelu_grad(x, t):
    return 0.5 * (1.0 + t) + 0.5 * x * (1.0 - t * t) * GELU_C * (1.0 + 3.0 * GELU_A * x * x)


def _adamw(w, g, m, v):
    m = B1 * m + (1.0 - B1) * g
    v = B2 * v + (1.0 - B2) * (g * g)
    m_hat = m / BC1
    v_hat = v / BC2
    delta = -LR * (m_hat / (jnp.sqrt(v_hat) + AEPS) + WD * w)
    return delta, m, v


def _cast_shards(shards):
    n = len(shards)

    def body(*refs):
        for a in range(n):
            refs[n + a][...] = refs[a][...].astype(BF)

    resident = pl.BlockSpec(memory_space=pltpu.VMEM)
    return pl.pallas_call(
        body, name="cast_shards", out_shape=[jax.ShapeDtypeStruct(s.shape, BF) for s in shards],
        in_specs=[resident] * n, out_specs=[resident] * n, compiler_params=_cparams(40),
    )(*shards)


def _ada_forward(c8, w_ada, b8, shards):
    wc = w_ada.shape[1]
    n = len(shards)

    def body(*refs):
        c8_ref, w_ref, b8_ref = refs[:3]
        xs = refs[3:3 + n]
        mod_ref, cact_ref = refs[3 + n:5 + n]
        gathered = refs[5 + n:5 + 2 * n]
        call_ref, mall_ref, send_sems, recv_sems = refs[5 + 2 * n:9 + 2 * n]
        gsems = refs[9 + 2 * n:]
        me = _position()
        my = _slot(me)
        row = lax.broadcasted_iota(jnp.int32, (8, 1), 0)
        call_ref[my] = c8_ref[...]
        sends = []
        for d in range(1, NDEV):
            to = _flip(me, d)
            sends.append(_remote(call_ref.at[my], call_ref.at[my], send_sems.at[0, d - 1], recv_sems.at[0, d - 1], to))
        for cp in sends:
            cp.start()
        _gather_phase("start", xs, gathered, gsems)
        for cp in sends:
            cp.wait()
        c_all = jnp.zeros((8, D), F32)
        for k in range(NDEV):
            c_all = c_all + jnp.where(row == k, call_ref[k], 0.0)
        cact = c_all * jax.nn.sigmoid(c_all)
        cact_ref[...] = cact
        part = _dot(cact.astype(BF), w_ref[...].astype(BF))
        mall_ref[my] = part
        sends = []
        for d in range(1, NDEV):
            to = _flip(me, d)
            sends.append(_remote(mall_ref.at[my], mall_ref.at[my], send_sems.at[1, d - 1], recv_sems.at[1, d - 1], to))
        for cp in sends:
            cp.start()
        for cp in sends:
            cp.wait()
        out = jnp.zeros((8, wc), F32)
        for k in range(NDEV):
            piece = jnp.sum(jnp.where(row == my, mall_ref[k], 0.0), axis=0, keepdims=True)
            out = out + jnp.where(row == k, piece, 0.0)
        mod_ref[...] = out + b8_ref[...]
        _gather_phase("forward", xs, gathered, gsems)
        _gather_phase("finish", xs, gathered, gsems)

    outs = [jax.ShapeDtypeStruct((8, wc), F32), jax.ShapeDtypeStruct((8, D), F32)]
    return pl.pallas_call(
        body, name="ada_forward", grid=(1,),
        out_shape=outs + [jax.ShapeDtypeStruct((NDEV,) + s.shape, s.dtype) for s in shards],
        in_specs=[_whole(a) for a in (c8, w_ada, b8)] + [HBM] * n,
        out_specs=[_whole(a) for a in outs] + [HBM] * n,
        scratch_shapes=[pltpu.VMEM((NDEV, 8, D), F32), pltpu.VMEM((NDEV, 8, wc), F32),
                        pltpu.SemaphoreType.DMA((2, 7)), pltpu.SemaphoreType.DMA((2, 7))] + _comm_sems(n),
        compiler_params=_cparams(32, ("arbitrary",)),
    )(c8, w_ada, b8, *shards)


MATS = ("w_spatial", "w_pool", "b_spatial")
VECS = ("norm_ffn1_g", "norm_mix_g", "norm_ffn2_g", "norm_final_g", "pool_scale", "gmlp_ln_g", "gmlp_ln_b", "b_ada")
VEC_WIDTH = dict(norm_ffn1_g=D, norm_mix_g=D, norm_ffn2_g=D, norm_final_g=D, pool_scale=DP, gmlp_ln_g=DG,
                 gmlp_ln_b=DG, b_ada=9 * D)
MAT_ROWS = 1600
MAT_SLICE = MAT_ROWS // NDEV
VEC_LANES = sum(VEC_WIDTH.values()) + 128
DMOD_AT = VEC_LANES - 128 - 9 * D
SMALL = MATS + VECS


def _small_reduce(g_ws, g_wp, dbias, st1, st2, st3, st_f, vec2, dgate1, dgate3):
    wc = 9 * D // NDEV

    def body(g_ws_ref, g_wp_ref, dbias_ref, st1_ref, st2_ref, st3_ref, stf_ref, vec2_ref, dg1_ref, dg3_ref,
             tot_ref, rsum_ref, dmine_ref,
             pack_ref, rs_ref, ag_ref, rv_ref, dmp_ref, dw_ref, send_sems, recv_sems):
        me = _position()
        my = _slot(me)

        pack_ref[0:1024, :] = g_ws_ref[...].reshape(1024, 128)
        pack_ref[1024:1536, :] = g_wp_ref[...].reshape(512, 128)
        ch = lax.broadcasted_iota(jnp.int32, (DG, 128), 0)
        hd = lax.broadcasted_iota(jnp.int32, (DG, 128), 1)
        sel = jnp.where(ch // 64 == hd, 1.0, 0.0).astype(F32)
        heads = jnp.dot(dbias_ref[...], sel, preferred_element_type=F32, precision=lax.Precision.HIGHEST)
        pack_ref[1536:1544, :] = heads.T[0:8, :]
        pack_ref[1544:MAT_ROWS, :] = jnp.zeros((MAT_ROWS - 1544, 128), F32)
        dgate1 = dg1_ref[0:1, :] + dg1_ref[8:9, :]
        dgate3 = dg3_ref[0:1, :] + dg3_ref[8:9, :]
        row = jnp.concatenate(
            [st1_ref[2:3, :], st2_ref[2:3, :], st3_ref[2:3, :], stf_ref[0:1, :],
             vec2_ref[0:1, :], vec2_ref[1:2, :], vec2_ref[2:3, :],
             st1_ref[0:1, :], st1_ref[1:2, :], dgate1, st2_ref[0:1, :], st2_ref[1:2, :], st2_ref[3:4, :],
             st3_ref[0:1, :], st3_ref[1:2, :], dgate3, stf_ref[1:2, 0:128]], axis=1)
        rv_ref[my] = row
        for k in range(NDEV):
            dmp_ref[k] = row[:, DMOD_AT + wc * k:DMOD_AT + wc * (k + 1)]
        dw_ref[my] = dmp_ref[my]
        rs_ref[my] = pack_ref[pl.ds(pl.multiple_of(my * MAT_SLICE, 8), MAT_SLICE), :]

        first = []
        for d in range(1, NDEV):
            to = _flip(me, d)
            theirs = pl.ds(pl.multiple_of(_slot(to) * MAT_SLICE, 8), MAT_SLICE)
            first.append(_remote(pack_ref.at[theirs, :], rs_ref.at[my], send_sems.at[0, d - 1], recv_sems.at[0, d - 1], to))
            first.append(_remote(dmp_ref.at[_slot(to)], dw_ref.at[my], send_sems.at[1, d - 1], recv_sems.at[1, d - 1], to))
            first.append(_remote(rv_ref.at[my], rv_ref.at[my], send_sems.at[2, d - 1], recv_sems.at[2, d - 1], to))
        for cp in first:
            cp.start()
        for cp in first:
            cp.wait()
        red = rs_ref[0]
        for k in range(1, NDEV):
            red = red + rs_ref[k]
        ag_ref[my] = red
        second = []
        for d in range(1, NDEV):
            to = _flip(me, d)
            second.append(_remote(ag_ref.at[my], ag_ref.at[my], send_sems.at[3, d - 1], recv_sems.at[3, d - 1], to))
        for cp in second:
            cp.start()

        rsum = rv_ref[0]
        for k in range(1, NDEV):
            rsum = rsum + rv_ref[k]
        rsum_ref[...] = rsum
        r8 = lax.broadcasted_iota(jnp.int32, (8, 1), 0)
        dmine = jnp.zeros((8, wc), F32)
        for k in range(NDEV):
            dmine = dmine + jnp.where(r8 == k, dw_ref[k], 0.0)
        dmine_ref[...] = dmine

        for cp in second:
            cp.wait()
        for k in range(NDEV):
            tot_ref[k * MAT_SLICE:(k + 1) * MAT_SLICE, :] = ag_ref[k]

    ins = (g_ws, g_wp, dbias, st1, st2, st3, st_f, vec2, dgate1, dgate3)
    outs = [jax.ShapeDtypeStruct((MAT_ROWS, 128), F32), jax.ShapeDtypeStruct((1, VEC_LANES), F32),
            jax.ShapeDtypeStruct((8, wc), F32)]
    return pl.pallas_call(
        body, name="small_reduce", grid=(1,), out_shape=outs,
        in_specs=[_whole(a) for a in ins], out_specs=[_whole(a) for a in outs],
        scratch_shapes=[pltpu.VMEM((MAT_ROWS, 128), F32), pltpu.VMEM((NDEV, MAT_SLICE, 128), F32),
                        pltpu.VMEM((NDEV, MAT_SLICE, 128), F32),
                        pltpu.VMEM((NDEV, 1, VEC_LANES), F32), pltpu.VMEM((NDEV, 1, wc), F32),
                        pltpu.VMEM((NDEV, 1, wc), F32),
                        pltpu.SemaphoreType.DMA((4, 7)), pltpu.SemaphoreType.DMA((4, 7))],
        compiler_params=_cparams(32, ("arbitrary",)),
    )(*ins)


def _small_update(tot, rsum, params):
    flat = [a for k in SMALL for a in params[k]]
    n_in = 2 + len(flat)

    def body(*refs):
        tot_ref, rsum_ref = refs[:2]
        p_hbm = refs[2:n_in]
        o_refs = refs[n_in:n_in + 4 * len(SMALL)]
        loss_ref = refs[n_in + 4 * len(SMALL)]
        p_refs = refs[n_in + 4 * len(SMALL) + 1:-1]
        sem = refs[-1]
        fetch = [pltpu.make_async_copy(p_hbm[k], p_refs[k], sem.at[k]) for k in range(len(flat))]
        for cp in fetch:
            cp.start()
        for cp in fetch:
            cp.wait()
        loss_ref[...] = rsum_ref[:, VEC_LANES - 128:VEC_LANES]

        def update(idx, g):
            w_ref, m_ref, v_ref = p_refs[3 * idx:3 * idx + 3]
            g_out, d_out, m_out, v_out = o_refs[4 * idx:4 * idx + 4]
            g = g.reshape(w_ref.shape)
            g_out[...] = g
            d_out[...], m_out[...], v_out[...] = _adamw(w_ref[...], g, m_ref[...], v_ref[...])

        update(0, tot_ref[0:1024, :])
        update(1, tot_ref[1024:1536, :])
        update(2, tot_ref[1536:1544, :])
        at = 0
        for idx, k in enumerate(VECS):
            update(3 + idx, rsum_ref[:, at:at + VEC_WIDTH[k]])
            at += VEC_WIDTH[k]

    outs = []
    for k in SMALL:
        outs += [jax.ShapeDtypeStruct(params[k][0].shape, F32)] * 4
    outs += [jax.ShapeDtypeStruct((1, 128), F32)]
    res = pl.pallas_call(
        body, name="small_update", grid=(1,), out_shape=outs,
        in_specs=[_whole(tot), _whole(rsum)] + [HBM] * len(flat), out_specs=[_whole(a) for a in outs],
        scratch_shapes=[pltpu.VMEM(a.shape, F32) for a in flat] + [pltpu.SemaphoreType.DMA((len(flat),))],
        compiler_params=_cparams(32, ("arbitrary",)),
    )(tot, rsum, *flat)
    return {k: tuple(res[4 * i:4 * i + 4]) for i, k in enumerate(SMALL)}, res[-1]


def _sum_adamw(recv, w, m, v, tr, name):
    R, C = w.shape

    def body(r_ref, w_ref, m_ref, v_ref, g_ref, d_ref, nm_ref, nv_ref):
        g = r_ref[0].astype(F32)
        for k in range(1, NDEV):
            g = g + r_ref[k].astype(F32)
        g_ref[...] = g
        d_ref[...], nm_ref[...], nv_ref[...] = _adamw(w_ref[...], g, m_ref[...], v_ref[...])

    blk = pl.BlockSpec((tr, C), lambda i: (i, 0))
    out = jax.ShapeDtypeStruct((R, C), F32)
    return pl.pallas_call(
        body, name=name, grid=(R // tr,), out_shape=[out] * 4,
        in_specs=[pl.BlockSpec((NDEV, tr, C), lambda i: (0, i, 0)), blk, blk, blk], out_specs=[blk] * 4,
        compiler_params=_cparams(48, ("arbitrary",)),
    )(recv, w, m, v)


def _ada_update(cact_all, dmine, w, m, v, tr):
    R, C = w.shape

    def body(c_ref, dm_ref, w_ref, m_ref, v_ref, g_ref, d_ref, nm_ref, nv_ref):
        g = _dot_tn(c_ref[...].astype(BF), dm_ref[...].astype(BF))
        g_ref[...] = g
        d_ref[...], nm_ref[...], nv_ref[...] = _adamw(w_ref[...], g, m_ref[...], v_ref[...])

    blk = pl.BlockSpec((tr, C), lambda i: (i, 0))
    out = jax.ShapeDtypeStruct((R, C), F32)
    return pl.pallas_call(
        body, name="update_w_ada", grid=(R // tr,), out_shape=[out] * 4,
        in_specs=[pl.BlockSpec((8, tr), lambda i: (0, i)), pl.BlockSpec((8, C), lambda i: (0, 0)), blk, blk, blk],
        out_specs=[blk] * 4,
        compiler_params=_cparams(48, ("arbitrary",)),
    )(cact_all, dmine, w, m, v)


FC = F // 2


def _ffn_fwd(x, mod, gn, w_in_t, w_out, sub, name, gather=(), loss=None, h=None):
    S = x.shape[0]
    T = min(T_FFN, S)
    nS, nJ = S // T, F // FC
    ng = len(gather)
    nl = 2 if loss else 0
    nh = 0 if h is None else 1
    forward_step = nS // 2

    def body(*refs):
        x_ref, mod_ref, gn_ref, wg_ref, wu_ref, wo_ref = refs[:6]
        gf_ref, t_ref = refs[6 + nh:6 + nh + nl] if loss else (None, None)
        shards = refs[6 + nh + nl:6 + nh + nl + ng]
        at = 6 + nh + nl + ng
        xo_ref, gu_ref = refs[at:at + 2]
        h_ref = refs[6] if nh else refs[at + 2]
        at += 3 - nh
        gathered = refs[at:at + ng]
        at += ng
        st_ref = refs[at] if loss else None
        at += nl // 2
        acc_scr = refs[at]
        sems = refs[at + 1:]
        i, j = pl.program_id(0), pl.program_id(1)

        if ng:
            @pl.when((i == 0) & (j == 0))
            def _():
                _gather_phase("start", shards, gathered, sems)

            @pl.when((i == forward_step) & (j == 0))
            def _():
                _gather_phase("forward", shards, gathered, sems)

        @pl.when(j == 0)
        def _():
            if not nh:
                _, _, _, hh = _rms_mod(x_ref[...], gn_ref[...], mod_ref[3 * sub:3 * sub + 1, :],
                                       mod_ref[3 * sub + 1:3 * sub + 2, :])
                h_ref[...] = hh.astype(BF)
            acc_scr[...] = jnp.zeros_like(acc_scr)

        hb = h_ref[...]
        g = _dot_nt(hb, wg_ref[0])
        u = _dot_nt(hb, wu_ref[0])
        gu_ref[0] = g.astype(BF)
        gu_ref[1] = u.astype(BF)
        a = (g * _sigmoid(g) * u).astype(BF)
        acc_scr[...] += _dot(a, wo_ref[...])

        @pl.when(j == nJ - 1)
        def _():
            xo = x_ref[...] + (0.5 * mod_ref[3 * sub + 2:3 * sub + 3, :]) * acc_scr[...]
            if not loss:
                xo_ref[...] = xo
            else:
                dx, dgf, part = _final_norm_loss(xo, gf_ref[...], t_ref[...])
                xo_ref[...] = dx
                upd = _rows3(dgf, jnp.broadcast_to(part, (1, D)), jnp.zeros((1, D), F32), D)

                @pl.when(i == 0)
                def _():
                    st_ref[...] = upd

                @pl.when(i > 0)
                def _():
                    st_ref[...] += upd

        if ng:
            @pl.when((i == nS - 1) & (j == nJ - 1))
            def _():
                _gather_phase("finish", shards, gathered, sems)

    tile = pl.BlockSpec((T, D), lambda i, j: (i, 0))
    res = pl.pallas_call(
        body, name=name, grid=(nS, nJ),
        out_shape=[jax.ShapeDtypeStruct((S, D), F32), jax.ShapeDtypeStruct((2, S, F), BF)]
                  + ([] if nh else [jax.ShapeDtypeStruct((S, D), BF)])
                  + [jax.ShapeDtypeStruct((NDEV,) + s.shape, s.dtype) for s in gather]
                  + ([jax.ShapeDtypeStruct((8, D), F32)] if loss else []),
        in_specs=[tile,
                  pl.BlockSpec((9, D), lambda i, j: (0, 0)),
                  pl.BlockSpec((1, D), lambda i, j: (0, 0)),
                  pl.BlockSpec((1, FC, D), lambda i, j: (0, j, 0)),
                  pl.BlockSpec((1, FC, D), lambda i, j: (1, j, 0)),
                  pl.BlockSpec((FC, D), lambda i, j: (j, 0))] + [tile] * nh
                 + ([pl.BlockSpec((1, D), lambda i, j: (0, 0)), tile] if loss else []) + [HBM] * ng,
        out_specs=[tile, pl.BlockSpec((2, T, FC), lambda i, j: (0, i, j))] + [tile] * (1 - nh) + [HBM] * ng
                  + ([pl.BlockSpec((8, D), lambda i, j: (0, 0))] if loss else []),
        scratch_shapes=[pltpu.VMEM((T, D), F32)] + (_comm_sems(ng) if ng else []),
        compiler_params=_cparams(56, ("arbitrary", "arbitrary")),
    )(x, mod, gn, w_in_t, w_in_t, w_out, *(() if h is None else (h,)), *(loss or ()), *gather)
    return res if h is None else [res[0], res[1], h, *res[2:]]


def _ffn_bwd_hidden(dx, mod, gu, w_out, sub, name, exchange=()):
    S = dx.shape[0]
    T = min(T_FFN, S)
    nS, nJ = S // T, F // FC
    ne = len(exchange)

    def body(*refs):
        dx_ref, mod_ref, gu_ref, wo_ref = refs[:4]
        sendbufs = refs[4:4 + ne]
        dgu_ref, gw_ref, dgate_ref = refs[4 + ne:7 + ne]
        recvbufs = refs[7 + ne:7 + 2 * ne]
        acc_scr = refs[7 + 2 * ne]
        sems = refs[8 + 2 * ne:]
        j, i = pl.program_id(0), pl.program_id(1)

        if ne:
            @pl.when((i == 0) & (j == 0))
            def _():
                _exchange_phase("start", sendbufs, recvbufs, sems)

        gate = mod_ref[3 * sub + 2:3 * sub + 3, :]
        dx = dx_ref[...]
        da = _dot_nt((dx * (0.5 * gate)).astype(BF), wo_ref[...])
        g = gu_ref[0].astype(F32)
        u = gu_ref[1].astype(F32)
        sg = _sigmoid(g)
        s = g * sg
        dgu_ref[0] = (da * u * (sg * (1.0 + g * (1.0 - sg)))).astype(BF)
        dgu_ref[1] = (da * s).astype(BF)
        contrib = _dot_tn((s * u).astype(BF), dx.astype(BF))

        @pl.when(i == 0)
        def _():
            acc_scr[...] = contrib

        @pl.when(i > 0)
        def _():
            acc_scr[...] += contrib

        @pl.when(i == nS - 1)
        def _():
            acc = acc_scr[...]
            dgate = 0.5 * jnp.sum(acc * wo_ref[...].astype(F32), axis=0, keepdims=True)
            dgate_ref[...] = jnp.broadcast_to(dgate, (8, D))
            gw_ref[...] = (acc * (0.5 * gate)).astype(BF)

        if ne:
            @pl.when((i == nS - 1) & (j == nJ - 1))
            def _():
                _exchange_phase("wait", sendbufs, recvbufs, sems)

    return pl.pallas_call(
        body, name=name, grid=(nJ, nS),
        out_shape=[jax.ShapeDtypeStruct((2, S, F), BF), jax.ShapeDtypeStruct((F, D), BF),
                   jax.ShapeDtypeStruct((8 * nJ, D), F32)] + _like(exchange),
        in_specs=[pl.BlockSpec((T, D), lambda j, i: (i, 0)),
                  pl.BlockSpec((9, D), lambda j, i: (0, 0)),
                  pl.BlockSpec((2, T, FC), lambda j, i: (0, i, j)),
                  pl.BlockSpec((FC, D), lambda j, i: (j, 0))] + [HBM] * ne,
        out_specs=[pl.BlockSpec((2, T, FC), lambda j, i: (0, i, j)),
                   pl.BlockSpec((FC, D), lambda j, i: (j, 0)),
                   pl.BlockSpec((8, D), lambda j, i: (j, 0))] + [HBM] * ne,
        scratch_shapes=[pltpu.VMEM((FC, D), F32)] + (_comm_sems(ne) if ne else []),
        compiler_params=_cparams(56, ("arbitrary", "arbitrary")),
    )(dx, mod, gu, w_out, *exchange)


def _ffn_bwd_input(dgu, w_in_t, x, dx, mod, gn, sub, name, exchange=()):
    S = x.shape[0]
    T = min(T_FFN, S)
    nS = S // T
    ne = len(exchange)
    NC = 256
    chunks = [slice(k * NC, (k + 1) * NC) for k in range(D // NC)]

    def body(*refs):
        dgu_ref, w_ref, x_ref, dx_ref, mod_ref, gn_ref = refs[:6]
        sendbufs = refs[6:6 + ne]
        dxin_ref, st_ref = refs[6 + ne:8 + ne]
        recvbufs = refs[8 + ne:8 + 2 * ne]
        dxh_scr = refs[8 + 2 * ne]
        sems = refs[9 + 2 * ne:]
        i = pl.program_id(0)

        if ne:
            @pl.when(i == 0)
            def _():
                _exchange_phase("start", sendbufs, recvbufs, sems)

        gn = gn_ref[...]
        scale = mod_ref[3 * sub + 1:3 * sub + 2, :]
        r, xhat, n, _ = _rms_mod(x_ref[...], gn, mod_ref[3 * sub:3 * sub + 1, :], scale)
        dg = dgu_ref[0]
        du = dgu_ref[1]
        rowsum = jnp.zeros((T, 1), F32)
        dshift, dscale, dgn = [], [], []
        for cols in chunks:
            dh = _dot(dg, w_ref[0, :, cols]) + _dot(du, w_ref[1, :, cols])
            dshift.append(jnp.sum(dh, axis=0, keepdims=True))
            dscale.append(jnp.sum(dh * n[:, cols], axis=0, keepdims=True))
            dn = dh * (1.0 + scale[:, cols])
            dgn.append(jnp.sum(dn * xhat[:, cols], axis=0, keepdims=True))
            dxhat = dn * gn[:, cols]
            rowsum = rowsum + jnp.sum(dxhat * xhat[:, cols], axis=-1, keepdims=True)
            dxh_scr[:, cols] = dxhat
        dxin_ref[...] = dx_ref[...] + r * (dxh_scr[...] - xhat * (rowsum / D))
        cat = lambda parts: jnp.concatenate(parts, axis=1)
        upd = _rows3(cat(dshift), cat(dscale), cat(dgn), D)

        @pl.when(i == 0)
        def _():
            st_ref[...] = upd

        @pl.when(i > 0)
        def _():
            st_ref[...] += upd

        if ne:
            @pl.when(i == nS - 1)
            def _():
                _exchange_phase("wait", sendbufs, recvbufs, sems)

    tile = pl.BlockSpec((T, D), lambda i: (i, 0))
    return pl.pallas_call(
        body, name=name, grid=(nS,),
        out_shape=[jax.ShapeDtypeStruct((S, D), F32), jax.ShapeDtypeStruct((8, D), F32)] + _like(exchange),
        in_specs=[pl.BlockSpec((2, T, F), lambda i: (0, i, 0)),
                  pl.BlockSpec((2, F, D), lambda i: (0, 0, 0), pipeline_mode=pl.Buffered(1)),
                  tile, tile,
                  pl.BlockSpec((9, D), lambda i: (0, 0)),
                  pl.BlockSpec((1, D), lambda i: (0, 0))] + [HBM] * ne,
        out_specs=[tile, pl.BlockSpec((8, D), lambda i: (0, 0))] + [HBM] * ne,
        scratch_shapes=[pltpu.VMEM((T, D), F32)] + (_comm_sems(ne) if ne else []),
        compiler_params=_cparams(60, ("arbitrary",)),
    )(dgu, w_in_t, x, dx, mod, gn, *exchange)


def _ffn_bwd_win(h, dgu, name, exchange=()):
    S = h.shape[0]
    T = min(T_WIN, S)
    nS, nJ = S // T, F // FC
    ne = len(exchange)

    def body(*refs):
        h_ref, dgu_ref = refs[:2]
        sendbufs = refs[2:2 + ne]
        out_ref = refs[2 + ne]
        recvbufs = refs[3 + ne:3 + 2 * ne]
        acc_scr = refs[3 + 2 * ne]
        sems = refs[4 + 2 * ne:]
        p, j, i = pl.program_id(0), pl.program_id(1), pl.program_id(2)

        if ne:
            @pl.when((p == 0) & (j == 0) & (i == 0))
            def _():
                _exchange_phase("start", sendbufs, recvbufs, sems)

        contrib = _dot_tn(dgu_ref[0], h_ref[...])

        @pl.when(i == 0)
        def _():
            acc_scr[...] = contrib

        @pl.when(i > 0)
        def _():
            acc_scr[...] += contrib

        @pl.when(i == nS - 1)
        def _():
            out_ref[0] = acc_scr[...].astype(BF)

        if ne:
            @pl.when((p == 1) & (j == nJ - 1) & (i == nS - 1))
            def _():
                _exchange_phase("wait", sendbufs, recvbufs, sems)

    return pl.pallas_call(
        body, name=name, grid=(2, nJ, nS),
        out_shape=[jax.ShapeDtypeStruct((2, F, D), BF)] + _like(exchange),
        in_specs=[pl.BlockSpec((T, D), lambda p, j, i: (i, 0)),
                  pl.BlockSpec((1, T, FC), lambda p, j, i: (p, i, j))] + [HBM] * ne,
        out_specs=[pl.BlockSpec((1, FC, D), lambda p, j, i: (p, j, 0))] + [HBM] * ne,
        scratch_shapes=[pltpu.VMEM((FC, D), F32)] + (_comm_sems(ne) if ne else []),
        compiler_params=_cparams(56, ("arbitrary", "arbitrary", "arbitrary")),
    )(h, dgu, *exchange)


def _pool_counts(pos0, T):
    pos = pos0 + lax.broadcasted_iota(jnp.int32, (T, 1), 0)
    return [jnp.minimum(pos + 1, w).astype(F32) for w in WINDOWS]


def _pool_fwd(xa, halo, ext_scr, cnts, T):
    ext_scr[0:HALO, :] = halo
    ext_scr[HALO:HALO + T, :] = xa
    out = []
    for gi, w in enumerate(WINDOWS):
        cols = slice(128 * gi, 128 * gi + 128)
        acc = xa[:, cols]
        for k in range(1, w):
            acc = acc + ext_scr[HALO - k:HALO - k + T, cols]
        out.append(acc / cnts[gi] - xa[:, cols])
    return out


def _sgu_fwd(vnb, ws_ref, sv_scr, T):
    lane = lax.broadcasted_iota(jnp.int32, (CHUNK, 128), 1)
    for n in range(T // CHUNK):
        rows = slice(n * CHUNK, (n + 1) * CHUNK)
        for b in range(DG // 128):
            cols = slice(128 * b, 128 * b + 128)
            vb = vnb[rows, cols]
            sv_scr[rows, cols] = jnp.where(lane < 64, _dot(ws_ref[2 * b], vb), _dot(ws_ref[2 * b + 1], vb))


def _mix_fwd(x, mod, gn, gn_next, wmi, wmo, wp, ps, lg, lb, ws, bias, name):
    S = x.shape[0]
    T = min(T_MIX_FWD, S)

    def body(x_ref, mod_ref, gn_ref, gnn_ref, wmi_ref, wmo_ref, wp_ref, ps_ref, lg_ref, lb_ref, ws_ref, bias_ref,
             xo_ref, hn_ref, carry_scr, ext_scr, sv_scr, ycat_scr):
        i = pl.program_id(0)

        @pl.when(i == 0)
        def _():
            carry_scr[...] = jnp.zeros_like(carry_scr)

        x = x_ref[...]
        _, _, _, h = _rms_mod(x, gn_ref[...], mod_ref[3:4, :], mod_ref[4:5, :])
        proj = _dot_nt(h.astype(BF), wmi_ref[...])
        xa = proj[:, 0:DP]
        p = _pool_fwd(xa, carry_scr[...], ext_scr, _pool_counts(i * T, T), T)
        carry_scr[...] = xa[T - HALO:T, :]
        for gi in range(4):
            cols = slice(128 * gi, 128 * gi + 128)
            ycat_scr[:, cols] = (_dot(p[gi].astype(BF), wp_ref[gi]) * ps_ref[:, cols]).astype(BF)
        u, _ = _gelu(proj[:, DP:DP + DG])
        v, _ = _gelu(proj[:, DP + DG:DPROJ])
        mu = jnp.mean(v, axis=-1, keepdims=True)
        vc = v - mu
        rstd = lax.rsqrt(jnp.mean(vc * vc, axis=-1, keepdims=True) + EPS)
        vn = vc * rstd * lg_ref[...] + lb_ref[...]
        _sgu_fwd(vn.astype(BF), ws_ref, sv_scr, T)
        for n in range(T // CHUNK):
            rows = slice(n * CHUNK, (n + 1) * CHUNK)
            ycat_scr[rows, DP:D] = (u[rows, :] * (sv_scr[rows, :] + bias_ref[...])).astype(BF)
        xo = x + mod_ref[5:6, :] * _dot(ycat_scr[...], wmo_ref[...])
        xo_ref[...] = xo
        _, _, _, hn = _rms_mod(xo, gnn_ref[...], mod_ref[6:7, :], mod_ref[7:8, :])
        hn_ref[...] = hn.astype(BF)

    full = lambda shape: pl.BlockSpec(shape, lambda i: (0,) * len(shape))
    tile = pl.BlockSpec((T, D), lambda i: (i, 0))
    return pl.pallas_call(
        body, name=name, grid=(S // T,),
        out_shape=[jax.ShapeDtypeStruct((S, D), F32), jax.ShapeDtypeStruct((S, D), BF)],
        in_specs=[tile, full((9, D)), full((1, D)), full((1, D)), full((DPROJ, D)), full((D, D)),
                  full((4, 128, 128)), full((1, DP)), full((1, DG)), full((1, DG)), full((8, CHUNK, CHUNK)),
                  full((CHUNK, DG))],
        out_specs=[tile, tile],
        scratch_shapes=[pltpu.VMEM((HALO, DP), F32), pltpu.VMEM((T + HALO, DP), F32), pltpu.VMEM((T, DG), F32),
                        pltpu.VMEM((T, D), BF)],
        compiler_params=_cparams(48, ("arbitrary",)),
    )(x, mod, gn, gn_next, wmi, wmo, wp, ps, lg, lb, ws, bias)


def _mix_bwd(x, dxo, mod, gn, wmi, wmo, wp, ps, lg, lb, ws, bias, name, exchange=()):
    S = x.shape[0]
    T = min(T_MIX, S)
    nS = S // T
    hb = T // HALO
    ne = len(exchange)

    def body(*refs):
        (x_ref, xh_ref, dxo_ref, mod_ref, gn_ref, wmi_ref, wmo_ref, wp_ref, ps_ref, lg_ref, lb_ref, ws_ref,
         bias_ref) = refs[:13]
        sendbufs = refs[13:13 + ne]
        dxi_ref, gwmi_out, gwmo_out, gwp_ref, gws_ref, st_ref, vec_ref, dbias_ref = refs[13 + ne:21 + ne]
        recvbufs = refs[21 + ne:21 + 2 * ne]
        (carry_scr, ext_scr, qext_scr, sv_scr, dvn_scr, ycat_scr, dproj_scr, gwmi_ref,
         gwmo_ref) = refs[21 + 2 * ne:30 + 2 * ne]
        sems = refs[30 + 2 * ne:]
        i = pl.program_id(0)
        t = nS - 1 - i
        gn = gn_ref[...]
        shift, scale, gate = mod_ref[3:4, :], mod_ref[4:5, :], mod_ref[5:6, :]

        @pl.when(i == 0)
        def _():
            if ne:
                _exchange_phase("start", sendbufs, recvbufs, sems)
            carry_scr[...] = jnp.zeros_like(carry_scr)
            gwmi_ref[...] = jnp.zeros_like(gwmi_ref)
            gwmo_ref[...] = jnp.zeros_like(gwmo_ref)
            gwp_ref[...] = jnp.zeros_like(gwp_ref)
            gws_ref[...] = jnp.zeros_like(gws_ref)
            st_ref[...] = jnp.zeros_like(st_ref)
            vec_ref[...] = jnp.zeros_like(vec_ref)
            dbias_ref[...] = jnp.zeros_like(dbias_ref)

        x = x_ref[...]
        dxo = dxo_ref[...]
        r, xhat, n, h = _rms_mod(x, gn, shift, scale)
        hbf = h.astype(BF)
        proj = _dot_nt(hbf, wmi_ref[...])
        xa = proj[:, 0:DP]
        zu = proj[:, DP:DP + DG]
        zv = proj[:, DP + DG:DPROJ]
        _, _, _, hh = _rms_mod(xh_ref[...], gn, shift, scale)
        halo = _dot_nt(hh.astype(BF), wmi_ref[0:DP, :])
        halo = jnp.where(t == 0, 0.0, halo)
        cnts = _pool_counts(t * T, T)
        p = _pool_fwd(xa, halo, ext_scr, cnts, T)
        m = []
        for gi in range(4):
            cols = slice(128 * gi, 128 * gi + 128)
            m.append(_dot(p[gi].astype(BF), wp_ref[gi]))
            ycat_scr[:, cols] = (m[gi] * ps_ref[:, cols]).astype(BF)
        u, tu = _gelu(zu)
        v, tv = _gelu(zv)
        mu = jnp.mean(v, axis=-1, keepdims=True)
        vc = v - mu
        rstd = lax.rsqrt(jnp.mean(vc * vc, axis=-1, keepdims=True) + EPS)
        vhat = vc * rstd
        lg = lg_ref[...]
        vnb = (vhat * lg + lb_ref[...]).astype(BF)
        _sgu_fwd(vnb, ws_ref, sv_scr, T)
        for nck in range(T // CHUNK):
            rows = slice(nck * CHUNK, (nck + 1) * CHUNK)
            sv_scr[rows, :] = sv_scr[rows, :] + bias_ref[...]
        sv = sv_scr[...]
        ycat_scr[:, DP:D] = (u * sv).astype(BF)

        gwmo_ref[...] += _dot_tn(ycat_scr[...], dxo.astype(BF))
        dyc = _dot_nt((dxo * gate).astype(BF), wmo_ref[...])
        dya = dyc[:, 0:DP]
        dyb = dyc[:, DP:D]

        dps = []
        dp = []
        for gi in range(4):
            cols = slice(128 * gi, 128 * gi + 128)
            dps.append(jnp.sum(dya[:, cols] * m[gi], axis=0, keepdims=True))
            dm = (dya[:, cols] * ps_ref[:, cols]).astype(BF)
            gwp_ref[gi] += _dot_tn(p[gi].astype(BF), dm)
            dp.append(_dot_nt(dm, wp_ref[gi]))
            qext_scr[0:T, cols] = dp[gi] / cnts[gi]
        qext_scr[T:T + HALO, :] = carry_scr[...]
        for gi, w in enumerate(WINDOWS):
            cols = slice(128 * gi, 128 * gi + 128)
            acc = qext_scr[0:T, cols]
            for k in range(1, w):
                acc = acc + qext_scr[k:k + T, cols]
            dproj_scr[:, cols] = (acc - dp[gi]).astype(BF)
        carry_scr[...] = qext_scr[0:HALO, :]

        du = dyb * sv
        dsv = dyb * u
        lane = lax.broadcasted_iota(jnp.int32, (CHUNK, 128), 1)
        dbias = jnp.zeros((CHUNK, DG), F32)
        for nck in range(T // CHUNK):
            rows = slice(nck * CHUNK, (nck + 1) * CHUNK)
            dbias = dbias + dsv[rows, :]
            for b in range(DG // 128):
                cols = slice(128 * b, 128 * b + 128)
                dsvb = dsv[rows, cols]
                vb = vnb[rows, cols]
                gws_ref[2 * b] += _dot_nt(jnp.where(lane < 64, dsvb, 0.0).astype(BF), vb)
                gws_ref[2 * b + 1] += _dot_nt(jnp.where(lane < 64, 0.0, dsvb).astype(BF), vb)
                dsvbb = dsvb.astype(BF)
                dvn_scr[rows, cols] = jnp.where(lane < 64, _dot_tn(ws_ref[2 * b], dsvbb),
                                                _dot_tn(ws_ref[2 * b + 1], dsvbb))
        dbias_ref[...] += dbias
        dvn = dvn_scr[...]
        dlg = jnp.sum(dvn * vhat, axis=0, keepdims=True)
        dlb = jnp.sum(dvn, axis=0, keepdims=True)
        dvhat = dvn * lg
        dv = rstd * (dvhat - jnp.mean(dvhat, axis=-1, keepdims=True)
                     - vhat * jnp.mean(dvhat * vhat, axis=-1, keepdims=True))
        dproj_scr[:, DP:DP + DG] = (du * _gelu_grad(zu, tu)).astype(BF)
        dproj_scr[:, DP + DG:DPROJ] = (dv * _gelu_grad(zv, tv)).astype(BF)
        vec_ref[...] += _rows3(jnp.concatenate(dps, axis=1), dlg, dlb, DP)

        dproj = dproj_scr[...]
        gwmi_ref[...] += _dot_tn(dproj, hbf)
        dh = _dot(dproj, wmi_ref[...])
        dxi, dshift, dscale, dgn = _rms_mod_bwd(dh, dxo, r, xhat, n, gn, scale)
        dxi_ref[...] = dxi
        st_ref[...] += _rows3(dshift, dscale, dgn, D)

        @pl.when(i == nS - 1)
        def _():
            acc = gwmo_ref[...]
            dgate = jnp.sum(acc * wmo_ref[...].astype(F32), axis=0, keepdims=True)
            row = lax.broadcasted_iota(jnp.int32, (8, D), 0)
            st_ref[...] += jnp.where(row == 3, dgate, 0.0)
            gwmo_out[...] = (acc * gate).astype(BF)
            gwmi_out[...] = gwmi_ref[...].astype(BF)
            tt = lax.broadcasted_iota(jnp.int32, (CHUNK, CHUNK), 0)
            ss = lax.broadcasted_iota(jnp.int32, (CHUNK, CHUNK), 1)
            for hd in range(8):
                gws_ref[hd] = jnp.where(tt >= ss, gws_ref[hd], 0.0)
            if ne:
                _exchange_phase("wait", sendbufs, recvbufs, sems)

    full = lambda shape: pl.BlockSpec(shape, lambda i: (0,) * len(shape))
    return pl.pallas_call(
        body, name=name, grid=(nS,),
        out_shape=[jax.ShapeDtypeStruct((S, D), F32), jax.ShapeDtypeStruct((DPROJ, D), BF),
                   jax.ShapeDtypeStruct((D, D), BF), jax.ShapeDtypeStruct((4, 128, 128), F32),
                   jax.ShapeDtypeStruct((8, CHUNK, CHUNK), F32), jax.ShapeDtypeStruct((8, D), F32),
                   jax.ShapeDtypeStruct((8, DP), F32), jax.ShapeDtypeStruct((CHUNK, DG), F32)] + _like(exchange),
        in_specs=[pl.BlockSpec((T, D), lambda i: (nS - 1 - i, 0)),
                  pl.BlockSpec((HALO, D), lambda i: (jnp.maximum((nS - 1 - i) * hb - 1, 0), 0)),
                  pl.BlockSpec((T, D), lambda i: (nS - 1 - i, 0)),
                  full((9, D)), full((1, D)), full((DPROJ, D)), full((D, D)),
                  full((4, 128, 128)), full((1, DP)), full((1, DG)), full((1, DG)), full((8, CHUNK, CHUNK)),
                  full((CHUNK, DG))] + [HBM] * ne,
        out_specs=[pl.BlockSpec((T, D), lambda i: (nS - 1 - i, 0)), full((DPROJ, D)), full((D, D)),
                   full((4, 128, 128)), full((8, CHUNK, CHUNK)), full((8, D)), full((8, DP)), full((CHUNK, DG))]
                  + [HBM] * ne,
        scratch_shapes=[pltpu.VMEM((HALO, DP), F32), pltpu.VMEM((T + HALO, DP), F32),
                        pltpu.VMEM((T + HALO, DP), F32), pltpu.VMEM((T, DG), F32), pltpu.VMEM((T, DG), F32),
                        pltpu.VMEM((T, D), BF), pltpu.VMEM((T, DPROJ), BF), pltpu.VMEM((DPROJ, D), F32),
                        pltpu.VMEM((D, D), F32)] + (_comm_sems(ne) if ne else []),
        compiler_params=_cparams(56, ("arbitrary",)),
    )(x, x, dxo, mod, gn, wmi, wmo, wp, ps, lg, lb, ws, bias, *exchange)


def kernel(x, c, w_ada, b_ada, norm_ffn1_g, ffn1_w_in, ffn1_w_out, norm_mix_g, w_mix_in, w_pool, pool_scale, gmlp_ln_g, gmlp_ln_b, w_spatial, b_spatial, w_mix_out, norm_ffn2_g, ffn2_w_in, ffn2_w_out, norm_final_g, loss_target, m_w_ada, m_b_ada, m_norm_ffn1_g, m_ffn1_w_in, m_ffn1_w_out, m_norm_mix_g, m_w_mix_in, m_w_pool, m_pool_scale, m_gmlp_ln_g, m_gmlp_ln_b, m_w_spatial, m_b_spatial, m_w_mix_out, m_norm_ffn2_g, m_ffn2_w_in, m_ffn2_w_out, m_norm_final_g, v_w_ada, v_b_ada, v_norm_ffn1_g, v_ffn1_w_in, v_ffn1_w_out, v_norm_mix_g, v_w_mix_in, v_w_pool, v_pool_scale, v_gmlp_ln_g, v_gmlp_ln_b, v_w_spatial, v_b_spatial, v_w_mix_out, v_norm_ffn2_g, v_ffn2_w_in, v_ffn2_w_out, v_norm_final_g):
    weights = dict(w_ada=w_ada, b_ada=b_ada, norm_ffn1_g=norm_ffn1_g, ffn1_w_in=ffn1_w_in, ffn1_w_out=ffn1_w_out,
                   norm_mix_g=norm_mix_g, w_mix_in=w_mix_in, w_pool=w_pool, pool_scale=pool_scale,
                   gmlp_ln_g=gmlp_ln_g, gmlp_ln_b=gmlp_ln_b, w_spatial=w_spatial, b_spatial=b_spatial,
                   w_mix_out=w_mix_out, norm_ffn2_g=norm_ffn2_g, ffn2_w_in=ffn2_w_in, ffn2_w_out=ffn2_w_out,
                   norm_final_g=norm_final_g)
    mom1 = dict(w_ada=m_w_ada, b_ada=m_b_ada, norm_ffn1_g=m_norm_ffn1_g, ffn1_w_in=m_ffn1_w_in,
                ffn1_w_out=m_ffn1_w_out, norm_mix_g=m_norm_mix_g, w_mix_in=m_w_mix_in, w_pool=m_w_pool,
                pool_scale=m_pool_scale, gmlp_ln_g=m_gmlp_ln_g, gmlp_ln_b=m_gmlp_ln_b, w_spatial=m_w_spatial,
                b_spatial=m_b_spatial, w_mix_out=m_w_mix_out, norm_ffn2_g=m_norm_ffn2_g, ffn2_w_in=m_ffn2_w_in,
                ffn2_w_out=m_ffn2_w_out, norm_final_g=m_norm_final_g)
    mom2 = dict(w_ada=v_w_ada, b_ada=v_b_ada, norm_ffn1_g=v_norm_ffn1_g, ffn1_w_in=v_ffn1_w_in,
                ffn1_w_out=v_ffn1_w_out, norm_mix_g=v_norm_mix_g, w_mix_in=v_w_mix_in, w_pool=v_w_pool,
                pool_scale=v_pool_scale, gmlp_ln_g=v_gmlp_ln_g, gmlp_ln_b=v_gmlp_ln_b, w_spatial=v_w_spatial,
                b_spatial=v_b_spatial, w_mix_out=v_w_mix_out, norm_ffn2_g=v_norm_ffn2_g, ffn2_w_in=v_ffn2_w_in,
                ffn2_w_out=v_ffn2_w_out, norm_final_g=v_norm_final_g)
    order = list(weights)
    xs = x[0]
    target = loss_target[0]
    transposed = ("ffn1_w_in", "w_mix_in", "ffn2_w_in")
    big = ("ffn1_w_in", "ffn1_w_out", "w_mix_in", "w_mix_out", "ffn2_w_in", "ffn2_w_out")
    local = lambda a, k: a[0].T if k in transposed else a[0]
    wc = w_ada.shape[2]

    shard = dict(zip(big, _cast_shards([local(weights[k], k) for k in big])))
    modp, cact_all, g_w1_in, g_w1_out = _ada_forward(jnp.broadcast_to(c, (8, D)), w_ada[0], b_ada.reshape(NDEV, wc),
                                                     [shard["ffn1_w_in"], shard["ffn1_w_out"]])
    mod = modp.reshape(9, D)
    w1_in = g_w1_in.reshape(2, F, D)
    w1_out = g_w1_out.reshape(F, D)

    x1, gu1, h1, g_wmi, g_wmo, g_w2_out, g_w2_in = _ffn_fwd(
        xs, mod, norm_ffn1_g, w1_in, w1_out, 0, "ffn1_fwd",
        gather=[shard["w_mix_in"], shard["w_mix_out"], shard["ffn2_w_out"], shard["ffn2_w_in"]])
    wmi = g_wmi.reshape(DPROJ, D)
    wmo = g_wmo.reshape(D, D)
    w2_in = g_w2_in.reshape(2, F, D)
    w2_out = g_w2_out.reshape(F, D)
    tril = jnp.tril(jnp.ones((CHUNK, CHUNK), dtype=bool))
    ws_b = jnp.where(tril[None], w_spatial[0], 0.0).astype(BF)
    wp_b = w_pool[0].astype(BF)
    bias = jnp.repeat(b_spatial[0].T, DG // 8, axis=1)
    mix_args = (wmi, wmo, wp_b, pool_scale, gmlp_ln_g, gmlp_ln_b, ws_b, bias)
    x2, h3 = _mix_fwd(x1, mod, norm_mix_g, norm_ffn2_g, *mix_args, "mix_fwd")
    dx3, gu3, h3, st_f = _ffn_fwd(x2, mod, norm_ffn2_g, w2_in, w2_out, 2, "ffn2_fwd", h=h3,
                                  loss=(norm_final_g.reshape(1, D), target))

    slots = lambda a: a.reshape(NDEV, a.size // (NDEV * D), D)
    dgu3, d_w2_out, dgate3 = _ffn_bwd_hidden(dx3, mod, gu3, w2_out, 2, "ffn2_bwd_hidden")
    d_w2_in = _ffn_bwd_win(h3, dgu3, "ffn2_bwd_win")[0]
    dx2, st3 = _ffn_bwd_input(dgu3, w2_in, x2, dx3, mod, norm_ffn2_g, 2, "ffn2_bwd_input")
    dx1, d_wmi, d_wmo, d_wp, d_ws, st2, vec2, dbias, r_w2_in, r_w2_out = _mix_bwd(
        x1, dx2, mod, norm_mix_g, *mix_args, "mix_bwd", exchange=[slots(d_w2_in), slots(d_w2_out)])
    dgu1, d_w1_out, dgate1, r_wmi, r_wmo = _ffn_bwd_hidden(
        dx1, mod, gu1, w1_out, 0, "ffn1_bwd_hidden", exchange=[slots(d_wmi), slots(d_wmo)])
    d_w1_in, r_w1_out = _ffn_bwd_win(h1, dgu1, "ffn1_bwd_win", exchange=[slots(d_w1_out)])
    dx0, st1, r_w1_in = _ffn_bwd_input(dgu1, w1_in, xs, dx1, mod, norm_ffn1_g, 0, "ffn1_bwd_input",
                                       exchange=[slots(d_w1_in)])

    received = dict(ffn1_w_in=r_w1_in, ffn1_w_out=r_w1_out, w_mix_in=r_wmi, w_mix_out=r_wmo,
                    ffn2_w_in=r_w2_in, ffn2_w_out=r_w2_out)
    tiles = dict(ffn1_w_in=176, ffn1_w_out=176, w_mix_in=96, w_mix_out=128, ffn2_w_in=176, ffn2_w_out=176)
    result = {}
    for k, recv in received.items():
        res = _sum_adamw(recv, local(weights[k], k), local(mom1[k], k), local(mom2[k], k), tiles[k], "update_" + k)
        result[k] = tuple((a.T if k in transposed else a)[None] for a in res)
    row = lambda a: a.reshape(1, D)
    params = {k: (weights[k], mom1[k], mom2[k]) for k in SMALL}
    params["norm_final_g"] = (row(norm_final_g), row(m_norm_final_g), row(v_norm_final_g))
    tot, rsum, dmine = _small_reduce(d_ws, d_wp, dbias, st1, st2, st3, st_f, vec2, dgate1, dgate3)
    small, loss_row = _small_update(tot, rsum, params)
    result.update(small)
    result["norm_final_g"] = tuple(a.reshape(D) for a in small["norm_final_g"])
    result["w_ada"] = tuple(a[None] for a in _ada_update(cact_all, dmine, w_ada[0], m_w_ada[0], v_w_ada[0], 256))

    return (loss_row[0, 0], dx0[None], *[result[k][0] for k in order], *[result[k][1] for k in order],
            *[result[k][2] for k in order], *[result[k][3] for k in order])
```

```python
import math

import jax
import jax.numpy as jnp
from jax import lax
from jax.experimental import pallas as pl
from jax.experimental.pallas import tpu as pltpu

D = 1024
F = 2816
DP = 512
DG = 512
DPROJ = DP + 2 * DG
CHUNK = 128
WINDOWS = (2, 4, 8, 16)
HALO = 16
NDEV = 8
T_FFN = 512
T_MIX = 256
T_MIX_FWD = 512
T_WIN = 2048
EPS = 1e-6
LR, B1, B2, AEPS, WD, STEP = 0.001, 0.9, 0.999, 1e-08, 0.01, 10
BC1 = 1.0 - B1 ** STEP
BC2 = 1.0 - B2 ** STEP
GELU_C = math.sqrt(2.0 / math.pi)
GELU_A = 0.044715

BF = jnp.bfloat16
F32 = jnp.float32
MESH = pl.DeviceIdType.MESH
HBM = pl.BlockSpec(memory_space=pltpu.HBM)


def _whole(a):
    return pl.BlockSpec(a.shape, lambda i: (0,) * len(a.shape))


NT = (((1,), (1,)), ((), ()))
TN = (((0,), (0,)), ((), ()))


def _dot(a, b):
    return jnp.dot(a, b, preferred_element_type=F32)


def _dot_nt(a, b):
    return lax.dot_general(a, b, NT, preferred_element_type=F32)


def _dot_tn(a, b):
    return lax.dot_general(a, b, TN, preferred_element_type=F32)


def _cparams(vmem_mb, sem=None):
    kw = dict(vmem_limit_bytes=vmem_mb * 1024 * 1024)
    if sem is not None:
        kw["dimension_semantics"] = sem
    return pltpu.CompilerParams(**kw)


def _position():
    return lax.axis_index("x"), lax.axis_index("y"), lax.axis_index("c")


def _slot(p):
    return 4 * p[0] + 2 * p[1] + p[2]


def _flip(me, d):
    x, y, c = me
    return (1 - x if d & 4 else x, 1 - y if d & 2 else y, 1 - c if d & 1 else c)


def _remote(src, dst, send_sem, recv_sem, to):
    return pltpu.make_async_remote_copy(src_ref=src, dst_ref=dst, send_sem=send_sem, recv_sem=recv_sem,
                                        device_id=to, device_id_type=MESH)


def _comm_sems(n):
    return [pltpu.SemaphoreType.DMA((n, 7)), pltpu.SemaphoreType.DMA((n, 7)), pltpu.SemaphoreType.DMA((n,))]


def _gather_phase(phase, xs, outs, sems):
    send_sems, recv_sems, local_sems = sems
    n = len(xs)
    me = _position()
    x, y, c = me
    sibling = (x, y, 1 - c)
    xn, yn, diag = (1 - x, y), (x, 1 - y), (1 - x, 1 - y)
    relay_from = (x + c * (1 - 2 * x), y + (1 - c) * (1 - 2 * y))
    relay_to = (x + (1 - c) * (1 - 2 * x), y + c * (1 - 2 * y))

    def copy(a, k, block, to, src=None):
        dst = outs[a].at[_slot(block)]
        return _remote(dst if src is None else src, dst, send_sems.at[a, k], recv_sems.at[a, k], to)

    def mine(a):
        return pltpu.make_async_copy(xs[a], outs[a].at[_slot(me)], local_sems.at[a])

    def first(a):
        return [copy(a, 0, me, sibling, src=xs[a]), copy(a, 1, me, (*xn, c), src=xs[a]),
                copy(a, 2, me, (*yn, c), src=xs[a])]

    def second(a):
        return [copy(a, 3, (*relay_from, c), (*relay_to, c)), copy(a, 4, (*xn, c), sibling),
                copy(a, 5, (*yn, c), sibling)]

    def third(a):
        return copy(a, 6, (*diag, c), sibling)

    if phase == "start":
        for a in range(n):
            mine(a).start()
            for cp in first(a):
                cp.start()
    elif phase == "forward":
        for a in range(n):
            copy(a, 1, (*xn, c), me).wait_recv()
            copy(a, 2, (*yn, c), me).wait_recv()
            for cp in second(a):
                cp.start()
    else:
        for a in range(n):
            copy(a, 3, (*diag, c), me).wait_recv()
            third(a).start()
        for a in range(n):
            copy(a, 0, sibling, me).wait_recv()
            for k, chip in ((4, xn), (5, yn), (6, diag)):
                copy(a, k, (*chip, 1 - c), me).wait_recv()
        for a in range(n):
            for cp in first(a) + second(a) + [third(a)]:
                cp.wait_send()
            mine(a).wait()


def _exchange_phase(phase, xs, outs, sems):
    send_sems, recv_sems, local_sems = sems
    me = _position()
    for a in range(len(xs)):
        copies = [pltpu.make_async_copy(xs[a].at[_slot(me)], outs[a].at[_slot(me)], local_sems.at[a])]
        for d in range(1, NDEV):
            to = _flip(me, d)
            copies.append(_remote(xs[a].at[_slot(to)], outs[a].at[_slot(me)],
                                  send_sems.at[a, d - 1], recv_sems.at[a, d - 1], to))
        for cp in copies:
            if phase == "start":
                cp.start()
            else:
                cp.wait()


def _like(bufs):
    return [jax.ShapeDtypeStruct(b.shape, b.dtype) for b in bufs]


def _rms_mod(x, gn, shift, scale):
    ms = jnp.mean(x * x, axis=-1, keepdims=True)
    r = lax.rsqrt(ms + EPS)
    xhat = x * r
    n = xhat * gn
    h = n * (1.0 + scale) + shift
    return r, xhat, n, h


def _rms_mod_bwd(dh, dres, r, xhat, n, gn, scale):
    dshift = jnp.sum(dh, axis=0, keepdims=True)
    dscale = jnp.sum(dh * n, axis=0, keepdims=True)
    dn = dh * (1.0 + scale)
    dgn = jnp.sum(dn * xhat, axis=0, keepdims=True)
    dxhat = dn * gn
    dx = dres + r * (dxhat - xhat * jnp.mean(dxhat * xhat, axis=-1, keepdims=True))
    return dx, dshift, dscale, dgn


def _final_norm_loss(x, gf, target):
    r = lax.rsqrt(jnp.mean(x * x, axis=-1, keepdims=True) + EPS)
    xhat = x * r
    e = xhat * gf - target
    part = 0.5 * jnp.sum(jnp.sum(e * e, axis=-1, keepdims=True), axis=0, keepdims=True) / D
    dy = e / D
    dgf = jnp.sum(dy * xhat, axis=0, keepdims=True)
    dxhat = dy * gf
    dx = r * (dxhat - xhat * jnp.mean(dxhat * xhat, axis=-1, keepdims=True))
    return dx, dgf, part


def _rows3(a, b, c, width):
    row = lax.broadcasted_iota(jnp.int32, (8, width), 0)
    z = jnp.zeros((8, width), F32)
    return jnp.where(row == 0, a, z) + jnp.where(row == 1, b, z) + jnp.where(row == 2, c, z)


def _sigmoid(x):
    return 0.5 * jnp.tanh(0.5 * x) + 0.5


def _gelu(x):
    t = jnp.tanh(GELU_C * (x + GELU_A * x * x * x))
    return 0.5 * x * (1.0 + t), t


def _gelu_grad(x, t):
    return 0.5 * (1.0 + t) + 0.5 * x * (1.0 - t * t) * GELU_C * (1.0 + 3.0 * GELU_A * x * x)


def _adamw(w, g, m, v):
    m = B1 * m + (1.0 - B1) * g
    v = B2 * v + (1.0 - B2) * (g * g)
    m_hat = m / BC1
    v_hat = v / BC2
    delta = -LR * (m_hat / (jnp.sqrt(v_hat) + AEPS) + WD * w)
    return delta, m, v


def _cast_shards(shards):
    n = len(shards)

    def body(*refs):
        for a in range(n):
            refs[n + a][...] = refs[a][...].astype(BF)

    resident = pl.BlockSpec(memory_space=pltpu.VMEM)
    return pl.pallas_call(
        body, name="cast_shards", out_shape=[jax.ShapeDtypeStruct(s.shape, BF) for s in shards],
        in_specs=[resident] * n, out_specs=[resident] * n, compiler_params=_cparams(40),
    )(*shards)


def _ada_forward(c8, w_ada, b8, shards):
    wc = w_ada.shape[1]
    n = len(shards)

    def body(*refs):
        c8_ref, w_ref, b8_ref = refs[:3]
        xs = refs[3:3 + n]
        mod_ref, cact_ref = refs[3 + n:5 + n]
        gathered = refs[5 + n:5 + 2 * n]
        call_ref, mall_ref, send_sems, recv_sems = refs[5 + 2 * n:9 + 2 * n]
        gsems = refs[9 + 2 * n:]
        me = _position()
        my = _slot(me)
        row = lax.broadcasted_iota(jnp.int32, (8, 1), 0)
        call_ref[my] = c8_ref[...]
        sends = []
        for d in range(1, NDEV):
            to = _flip(me, d)
            sends.append(_remote(call_ref.at[my], call_ref.at[my], send_sems.at[0, d - 1], recv_sems.at[0, d - 1], to))
        for cp in sends:
            cp.start()
        _gather_phase("start", xs, gathered, gsems)
        for cp in sends:
            cp.wait()
        c_all = jnp.zeros((8, D), F32)
        for k in range(NDEV):
            c_all = c_all + jnp.where(row == k, call_ref[k], 0.0)
        cact = c_all * jax.nn.sigmoid(c_all)
        cact_ref[...] = cact
        part = _dot(cact.astype(BF), w_ref[...].astype(BF))
        mall_ref[my] = part
        sends = []
        for d in range(1, NDEV):
            to = _flip(me, d)
            sends.append(_remote(mall_ref.at[my], mall_ref.at[my], send_sems.at[1, d - 1], recv_sems.at[1, d - 1], to))
        for cp in sends:
            cp.start()
        for cp in sends:
            cp.wait()
        out = jnp.zeros((8, wc), F32)
        for k in range(NDEV):
            piece = jnp.sum(jnp.where(row == my, mall_ref[k], 0.0), axis=0, keepdims=True)
            out = out + jnp.where(row == k, piece, 0.0)
        mod_ref[...] = out + b8_ref[...]
        _gather_phase("forward", xs, gathered, gsems)
        _gather_phase("finish", xs, gathered, gsems)

    outs = [jax.ShapeDtypeStruct((8, wc), F32), jax.ShapeDtypeStruct((8, D), F32)]
    return pl.pallas_call(
        body, name="ada_forward", grid=(1,),
        out_shape=outs + [jax.ShapeDtypeStruct((NDEV,) + s.shape, s.dtype) for s in shards],
        in_specs=[_whole(a) for a in (c8, w_ada, b8)] + [HBM] * n,
        out_specs=[_whole(a) for a in outs] + [HBM] * n,
        scratch_shapes=[pltpu.VMEM((NDEV, 8, D), F32), pltpu.VMEM((NDEV, 8, wc), F32),
                        pltpu.SemaphoreType.DMA((2, 7)), pltpu.SemaphoreType.DMA((2, 7))] + _comm_sems(n),
        compiler_params=_cparams(32, ("arbitrary",)),
    )(c8, w_ada, b8, *shards)


MATS = ("w_spatial", "w_pool", "b_spatial")
VECS = ("norm_ffn1_g", "norm_mix_g", "norm_ffn2_g", "norm_final_g", "pool_scale", "gmlp_ln_g", "gmlp_ln_b", "b_ada")
VEC_WIDTH = dict(norm_ffn1_g=D, norm_mix_g=D, norm_ffn2_g=D, norm_final_g=D, pool_scale=DP, gmlp_ln_g=DG,
                 gmlp_ln_b=DG, b_ada=9 * D)
MAT_ROWS = 1600
MAT_SLICE = MAT_ROWS // NDEV
VEC_LANES = sum(VEC_WIDTH.values()) + 128
DMOD_AT = VEC_LANES - 128 - 9 * D
SMALL = MATS + VECS


def _small_reduce(g_ws, g_wp, dbias, st1, st2, st3, st_f, vec2, dgate1, dgate3):
    wc = 9 * D // NDEV

    def body(g_ws_ref, g_wp_ref, dbias_ref, st1_ref, st2_ref, st3_ref, stf_ref, vec2_ref, dg1_ref, dg3_ref,
             tot_ref, rsum_ref, dmine_ref,
             pack_ref, rs_ref, ag_ref, rv_ref, dmp_ref, dw_ref, send_sems, recv_sems):
        me = _position()
        my = _slot(me)

        pack_ref[0:1024, :] = g_ws_ref[...].reshape(1024, 128)
        pack_ref[1024:1536, :] = g_wp_ref[...].reshape(512, 128)
        ch = lax.broadcasted_iota(jnp.int32, (DG, 128), 0)
        hd = lax.broadcasted_iota(jnp.int32, (DG, 128), 1)
        sel = jnp.where(ch // 64 == hd, 1.0, 0.0).astype(F32)
        heads = jnp.dot(dbias_ref[...], sel, preferred_element_type=F32, precision=lax.Precision.HIGHEST)
        pack_ref[1536:1544, :] = heads.T[0:8, :]
        pack_ref[1544:MAT_ROWS, :] = jnp.zeros((MAT_ROWS - 1544, 128), F32)
        dgate1 = dg1_ref[0:1, :] + dg1_ref[8:9, :]
        dgate3 = dg3_ref[0:1, :] + dg3_ref[8:9, :]
        row = jnp.concatenate(
            [st1_ref[2:3, :], st2_ref[2:3, :], st3_ref[2:3, :], stf_ref[0:1, :],
             vec2_ref[0:1, :], vec2_ref[1:2, :], vec2_ref[2:3, :],
             st1_ref[0:1, :], st1_ref[1:2, :], dgate1, st2_ref[0:1, :], st2_ref[1:2, :], st2_ref[3:4, :],
             st3_ref[0:1, :], st3_ref[1:2, :], dgate3, stf_ref[1:2, 0:128]], axis=1)
        rv_ref[my] = row
        for k in range(NDEV):
            dmp_ref[k] = row[:, DMOD_AT + wc * k:DMOD_AT + wc * (k + 1)]
        dw_ref[my] = dmp_ref[my]
        rs_ref[my] = pack_ref[pl.ds(pl.multiple_of(my * MAT_SLICE, 8), MAT_SLICE), :]

        first = []
        for d in range(1, NDEV):
            to = _flip(me, d)
            theirs = pl.ds(pl.multiple_of(_slot(to) * MAT_SLICE, 8), MAT_SLICE)
            first.append(_remote(pack_ref.at[theirs, :], rs_ref.at[my], send_sems.at[0, d - 1], recv_sems.at[0, d - 1], to))
            first.append(_remote(dmp_ref.at[_slot(to)], dw_ref.at[my], send_sems.at[1, d - 1], recv_sems.at[1, d - 1], to))
            first.append(_remote(rv_ref.at[my], rv_ref.at[my], send_sems.at[2, d - 1], recv_sems.at[2, d - 1], to))
        for cp in first:
            cp.start()
        for cp in first:
            cp.wait()
        red = rs_ref[0]
        for k in range(1, NDEV):
            red = red + rs_ref[k]
        ag_ref[my] = red
        second = []
        for d in range(1, NDEV):
            to = _flip(me, d)
            second.append(_remote(ag_ref.at[my], ag_ref.at[my], send_sems.at[3, d - 1], recv_sems.at[3, d - 1], to))
        for cp in second:
            cp.start()

        rsum = rv_ref[0]
        for k in range(1, NDEV):
            rsum = rsum + rv_ref[k]
        rsum_ref[...] = rsum
        r8 = lax.broadcasted_iota(jnp.int32, (8, 1), 0)
        dmine = jnp.zeros((8, wc), F32)
        for k in range(NDEV):
            dmine = dmine + jnp.where(r8 == k, dw_ref[k], 0.0)
        dmine_ref[...] = dmine

        for cp in second:
            cp.wait()
        for k in range(NDEV):
            tot_ref[k * MAT_SLICE:(k + 1) * MAT_SLICE, :] = ag_ref[k]

    ins = (g_ws, g_wp, dbias, st1, st2, st3, st_f, vec2, dgate1, dgate3)
    outs = [jax.ShapeDtypeStruct((MAT_ROWS, 128), F32), jax.ShapeDtypeStruct((1, VEC_LANES), F32),
            jax.ShapeDtypeStruct((8, wc), F32)]
    return pl.pallas_call(
        body, name="small_reduce", grid=(1,), out_shape=outs,
        in_specs=[_whole(a) for a in ins], out_specs=[_whole(a) for a in outs],
        scratch_shapes=[pltpu.VMEM((MAT_ROWS, 128), F32), pltpu.VMEM((NDEV, MAT_SLICE, 128), F32),
                        pltpu.VMEM((NDEV, MAT_SLICE, 128), F32),
                        pltpu.VMEM((NDEV, 1, VEC_LANES), F32), pltpu.VMEM((NDEV, 1, wc), F32),
                        pltpu.VMEM((NDEV, 1, wc), F32),
                        pltpu.SemaphoreType.DMA((4, 7)), pltpu.SemaphoreType.DMA((4, 7))],
        compiler_params=_cparams(32, ("arbitrary",)),
    )(*ins)


def _small_update(tot, rsum, params):
    flat = [a for k in SMALL for a in params[k]]
    n_in = 2 + len(flat)

    def body(*refs):
        tot_ref, rsum_ref = refs[:2]
        p_hbm = refs[2:n_in]
        o_refs = refs[n_in:n_in + 4 * len(SMALL)]
        loss_ref = refs[n_in + 4 * len(SMALL)]
        p_refs = refs[n_in + 4 * len(SMALL) + 1:-1]
        sem = refs[-1]
        fetch = [pltpu.make_async_copy(p_hbm[k], p_refs[k], sem.at[k]) for k in range(len(flat))]
        for cp in fetch:
            cp.start()
        for cp in fetch:
            cp.wait()
        loss_ref[...] = rsum_ref[:, VEC_LANES - 128:VEC_LANES]

        def update(idx, g):
            w_ref, m_ref, v_ref = p_refs[3 * idx:3 * idx + 3]
            g_out, d_out, m_out, v_out = o_refs[4 * idx:4 * idx + 4]
            g = g.reshape(w_ref.shape)
            g_out[...] = g
            d_out[...], m_out[...], v_out[...] = _adamw(w_ref[...], g, m_ref[...], v_ref[...])

        update(0, tot_ref[0:1024, :])
        update(1, tot_ref[1024:1536, :])
        update(2, tot_ref[1536:1544, :])
        at = 0
        for idx, k in enumerate(VECS):
            update(3 + idx, rsum_ref[:, at:at + VEC_WIDTH[k]])
            at += VEC_WIDTH[k]

    outs = []
    for k in SMALL:
        outs += [jax.ShapeDtypeStruct(params[k][0].shape, F32)] * 4
    outs += [jax.ShapeDtypeStruct((1, 128), F32)]
    res = pl.pallas_call(
        body, name="small_update", grid=(1,), out_shape=outs,
        in_specs=[_whole(tot), _whole(rsum)] + [HBM] * len(flat), out_specs=[_whole(a) for a in outs],
        scratch_shapes=[pltpu.VMEM(a.shape, F32) for a in flat] + [pltpu.SemaphoreType.DMA((len(flat),))],
        compiler_params=_cparams(32, ("arbitrary",)),
    )(tot, rsum, *[pltpu.with_memory_space_constraint(a, pltpu.HBM) for a in flat])
    return {k: tuple(res[4 * i:4 * i + 4]) for i, k in enumerate(SMALL)}, res[-1]


def _sum_adamw(recv, w, m, v, tr, name):
    R, C = w.shape

    def body(r_ref, w_ref, m_ref, v_ref, g_ref, d_ref, nm_ref, nv_ref):
        g = r_ref[0].astype(F32)
        for k in range(1, NDEV):
            g = g + r_ref[k].astype(F32)
        g_ref[...] = g
        d_ref[...], nm_ref[...], nv_ref[...] = _adamw(w_ref[...], g, m_ref[...], v_ref[...])

    blk = pl.BlockSpec((tr, C), lambda i: (i, 0))
    out = jax.ShapeDtypeStruct((R, C), F32)
    return pl.pallas_call(
        body, name=name, grid=(R // tr,), out_shape=[out] * 4,
        in_specs=[pl.BlockSpec((NDEV, tr, C), lambda i: (0, i, 0)), blk, blk, blk], out_specs=[blk] * 4,
        compiler_params=_cparams(48, ("arbitrary",)),
    )(recv, w, m, v)


def _ada_update(cact_all, dmine, w, m, v, tr):
    R, C = w.shape

    def body(c_ref, dm_ref, w_ref, m_ref, v_ref, g_ref, d_ref, nm_ref, nv_ref):
        g = _dot_tn(c_ref[...].astype(BF), dm_ref[...].astype(BF))
        g_ref[...] = g
        d_ref[...], nm_ref[...], nv_ref[...] = _adamw(w_ref[...], g, m_ref[...], v_ref[...])

    blk = pl.BlockSpec((tr, C), lambda i: (i, 0))
    out = jax.ShapeDtypeStruct((R, C), F32)
    return pl.pallas_call(
        body, name="update_w_ada", grid=(R // tr,), out_shape=[out] * 4,
        in_specs=[pl.BlockSpec((8, tr), lambda i: (0, i)), pl.BlockSpec((8, C), lambda i: (0, 0)), blk, blk, blk],
        out_specs=[blk] * 4,
        compiler_params=_cparams(48, ("arbitrary",)),
    )(cact_all, dmine, w, m, v)


FC = F // 2


def _ffn_fwd(x, mod, gn, w_in_t, w_out, sub, name, gather=(), loss=None, h=None):
    S = x.shape[0]
    T = min(T_FFN, S)
    nS, nJ = S // T, F // FC
    ng = len(gather)
    nl = 2 if loss else 0
    nh = 0 if h is None else 1
    forward_step = nS // 2

    def body(*refs):
        x_ref, mod_ref, gn_ref, wg_ref, wu_ref, wo_ref = refs[:6]
        gf_ref, t_ref = refs[6 + nh:6 + nh + nl] if loss else (None, None)
        shards = refs[6 + nh + nl:6 + nh + nl + ng]
        at = 6 + nh + nl + ng
        xo_ref, gu_ref = refs[at:at + 2]
        h_ref = refs[6] if nh else refs[at + 2]
        at += 3 - nh
        gathered = refs[at:at + ng]
        at += ng
        st_ref = refs[at] if loss else None
        at += nl // 2
        acc_scr = refs[at]
        sems = refs[at + 1:]
        i, j = pl.program_id(0), pl.program_id(1)

        if ng:
            @pl.when((i == 0) & (j == 0))
            def _():
                _gather_phase("start", shards, gathered, sems)

            @pl.when((i == forward_step) & (j == 0))
            def _():
                _gather_phase("forward", shards, gathered, sems)

        @pl.when(j == 0)
        def _():
            if not nh:
                _, _, _, hh = _rms_mod(x_ref[...], gn_ref[...], mod_ref[3 * sub:3 * sub + 1, :],
                                       mod_ref[3 * sub + 1:3 * sub + 2, :])
                h_ref[...] = hh.astype(BF)
            acc_scr[...] = jnp.zeros_like(acc_scr)

        hb = h_ref[...]
        g = _dot_nt(hb, wg_ref[0])
        u = _dot_nt(hb, wu_ref[0])
        gu_ref[0] = g.astype(BF)
        gu_ref[1] = u.astype(BF)
        a = (g * _sigmoid(g) * u).astype(BF)
        acc_scr[...] += _dot(a, wo_ref[...])

        @pl.when(j == nJ - 1)
        def _():
            xo = x_ref[...] + (0.5 * mod_ref[3 * sub + 2:3 * sub + 3, :]) * acc_scr[...]
            if not loss:
                xo_ref[...] = xo
            else:
                dx, dgf, part = _final_norm_loss(xo, gf_ref[...], t_ref[...])
                xo_ref[...] = dx
                upd = _rows3(dgf, jnp.broadcast_to(part, (1, D)), jnp.zeros((1, D), F32), D)

                @pl.when(i == 0)
                def _():
                    st_ref[...] = upd

                @pl.when(i > 0)
                def _():
                    st_ref[...] += upd

        if ng:
            @pl.when((i == nS - 1) & (j == nJ - 1))
            def _():
                _gather_phase("finish", shards, gathered, sems)

    tile = pl.BlockSpec((T, D), lambda i, j: (i, 0))
    res = pl.pallas_call(
        body, name=name, grid=(nS, nJ),
        out_shape=[jax.ShapeDtypeStruct((S, D), F32), jax.ShapeDtypeStruct((2, S, F), BF)]
                  + ([] if nh else [jax.ShapeDtypeStruct((S, D), BF)])
                  + [jax.ShapeDtypeStruct((NDEV,) + s.shape, s.dtype) for s in gather]
                  + ([jax.ShapeDtypeStruct((8, D), F32)] if loss else []),
        in_specs=[tile,
                  pl.BlockSpec((9, D), lambda i, j: (0, 0)),
                  pl.BlockSpec((1, D), lambda i, j: (0, 0)),
                  pl.BlockSpec((1, FC, D), lambda i, j: (0, j, 0)),
                  pl.BlockSpec((1, FC, D), lambda i, j: (1, j, 0)),
                  pl.BlockSpec((FC, D), lambda i, j: (j, 0))] + [tile] * nh
                 + ([pl.BlockSpec((1, D), lambda i, j: (0, 0)), tile] if loss else []) + [HBM] * ng,
        out_specs=[tile, pl.BlockSpec((2, T, FC), lambda i, j: (0, i, j))] + [tile] * (1 - nh) + [HBM] * ng
                  + ([pl.BlockSpec((8, D), lambda i, j: (0, 0))] if loss else []),
        scratch_shapes=[pltpu.VMEM((T, D), F32)] + (_comm_sems(ng) if ng else []),
        compiler_params=_cparams(56, ("arbitrary", "arbitrary")),
    )(x, mod, gn, w_in_t, w_in_t, w_out, *(() if h is None else (h,)), *(loss or ()), *gather)
    return res if h is None else [res[0], res[1], h, *res[2:]]


def _ffn_bwd_hidden(dx, mod, gu, w_out, sub, name, exchange=()):
    S = dx.shape[0]
    T = min(T_FFN, S)
    nS, nJ = S // T, F // FC
    ne = len(exchange)

    def body(*refs):
        dx_ref, mod_ref, gu_ref, wo_ref = refs[:4]
        sendbufs = refs[4:4 + ne]
        dgu_ref, gw_ref, dgate_ref = refs[4 + ne:7 + ne]
        recvbufs = refs[7 + ne:7 + 2 * ne]
        acc_scr = refs[7 + 2 * ne]
        sems = refs[8 + 2 * ne:]
        j, i = pl.program_id(0), pl.program_id(1)

        if ne:
            @pl.when((i == 0) & (j == 0))
            def _():
                _exchange_phase("start", sendbufs, recvbufs, sems)

        gate = mod_ref[3 * sub + 2:3 * sub + 3, :]
        dx = dx_ref[...]
        da = _dot_nt((dx * (0.5 * gate)).astype(BF), wo_ref[...])
        g = gu_ref[0].astype(F32)
        u = gu_ref[1].astype(F32)
        sg = _sigmoid(g)
        s = g * sg
        dgu_ref[0] = (da * u * (sg * (1.0 + g * (1.0 - sg)))).astype(BF)
        dgu_ref[1] = (da * s).astype(BF)
        contrib = _dot_tn((s * u).astype(BF), dx.astype(BF))

        @pl.when(i == 0)
        def _():
            acc_scr[...] = contrib

        @pl.when(i > 0)
        def _():
            acc_scr[...] += contrib

        @pl.when(i == nS - 1)
        def _():
            acc = acc_scr[...]
            dgate = 0.5 * jnp.sum(acc * wo_ref[...].astype(F32), axis=0, keepdims=True)
            dgate_ref[...] = jnp.broadcast_to(dgate, (8, D))
            gw_ref[...] = (acc * (0.5 * gate)).astype(BF)

        if ne:
            @pl.when((i == nS - 1) & (j == nJ - 1))
            def _():
                _exchange_phase("wait", sendbufs, recvbufs, sems)

    return pl.pallas_call(
        body, name=name, grid=(nJ, nS),
        out_shape=[jax.ShapeDtypeStruct((2, S, F), BF), jax.ShapeDtypeStruct((F, D), BF),
                   jax.ShapeDtypeStruct((8 * nJ, D), F32)] + _like(exchange),
        in_specs=[pl.BlockSpec((T, D), lambda j, i: (i, 0)),
                  pl.BlockSpec((9, D), lambda j, i: (0, 0)),
                  pl.BlockSpec((2, T, FC), lambda j, i: (0, i, j)),
                  pl.BlockSpec((FC, D), lambda j, i: (j, 0))] + [HBM] * ne,
        out_specs=[pl.BlockSpec((2, T, FC), lambda j, i: (0, i, j)),
                   pl.BlockSpec((FC, D), lambda j, i: (j, 0)),
                   pl.BlockSpec((8, D), lambda j, i: (j, 0))] + [HBM] * ne,
        scratch_shapes=[pltpu.VMEM((FC, D), F32)] + (_comm_sems(ne) if ne else []),
        compiler_params=_cparams(56, ("arbitrary", "arbitrary")),
    )(dx, mod, gu, w_out, *exchange)


def _ffn_bwd_input(dgu, w_in_t, x, dx, mod, gn, sub, name, exchange=()):
    S = x.shape[0]
    T = min(T_FFN, S)
    nS = S // T
    ne = len(exchange)
    NC = 256
    chunks = [slice(k * NC, (k + 1) * NC) for k in range(D // NC)]

    def body(*refs):
        dgu_ref, w_ref, x_ref, dx_ref, mod_ref, gn_ref = refs[:6]
        sendbufs = refs[6:6 + ne]
        dxin_ref, st_ref = refs[6 + ne:8 + ne]
        recvbufs = refs[8 + ne:8 + 2 * ne]
        dxh_scr = refs[8 + 2 * ne]
        sems = refs[9 + 2 * ne:]
        i = pl.program_id(0)

        if ne:
            @pl.when(i == 0)
            def _():
                _exchange_phase("start", sendbufs, recvbufs, sems)

        gn = gn_ref[...]
        scale = mod_ref[3 * sub + 1:3 * sub + 2, :]
        r, xhat, n, _ = _rms_mod(x_ref[...], gn, mod_ref[3 * sub:3 * sub + 1, :], scale)
        dg = dgu_ref[0]
        du = dgu_ref[1]
        rowsum = jnp.zeros((T, 1), F32)
        dshift, dscale, dgn = [], [], []
        for cols in chunks:
            dh = _dot(dg, w_ref[0, :, cols]) + _dot(du, w_ref[1, :, cols])
            dshift.append(jnp.sum(dh, axis=0, keepdims=True))
            dscale.append(jnp.sum(dh * n[:, cols], axis=0, keepdims=True))
            dn = dh * (1.0 + scale[:, cols])
            dgn.append(jnp.sum(dn * xhat[:, cols], axis=0, keepdims=True))
            dxhat = dn * gn[:, cols]
            rowsum = rowsum + jnp.sum(dxhat * xhat[:, cols], axis=-1, keepdims=True)
            dxh_scr[:, cols] = dxhat
        dxin_ref[...] = dx_ref[...] + r * (dxh_scr[...] - xhat * (rowsum / D))
        cat = lambda parts: jnp.concatenate(parts, axis=1)
        upd = _rows3(cat(dshift), cat(dscale), cat(dgn), D)

        @pl.when(i == 0)
        def _():
            st_ref[...] = upd

        @pl.when(i > 0)
        def _():
            st_ref[...] += upd

        if ne:
            @pl.when(i == nS - 1)
            def _():
                _exchange_phase("wait", sendbufs, recvbufs, sems)

    tile = pl.BlockSpec((T, D), lambda i: (i, 0))
    return pl.pallas_call(
        body, name=name, grid=(nS,),
        out_shape=[jax.ShapeDtypeStruct((S, D), F32), jax.ShapeDtypeStruct((8, D), F32)] + _like(exchange),
        in_specs=[pl.BlockSpec((2, T, F), lambda i: (0, i, 0)),
                  pl.BlockSpec((2, F, D), lambda i: (0, 0, 0), pipeline_mode=pl.Buffered(1)),
                  tile, tile,
                  pl.BlockSpec((9, D), lambda i: (0, 0)),
                  pl.BlockSpec((1, D), lambda i: (0, 0))] + [HBM] * ne,
        out_specs=[tile, pl.BlockSpec((8, D), lambda i: (0, 0))] + [HBM] * ne,
        scratch_shapes=[pltpu.VMEM((T, D), F32)] + (_comm_sems(ne) if ne else []),
        compiler_params=_cparams(60, ("arbitrary",)),
    )(dgu, w_in_t, x, dx, mod, gn, *exchange)


def _ffn_bwd_win(h, dgu, name, exchange=()):
    S = h.shape[0]
    T = min(T_WIN, S)
    nS, nJ = S // T, F // FC
    ne = len(exchange)

    def body(*refs):
        h_ref, dgu_ref = refs[:2]
        sendbufs = refs[2:2 + ne]
        out_ref = refs[2 + ne]
        recvbufs = refs[3 + ne:3 + 2 * ne]
        acc_scr = refs[3 + 2 * ne]
        sems = refs[4 + 2 * ne:]
        p, j, i = pl.program_id(0), pl.program_id(1), pl.program_id(2)

        if ne:
            @pl.when((p == 0) & (j == 0) & (i == 0))
            def _():
                _exchange_phase("start", sendbufs, recvbufs, sems)

        contrib = _dot_tn(dgu_ref[0], h_ref[...])

        @pl.when(i == 0)
        def _():
            acc_scr[...] = contrib

        @pl.when(i > 0)
        def _():
            acc_scr[...] += contrib

        @pl.when(i == nS - 1)
        def _():
            out_ref[0] = acc_scr[...].astype(BF)

        if ne:
            @pl.when((p == 1) & (j == nJ - 1) & (i == nS - 1))
            def _():
                _exchange_phase("wait", sendbufs, recvbufs, sems)

    return pl.pallas_call(
        body, name=name, grid=(2, nJ, nS),
        out_shape=[jax.ShapeDtypeStruct((2, F, D), BF)] + _like(exchange),
        in_specs=[pl.BlockSpec((T, D), lambda p, j, i: (i, 0)),
                  pl.BlockSpec((1, T, FC), lambda p, j, i: (p, i, j))] + [HBM] * ne,
        out_specs=[pl.BlockSpec((1, FC, D), lambda p, j, i: (p, j, 0))] + [HBM] * ne,
        scratch_shapes=[pltpu.VMEM((FC, D), F32)] + (_comm_sems(ne) if ne else []),
        compiler_params=_cparams(56, ("arbitrary", "arbitrary", "arbitrary")),
    )(h, dgu, *exchange)


def _pool_counts(pos0, T):
    pos = pos0 + lax.broadcasted_iota(jnp.int32, (T, 1), 0)
    return [jnp.minimum(pos + 1, w).astype(F32) for w in WINDOWS]


def _pool_fwd(xa, halo, ext_scr, cnts, T):
    ext_scr[0:HALO, :] = halo
    ext_scr[HALO:HALO + T, :] = xa
    out = []
    for gi, w in enumerate(WINDOWS):
        cols = slice(128 * gi, 128 * gi + 128)
        acc = xa[:, cols]
        for k in range(1, w):
            acc = acc + ext_scr[HALO - k:HALO - k + T, cols]
        out.append(acc / cnts[gi] - xa[:, cols])
    return out


def _sgu_fwd(vnb, ws_ref, sv_scr, T):
    lane = lax.broadcasted_iota(jnp.int32, (CHUNK, 128), 1)
    for n in range(T // CHUNK):
        rows = slice(n * CHUNK, (n + 1) * CHUNK)
        for b in range(DG // 128):
            cols = slice(128 * b, 128 * b + 128)
            vb = vnb[rows, cols]
            sv_scr[rows, cols] = jnp.where(lane < 64, _dot(ws_ref[2 * b], vb), _dot(ws_ref[2 * b + 1], vb))


def _mix_fwd(x, mod, gn, gn_next, wmi, wmo, wp, ps, lg, lb, ws, bias, name):
    S = x.shape[0]
    T = min(T_MIX_FWD, S)

    def body(x_ref, mod_ref, gn_ref, gnn_ref, wmi_ref, wmo_ref, wp_ref, ps_ref, lg_ref, lb_ref, ws_ref, bias_ref,
             xo_ref, hn_ref, carry_scr, ext_scr, sv_scr, ycat_scr):
        i = pl.program_id(0)

        @pl.when(i == 0)
        def _():
            carry_scr[...] = jnp.zeros_like(carry_scr)

        x = x_ref[...]
        _, _, _, h = _rms_mod(x, gn_ref[...], mod_ref[3:4, :], mod_ref[4:5, :])
        proj = _dot_nt(h.astype(BF), wmi_ref[...])
        xa = proj[:, 0:DP]
        p = _pool_fwd(xa, carry_scr[...], ext_scr, _pool_counts(i * T, T), T)
        carry_scr[...] = xa[T - HALO:T, :]
        for gi in range(4):
            cols = slice(128 * gi, 128 * gi + 128)
            ycat_scr[:, cols] = (_dot(p[gi].astype(BF), wp_ref[gi]) * ps_ref[:, cols]).astype(BF)
        u, _ = _gelu(proj[:, DP:DP + DG])
        v, _ = _gelu(proj[:, DP + DG:DPROJ])
        mu = jnp.mean(v, axis=-1, keepdims=True)
        vc = v - mu
        rstd = lax.rsqrt(jnp.mean(vc * vc, axis=-1, keepdims=True) + EPS)
        vn = vc * rstd * lg_ref[...] + lb_ref[...]
        _sgu_fwd(vn.astype(BF), ws_ref, sv_scr, T)
        for n in range(T // CHUNK):
            rows = slice(n * CHUNK, (n + 1) * CHUNK)
            ycat_scr[rows, DP:D] = (u[rows, :] * (sv_scr[rows, :] + bias_ref[...])).astype(BF)
        xo = x + mod_ref[5:6, :] * _dot(ycat_scr[...], wmo_ref[...])
        xo_ref[...] = xo
        _, _, _, hn = _rms_mod(xo, gnn_ref[...], mod_ref[6:7, :], mod_ref[7:8, :])
        hn_ref[...] = hn.astype(BF)

    full = lambda shape: pl.BlockSpec(shape, lambda i: (0,) * len(shape))
    tile = pl.BlockSpec((T, D), lambda i: (i, 0))
    return pl.pallas_call(
        body, name=name, grid=(S // T,),
        out_shape=[jax.ShapeDtypeStruct((S, D), F32), jax.ShapeDtypeStruct((S, D), BF)],
        in_specs=[tile, full((9, D)), full((1, D)), full((1, D)), full((DPROJ, D)), full((D, D)),
                  full((4, 128, 128)), full((1, DP)), full((1, DG)), full((1, DG)), full((8, CHUNK, CHUNK)),
                  full((CHUNK, DG))],
        out_specs=[tile, tile],
        scratch_shapes=[pltpu.VMEM((HALO, DP), F32), pltpu.VMEM((T + HALO, DP), F32), pltpu.VMEM((T, DG), F32),
                        pltpu.VMEM((T, D), BF)],
        compiler_params=_cparams(48, ("arbitrary",)),
    )(x, mod, gn, gn_next, wmi, wmo, wp, ps, lg, lb, ws, bias)


def _mix_bwd(x, dxo, mod, gn, wmi, wmo, wp, ps, lg, lb, ws, bias, name, exchange=()):
    S = x.shape[0]
    T = min(T_MIX, S)
    nS = S // T
    hb = T // HALO
    ne = len(exchange)

    def body(*refs):
        (x_ref, xh_ref, dxo_ref, mod_ref, gn_ref, wmi_ref, wmo_ref, wp_ref, ps_ref, lg_ref, lb_ref, ws_ref,
         bias_ref) = refs[:13]
        sendbufs = refs[13:13 + ne]
        dxi_ref, gwmi_out, gwmo_out, gwp_ref, gws_ref, st_ref, vec_ref, dbias_ref = refs[13 + ne:21 + ne]
        recvbufs = refs[21 + ne:21 + 2 * ne]
        (carry_scr, ext_scr, qext_scr, sv_scr, dvn_scr, ycat_scr, dproj_scr, gwmi_ref,
         gwmo_ref) = refs[21 + 2 * ne:30 + 2 * ne]
        sems = refs[30 + 2 * ne:]
        i = pl.program_id(0)
        t = nS - 1 - i
        gn = gn_ref[...]
        shift, scale, gate = mod_ref[3:4, :], mod_ref[4:5, :], mod_ref[5:6, :]

        @pl.when(i == 0)
        def _():
            if ne:
                _exchange_phase("start", sendbufs, recvbufs, sems)
            carry_scr[...] = jnp.zeros_like(carry_scr)
            gwmi_ref[...] = jnp.zeros_like(gwmi_ref)
            gwmo_ref[...] = jnp.zeros_like(gwmo_ref)
            gwp_ref[...] = jnp.zeros_like(gwp_ref)
            gws_ref[...] = jnp.zeros_like(gws_ref)
            st_ref[...] = jnp.zeros_like(st_ref)
            vec_ref[...] = jnp.zeros_like(vec_ref)
            dbias_ref[...] = jnp.zeros_like(dbias_ref)

        x = x_ref[...]
        dxo = dxo_ref[...]
        r, xhat, n, h = _rms_mod(x, gn, shift, scale)
        hbf = h.astype(BF)
        proj = _dot_nt(hbf, wmi_ref[...])
        xa = proj[:, 0:DP]
        zu = proj[:, DP:DP + DG]
        zv = proj[:, DP + DG:DPROJ]
        _, _, _, hh = _rms_mod(xh_ref[...], gn, shift, scale)
        halo = _dot_nt(hh.astype(BF), wmi_ref[0:DP, :])
        halo = jnp.where(t == 0, 0.0, halo)
        cnts = _pool_counts(t * T, T)
        p = _pool_fwd(xa, halo, ext_scr, cnts, T)
        m = []
        for gi in range(4):
            cols = slice(128 * gi, 128 * gi + 128)
            m.append(_dot(p[gi].astype(BF), wp_ref[gi]))
            ycat_scr[:, cols] = (m[gi] * ps_ref[:, cols]).astype(BF)
        u, tu = _gelu(zu)
        v, tv = _gelu(zv)
        mu = jnp.mean(v, axis=-1, keepdims=True)
        vc = v - mu
        rstd = lax.rsqrt(jnp.mean(vc * vc, axis=-1, keepdims=True) + EPS)
        vhat = vc * rstd
        lg = lg_ref[...]
        vnb = (vhat * lg + lb_ref[...]).astype(BF)
        _sgu_fwd(vnb, ws_ref, sv_scr, T)
        for nck in range(T // CHUNK):
            rows = slice(nck * CHUNK, (nck + 1) * CHUNK)
            sv_scr[rows, :] = sv_scr[rows, :] + bias_ref[...]
        sv = sv_scr[...]
        ycat_scr[:, DP:D] = (u * sv).astype(BF)

        gwmo_ref[...] += _dot_tn(ycat_scr[...], dxo.astype(BF))
        dyc = _dot_nt((dxo * gate).astype(BF), wmo_ref[...])
        dya = dyc[:, 0:DP]
        dyb = dyc[:, DP:D]

        dps = []
        dp = []
        for gi in range(4):
            cols = slice(128 * gi, 128 * gi + 128)
            dps.append(jnp.sum(dya[:, cols] * m[gi], axis=0, keepdims=True))
            dm = (dya[:, cols] * ps_ref[:, cols]).astype(BF)
            gwp_ref[gi] += _dot_tn(p[gi].astype(BF), dm)
            dp.append(_dot_nt(dm, wp_ref[gi]))
            qext_scr[0:T, cols] = dp[gi] / cnts[gi]
        qext_scr[T:T + HALO, :] = carry_scr[...]
        for gi, w in enumerate(WINDOWS):
            cols = slice(128 * gi, 128 * gi + 128)
            acc = qext_scr[0:T, cols]
            for k in range(1, w):
                acc = acc + qext_scr[k:k + T, cols]
            dproj_scr[:, cols] = (acc - dp[gi]).astype(BF)
        carry_scr[...] = qext_scr[0:HALO, :]

        du = dyb * sv
        dsv = dyb * u
        lane = lax.broadcasted_iota(jnp.int32, (CHUNK, 128), 1)
        dbias = jnp.zeros((CHUNK, DG), F32)
        for nck in range(T // CHUNK):
            rows = slice(nck * CHUNK, (nck + 1) * CHUNK)
            dbias = dbias + dsv[rows, :]
            for b in range(DG // 128):
                cols = slice(128 * b, 128 * b + 128)
                dsvb = dsv[rows, cols]
                vb = vnb[rows, cols]
                gws_ref[2 * b] += _dot_nt(jnp.where(lane < 64, dsvb, 0.0).astype(BF), vb)
                gws_ref[2 * b + 1] += _dot_nt(jnp.where(lane < 64, 0.0, dsvb).astype(BF), vb)
                dsvbb = dsvb.astype(BF)
                dvn_scr[rows, cols] = jnp.where(lane < 64, _dot_tn(ws_ref[2 * b], dsvbb),
                                                _dot_tn(ws_ref[2 * b + 1], dsvbb))
        dbias_ref[...] += dbias
        dvn = dvn_scr[...]
        dlg = jnp.sum(dvn * vhat, axis=0, keepdims=True)
        dlb = jnp.sum(dvn, axis=0, keepdims=True)
        dvhat = dvn * lg
        dv = rstd * (dvhat - jnp.mean(dvhat, axis=-1, keepdims=True)
                     - vhat * jnp.mean(dvhat * vhat, axis=-1, keepdims=True))
        dproj_scr[:, DP:DP + DG] = (du * _gelu_grad(zu, tu)).astype(BF)
        dproj_scr[:, DP + DG:DPROJ] = (dv * _gelu_grad(zv, tv)).astype(BF)
        vec_ref[...] += _rows3(jnp.concatenate(dps, axis=1), dlg, dlb, DP)

        dproj = dproj_scr[...]
        gwmi_ref[...] += _dot_tn(dproj, hbf)
        dh = _dot(dproj, wmi_ref[...])
        dxi, dshift, dscale, dgn = _rms_mod_bwd(dh, dxo, r, xhat, n, gn, scale)
        dxi_ref[...] = dxi
        st_ref[...] += _rows3(dshift, dscale, dgn, D)

        @pl.when(i == nS - 1)
        def _():
            acc = gwmo_ref[...]
            dgate = jnp.sum(acc * wmo_ref[...].astype(F32), axis=0, keepdims=True)
            row = lax.broadcasted_iota(jnp.int32, (8, D), 0)
            st_ref[...] += jnp.where(row == 3, dgate, 0.0)
            gwmo_out[...] = (acc * gate).astype(BF)
            gwmi_out[...] = gwmi_ref[...].astype(BF)
            tt = lax.broadcasted_iota(jnp.int32, (CHUNK, CHUNK), 0)
            ss = lax.broadcasted_iota(jnp.int32, (CHUNK, CHUNK), 1)
            for hd in range(8):
                gws_ref[hd] = jnp.where(tt >= ss, gws_ref[hd], 0.0)
            if ne:
                _exchange_phase("wait", sendbufs, recvbufs, sems)

    full = lambda shape: pl.BlockSpec(shape, lambda i: (0,) * len(shape))
    return pl.pallas_call(
        body, name=name, grid=(nS,),
        out_shape=[jax.ShapeDtypeStruct((S, D), F32), jax.ShapeDtypeStruct((DPROJ, D), BF),
                   jax.ShapeDtypeStruct((D, D), BF), jax.ShapeDtypeStruct((4, 128, 128), F32),
                   jax.ShapeDtypeStruct((8, CHUNK, CHUNK), F32), jax.ShapeDtypeStruct((8, D), F32),
                   jax.ShapeDtypeStruct((8, DP), F32), jax.ShapeDtypeStruct((CHUNK, DG), F32)] + _like(exchange),
        in_specs=[pl.BlockSpec((T, D), lambda i: (nS - 1 - i, 0)),
                  pl.BlockSpec((HALO, D), lambda i: (jnp.maximum((nS - 1 - i) * hb - 1, 0), 0)),
                  pl.BlockSpec((T, D), lambda i: (nS - 1 - i, 0)),
                  full((9, D)), full((1, D)), full((DPROJ, D)), full((D, D)),
                  full((4, 128, 128)), full((1, DP)), full((1, DG)), full((1, DG)), full((8, CHUNK, CHUNK)),
                  full((CHUNK, DG))] + [HBM] * ne,
        out_specs=[pl.BlockSpec((T, D), lambda i: (nS - 1 - i, 0)), full((DPROJ, D)), full((D, D)),
                   full((4, 128, 128)), full((8, CHUNK, CHUNK)), full((8, D)), full((8, DP)), full((CHUNK, DG))]
                  + [HBM] * ne,
        scratch_shapes=[pltpu.VMEM((HALO, DP), F32), pltpu.VMEM((T + HALO, DP), F32),
                        pltpu.VMEM((T + HALO, DP), F32), pltpu.VMEM((T, DG), F32), pltpu.VMEM((T, DG), F32),
                        pltpu.VMEM((T, D), BF), pltpu.VMEM((T, DPROJ), BF), pltpu.VMEM((DPROJ, D), F32),
                        pltpu.VMEM((D, D), F32)] + (_comm_sems(ne) if ne else []),
        compiler_params=_cparams(56, ("arbitrary",)),
    )(x, x, dxo, mod, gn, wmi, wmo, wp, ps, lg, lb, ws, bias, *exchange)


def kernel(x, c, w_ada, b_ada, norm_ffn1_g, ffn1_w_in, ffn1_w_out, norm_mix_g, w_mix_in, w_pool, pool_scale, gmlp_ln_g, gmlp_ln_b, w_spatial, b_spatial, w_mix_out, norm_ffn2_g, ffn2_w_in, ffn2_w_out, norm_final_g, loss_target, m_w_ada, m_b_ada, m_norm_ffn1_g, m_ffn1_w_in, m_ffn1_w_out, m_norm_mix_g, m_w_mix_in, m_w_pool, m_pool_scale, m_gmlp_ln_g, m_gmlp_ln_b, m_w_spatial, m_b_spatial, m_w_mix_out, m_norm_ffn2_g, m_ffn2_w_in, m_ffn2_w_out, m_norm_final_g, v_w_ada, v_b_ada, v_norm_ffn1_g, v_ffn1_w_in, v_ffn1_w_out, v_norm_mix_g, v_w_mix_in, v_w_pool, v_pool_scale, v_gmlp_ln_g, v_gmlp_ln_b, v_w_spatial, v_b_spatial, v_w_mix_out, v_norm_ffn2_g, v_ffn2_w_in, v_ffn2_w_out, v_norm_final_g):
    weights = dict(w_ada=w_ada, b_ada=b_ada, norm_ffn1_g=norm_ffn1_g, ffn1_w_in=ffn1_w_in, ffn1_w_out=ffn1_w_out,
                   norm_mix_g=norm_mix_g, w_mix_in=w_mix_in, w_pool=w_pool, pool_scale=pool_scale,
                   gmlp_ln_g=gmlp_ln_g, gmlp_ln_b=gmlp_ln_b, w_spatial=w_spatial, b_spatial=b_spatial,
                   w_mix_out=w_mix_out, norm_ffn2_g=norm_ffn2_g, ffn2_w_in=ffn2_w_in, ffn2_w_out=ffn2_w_out,
                   norm_final_g=norm_final_g)
    mom1 = dict(w_ada=m_w_ada, b_ada=m_b_ada, norm_ffn1_g=m_norm_ffn1_g, ffn1_w_in=m_ffn1_w_in,
                ffn1_w_out=m_ffn1_w_out, norm_mix_g=m_norm_mix_g, w_mix_in=m_w_mix_in, w_pool=m_w_pool,
                pool_scale=m_pool_scale, gmlp_ln_g=m_gmlp_ln_g, gmlp_ln_b=m_gmlp_ln_b, w_spatial=m_w_spatial,
                b_spatial=m_b_spatial, w_mix_out=m_w_mix_out, norm_ffn2_g=m_norm_ffn2_g, ffn2_w_in=m_ffn2_w_in,
                ffn2_w_out=m_ffn2_w_out, norm_final_g=m_norm_final_g)
    mom2 = dict(w_ada=v_w_ada, b_ada=v_b_ada, norm_ffn1_g=v_norm_ffn1_g, ffn1_w_in=v_ffn1_w_in,
                ffn1_w_out=v_ffn1_w_out, norm_mix_g=v_norm_mix_g, w_mix_in=v_w_mix_in, w_pool=v_w_pool,
                pool_scale=v_pool_scale, gmlp_ln_g=v_gmlp_ln_g, gmlp_ln_b=v_gmlp_ln_b, w_spatial=v_w_spatial,
                b_spatial=v_b_spatial, w_mix_out=v_w_mix_out, norm_ffn2_g=v_norm_ffn2_g, ffn2_w_in=v_ffn2_w_in,
                ffn2_w_out=v_ffn2_w_out, norm_final_g=v_norm_final_g)
    order = list(weights)
    xs = x[0]
    target = loss_target[0]
    transposed = ("ffn1_w_in", "w_mix_in", "ffn2_w_in")
    big = ("ffn1_w_in", "ffn1_w_out", "w_mix_in", "w_mix_out", "ffn2_w_in", "ffn2_w_out")
    local = lambda a, k: a[0].T if k in transposed else a[0]
    wc = w_ada.shape[2]

    shard = dict(zip(big, _cast_shards([local(weights[k], k) for k in big])))
    modp, cact_all, g_w1_in, g_w1_out = _ada_forward(jnp.broadcast_to(c, (8, D)), w_ada[0], b_ada.reshape(NDEV, wc),
                                                     [shard["ffn1_w_in"], shard["ffn1_w_out"]])
    mod = modp.reshape(9, D)
    w1_in = g_w1_in.reshape(2, F, D)
    w1_out = g_w1_out.reshape(F, D)

    x1, gu1, h1, g_wmi, g_wmo, g_w2_out, g_w2_in = _ffn_fwd(
        xs, mod, norm_ffn1_g, w1_in, w1_out, 0, "ffn1_fwd",
        gather=[shard["w_mix_in"], shard["w_mix_out"], shard["ffn2_w_out"], shard["ffn2_w_in"]])
    wmi = g_wmi.reshape(DPROJ, D)
    wmo = g_wmo.reshape(D, D)
    w2_in = g_w2_in.reshape(2, F, D)
    w2_out = g_w2_out.reshape(F, D)
    tril = jnp.tril(jnp.ones((CHUNK, CHUNK), dtype=bool))
    ws_b = jnp.where(tril[None], w_spatial[0], 0.0).astype(BF)
    wp_b = w_pool[0].astype(BF)
    bias = jnp.repeat(b_spatial[0].T, DG // 8, axis=1)
    mix_args = (wmi, wmo, wp_b, pool_scale, gmlp_ln_g, gmlp_ln_b, ws_b, bias)
    x2, h3 = _mix_fwd(x1, mod, norm_mix_g, norm_ffn2_g, *mix_args, "mix_fwd")
    dx3, gu3, h3, st_f = _ffn_fwd(x2, mod, norm_ffn2_g, w2_in, w2_out, 2, "ffn2_fwd", h=h3,
                                  loss=(norm_final_g.reshape(1, D), target))

    slots = lambda a: a.reshape(NDEV, a.size // (NDEV * D), D)
    dgu3, d_w2_out, dgate3 = _ffn_bwd_hidden(dx3, mod, gu3, w2_out, 2, "ffn2_bwd_hidden")
    d_w2_in = _ffn_bwd_win(h3, dgu3, "ffn2_bwd_win")[0]
    dx2, st3 = _ffn_bwd_input(dgu3, w2_in, x2, dx3, mod, norm_ffn2_g, 2, "ffn2_bwd_input")
    dx1, d_wmi, d_wmo, d_wp, d_ws, st2, vec2, dbias, r_w2_in, r_w2_out = _mix_bwd(
        x1, dx2, mod, norm_mix_g, *mix_args, "mix_bwd", exchange=[slots(d_w2_in), slots(d_w2_out)])
    dgu1, d_w1_out, dgate1, r_wmi, r_wmo = _ffn_bwd_hidden(
        dx1, mod, gu1, w1_out, 0, "ffn1_bwd_hidden", exchange=[slots(d_wmi), slots(d_wmo)])
    d_w1_in, r_w1_out = _ffn_bwd_win(h1, dgu1, "ffn1_bwd_win", exchange=[slots(d_w1_out)])
    dx0, st1, r_w1_in = _ffn_bwd_input(dgu1, w1_in, xs, dx1, mod, norm_ffn1_g, 0, "ffn1_bwd_input",
                                       exchange=[slots(d_w1_in)])

    received = dict(ffn1_w_in=r_w1_in, ffn1_w_out=r_w1_out, w_mix_in=r_wmi, w_mix_out=r_wmo,
                    ffn2_w_in=r_w2_in, ffn2_w_out=r_w2_out)
    tiles = dict(ffn1_w_in=176, ffn1_w_out=176, w_mix_in=96, w_mix_out=128, ffn2_w_in=176, ffn2_w_out=176)
    result = {}
    for k, recv in received.items():
        res = _sum_adamw(recv, local(weights[k], k), local(mom1[k], k), local(mom2[k], k), tiles[k], "update_" + k)
        result[k] = tuple((a.T if k in transposed else a)[None] for a in res)
    row = lambda a: a.reshape(1, D)
    params = {k: (weights[k], mom1[k], mom2[k]) for k in SMALL}
    params["norm_final_g"] = (row(norm_final_g), row(m_norm_final_g), row(v_norm_final_g))
    tot, rsum, dmine = _small_reduce(d_ws, d_wp, dbias, st1, st2, st3, st_f, vec2, dgate1, dgate3)
    small, loss_row = _small_update(tot, rsum, params)
    result.update(small)
    result["norm_final_g"] = tuple(a.reshape(D) for a in small["norm_final_g"])
    result["w_ada"] = tuple(a[None] for a in _ada_update(cact_all, dmine, w_ada[0], m_w_ada[0], v_w_ada[0], 256))

    return (loss_row[0, 0], dx0[None], *[result[k][0] for k in order], *[result[k][1] for k in order],
            *[result[k][2] for k in order], *[result[k][3] for k in order])
```

```python
import math

import jax
import jax.numpy as jnp
from jax import lax
from jax.experimental import pallas as pl
from jax.experimental.pallas import tpu as pltpu

D = 1024
F = 2816
DP = 512
DG = 512
DPROJ = DP + 2 * DG
CHUNK = 128
WINDOWS = (2, 4, 8, 16)
HALO = 16
NDEV = 8
T_FFN = 512
T_MIX = 256
T_MIX_FWD = 512
T_WIN = 2048
EPS = 1e-6
LR, B1, B2, AEPS, WD, STEP = 0.001, 0.9, 0.999, 1e-08, 0.01, 10
BC1 = 1.0 - B1 ** STEP
BC2 = 1.0 - B2 ** STEP
GELU_C = math.sqrt(2.0 / math.pi)
GELU_A = 0.044715

BF = jnp.bfloat16
F32 = jnp.float32
MESH = pl.DeviceIdType.MESH
HBM = pl.BlockSpec(memory_space=pltpu.HBM)


def _whole(a):
    return pl.BlockSpec(a.shape, lambda i: (0,) * len(a.shape))


NT = (((1,), (1,)), ((), ()))
TN = (((0,), (0,)), ((), ()))


def _dot(a, b):
    return jnp.dot(a, b, preferred_element_type=F32)


def _dot_nt(a, b):
    return lax.dot_general(a, b, NT, preferred_element_type=F32)


def _dot_tn(a, b):
    return lax.dot_general(a, b, TN, preferred_element_type=F32)


def _cparams(vmem_mb, sem=None):
    kw = dict(vmem_limit_bytes=vmem_mb * 1024 * 1024)
    if sem is not None:
        kw["dimension_semantics"] = sem
    return pltpu.CompilerParams(**kw)


def _position():
    return lax.axis_index("x"), lax.axis_index("y"), lax.axis_index("c")


def _slot(p):
    return 4 * p[0] + 2 * p[1] + p[2]


def _flip(me, d):
    x, y, c = me
    return (1 - x if d & 4 else x, 1 - y if d & 2 else y, 1 - c if d & 1 else c)


def _remote(src, dst, send_sem, recv_sem, to):
    return pltpu.make_async_remote_copy(src_ref=src, dst_ref=dst, send_sem=send_sem, recv_sem=recv_sem,
                                        device_id=to, device_id_type=MESH)


def _comm_sems(n):
    return [pltpu.SemaphoreType.DMA((n, 7)), pltpu.SemaphoreType.DMA((n, 7)), pltpu.SemaphoreType.DMA((n,))]


def _gather_phase(phase, xs, outs, sems):
    send_sems, recv_sems, local_sems = sems
    n = len(xs)
    me = _position()
    x, y, c = me
    sibling = (x, y, 1 - c)
    xn, yn, diag = (1 - x, y), (x, 1 - y), (1 - x, 1 - y)
    relay_from = (x + c * (1 - 2 * x), y + (1 - c) * (1 - 2 * y))
    relay_to = (x + (1 - c) * (1 - 2 * x), y + c * (1 - 2 * y))

    def copy(a, k, block, to, src=None):
        dst = outs[a].at[_slot(block)]
        return _remote(dst if src is None else src, dst, send_sems.at[a, k], recv_sems.at[a, k], to)

    def mine(a):
        return pltpu.make_async_copy(xs[a], outs[a].at[_slot(me)], local_sems.at[a])

    def first(a):
        return [copy(a, 0, me, sibling, src=xs[a]), copy(a, 1, me, (*xn, c), src=xs[a]),
                copy(a, 2, me, (*yn, c), src=xs[a])]

    def second(a):
        return [copy(a, 3, (*relay_from, c), (*relay_to, c)), copy(a, 4, (*xn, c), sibling),
                copy(a, 5, (*yn, c), sibling)]

    def third(a):
        return copy(a, 6, (*diag, c), sibling)

    if phase == "start":
        for a in range(n):
            mine(a).start()
            for cp in first(a):
                cp.start()
    elif phase == "forward":
        for a in range(n):
            copy(a, 1, (*xn, c), me).wait_recv()
            copy(a, 2, (*yn, c), me).wait_recv()
            for cp in second(a):
                cp.start()
    else:
        for a in range(n):
            copy(a, 3, (*diag, c), me).wait_recv()
            third(a).start()
        for a in range(n):
            copy(a, 0, sibling, me).wait_recv()
            for k, chip in ((4, xn), (5, yn), (6, diag)):
                copy(a, k, (*chip, 1 - c), me).wait_recv()
        for a in range(n):
            for cp in first(a) + second(a) + [third(a)]:
                cp.wait_send()
            mine(a).wait()


def _exchange_phase(phase, xs, outs, sems):
    send_sems, recv_sems, local_sems = sems
    me = _position()
    for a in range(len(xs)):
        copies = [pltpu.make_async_copy(xs[a].at[_slot(me)], outs[a].at[_slot(me)], local_sems.at[a])]
        for d in range(1, NDEV):
            to = _flip(me, d)
            copies.append(_remote(xs[a].at[_slot(to)], outs[a].at[_slot(me)],
                                  send_sems.at[a, d - 1], recv_sems.at[a, d - 1], to))
        for cp in copies:
            if phase == "start":
                cp.start()
            else:
                cp.wait()


def _like(bufs):
    return [jax.ShapeDtypeStruct(b.shape, b.dtype) for b in bufs]


def _rms_mod(x, gn, shift, scale):
    ms = jnp.mean(x * x, axis=-1, keepdims=True)
    r = lax.rsqrt(ms + EPS)
    xhat = x * r
    n = xhat * gn
    h = n * (1.0 + scale) + shift
    return r, xhat, n, h


def _rms_mod_bwd(dh, dres, r, xhat, n, gn, scale):
    dshift = jnp.sum(dh, axis=0, keepdims=True)
    dscale = jnp.sum(dh * n, axis=0, keepdims=True)
    dn = dh * (1.0 + scale)
    dgn = jnp.sum(dn * xhat, axis=0, keepdims=True)
    dxhat = dn * gn
    dx = dres + r * (dxhat - xhat * jnp.mean(dxhat * xhat, axis=-1, keepdims=True))
    return dx, dshift, dscale, dgn


def _final_norm_loss(x, gf, target):
    r = lax.rsqrt(jnp.mean(x * x, axis=-1, keepdims=True) + EPS)
    xhat = x * r
    e = xhat * gf - target
    part = 0.5 * jnp.sum(jnp.sum(e * e, axis=-1, keepdims=True), axis=0, keepdims=True) / D
    dy = e / D
    dgf = jnp.sum(dy * xhat, axis=0, keepdims=True)
    dxhat = dy * gf
    dx = r * (dxhat - xhat * jnp.mean(dxhat * xhat, axis=-1, keepdims=True))
    return dx, dgf, part


def _rows3(a, b, c, width):
    row = lax.broadcasted_iota(jnp.int32, (8, width), 0)
    z = jnp.zeros((8, width), F32)
    return jnp.where(row == 0, a, z) + jnp.where(row == 1, b, z) + jnp.where(row == 2, c, z)


def _sigmoid(x):
    return 0.5 * jnp.tanh(0.5 * x) + 0.5


def _gelu(x):
    t = jnp.tanh(GELU_C * (x + GELU_A * x * x * x))
    return 0.5 * x * (1.0 + t), t


def _gelu_grad(x, t):
    return 0.5 * (1.0 + t) + 0.5 * x * (1.0 - t * t) * GELU_C * (1.0 + 3.0 * GELU_A * x * x)


def _adamw(w, g, m, v):
    m = B1 * m + (1.0 - B1) * g
    v = B2 * v + (1.0 - B2) * (g * g)
    m_hat = m / BC1
    v_hat = v / BC2
    delta = -LR * (m_hat / (jnp.sqrt(v_hat) + AEPS) + WD * w)
    return delta, m, v


def _cast_shards(shards):
    n = len(shards)

    def body(*refs):
        for a in range(n):
            refs[n + a][...] = refs[a][...].astype(BF)

    resident = pl.BlockSpec(memory_space=pltpu.VMEM)
    return pl.pallas_call(
        body, name="cast_shards", out_shape=[jax.ShapeDtypeStruct(s.shape, BF) for s in shards],
        in_specs=[resident] * n, out_specs=[resident] * n, compiler_params=_cparams(40),
    )(*shards)


def _ada_forward(c8, w_ada, b8, shards):
    wc = w_ada.shape[1]
    n = len(shards)

    def body(*refs):
        c8_ref, w_ref, b8_ref = refs[:3]
        xs = refs[3:3 + n]
        mod_ref, cact_ref = refs[3 + n:5 + n]
        gathered = refs[5 + n:5 + 2 * n]
        call_ref, mall_ref, send_sems, recv_sems = refs[5 + 2 * n:9 + 2 * n]
        gsems = refs[9 + 2 * n:]
        me = _position()
        my = _slot(me)
        row = lax.broadcasted_iota(jnp.int32, (8, 1), 0)
        call_ref[my] = c8_ref[...]
        sends = []
        for d in range(1, NDEV):
            to = _flip(me, d)
            sends.append(_remote(call_ref.at[my], call_ref.at[my], send_sems.at[0, d - 1], recv_sems.at[0, d - 1], to))
        for cp in sends:
            cp.start()
        _gather_phase("start", xs, gathered, gsems)
        for cp in sends:
            cp.wait()
        c_all = jnp.zeros((8, D), F32)
        for k in range(NDEV):
            c_all = c_all + jnp.where(row == k, call_ref[k], 0.0)
        cact = c_all * jax.nn.sigmoid(c_all)
        cact_ref[...] = cact
        part = _dot(cact.astype(BF), w_ref[...].astype(BF))
        mall_ref[my] = part
        sends = []
        for d in range(1, NDEV):
            to = _flip(me, d)
            sends.append(_remote(mall_ref.at[my], mall_ref.at[my], send_sems.at[1, d - 1], recv_sems.at[1, d - 1], to))
        for cp in sends:
            cp.start()
        _gather_phase("forward", xs, gathered, gsems)
        for cp in sends:
            cp.wait()
        out = jnp.zeros((8, wc), F32)
        for k in range(NDEV):
            piece = jnp.sum(jnp.where(row == my, mall_ref[k], 0.0), axis=0, keepdims=True)
            out = out + jnp.where(row == k, piece, 0.0)
        mod_ref[...] = out + b8_ref[...]
        _gather_phase("finish", xs, gathered, gsems)

    outs = [jax.ShapeDtypeStruct((8, wc), F32), jax.ShapeDtypeStruct((8, D), F32)]
    return pl.pallas_call(
        body, name="ada_forward", grid=(1,),
        out_shape=outs + [jax.ShapeDtypeStruct((NDEV,) + s.shape, s.dtype) for s in shards],
        in_specs=[_whole(a) for a in (c8, w_ada, b8)] + [HBM] * n,
        out_specs=[_whole(a) for a in outs] + [HBM] * n,
        scratch_shapes=[pltpu.VMEM((NDEV, 8, D), F32), pltpu.VMEM((NDEV, 8, wc), F32),
                        pltpu.SemaphoreType.DMA((2, 7)), pltpu.SemaphoreType.DMA((2, 7))] + _comm_sems(n),
        compiler_params=_cparams(32, ("arbitrary",)),
    )(c8, w_ada, b8, *shards)


MATS = ("w_spatial", "w_pool", "b_spatial")
VECS = ("norm_ffn1_g", "norm_mix_g", "norm_ffn2_g", "norm_final_g", "pool_scale", "gmlp_ln_g", "gmlp_ln_b", "b_ada")
VEC_WIDTH = dict(norm_ffn1_g=D, norm_mix_g=D, norm_ffn2_g=D, norm_final_g=D, pool_scale=DP, gmlp_ln_g=DG,
                 gmlp_ln_b=DG, b_ada=9 * D)
MAT_ROWS = 1600
MAT_SLICE = MAT_ROWS // NDEV
VEC_LANES = sum(VEC_WIDTH.values()) + 128
DMOD_AT = VEC_LANES - 128 - 9 * D
SMALL = MATS + VECS


def _small_reduce(g_ws, g_wp, dbias, st1, st2, st3, st_f, vec2, dgate1, dgate3):
    wc = 9 * D // NDEV

    def body(g_ws_ref, g_wp_ref, dbias_ref, st1_ref, st2_ref, st3_ref, stf_ref, vec2_ref, dg1_ref, dg3_ref,
             tot_ref, rsum_ref, dmine_ref,
             pack_ref, rs_ref, ag_ref, rv_ref, dmp_ref, dw_ref, send_sems, recv_sems):
        me = _position()
        my = _slot(me)

        pack_ref[0:1024, :] = g_ws_ref[...].reshape(1024, 128)
        pack_ref[1024:1536, :] = g_wp_ref[...].reshape(512, 128)
        ch = lax.broadcasted_iota(jnp.int32, (DG, 128), 0)
        hd = lax.broadcasted_iota(jnp.int32, (DG, 128), 1)
        sel = jnp.where(ch // 64 == hd, 1.0, 0.0).astype(F32)
        heads = jnp.dot(dbias_ref[...], sel, preferred_element_type=F32, precision=lax.Precision.HIGHEST)
        pack_ref[1536:1544, :] = heads.T[0:8, :]
        pack_ref[1544:MAT_ROWS, :] = jnp.zeros((MAT_ROWS - 1544, 128), F32)
        dgate1 = dg1_ref[0:1, :] + dg1_ref[8:9, :]
        dgate3 = dg3_ref[0:1, :] + dg3_ref[8:9, :]
        row = jnp.concatenate(
            [st1_ref[2:3, :], st2_ref[2:3, :], st3_ref[2:3, :], stf_ref[0:1, :],
             vec2_ref[0:1, :], vec2_ref[1:2, :], vec2_ref[2:3, :],
             st1_ref[0:1, :], st1_ref[1:2, :], dgate1, st2_ref[0:1, :], st2_ref[1:2, :], st2_ref[3:4, :],
             st3_ref[0:1, :], st3_ref[1:2, :], dgate3, stf_ref[1:2, 0:128]], axis=1)
        rv_ref[my] = row
        for k in range(NDEV):
            dmp_ref[k] = row[:, DMOD_AT + wc * k:DMOD_AT + wc * (k + 1)]
        dw_ref[my] = dmp_ref[my]
        rs_ref[my] = pack_ref[pl.ds(pl.multiple_of(my * MAT_SLICE, 8), MAT_SLICE), :]

        first = []
        for d in range(1, NDEV):
            to = _flip(me, d)
            theirs = pl.ds(pl.multiple_of(_slot(to) * MAT_SLICE, 8), MAT_SLICE)
            first.append(_remote(pack_ref.at[theirs, :], rs_ref.at[my], send_sems.at[0, d - 1], recv_sems.at[0, d - 1], to))
            first.append(_remote(dmp_ref.at[_slot(to)], dw_ref.at[my], send_sems.at[1, d - 1], recv_sems.at[1, d - 1], to))
            first.append(_remote(rv_ref.at[my], rv_ref.at[my], send_sems.at[2, d - 1], recv_sems.at[2, d - 1], to))
        for cp in first:
            cp.start()
        for cp in first:
            cp.wait()
        red = rs_ref[0]
        for k in range(1, NDEV):
            red = red + rs_ref[k]
        ag_ref[my] = red
        second = []
        for d in range(1, NDEV):
            to = _flip(me, d)
            second.append(_remote(ag_ref.at[my], ag_ref.at[my], send_sems.at[3, d - 1], recv_sems.at[3, d - 1], to))
        for cp in second:
            cp.start()

        rsum = rv_ref[0]
        for k in range(1, NDEV):
            rsum = rsum + rv_ref[k]
        rsum_ref[...] = rsum
        r8 = lax.broadcasted_iota(jnp.int32, (8, 1), 0)
        dmine = jnp.zeros((8, wc), F32)
        for k in range(NDEV):
            dmine = dmine + jnp.where(r8 == k, dw_ref[k], 0.0)
        dmine_ref[...] = dmine

        for cp in second:
            cp.wait()
        for k in range(NDEV):
            tot_ref[k * MAT_SLICE:(k + 1) * MAT_SLICE, :] = ag_ref[k]

    ins = (g_ws, g_wp, dbias, st1, st2, st3, st_f, vec2, dgate1, dgate3)
    outs = [jax.ShapeDtypeStruct((MAT_ROWS, 128), F32), jax.ShapeDtypeStruct((1, VEC_LANES), F32),
            jax.ShapeDtypeStruct((8, wc), F32)]
    return pl.pallas_call(
        body, name="small_reduce", grid=(1,), out_shape=outs,
        in_specs=[_whole(a) for a in ins], out_specs=[_whole(a) for a in outs],
        scratch_shapes=[pltpu.VMEM((MAT_ROWS, 128), F32), pltpu.VMEM((NDEV, MAT_SLICE, 128), F32),
                        pltpu.VMEM((NDEV, MAT_SLICE, 128), F32),
                        pltpu.VMEM((NDEV, 1, VEC_LANES), F32), pltpu.VMEM((NDEV, 1, wc), F32),
                        pltpu.VMEM((NDEV, 1, wc), F32),
                        pltpu.SemaphoreType.DMA((4, 7)), pltpu.SemaphoreType.DMA((4, 7))],
        compiler_params=_cparams(32, ("arbitrary",)),
    )(*ins)


def _small_update(tot, rsum, params):
    flat = [a for k in SMALL for a in params[k]]
    n_in = 2 + len(flat)

    def body(*refs):
        tot_ref, rsum_ref = refs[:2]
        p_hbm = refs[2:n_in]
        o_refs = refs[n_in:n_in + 4 * len(SMALL)]
        loss_ref = refs[n_in + 4 * len(SMALL)]
        p_refs = refs[n_in + 4 * len(SMALL) + 1:-1]
        sem = refs[-1]
        fetch = [pltpu.make_async_copy(p_hbm[k], p_refs[k], sem.at[k]) for k in range(len(flat))]
        for cp in fetch:
            cp.start()
        for cp in fetch:
            cp.wait()
        loss_ref[...] = rsum_ref[:, VEC_LANES - 128:VEC_LANES]

        def update(idx, g):
            w_ref, m_ref, v_ref = p_refs[3 * idx:3 * idx + 3]
            g_out, d_out, m_out, v_out = o_refs[4 * idx:4 * idx + 4]
            g = g.reshape(w_ref.shape)
            g_out[...] = g
            d_out[...], m_out[...], v_out[...] = _adamw(w_ref[...], g, m_ref[...], v_ref[...])

        update(0, tot_ref[0:1024, :])
        update(1, tot_ref[1024:1536, :])
        update(2, tot_ref[1536:1544, :])
        at = 0
        for idx, k in enumerate(VECS):
            update(3 + idx, rsum_ref[:, at:at + VEC_WIDTH[k]])
            at += VEC_WIDTH[k]

    outs = []
    for k in SMALL:
        outs += [jax.ShapeDtypeStruct(params[k][0].shape, F32)] * 4
    outs += [jax.ShapeDtypeStruct((1, 128), F32)]
    res = pl.pallas_call(
        body, name="small_update", grid=(1,), out_shape=outs,
        in_specs=[_whole(tot), _whole(rsum)] + [HBM] * len(flat), out_specs=[_whole(a) for a in outs],
        scratch_shapes=[pltpu.VMEM(a.shape, F32) for a in flat] + [pltpu.SemaphoreType.DMA((len(flat),))],
        compiler_params=_cparams(32, ("arbitrary",)),
    )(tot, rsum, *[pltpu.with_memory_space_constraint(a, pltpu.HBM) for a in flat])
    return {k: tuple(res[4 * i:4 * i + 4]) for i, k in enumerate(SMALL)}, res[-1]


def _sum_adamw(recv, w, m, v, tr, name):
    R, C = w.shape

    def body(r_ref, w_ref, m_ref, v_ref, g_ref, d_ref, nm_ref, nv_ref):
        g = r_ref[0].astype(F32)
        for k in range(1, NDEV):
            g = g + r_ref[k].astype(F32)
        g_ref[...] = g
        d_ref[...], nm_ref[...], nv_ref[...] = _adamw(w_ref[...], g, m_ref[...], v_ref[...])

    blk = pl.BlockSpec((tr, C), lambda i: (i, 0))
    out = jax.ShapeDtypeStruct((R, C), F32)
    return pl.pallas_call(
        body, name=name, grid=(R // tr,), out_shape=[out] * 4,
        in_specs=[pl.BlockSpec((NDEV, tr, C), lambda i: (0, i, 0)), blk, blk, blk], out_specs=[blk] * 4,
        compiler_params=_cparams(48, ("arbitrary",)),
    )(recv, *[pltpu.with_memory_space_constraint(a, pltpu.HBM) for a in (w, m, v)])


def _ada_update(cact_all, dmine, w, m, v, tr):
    R, C = w.shape

    def body(c_ref, dm_ref, w_ref, m_ref, v_ref, g_ref, d_ref, nm_ref, nv_ref):
        g = _dot_tn(c_ref[...].astype(BF), dm_ref[...].astype(BF))
        g_ref[...] = g
        d_ref[...], nm_ref[...], nv_ref[...] = _adamw(w_ref[...], g, m_ref[...], v_ref[...])

    blk = pl.BlockSpec((tr, C), lambda i: (i, 0))
    out = jax.ShapeDtypeStruct((R, C), F32)
    return pl.pallas_call(
        body, name="update_w_ada", grid=(R // tr,), out_shape=[out] * 4,
        in_specs=[pl.BlockSpec((8, tr), lambda i: (0, i)), pl.BlockSpec((8, C), lambda i: (0, 0)), blk, blk, blk],
        out_specs=[blk] * 4,
        compiler_params=_cparams(48, ("arbitrary",)),
    )(cact_all, dmine, *[pltpu.with_memory_space_constraint(a, pltpu.HBM) for a in (w, m, v)])


FC = F // 2


def _ffn_fwd(x, mod, gn, w_in_t, w_out, sub, name, gather=(), loss=None, h=None):
    S = x.shape[0]
    T = min(T_FFN, S)
    nS, nJ = S // T, F // FC
    ng = len(gather)
    nl = 2 if loss else 0
    nh = 0 if h is None else 1
    forward_step = nS // 2

    def body(*refs):
        x_ref, mod_ref, gn_ref, wg_ref, wu_ref, wo_ref = refs[:6]
        gf_ref, t_ref = refs[6 + nh:6 + nh + nl] if loss else (None, None)
        shards = refs[6 + nh + nl:6 + nh + nl + ng]
        at = 6 + nh + nl + ng
        xo_ref, gu_ref = refs[at:at + 2]
        h_ref = refs[6] if nh else refs[at + 2]
        at += 3 - nh
        gathered = refs[at:at + ng]
        at += ng
        st_ref = refs[at] if loss else None
        at += nl // 2
        acc_scr = refs[at]
        sems = refs[at + 1:]
        i, j = pl.program_id(0), pl.program_id(1)

        if ng:
            @pl.when((i == 0) & (j == 0))
            def _():
                _gather_phase("start", shards, gathered, sems)

            @pl.when((i == forward_step) & (j == 0))
            def _():
                _gather_phase("forward", shards, gathered, sems)

        @pl.when(j == 0)
        def _():
            if not nh:
                _, _, _, hh = _rms_mod(x_ref[...], gn_ref[...], mod_ref[3 * sub:3 * sub + 1, :],
                                       mod_ref[3 * sub + 1:3 * sub + 2, :])
                h_ref[...] = hh.astype(BF)
            acc_scr[...] = jnp.zeros_like(acc_scr)

        hb = h_ref[...]
        g = _dot_nt(hb, wg_ref[0])
        u = _dot_nt(hb, wu_ref[0])
        gu_ref[0] = g.astype(BF)
        gu_ref[1] = u.astype(BF)
        a = (g * _sigmoid(g) * u).astype(BF)
        acc_scr[...] += _dot(a, wo_ref[...])

        @pl.when(j == nJ - 1)
        def _():
            xo = x_ref[...] + (0.5 * mod_ref[3 * sub + 2:3 * sub + 3, :]) * acc_scr[...]
            if not loss:
                xo_ref[...] = xo
            else:
                dx, dgf, part = _final_norm_loss(xo, gf_ref[...], t_ref[...])
                xo_ref[...] = dx
                upd = _rows3(dgf, jnp.broadcast_to(part, (1, D)), jnp.zeros((1, D), F32), D)

                @pl.when(i == 0)
                def _():
                    st_ref[...] = upd

                @pl.when(i > 0)
                def _():
                    st_ref[...] += upd

        if ng:
            @pl.when((i == nS - 1) & (j == nJ - 1))
            def _():
                _gather_phase("finish", shards, gathered, sems)

    tile = pl.BlockSpec((T, D), lambda i, j: (i, 0))
    res = pl.pallas_call(
        body, name=name, grid=(nS, nJ),
        out_shape=[jax.ShapeDtypeStruct((S, D), F32), jax.ShapeDtypeStruct((2, S, F), BF)]
                  + ([] if nh else [jax.ShapeDtypeStruct((S, D), BF)])
                  + [jax.ShapeDtypeStruct((NDEV,) + s.shape, s.dtype) for s in gather]
                  + ([jax.ShapeDtypeStruct((8, D), F32)] if loss else []),
        in_specs=[tile,
                  pl.BlockSpec((9, D), lambda i, j: (0, 0)),
                  pl.BlockSpec((1, D), lambda i, j: (0, 0)),
                  pl.BlockSpec((1, FC, D), lambda i, j: (0, j, 0)),
                  pl.BlockSpec((1, FC, D), lambda i, j: (1, j, 0)),
                  pl.BlockSpec((FC, D), lambda i, j: (j, 0))] + [tile] * nh
                 + ([pl.BlockSpec((1, D), lambda i, j: (0, 0)), tile] if loss else []) + [HBM] * ng,
        out_specs=[tile, pl.BlockSpec((2, T, FC), lambda i, j: (0, i, j))] + [tile] * (1 - nh) + [HBM] * ng
                  + ([pl.BlockSpec((8, D), lambda i, j: (0, 0))] if loss else []),
        scratch_shapes=[pltpu.VMEM((T, D), F32)] + (_comm_sems(ng) if ng else []),
        compiler_params=_cparams(56, ("arbitrary", "arbitrary")),
    )(x, mod, gn, w_in_t, w_in_t, w_out, *(() if h is None else (h,)), *(loss or ()), *gather)
    return res if h is None else [res[0], res[1], h, *res[2:]]


def _ffn_bwd_hidden(dx, mod, gu, w_out, sub, name, exchange=()):
    S = dx.shape[0]
    T = min(T_FFN, S)
    nS, nJ = S // T, F // FC
    ne = len(exchange)

    def body(*refs):
        dx_ref, mod_ref, gu_ref, wo_ref = refs[:4]
        sendbufs = refs[4:4 + ne]
        dgu_ref, gw_ref, dgate_ref = refs[4 + ne:7 + ne]
        recvbufs = refs[7 + ne:7 + 2 * ne]
        acc_scr = refs[7 + 2 * ne]
        sems = refs[8 + 2 * ne:]
        j, i = pl.program_id(0), pl.program_id(1)

        if ne:
            @pl.when((i == 0) & (j == 0))
            def _():
                _exchange_phase("start", sendbufs, recvbufs, sems)

        gate = mod_ref[3 * sub + 2:3 * sub + 3, :]
        dx = dx_ref[...]
        da = _dot_nt((dx * (0.5 * gate)).astype(BF), wo_ref[...])
        g = gu_ref[0].astype(F32)
        u = gu_ref[1].astype(F32)
        sg = _sigmoid(g)
        s = g * sg
        dgu_ref[0] = (da * u * (sg * (1.0 + g * (1.0 - sg)))).astype(BF)
        dgu_ref[1] = (da * s).astype(BF)
        contrib = _dot_tn((s * u).astype(BF), dx.astype(BF))

        @pl.when(i == 0)
        def _():
            acc_scr[...] = contrib

        @pl.when(i > 0)
        def _():
            acc_scr[...] += contrib

        @pl.when(i == nS - 1)
        def _():
            acc = acc_scr[...]
            dgate = 0.5 * jnp.sum(acc * wo_ref[...].astype(F32), axis=0, keepdims=True)
            dgate_ref[...] = jnp.broadcast_to(dgate, (8, D))
            gw_ref[...] = (acc * (0.5 * gate)).astype(BF)

        if ne:
            @pl.when((i == nS - 1) & (j == nJ - 1))
            def _():
                _exchange_phase("wait", sendbufs, recvbufs, sems)

    return pl.pallas_call(
        body, name=name, grid=(nJ, nS),
        out_shape=[jax.ShapeDtypeStruct((2, S, F), BF), jax.ShapeDtypeStruct((F, D), BF),
                   jax.ShapeDtypeStruct((8 * nJ, D), F32)] + _like(exchange),
        in_specs=[pl.BlockSpec((T, D), lambda j, i: (i, 0)),
                  pl.BlockSpec((9, D), lambda j, i: (0, 0)),
                  pl.BlockSpec((2, T, FC), lambda j, i: (0, i, j)),
                  pl.BlockSpec((FC, D), lambda j, i: (j, 0))] + [HBM] * ne,
        out_specs=[pl.BlockSpec((2, T, FC), lambda j, i: (0, i, j)),
                   pl.BlockSpec((FC, D), lambda j, i: (j, 0)),
                   pl.BlockSpec((8, D), lambda j, i: (j, 0))] + [HBM] * ne,
        scratch_shapes=[pltpu.VMEM((FC, D), F32)] + (_comm_sems(ne) if ne else []),
        compiler_params=_cparams(56, ("arbitrary", "arbitrary")),
    )(dx, mod, gu, w_out, *exchange)


def _ffn_bwd_input(dgu, w_in_t, x, dx, mod, gn, sub, name, exchange=()):
    S = x.shape[0]
    T = min(T_FFN, S)
    nS = S // T
    ne = len(exchange)
    NC = 256
    chunks = [slice(k * NC, (k + 1) * NC) for k in range(D // NC)]

    def body(*refs):
        dgu_ref, w_ref, x_ref, dx_ref, mod_ref, gn_ref = refs[:6]
        sendbufs = refs[6:6 + ne]
        dxin_ref, st_ref = refs[6 + ne:8 + ne]
        recvbufs = refs[8 + ne:8 + 2 * ne]
        dxh_scr = refs[8 + 2 * ne]
        sems = refs[9 + 2 * ne:]
        i = pl.program_id(0)

        if ne:
            @pl.when(i == 0)
            def _():
                _exchange_phase("start", sendbufs, recvbufs, sems)

        gn = gn_ref[...]
        scale = mod_ref[3 * sub + 1:3 * sub + 2, :]
        r, xhat, n, _ = _rms_mod(x_ref[...], gn, mod_ref[3 * sub:3 * sub + 1, :], scale)
        dg = dgu_ref[0]
        du = dgu_ref[1]
        rowsum = jnp.zeros((T, 1), F32)
        dshift, dscale, dgn = [], [], []
        for cols in chunks:
            dh = _dot(dg, w_ref[0, :, cols]) + _dot(du, w_ref[1, :, cols])
            dshift.append(jnp.sum(dh, axis=0, keepdims=True))
            dscale.append(jnp.sum(dh * n[:, cols], axis=0, keepdims=True))
            dn = dh * (1.0 + scale[:, cols])
            dgn.append(jnp.sum(dn * xhat[:, cols], axis=0, keepdims=True))
            dxhat = dn * gn[:, cols]
            rowsum = rowsum + jnp.sum(dxhat * xhat[:, cols], axis=-1, keepdims=True)
            dxh_scr[:, cols] = dxhat
        dxin_ref[...] = dx_ref[...] + r * (dxh_scr[...] - xhat * (rowsum / D))
        cat = lambda parts: jnp.concatenate(parts, axis=1)
        upd = _rows3(cat(dshift), cat(dscale), cat(dgn), D)

        @pl.when(i == 0)
        def _():
            st_ref[...] = upd

        @pl.when(i > 0)
        def _():
            st_ref[...] += upd

        if ne:
            @pl.when(i == nS - 1)
            def _():
                _exchange_phase("wait", sendbufs, recvbufs, sems)

    tile = pl.BlockSpec((T, D), lambda i: (i, 0))
    return pl.pallas_call(
        body, name=name, grid=(nS,),
        out_shape=[jax.ShapeDtypeStruct((S, D), F32), jax.ShapeDtypeStruct((8, D), F32)] + _like(exchange),
        in_specs=[pl.BlockSpec((2, T, F), lambda i: (0, i, 0)),
                  pl.BlockSpec((2, F, D), lambda i: (0, 0, 0), pipeline_mode=pl.Buffered(1)),
                  tile, tile,
                  pl.BlockSpec((9, D), lambda i: (0, 0)),
                  pl.BlockSpec((1, D), lambda i: (0, 0))] + [HBM] * ne,
        out_specs=[tile, pl.BlockSpec((8, D), lambda i: (0, 0))] + [HBM] * ne,
        scratch_shapes=[pltpu.VMEM((T, D), F32)] + (_comm_sems(ne) if ne else []),
        compiler_params=_cparams(60, ("arbitrary",)),
    )(dgu, w_in_t, x, dx, mod, gn, *exchange)


def _ffn_bwd_win(h, dgu, name, exchange=()):
    S = h.shape[0]
    T = min(T_WIN, S)
    nS, nJ = S // T, F // FC
    ne = len(exchange)

    def body(*refs):
        h_ref, dgu_ref = refs[:2]
        sendbufs = refs[2:2 + ne]
        out_ref = refs[2 + ne]
        recvbufs = refs[3 + ne:3 + 2 * ne]
        acc_scr = refs[3 + 2 * ne]
        sems = refs[4 + 2 * ne:]
        p, j, i = pl.program_id(0), pl.program_id(1), pl.program_id(2)

        if ne:
            @pl.when((p == 0) & (j == 0) & (i == 0))
            def _():
                _exchange_phase("start", sendbufs, recvbufs, sems)

        contrib = _dot_tn(dgu_ref[0], h_ref[...])

        @pl.when(i == 0)
        def _():
            acc_scr[...] = contrib

        @pl.when(i > 0)
        def _():
            acc_scr[...] += contrib

        @pl.when(i == nS - 1)
        def _():
            out_ref[0] = acc_scr[...].astype(BF)

        if ne:
            @pl.when((p == 1) & (j == nJ - 1) & (i == nS - 1))
            def _():
                _exchange_phase("wait", sendbufs, recvbufs, sems)

    return pl.pallas_call(
        body, name=name, grid=(2, nJ, nS),
        out_shape=[jax.ShapeDtypeStruct((2, F, D), BF)] + _like(exchange),
        in_specs=[pl.BlockSpec((T, D), lambda p, j, i: (i, 0)),
                  pl.BlockSpec((1, T, FC), lambda p, j, i: (p, i, j))] + [HBM] * ne,
        out_specs=[pl.BlockSpec((1, FC, D), lambda p, j, i: (p, j, 0))] + [HBM] * ne,
        scratch_shapes=[pltpu.VMEM((FC, D), F32)] + (_comm_sems(ne) if ne else []),
        compiler_params=_cparams(56, ("arbitrary", "arbitrary", "arbitrary")),
    )(h, dgu, *exchange)


def _pool_counts(pos0, T):
    pos = pos0 + lax.broadcasted_iota(jnp.int32, (T, 1), 0)
    return [jnp.minimum(pos + 1, w).astype(F32) for w in WINDOWS]


def _pool_fwd(xa, halo, ext_scr, cnts, T):
    ext_scr[0:HALO, :] = halo
    ext_scr[HALO:HALO + T, :] = xa
    out = []
    for gi, w in enumerate(WINDOWS):
        cols = slice(128 * gi, 128 * gi + 128)
        acc = xa[:, cols]
        for k in range(1, w):
            acc = acc + ext_scr[HALO - k:HALO - k + T, cols]
        out.append(acc / cnts[gi] - xa[:, cols])
    return out


def _sgu_fwd(vnb, ws_ref, sv_scr, T):
    lane = lax.broadcasted_iota(jnp.int32, (CHUNK, 128), 1)
    for n in range(T // CHUNK):
        rows = slice(n * CHUNK, (n + 1) * CHUNK)
        for b in range(DG // 128):
            cols = slice(128 * b, 128 * b + 128)
            vb = vnb[rows, cols]
            sv_scr[rows, cols] = jnp.where(lane < 64, _dot(ws_ref[2 * b], vb), _dot(ws_ref[2 * b + 1], vb))


def _mix_fwd(x, mod, gn, gn_next, wmi, wmo, wp, ps, lg, lb, ws, bias, name):
    S = x.shape[0]
    T = min(T_MIX_FWD, S)

    def body(x_ref, mod_ref, gn_ref, gnn_ref, wmi_ref, wmo_ref, wp_ref, ps_ref, lg_ref, lb_ref, ws_ref, bias_ref,
             xo_ref, hn_ref, carry_scr, ext_scr, sv_scr, ycat_scr):
        i = pl.program_id(0)

        @pl.when(i == 0)
        def _():
            carry_scr[...] = jnp.zeros_like(carry_scr)

        x = x_ref[...]
        _, _, _, h = _rms_mod(x, gn_ref[...], mod_ref[3:4, :], mod_ref[4:5, :])
        proj = _dot_nt(h.astype(BF), wmi_ref[...])
        xa = proj[:, 0:DP]
        p = _pool_fwd(xa, carry_scr[...], ext_scr, _pool_counts(i * T, T), T)
        carry_scr[...] = xa[T - HALO:T, :]
        for gi in range(4):
            cols = slice(128 * gi, 128 * gi + 128)
            ycat_scr[:, cols] = (_dot(p[gi].astype(BF), wp_ref[gi]) * ps_ref[:, cols]).astype(BF)
        u, _ = _gelu(proj[:, DP:DP + DG])
        v, _ = _gelu(proj[:, DP + DG:DPROJ])
        mu = jnp.mean(v, axis=-1, keepdims=True)
        vc = v - mu
        rstd = lax.rsqrt(jnp.mean(vc * vc, axis=-1, keepdims=True) + EPS)
        vn = vc * rstd * lg_ref[...] + lb_ref[...]
        _sgu_fwd(vn.astype(BF), ws_ref, sv_scr, T)
        for n in range(T // CHUNK):
            rows = slice(n * CHUNK, (n + 1) * CHUNK)
            ycat_scr[rows, DP:D] = (u[rows, :] * (sv_scr[rows, :] + bias_ref[...])).astype(BF)
        xo = x + mod_ref[5:6, :] * _dot(ycat_scr[...], wmo_ref[...])
        xo_ref[...] = xo
        _, _, _, hn = _rms_mod(xo, gnn_ref[...], mod_ref[6:7, :], mod_ref[7:8, :])
        hn_ref[...] = hn.astype(BF)

    full = lambda shape: pl.BlockSpec(shape, lambda i: (0,) * len(shape))
    tile = pl.BlockSpec((T, D), lambda i: (i, 0))
    return pl.pallas_call(
        body, name=name, grid=(S // T,),
        out_shape=[jax.ShapeDtypeStruct((S, D), F32), jax.ShapeDtypeStruct((S, D), BF)],
        in_specs=[tile, full((9, D)), full((1, D)), full((1, D)), full((DPROJ, D)), full((D, D)),
                  full((4, 128, 128)), full((1, DP)), full((1, DG)), full((1, DG)), full((8, CHUNK, CHUNK)),
                  full((CHUNK, DG))],
        out_specs=[tile, tile],
        scratch_shapes=[pltpu.VMEM((HALO, DP), F32), pltpu.VMEM((T + HALO, DP), F32), pltpu.VMEM((T, DG), F32),
                        pltpu.VMEM((T, D), BF)],
        compiler_params=_cparams(48, ("arbitrary",)),
    )(x, mod, gn, gn_next, wmi, wmo, wp, ps, lg, lb, ws, bias)


def _mix_bwd(x, dxo, mod, gn, wmi, wmo, wp, ps, lg, lb, ws, bias, name, exchange=()):
    S = x.shape[0]
    T = min(T_MIX, S)
    nS = S // T
    hb = T // HALO
    ne = len(exchange)

    def body(*refs):
        (x_ref, xh_ref, dxo_ref, mod_ref, gn_ref, wmi_ref, wmo_ref, wp_ref, ps_ref, lg_ref, lb_ref, ws_ref,
         bias_ref) = refs[:13]
        sendbufs = refs[13:13 + ne]
        dxi_ref, gwmi_out, gwmo_out, gwp_ref, gws_ref, st_ref, vec_ref, dbias_ref = refs[13 + ne:21 + ne]
        recvbufs = refs[21 + ne:21 + 2 * ne]
        (carry_scr, ext_scr, qext_scr, sv_scr, dvn_scr, ycat_scr, dproj_scr, gwmi_ref,
         gwmo_ref) = refs[21 + 2 * ne:30 + 2 * ne]
        sems = refs[30 + 2 * ne:]
        i = pl.program_id(0)
        t = nS - 1 - i
        gn = gn_ref[...]
        shift, scale, gate = mod_ref[3:4, :], mod_ref[4:5, :], mod_ref[5:6, :]

        @pl.when(i == 0)
        def _():
            if ne:
                _exchange_phase("start", sendbufs, recvbufs, sems)
            carry_scr[...] = jnp.zeros_like(carry_scr)
            gwmi_ref[...] = jnp.zeros_like(gwmi_ref)
            gwmo_ref[...] = jnp.zeros_like(gwmo_ref)
            gwp_ref[...] = jnp.zeros_like(gwp_ref)
            gws_ref[...] = jnp.zeros_like(gws_ref)
            st_ref[...] = jnp.zeros_like(st_ref)
            vec_ref[...] = jnp.zeros_like(vec_ref)
            dbias_ref[...] = jnp.zeros_like(dbias_ref)

        x = x_ref[...]
        dxo = dxo_ref[...]
        r, xhat, n, h = _rms_mod(x, gn, shift, scale)
        hbf = h.astype(BF)
        proj = _dot_nt(hbf, wmi_ref[...])
        xa = proj[:, 0:DP]
        zu = proj[:, DP:DP + DG]
        zv = proj[:, DP + DG:DPROJ]
        _, _, _, hh = _rms_mod(xh_ref[...], gn, shift, scale)
        halo = _dot_nt(hh.astype(BF), wmi_ref[0:DP, :])
        halo = jnp.where(t == 0, 0.0, halo)
        cnts = _pool_counts(t * T, T)
        p = _pool_fwd(xa, halo, ext_scr, cnts, T)
        m = []
        for gi in range(4):
            cols = slice(128 * gi, 128 * gi + 128)
            m.append(_dot(p[gi].astype(BF), wp_ref[gi]))
            ycat_scr[:, cols] = (m[gi] * ps_ref[:, cols]).astype(BF)
        u, tu = _gelu(zu)
        v, tv = _gelu(zv)
        mu = jnp.mean(v, axis=-1, keepdims=True)
        vc = v - mu
        rstd = lax.rsqrt(jnp.mean(vc * vc, axis=-1, keepdims=True) + EPS)
        vhat = vc * rstd
        lg = lg_ref[...]
        vnb = (vhat * lg + lb_ref[...]).astype(BF)
        _sgu_fwd(vnb, ws_ref, sv_scr, T)
        for nck in range(T // CHUNK):
            rows = slice(nck * CHUNK, (nck + 1) * CHUNK)
            sv_scr[rows, :] = sv_scr[rows, :] + bias_ref[...]
        sv = sv_scr[...]
        ycat_scr[:, DP:D] = (u * sv).astype(BF)

        gwmo_ref[...] += _dot_tn(ycat_scr[...], dxo.astype(BF))
        dyc = _dot_nt((dxo * gate).astype(BF), wmo_ref[...])
        dya = dyc[:, 0:DP]
        dyb = dyc[:, DP:D]

        dps = []
        dp = []
        for gi in range(4):
            cols = slice(128 * gi, 128 * gi + 128)
            dps.append(jnp.sum(dya[:, cols] * m[gi], axis=0, keepdims=True))
            dm = (dya[:, cols] * ps_ref[:, cols]).astype(BF)
            gwp_ref[gi] += _dot_tn(p[gi].astype(BF), dm)
            dp.append(_dot_nt(dm, wp_ref[gi]))
            qext_scr[0:T, cols] = dp[gi] / cnts[gi]
        qext_scr[T:T + HALO, :] = carry_scr[...]
        for gi, w in enumerate(WINDOWS):
            cols = slice(128 * gi, 128 * gi + 128)
            acc = qext_scr[0:T, cols]
            for k in range(1, w):
                acc = acc + qext_scr[k:k + T, cols]
            dproj_scr[:, cols] = (acc - dp[gi]).astype(BF)
        carry_scr[...] = qext_scr[0:HALO, :]

        du = dyb * sv
        dsv = dyb * u
        lane = lax.broadcasted_iota(jnp.int32, (CHUNK, 128), 1)
        dbias = jnp.zeros((CHUNK, DG), F32)
        for nck in range(T // CHUNK):
            rows = slice(nck * CHUNK, (nck + 1) * CHUNK)
            dbias = dbias + dsv[rows, :]
            for b in range(DG // 128):
                cols = slice(128 * b, 128 * b + 128)
                dsvb = dsv[rows, cols]
                vb = vnb[rows, cols]
                gws_ref[2 * b] += _dot_nt(jnp.where(lane < 64, dsvb, 0.0).astype(BF), vb)
                gws_ref[2 * b + 1] += _dot_nt(jnp.where(lane < 64, 0.0, dsvb).astype(BF), vb)
                dsvbb = dsvb.astype(BF)
                dvn_scr[rows, cols] = jnp.where(lane < 64, _dot_tn(ws_ref[2 * b], dsvbb),
                                                _dot_tn(ws_ref[2 * b + 1], dsvbb))
        dbias_ref[...] += dbias
        dvn = dvn_scr[...]
        dlg = jnp.sum(dvn * vhat, axis=0, keepdims=True)
        dlb = jnp.sum(dvn, axis=0, keepdims=True)
        dvhat = dvn * lg
        dv = rstd * (dvhat - jnp.mean(dvhat, axis=-1, keepdims=True)
                     - vhat * jnp.mean(dvhat * vhat, axis=-1, keepdims=True))
        dproj_scr[:, DP:DP + DG] = (du * _gelu_grad(zu, tu)).astype(BF)
        dproj_scr[:, DP + DG:DPROJ] = (dv * _gelu_grad(zv, tv)).astype(BF)
        vec_ref[...] += _rows3(jnp.concatenate(dps, axis=1), dlg, dlb, DP)

        dproj = dproj_scr[...]
        gwmi_ref[...] += _dot_tn(dproj, hbf)
        dh = _dot(dproj, wmi_ref[...])
        dxi, dshift, dscale, dgn = _rms_mod_bwd(dh, dxo, r, xhat, n, gn, scale)
        dxi_ref[...] = dxi
        st_ref[...] += _rows3(dshift, dscale, dgn, D)

        @pl.when(i == nS - 1)
        def _():
            acc = gwmo_ref[...]
            dgate = jnp.sum(acc * wmo_ref[...].astype(F32), axis=0, keepdims=True)
            row = lax.broadcasted_iota(jnp.int32, (8, D), 0)
            st_ref[...] += jnp.where(row == 3, dgate, 0.0)
            gwmo_out[...] = (acc * gate).astype(BF)
            gwmi_out[...] = gwmi_ref[...].astype(BF)
            tt = lax.broadcasted_iota(jnp.int32, (CHUNK, CHUNK), 0)
            ss = lax.broadcasted_iota(jnp.int32, (CHUNK, CHUNK), 1)
            for hd in range(8):
                gws_ref[hd] = jnp.where(tt >= ss, gws_ref[hd], 0.0)
            if ne:
                _exchange_phase("wait", sendbufs, recvbufs, sems)

    full = lambda shape: pl.BlockSpec(shape, lambda i: (0,) * len(shape))
    return pl.pallas_call(
        body, name=name, grid=(nS,),
        out_shape=[jax.ShapeDtypeStruct((S, D), F32), jax.ShapeDtypeStruct((DPROJ, D), BF),
                   jax.ShapeDtypeStruct((D, D), BF), jax.ShapeDtypeStruct((4, 128, 128), F32),
                   jax.ShapeDtypeStruct((8, CHUNK, CHUNK), F32), jax.ShapeDtypeStruct((8, D), F32),
                   jax.ShapeDtypeStruct((8, DP), F32), jax.ShapeDtypeStruct((CHUNK, DG), F32)] + _like(exchange),
        in_specs=[pl.BlockSpec((T, D), lambda i: (nS - 1 - i, 0)),
                  pl.BlockSpec((HALO, D), lambda i: (jnp.maximum((nS - 1 - i) * hb - 1, 0), 0)),
                  pl.BlockSpec((T, D), lambda i: (nS - 1 - i, 0)),
                  full((9, D)), full((1, D)), full((DPROJ, D)), full((D, D)),
                  full((4, 128, 128)), full((1, DP)), full((1, DG)), full((1, DG)), full((8, CHUNK, CHUNK)),
                  full((CHUNK, DG))] + [HBM] * ne,
        out_specs=[pl.BlockSpec((T, D), lambda i: (nS - 1 - i, 0)), full((DPROJ, D)), full((D, D)),
                   full((4, 128, 128)), full((8, CHUNK, CHUNK)), full((8, D)), full((8, DP)), full((CHUNK, DG))]
                  + [HBM] * ne,
        scratch_shapes=[pltpu.VMEM((HALO, DP), F32), pltpu.VMEM((T + HALO, DP), F32),
                        pltpu.VMEM((T + HALO, DP), F32), pltpu.VMEM((T, DG), F32), pltpu.VMEM((T, DG), F32),
                        pltpu.VMEM((T, D), BF), pltpu.VMEM((T, DPROJ), BF), pltpu.VMEM((DPROJ, D), F32),
                        pltpu.VMEM((D, D), F32)] + (_comm_sems(ne) if ne else []),
        compiler_params=_cparams(56, ("arbitrary",)),
    )(x, x, dxo, mod, gn, wmi, wmo, wp, ps, lg, lb, ws, bias, *exchange)


def kernel(x, c, w_ada, b_ada, norm_ffn1_g, ffn1_w_in, ffn1_w_out, norm_mix_g, w_mix_in, w_pool, pool_scale, gmlp_ln_g, gmlp_ln_b, w_spatial, b_spatial, w_mix_out, norm_ffn2_g, ffn2_w_in, ffn2_w_out, norm_final_g, loss_target, m_w_ada, m_b_ada, m_norm_ffn1_g, m_ffn1_w_in, m_ffn1_w_out, m_norm_mix_g, m_w_mix_in, m_w_pool, m_pool_scale, m_gmlp_ln_g, m_gmlp_ln_b, m_w_spatial, m_b_spatial, m_w_mix_out, m_norm_ffn2_g, m_ffn2_w_in, m_ffn2_w_out, m_norm_final_g, v_w_ada, v_b_ada, v_norm_ffn1_g, v_ffn1_w_in, v_ffn1_w_out, v_norm_mix_g, v_w_mix_in, v_w_pool, v_pool_scale, v_gmlp_ln_g, v_gmlp_ln_b, v_w_spatial, v_b_spatial, v_w_mix_out, v_norm_ffn2_g, v_ffn2_w_in, v_ffn2_w_out, v_norm_final_g):
    weights = dict(w_ada=w_ada, b_ada=b_ada, norm_ffn1_g=norm_ffn1_g, ffn1_w_in=ffn1_w_in, ffn1_w_out=ffn1_w_out,
                   norm_mix_g=norm_mix_g, w_mix_in=w_mix_in, w_pool=w_pool, pool_scale=pool_scale,
                   gmlp_ln_g=gmlp_ln_g, gmlp_ln_b=gmlp_ln_b, w_spatial=w_spatial, b_spatial=b_spatial,
                   w_mix_out=w_mix_out, norm_ffn2_g=norm_ffn2_g, ffn2_w_in=ffn2_w_in, ffn2_w_out=ffn2_w_out,
                   norm_final_g=norm_final_g)
    mom1 = dict(w_ada=m_w_ada, b_ada=m_b_ada, norm_ffn1_g=m_norm_ffn1_g, ffn1_w_in=m_ffn1_w_in,
                ffn1_w_out=m_ffn1_w_out, norm_mix_g=m_norm_mix_g, w_mix_in=m_w_mix_in, w_pool=m_w_pool,
                pool_scale=m_pool_scale, gmlp_ln_g=m_gmlp_ln_g, gmlp_ln_b=m_gmlp_ln_b, w_spatial=m_w_spatial,
                b_spatial=m_b_spatial, w_mix_out=m_w_mix_out, norm_ffn2_g=m_norm_ffn2_g, ffn2_w_in=m_ffn2_w_in,
                ffn2_w_out=m_ffn2_w_out, norm_final_g=m_norm_final_g)
    mom2 = dict(w_ada=v_w_ada, b_ada=v_b_ada, norm_ffn1_g=v_norm_ffn1_g, ffn1_w_in=v_ffn1_w_in,
                ffn1_w_out=v_ffn1_w_out, norm_mix_g=v_norm_mix_g, w_mix_in=v_w_mix_in, w_pool=v_w_pool,
                pool_scale=v_pool_scale, gmlp_ln_g=v_gmlp_ln_g, gmlp_ln_b=v_gmlp_ln_b, w_spatial=v_w_spatial,
                b_spatial=v_b_spatial, w_mix_out=v_w_mix_out, norm_ffn2_g=v_norm_ffn2_g, ffn2_w_in=v_ffn2_w_in,
                ffn2_w_out=v_ffn2_w_out, norm_final_g=v_norm_final_g)
    order = list(weights)
    xs = x[0]
    target = loss_target[0]
    transposed = ("ffn1_w_in", "w_mix_in", "ffn2_w_in")
    big = ("ffn1_w_in", "ffn1_w_out", "w_mix_in", "w_mix_out", "ffn2_w_in", "ffn2_w_out")
    local = lambda a, k: a[0].T if k in transposed else a[0]
    wc = w_ada.shape[2]

    shard = dict(zip(big, _cast_shards([local(weights[k], k) for k in big])))
    modp, cact_all, g_w1_in, g_w1_out = _ada_forward(jnp.broadcast_to(c, (8, D)), w_ada[0], b_ada.reshape(NDEV, wc),
                                                     [shard["ffn1_w_in"], shard["ffn1_w_out"]])
    mod = modp.reshape(9, D)
    w1_in = g_w1_in.reshape(2, F, D)
    w1_out = g_w1_out.reshape(F, D)

    x1, gu1, h1, g_wmi, g_wmo, g_w2_out, g_w2_in = _ffn_fwd(
        xs, mod, norm_ffn1_g, w1_in, w1_out, 0, "ffn1_fwd",
        gather=[shard["w_mix_in"], shard["w_mix_out"], shard["ffn2_w_out"], shard["ffn2_w_in"]])
    wmi = g_wmi.reshape(DPROJ, D)
    wmo = g_wmo.reshape(D, D)
    w2_in = g_w2_in.reshape(2, F, D)
    w2_out = g_w2_out.reshape(F, D)
    tril = jnp.tril(jnp.ones((CHUNK, CHUNK), dtype=bool))
    ws_b = jnp.where(tril[None], w_spatial[0], 0.0).astype(BF)
    wp_b = w_pool[0].astype(BF)
    bias = jnp.repeat(b_spatial[0].T, DG // 8, axis=1)
    mix_args = (wmi, wmo, wp_b, pool_scale, gmlp_ln_g, gmlp_ln_b, ws_b, bias)
    x2, h3 = _mix_fwd(x1, mod, norm_mix_g, norm_ffn2_g, *mix_args, "mix_fwd")
    dx3, gu3, h3, st_f = _ffn_fwd(x2, mod, norm_ffn2_g, w2_in, w2_out, 2, "ffn2_fwd", h=h3,
                                  loss=(norm_final_g.reshape(1, D), target))

    slots = lambda a: a.reshape(NDEV, a.size // (NDEV * D), D)
    dgu3, d_w2_out, dgate3 = _ffn_bwd_hidden(dx3, mod, gu3, w2_out, 2, "ffn2_bwd_hidden")
    d_w2_in = _ffn_bwd_win(h3, dgu3, "ffn2_bwd_win")[0]
    dx2, st3 = _ffn_bwd_input(dgu3, w2_in, x2, dx3, mod, norm_ffn2_g, 2, "ffn2_bwd_input")
    dx1, d_wmi, d_wmo, d_wp, d_ws, st2, vec2, dbias, r_w2_in, r_w2_out = _mix_bwd(
        x1, dx2, mod, norm_mix_g, *mix_args, "mix_bwd", exchange=[slots(d_w2_in), slots(d_w2_out)])
    dgu1, d_w1_out, dgate1, r_wmi, r_wmo = _ffn_bwd_hidden(
        dx1, mod, gu1, w1_out, 0, "ffn1_bwd_hidden", exchange=[slots(d_wmi), slots(d_wmo)])
    d_w1_in, r_w1_out = _ffn_bwd_win(h1, dgu1, "ffn1_bwd_win", exchange=[slots(d_w1_out)])
    dx0, st1, r_w1_in = _ffn_bwd_input(dgu1, w1_in, xs, dx1, mod, norm_ffn1_g, 0, "ffn1_bwd_input",
                                       exchange=[slots(d_w1_in)])

    received = dict(ffn1_w_in=r_w1_in, ffn1_w_out=r_w1_out, w_mix_in=r_wmi, w_mix_out=r_wmo,
                    ffn2_w_in=r_w2_in, ffn2_w_out=r_w2_out)
    tiles = dict(ffn1_w_in=176, ffn1_w_out=176, w_mix_in=96, w_mix_out=128, ffn2_w_in=176, ffn2_w_out=176)
    result = {}
    for k, recv in received.items():
        res = _sum_adamw(recv, local(weights[k], k), local(mom1[k], k), local(mom2[k], k), tiles[k], "update_" + k)
        result[k] = tuple((a.T if k in transposed else a)[None] for a in res)
    row = lambda a: a.reshape(1, D)
    params = {k: (weights[k], mom1[k], mom2[k]) for k in SMALL}
    params["norm_final_g"] = (row(norm_final_g), row(m_norm_final_g), row(v_norm_final_g))
    tot, rsum, dmine = _small_reduce(d_ws, d_wp, dbias, st1, st2, st3, st_f, vec2, dgate1, dgate3)
    small, loss_row = _small_update(tot, rsum, params)
    result.update(small)
    result["norm_final_g"] = tuple(a.reshape(D) for a in small["norm_final_g"])
    result["w_ada"] = tuple(a[None] for a in _ada_update(cact_all, dmine, w_ada[0], m_w_ada[0], v_w_ada[0], 256))

    return (loss_row[0, 0], dx0[None], *[result[k][0] for k in order], *[result[k][1] for k in order],
            *[result[k][2] for k in order], *[result[k][3] for k in order])
```

```python
import math

import jax
import jax.numpy as jnp
from jax import lax
from jax.experimental import pallas as pl
from jax.experimental.pallas import tpu as pltpu

D = 1024
F = 2816
DP = 512
DG = 512
DPROJ = DP + 2 * DG
CHUNK = 128
WINDOWS = (2, 4, 8, 16)
HALO = 16
NDEV = 8
T_FFN = 512
T_MIX = 256
T_MIX_FWD = 512
T_WIN = 2048
EPS = 1e-6
LR, B1, B2, AEPS, WD, STEP = 0.001, 0.9, 0.999, 1e-08, 0.01, 10
BC1 = 1.0 - B1 ** STEP
BC2 = 1.0 - B2 ** STEP
GELU_C = math.sqrt(2.0 / math.pi)
GELU_A = 0.044715

BF = jnp.bfloat16
F32 = jnp.float32
MESH = pl.DeviceIdType.MESH
HBM = pl.BlockSpec(memory_space=pltpu.HBM)


def _whole(a):
    return pl.BlockSpec(a.shape, lambda i: (0,) * len(a.shape))


NT = (((1,), (1,)), ((), ()))
TN = (((0,), (0,)), ((), ()))


def _dot(a, b):
    return jnp.dot(a, b, preferred_element_type=F32)


def _dot_nt(a, b):
    return lax.dot_general(a, b, NT, preferred_element_type=F32)


def _dot_tn(a, b):
    return lax.dot_general(a, b, TN, preferred_element_type=F32)


def _cparams(vmem_mb, sem=None):
    kw = dict(vmem_limit_bytes=vmem_mb * 1024 * 1024)
    if sem is not None:
        kw["dimension_semantics"] = sem
    return pltpu.CompilerParams(**kw)


def _position():
    return lax.axis_index("x"), lax.axis_index("y"), lax.axis_index("c")


def _slot(p):
    return 4 * p[0] + 2 * p[1] + p[2]


def _flip(me, d):
    x, y, c = me
    return (1 - x if d & 4 else x, 1 - y if d & 2 else y, 1 - c if d & 1 else c)


def _remote(src, dst, send_sem, recv_sem, to):
    return pltpu.make_async_remote_copy(src_ref=src, dst_ref=dst, send_sem=send_sem, recv_sem=recv_sem,
                                        device_id=to, device_id_type=MESH)


def _comm_sems(n):
    return [pltpu.SemaphoreType.DMA((n, 7)), pltpu.SemaphoreType.DMA((n, 7)), pltpu.SemaphoreType.DMA((n,))]


def _gather_phase(phase, xs, outs, sems):
    send_sems, recv_sems, local_sems = sems
    n = len(xs)
    me = _position()
    x, y, c = me
    sibling = (x, y, 1 - c)
    xn, yn, diag = (1 - x, y), (x, 1 - y), (1 - x, 1 - y)
    relay_from = (x + c * (1 - 2 * x), y + (1 - c) * (1 - 2 * y))
    relay_to = (x + (1 - c) * (1 - 2 * x), y + c * (1 - 2 * y))

    def copy(a, k, block, to, src=None):
        dst = outs[a].at[_slot(block)]
        return _remote(dst if src is None else src, dst, send_sems.at[a, k], recv_sems.at[a, k], to)

    def mine(a):
        return pltpu.make_async_copy(xs[a], outs[a].at[_slot(me)], local_sems.at[a])

    def first(a):
        return [copy(a, 0, me, sibling, src=xs[a]), copy(a, 1, me, (*xn, c), src=xs[a]),
                copy(a, 2, me, (*yn, c), src=xs[a])]

    def second(a):
        return [copy(a, 3, (*relay_from, c), (*relay_to, c)), copy(a, 4, (*xn, c), sibling),
                copy(a, 5, (*yn, c), sibling)]

    def third(a):
        return copy(a, 6, (*diag, c), sibling)

    if phase == "start":
        for a in range(n):
            mine(a).start()
            for cp in first(a):
                cp.start()
    elif phase == "forward":
        for a in range(n):
            copy(a, 1, (*xn, c), me).wait_recv()
            copy(a, 2, (*yn, c), me).wait_recv()
            for cp in second(a):
                cp.start()
    else:
        for a in range(n):
            copy(a, 3, (*diag, c), me).wait_recv()
            third(a).start()
        for a in range(n):
            copy(a, 0, sibling, me).wait_recv()
            for k, chip in ((4, xn), (5, yn), (6, diag)):
                copy(a, k, (*chip, 1 - c), me).wait_recv()
        for a in range(n):
            for cp in first(a) + second(a) + [third(a)]:
                cp.wait_send()
            mine(a).wait()


def _exchange_phase(phase, xs, outs, sems):
    send_sems, recv_sems, local_sems = sems
    me = _position()
    for a in range(len(xs)):
        copies = [pltpu.make_async_copy(xs[a].at[_slot(me)], outs[a].at[_slot(me)], local_sems.at[a])]
        for d in range(1, NDEV):
            to = _flip(me, d)
            copies.append(_remote(xs[a].at[_slot(to)], outs[a].at[_slot(me)],
                                  send_sems.at[a, d - 1], recv_sems.at[a, d - 1], to))
        for cp in copies:
            if phase == "start":
                cp.start()
            else:
                cp.wait()


def _like(bufs):
    return [jax.ShapeDtypeStruct(b.shape, b.dtype) for b in bufs]


def _rms_mod(x, gn, shift, scale):
    ms = jnp.mean(x * x, axis=-1, keepdims=True)
    r = lax.rsqrt(ms + EPS)
    xhat = x * r
    n = xhat * gn
    h = n * (1.0 + scale) + shift
    return r, xhat, n, h


def _rms_mod_bwd(dh, dres, r, xhat, n, gn, scale):
    dshift = jnp.sum(dh, axis=0, keepdims=True)
    dscale = jnp.sum(dh * n, axis=0, keepdims=True)
    dn = dh * (1.0 + scale)
    dgn = jnp.sum(dn * xhat, axis=0, keepdims=True)
    dxhat = dn * gn
    dx = dres + r * (dxhat - xhat * jnp.mean(dxhat * xhat, axis=-1, keepdims=True))
    return dx, dshift, dscale, dgn


def _final_norm_loss(x, gf, target):
    r = lax.rsqrt(jnp.mean(x * x, axis=-1, keepdims=True) + EPS)
    xhat = x * r
    e = xhat * gf - target
    part = 0.5 * jnp.sum(jnp.sum(e * e, axis=-1, keepdims=True), axis=0, keepdims=True) / D
    dy = e / D
    dgf = jnp.sum(dy * xhat, axis=0, keepdims=True)
    dxhat = dy * gf
    dx = r * (dxhat - xhat * jnp.mean(dxhat * xhat, axis=-1, keepdims=True))
    return dx, dgf, part


def _rows3(a, b, c, width):
    row = lax.broadcasted_iota(jnp.int32, (8, width), 0)
    z = jnp.zeros((8, width), F32)
    return jnp.where(row == 0, a, z) + jnp.where(row == 1, b, z) + jnp.where(row == 2, c, z)


def _sigmoid(x):
    return 0.5 * jnp.tanh(0.5 * x) + 0.5


def _gelu(x):
    t = jnp.tanh(GELU_C * (x + GELU_A * x * x * x))
    return 0.5 * x * (1.0 + t), t


def _gelu_grad(x, t):
    return 0.5 * (1.0 + t) + 0.5 * x * (1.0 - t * t) * GELU_C * (1.0 + 3.0 * GELU_A * x * x)


def _adamw(w, g, m, v):
    m = B1 * m + (1.0 - B1) * g
    v = B2 * v + (1.0 - B2) * (g * g)
    m_hat = m / BC1
    v_hat = v / BC2
    delta = -LR * (m_hat / (jnp.sqrt(v_hat) + AEPS) + WD * w)
    return delta, m, v


def _cast_shards(shards):
    n = len(shards)

    def body(*refs):
        for a in range(n):
            refs[n + a][...] = refs[a][...].astype(BF)

    resident = pl.BlockSpec(memory_space=pltpu.VMEM)
    return pl.pallas_call(
        body, name="cast_shards", out_shape=[jax.ShapeDtypeStruct(s.shape, BF) for s in shards],
        in_specs=[resident] * n, out_specs=[resident] * n, compiler_params=_cparams(40),
    )(*shards)


def _ada_forward(c8, w_ada, b8, shards, x, gn):
    wc = w_ada.shape[1]
    n = len(shards)
    S = x.shape[0]
    T = min(T_FFN, S)
    nS = S // T

    def ada(c8_ref, w_ref, b8_ref, xs, mod_ref, cact_ref, gathered, call_ref, mall_ref, modp_ref, send_sems,
            recv_sems, gsems):
        me = _position()
        my = _slot(me)
        row = lax.broadcasted_iota(jnp.int32, (8, 1), 0)
        call_ref[my] = c8_ref[...]
        sends = []
        for d in range(1, NDEV):
            to = _flip(me, d)
            sends.append(_remote(call_ref.at[my], call_ref.at[my], send_sems.at[0, d - 1], recv_sems.at[0, d - 1], to))
        for cp in sends:
            cp.start()
        _gather_phase("start", xs, gathered, gsems)
        for cp in sends:
            cp.wait()
        c_all = jnp.zeros((8, D), F32)
        for k in range(NDEV):
            c_all = c_all + jnp.where(row == k, call_ref[k], 0.0)
        cact = c_all * jax.nn.sigmoid(c_all)
        cact_ref[...] = cact
        part = _dot(cact.astype(BF), w_ref[...].astype(BF))
        mall_ref[my] = part
        sends = []
        for d in range(1, NDEV):
            to = _flip(me, d)
            sends.append(_remote(mall_ref.at[my], mall_ref.at[my], send_sems.at[1, d - 1], recv_sems.at[1, d - 1], to))
        for cp in sends:
            cp.start()
        _gather_phase("forward", xs, gathered, gsems)
        for cp in sends:
            cp.wait()
        out = jnp.zeros((8, wc), F32)
        for k in range(NDEV):
            piece = jnp.sum(jnp.where(row == my, mall_ref[k], 0.0), axis=0, keepdims=True)
            out = out + jnp.where(row == k, piece, 0.0)
        modp_ref[...] = out + b8_ref[...]
        for q in range(9 * NDEV):
            mod_ref[q // 8:q // 8 + 1, 128 * (q % 8):128 * (q % 8 + 1)] = \
                modp_ref[q // 9:q // 9 + 1, 128 * (q % 9):128 * (q % 9 + 1)]

    def body(*refs):
        c8_ref, w_ref, b8_ref, gn_ref, x_ref = refs[:5]
        xs = refs[5:5 + n]
        mod_ref, cact_ref, h_ref = refs[5 + n:8 + n]
        gathered = refs[8 + n:8 + 2 * n]
        call_ref, mall_ref, modp_ref, send_sems, recv_sems = refs[8 + 2 * n:13 + 2 * n]
        gsems = refs[13 + 2 * n:]
        i = pl.program_id(0)

        @pl.when(i == 0)
        def _():
            ada(c8_ref, w_ref, b8_ref, xs, mod_ref, cact_ref, gathered, call_ref, mall_ref, modp_ref, send_sems,
                recv_sems, gsems)

        _, _, _, h = _rms_mod(x_ref[...], gn_ref[...], mod_ref[0:1, :], mod_ref[1:2, :])
        h_ref[...] = h.astype(BF)

        @pl.when(i == nS - 1)
        def _():
            _gather_phase("finish", xs, gathered, gsems)

    outs = [jax.ShapeDtypeStruct((9, D), F32), jax.ShapeDtypeStruct((8, D), F32)]
    tile = pl.BlockSpec((T, D), lambda i: (i, 0))
    return pl.pallas_call(
        body, name="ada_forward", grid=(nS,),
        out_shape=outs + [jax.ShapeDtypeStruct((S, D), BF)]
                  + [jax.ShapeDtypeStruct((NDEV,) + s.shape, s.dtype) for s in shards],
        in_specs=[_whole(a) for a in (c8, w_ada, b8, gn)] + [tile] + [HBM] * n,
        out_specs=[_whole(a) for a in outs] + [tile] + [HBM] * n,
        scratch_shapes=[pltpu.VMEM((NDEV, 8, D), F32), pltpu.VMEM((NDEV, 8, wc), F32), pltpu.VMEM((8, wc), F32),
                        pltpu.SemaphoreType.DMA((2, 7)), pltpu.SemaphoreType.DMA((2, 7))] + _comm_sems(n),
        compiler_params=_cparams(40, ("arbitrary",)),
    )(c8, w_ada, b8, gn, x, *shards)


MATS = ("w_spatial", "w_pool", "b_spatial")
VECS = ("norm_ffn1_g", "norm_mix_g", "norm_ffn2_g", "norm_final_g", "pool_scale", "gmlp_ln_g", "gmlp_ln_b", "b_ada")
VEC_WIDTH = dict(norm_ffn1_g=D, norm_mix_g=D, norm_ffn2_g=D, norm_final_g=D, pool_scale=DP, gmlp_ln_g=DG,
                 gmlp_ln_b=DG, b_ada=9 * D)
MAT_ROWS = 1600
MAT_SLICE = MAT_ROWS // NDEV
VEC_LANES = sum(VEC_WIDTH.values()) + 128
DMOD_AT = VEC_LANES - 128 - 9 * D
SMALL = MATS + VECS


def _small_reduce(g_ws, g_wp, dbias, st1, st2, st3, st_f, vec2, dgate1, dgate3):
    wc = 9 * D // NDEV

    def body(g_ws_ref, g_wp_ref, dbias_ref, st1_ref, st2_ref, st3_ref, stf_ref, vec2_ref, dg1_ref, dg3_ref,
             tot_ref, rsum_ref, dmine_ref,
             pack_ref, rs_ref, ag_ref, rv_ref, dmp_ref, dw_ref, send_sems, recv_sems):
        me = _position()
        my = _slot(me)

        pack_ref[0:1024, :] = g_ws_ref[...].reshape(1024, 128)
        pack_ref[1024:1536, :] = g_wp_ref[...].reshape(512, 128)
        ch = lax.broadcasted_iota(jnp.int32, (DG, 128), 0)
        hd = lax.broadcasted_iota(jnp.int32, (DG, 128), 1)
        sel = jnp.where(ch // 64 == hd, 1.0, 0.0).astype(F32)
        heads = jnp.dot(dbias_ref[...], sel, preferred_element_type=F32, precision=lax.Precision.HIGHEST)
        pack_ref[1536:1544, :] = heads.T[0:8, :]
        pack_ref[1544:MAT_ROWS, :] = jnp.zeros((MAT_ROWS - 1544, 128), F32)
        dgate1 = dg1_ref[0:1, :] + dg1_ref[8:9, :]
        dgate3 = dg3_ref[0:1, :] + dg3_ref[8:9, :]
        row = jnp.concatenate(
            [st1_ref[2:3, :], st2_ref[2:3, :], st3_ref[2:3, :], stf_ref[0:1, :],
             vec2_ref[0:1, :], vec2_ref[1:2, :], vec2_ref[2:3, :],
             st1_ref[0:1, :], st1_ref[1:2, :], dgate1, st2_ref[0:1, :], st2_ref[1:2, :], st2_ref[3:4, :],
             st3_ref[0:1, :], st3_ref[1:2, :], dgate3, stf_ref[1:2, 0:128]], axis=1)
        rv_ref[my] = row
        for k in range(NDEV):
            dmp_ref[k] = row[:, DMOD_AT + wc * k:DMOD_AT + wc * (k + 1)]
        dw_ref[my] = dmp_ref[my]
        rs_ref[my] = pack_ref[pl.ds(pl.multiple_of(my * MAT_SLICE, 8), MAT_SLICE), :]

        first = []
        for d in range(1, NDEV):
            to = _flip(me, d)
            theirs = pl.ds(pl.multiple_of(_slot(to) * MAT_SLICE, 8), MAT_SLICE)
            first.append(_remote(pack_ref.at[theirs, :], rs_ref.at[my], send_sems.at[0, d - 1], recv_sems.at[0, d - 1], to))
            first.append(_remote(dmp_ref.at[_slot(to)], dw_ref.at[my], send_sems.at[1, d - 1], recv_sems.at[1, d - 1], to))
            first.append(_remote(rv_ref.at[my], rv_ref.at[my], send_sems.at[2, d - 1], recv_sems.at[2, d - 1], to))
        for cp in first:
            cp.start()
        for cp in first:
            cp.wait()
        red = rs_ref[0]
        for k in range(1, NDEV):
            red = red + rs_ref[k]
        ag_ref[my] = red
        second = []
        for d in range(1, NDEV):
            to = _flip(me, d)
            second.append(_remote(ag_ref.at[my], ag_ref.at[my], send_sems.at[3, d - 1], recv_sems.at[3, d - 1], to))
        for cp in second:
            cp.start()

        rsum = rv_ref[0]
        for k in range(1, NDEV):
            rsum = rsum + rv_ref[k]
        rsum_ref[...] = rsum
        r8 = lax.broadcasted_iota(jnp.int32, (8, 1), 0)
        dmine = jnp.zeros((8, wc), F32)
        for k in range(NDEV):
            dmine = dmine + jnp.where(r8 == k, dw_ref[k], 0.0)
        dmine_ref[...] = dmine

        for cp in second:
            cp.wait()
        for k in range(NDEV):
            tot_ref[k * MAT_SLICE:(k + 1) * MAT_SLICE, :] = ag_ref[k]

    ins = (g_ws, g_wp, dbias, st1, st2, st3, st_f, vec2, dgate1, dgate3)
    outs = [jax.ShapeDtypeStruct((MAT_ROWS, 128), F32), jax.ShapeDtypeStruct((1, VEC_LANES), F32),
            jax.ShapeDtypeStruct((8, wc), F32)]
    return pl.pallas_call(
        body, name="small_reduce", grid=(1,), out_shape=outs,
        in_specs=[_whole(a) for a in ins], out_specs=[_whole(a) for a in outs],
        scratch_shapes=[pltpu.VMEM((MAT_ROWS, 128), F32), pltpu.VMEM((NDEV, MAT_SLICE, 128), F32),
                        pltpu.VMEM((NDEV, MAT_SLICE, 128), F32),
                        pltpu.VMEM((NDEV, 1, VEC_LANES), F32), pltpu.VMEM((NDEV, 1, wc), F32),
                        pltpu.VMEM((NDEV, 1, wc), F32),
                        pltpu.SemaphoreType.DMA((4, 7)), pltpu.SemaphoreType.DMA((4, 7))],
        compiler_params=_cparams(32, ("arbitrary",)),
    )(*ins)


def _small_update(tot, rsum, params):
    flat = [a for k in SMALL for a in params[k]]
    n_in = 2 + len(flat)

    def body(*refs):
        tot_ref, rsum_ref = refs[:2]
        p_hbm = refs[2:n_in]
        o_refs = refs[n_in:n_in + 4 * len(SMALL)]
        loss_ref = refs[n_in + 4 * len(SMALL)]
        p_refs = refs[n_in + 4 * len(SMALL) + 1:-1]
        sem = refs[-1]
        fetch = [pltpu.make_async_copy(p_hbm[k], p_refs[k], sem.at[k]) for k in range(len(flat))]
        for cp in fetch:
            cp.start()
        for cp in fetch:
            cp.wait()
        loss_ref[...] = rsum_ref[:, VEC_LANES - 128:VEC_LANES]

        def update(idx, g):
            w_ref, m_ref, v_ref = p_refs[3 * idx:3 * idx + 3]
            g_out, d_out, m_out, v_out = o_refs[4 * idx:4 * idx + 4]
            g = g.reshape(w_ref.shape)
            g_out[...] = g
            d_out[...], m_out[...], v_out[...] = _adamw(w_ref[...], g, m_ref[...], v_ref[...])

        update(0, tot_ref[0:1024, :])
        update(1, tot_ref[1024:1536, :])
        update(2, tot_ref[1536:1544, :])
        at = 0
        for idx, k in enumerate(VECS):
            update(3 + idx, rsum_ref[:, at:at + VEC_WIDTH[k]])
            at += VEC_WIDTH[k]

    outs = []
    for k in SMALL:
        outs += [jax.ShapeDtypeStruct(params[k][0].shape, F32)] * 4
    outs += [jax.ShapeDtypeStruct((1, 128), F32)]
    res = pl.pallas_call(
        body, name="small_update", grid=(1,), out_shape=outs,
        in_specs=[_whole(tot), _whole(rsum)] + [HBM] * len(flat), out_specs=[_whole(a) for a in outs],
        scratch_shapes=[pltpu.VMEM(a.shape, F32) for a in flat] + [pltpu.SemaphoreType.DMA((len(flat),))],
        compiler_params=_cparams(32, ("arbitrary",)),
    )(tot, rsum, *[pltpu.with_memory_space_constraint(a, pltpu.HBM) for a in flat])
    return {k: tuple(res[4 * i:4 * i + 4]) for i, k in enumerate(SMALL)}, res[-1]


def _sum_adamw(recv, w, m, v, tr, name):
    R, C = w.shape

    def body(r_ref, w_ref, m_ref, v_ref, g_ref, d_ref, nm_ref, nv_ref):
        g = r_ref[0].astype(F32)
        for k in range(1, NDEV):
            g = g + r_ref[k].astype(F32)
        g_ref[...] = g
        d_ref[...], nm_ref[...], nv_ref[...] = _adamw(w_ref[...], g, m_ref[...], v_ref[...])

    blk = pl.BlockSpec((tr, C), lambda i: (i, 0))
    out = jax.ShapeDtypeStruct((R, C), F32)
    return pl.pallas_call(
        body, name=name, grid=(R // tr,), out_shape=[out] * 4,
        in_specs=[pl.BlockSpec((NDEV, tr, C), lambda i: (0, i, 0)), blk, blk, blk], out_specs=[blk] * 4,
        compiler_params=_cparams(48, ("arbitrary",)),
    )(recv, w, m, v)


def _ada_update(cact_all, dmine, w, m, v, tr):
    R, C = w.shape

    def body(c_ref, dm_ref, w_ref, m_ref, v_ref, g_ref, d_ref, nm_ref, nv_ref):
        g = _dot_tn(c_ref[...].astype(BF), dm_ref[...].astype(BF))
        g_ref[...] = g
        d_ref[...], nm_ref[...], nv_ref[...] = _adamw(w_ref[...], g, m_ref[...], v_ref[...])

    blk = pl.BlockSpec((tr, C), lambda i: (i, 0))
    out = jax.ShapeDtypeStruct((R, C), F32)
    return pl.pallas_call(
        body, name="update_w_ada", grid=(R // tr,), out_shape=[out] * 4,
        in_specs=[pl.BlockSpec((8, tr), lambda i: (0, i)), pl.BlockSpec((8, C), lambda i: (0, 0)), blk, blk, blk],
        out_specs=[blk] * 4,
        compiler_params=_cparams(48, ("arbitrary",)),
    )(cact_all, dmine, w, m, v)


FC = F // 2


def _ffn_fwd(x, mod, gn, w_in_t, w_out, sub, name, gather=(), loss=None, h=None):
    S = x.shape[0]
    T = min(T_FFN, S)
    nS, nJ = S // T, F // FC
    ng = len(gather)
    nl = 2 if loss else 0
    nh = 0 if h is None else 1
    forward_step = nS // 2

    def body(*refs):
        x_ref, mod_ref, gn_ref, wg_ref, wu_ref, wo_ref = refs[:6]
        gf_ref, t_ref = refs[6 + nh:6 + nh + nl] if loss else (None, None)
        shards = refs[6 + nh + nl:6 + nh + nl + ng]
        at = 6 + nh + nl + ng
        xo_ref, gu_ref = refs[at:at + 2]
        h_ref = refs[6] if nh else refs[at + 2]
        at += 3 - nh
        gathered = refs[at:at + ng]
        at += ng
        st_ref = refs[at] if loss else None
        at += nl // 2
        acc_scr = refs[at]
        sems = refs[at + 1:]
        i, j = pl.program_id(0), pl.program_id(1)

        if ng:
            @pl.when((i == 0) & (j == 0))
            def _():
                _gather_phase("start", shards, gathered, sems)

            @pl.when((i == forward_step) & (j == 0))
            def _():
                _gather_phase("forward", shards, gathered, sems)

        @pl.when(j == 0)
        def _():
            if not nh:
                _, _, _, hh = _rms_mod(x_ref[...], gn_ref[...], mod_ref[3 * sub:3 * sub + 1, :],
                                       mod_ref[3 * sub + 1:3 * sub + 2, :])
                h_ref[...] = hh.astype(BF)
            acc_scr[...] = jnp.zeros_like(acc_scr)

        hb = h_ref[...]
        g = _dot_nt(hb, wg_ref[0])
        u = _dot_nt(hb, wu_ref[0])
        gu_ref[0] = g.astype(BF)
        gu_ref[1] = u.astype(BF)
        a = (g * _sigmoid(g) * u).astype(BF)
        acc_scr[...] += _dot(a, wo_ref[...])

        @pl.when(j == nJ - 1)
        def _():
            xo = x_ref[...] + (0.5 * mod_ref[3 * sub + 2:3 * sub + 3, :]) * acc_scr[...]
            if not loss:
                xo_ref[...] = xo
            else:
                dx, dgf, part = _final_norm_loss(xo, gf_ref[...], t_ref[...])
                xo_ref[...] = dx
                upd = _rows3(dgf, jnp.broadcast_to(part, (1, D)), jnp.zeros((1, D), F32), D)

                @pl.when(i == 0)
                def _():
                    st_ref[...] = upd

                @pl.when(i > 0)
                def _():
                    st_ref[...] += upd

        if ng:
            @pl.when((i == nS - 1) & (j == nJ - 1))
            def _():
                _gather_phase("finish", shards, gathered, sems)

    tile = pl.BlockSpec((T, D), lambda i, j: (i, 0))
    res = pl.pallas_call(
        body, name=name, grid=(nS, nJ),
        out_shape=[jax.ShapeDtypeStruct((S, D), F32), jax.ShapeDtypeStruct((2, S, F), BF)]
                  + ([] if nh else [jax.ShapeDtypeStruct((S, D), BF)])
                  + [jax.ShapeDtypeStruct((NDEV,) + s.shape, s.dtype) for s in gather]
                  + ([jax.ShapeDtypeStruct((8, D), F32)] if loss else []),
        in_specs=[tile,
                  pl.BlockSpec((9, D), lambda i, j: (0, 0)),
                  pl.BlockSpec((1, D), lambda i, j: (0, 0)),
                  pl.BlockSpec((1, FC, D), lambda i, j: (0, j, 0)),
                  pl.BlockSpec((1, FC, D), lambda i, j: (1, j, 0)),
                  pl.BlockSpec((FC, D), lambda i, j: (j, 0))] + [tile] * nh
                 + ([pl.BlockSpec((1, D), lambda i, j: (0, 0)), tile] if loss else []) + [HBM] * ng,
        out_specs=[tile, pl.BlockSpec((2, T, FC), lambda i, j: (0, i, j))] + [tile] * (1 - nh) + [HBM] * ng
                  + ([pl.BlockSpec((8, D), lambda i, j: (0, 0))] if loss else []),
        scratch_shapes=[pltpu.VMEM((T, D), F32)] + (_comm_sems(ng) if ng else []),
        compiler_params=_cparams(56, ("arbitrary", "arbitrary")),
    )(x, mod, gn, w_in_t, w_in_t, w_out, *(() if h is None else (h,)), *(loss or ()), *gather)
    return res if h is None else [res[0], res[1], h, *res[2:]]


def _ffn_bwd_hidden(dx, mod, gu, w_out, sub, name, exchange=()):
    S = dx.shape[0]
    T = min(T_FFN, S)
    nS, nJ = S // T, F // FC
    ne = len(exchange)

    def body(*refs):
        dx_ref, mod_ref, gu_ref, wo_ref = refs[:4]
        sendbufs = refs[4:4 + ne]
        dgu_ref, gw_ref, dgate_ref = refs[4 + ne:7 + ne]
        recvbufs = refs[7 + ne:7 + 2 * ne]
        acc_scr = refs[7 + 2 * ne]
        sems = refs[8 + 2 * ne:]
        j, i = pl.program_id(0), pl.program_id(1)

        if ne:
            @pl.when((i == 0) & (j == 0))
            def _():
                _exchange_phase("start", sendbufs, recvbufs, sems)

        gate = mod_ref[3 * sub + 2:3 * sub + 3, :]
        dx = dx_ref[...]
        da = _dot_nt((dx * (0.5 * gate)).astype(BF), wo_ref[...])
        g = gu_ref[0].astype(F32)
        u = gu_ref[1].astype(F32)
        sg = _sigmoid(g)
        s = g * sg
        dgu_ref[0] = (da * u * (sg * (1.0 + g * (1.0 - sg)))).astype(BF)
        dgu_ref[1] = (da * s).astype(BF)
        contrib = _dot_tn((s * u).astype(BF), dx.astype(BF))

        @pl.when(i == 0)
        def _():
            acc_scr[...] = contrib

        @pl.when(i > 0)
        def _():
            acc_scr[...] += contrib

        @pl.when(i == nS - 1)
        def _():
            acc = acc_scr[...]
            dgate = 0.5 * jnp.sum(acc * wo_ref[...].astype(F32), axis=0, keepdims=True)
            dgate_ref[...] = jnp.broadcast_to(dgate, (8, D))
            gw_ref[...] = (acc * (0.5 * gate)).astype(BF)

        if ne:
            @pl.when((i == nS - 1) & (j == nJ - 1))
            def _():
                _exchange_phase("wait", sendbufs, recvbufs, sems)

    return pl.pallas_call(
        body, name=name, grid=(nJ, nS),
        out_shape=[jax.ShapeDtypeStruct((2, S, F), BF), jax.ShapeDtypeStruct((F, D), BF),
                   jax.ShapeDtypeStruct((8 * nJ, D), F32)] + _like(exchange),
        in_specs=[pl.BlockSpec((T, D), lambda j, i: (i, 0)),
                  pl.BlockSpec((9, D), lambda j, i: (0, 0)),
                  pl.BlockSpec((2, T, FC), lambda j, i: (0, i, j)),
                  pl.BlockSpec((FC, D), lambda j, i: (j, 0))] + [HBM] * ne,
        out_specs=[pl.BlockSpec((2, T, FC), lambda j, i: (0, i, j)),
                   pl.BlockSpec((FC, D), lambda j, i: (j, 0)),
                   pl.BlockSpec((8, D), lambda j, i: (j, 0))] + [HBM] * ne,
        scratch_shapes=[pltpu.VMEM((FC, D), F32)] + (_comm_sems(ne) if ne else []),
        compiler_params=_cparams(56, ("arbitrary", "arbitrary")),
    )(dx, mod, gu, w_out, *exchange)


def _ffn_bwd_input(dgu, w_in_t, x, dx, mod, gn, sub, name, exchange=()):
    S = x.shape[0]
    T = min(T_FFN, S)
    nS = S // T
    ne = len(exchange)
    NC = 256
    chunks = [slice(k * NC, (k + 1) * NC) for k in range(D // NC)]

    def body(*refs):
        dgu_ref, w_ref, x_ref, dx_ref, mod_ref, gn_ref = refs[:6]
        sendbufs = refs[6:6 + ne]
        dxin_ref, st_ref = refs[6 + ne:8 + ne]
        recvbufs = refs[8 + ne:8 + 2 * ne]
        dxh_scr = refs[8 + 2 * ne]
        sems = refs[9 + 2 * ne:]
        i = pl.program_id(0)

        if ne:
            @pl.when(i == 0)
            def _():
                _exchange_phase("start", sendbufs, recvbufs, sems)

        gn = gn_ref[...]
        scale = mod_ref[3 * sub + 1:3 * sub + 2, :]
        r, xhat, n, _ = _rms_mod(x_ref[...], gn, mod_ref[3 * sub:3 * sub + 1, :], scale)
        dg = dgu_ref[0]
        du = dgu_ref[1]
        rowsum = jnp.zeros((T, 1), F32)
        dshift, dscale, dgn = [], [], []
        for cols in chunks:
            dh = _dot(dg, w_ref[0, :, cols]) + _dot(du, w_ref[1, :, cols])
            dshift.append(jnp.sum(dh, axis=0, keepdims=True))
            dscale.append(jnp.sum(dh * n[:, cols], axis=0, keepdims=True))
            dn = dh * (1.0 + scale[:, cols])
            dgn.append(jnp.sum(dn * xhat[:, cols], axis=0, keepdims=True))
            dxhat = dn * gn[:, cols]
            rowsum = rowsum + jnp.sum(dxhat * xhat[:, cols], axis=-1, keepdims=True)
            dxh_scr[:, cols] = dxhat
        dxin_ref[...] = dx_ref[...] + r * (dxh_scr[...] - xhat * (rowsum / D))
        cat = lambda parts: jnp.concatenate(parts, axis=1)
        upd = _rows3(cat(dshift), cat(dscale), cat(dgn), D)

        @pl.when(i == 0)
        def _():
            st_ref[...] = upd

        @pl.when(i > 0)
        def _():
            st_ref[...] += upd

        if ne:
            @pl.when(i == nS - 1)
            def _():
                _exchange_phase("wait", sendbufs, recvbufs, sems)

    tile = pl.BlockSpec((T, D), lambda i: (i, 0))
    return pl.pallas_call(
        body, name=name, grid=(nS,),
        out_shape=[jax.ShapeDtypeStruct((S, D), F32), jax.ShapeDtypeStruct((8, D), F32)] + _like(exchange),
        in_specs=[pl.BlockSpec((2, T, F), lambda i: (0, i, 0)),
                  pl.BlockSpec((2, F, D), lambda i: (0, 0, 0), pipeline_mode=pl.Buffered(1)),
                  tile, tile,
                  pl.BlockSpec((9, D), lambda i: (0, 0)),
                  pl.BlockSpec((1, D), lambda i: (0, 0))] + [HBM] * ne,
        out_specs=[tile, pl.BlockSpec((8, D), lambda i: (0, 0))] + [HBM] * ne,
        scratch_shapes=[pltpu.VMEM((T, D), F32)] + (_comm_sems(ne) if ne else []),
        compiler_params=_cparams(60, ("arbitrary",)),
    )(dgu, w_in_t, x, dx, mod, gn, *exchange)


def _ffn_bwd_win(h, dgu, name, exchange=()):
    S = h.shape[0]
    T = min(T_WIN, S)
    nS, nJ = S // T, F // FC
    ne = len(exchange)

    def body(*refs):
        h_ref, dgu_ref = refs[:2]
        sendbufs = refs[2:2 + ne]
        out_ref = refs[2 + ne]
        recvbufs = refs[3 + ne:3 + 2 * ne]
        acc_scr = refs[3 + 2 * ne]
        sems = refs[4 + 2 * ne:]
        p, j, i = pl.program_id(0), pl.program_id(1), pl.program_id(2)

        if ne:
            @pl.when((p == 0) & (j == 0) & (i == 0))
            def _():
                _exchange_phase("start", sendbufs, recvbufs, sems)

        contrib = _dot_tn(dgu_ref[0], h_ref[...])

        @pl.when(i == 0)
        def _():
            acc_scr[...] = contrib

        @pl.when(i > 0)
        def _():
            acc_scr[...] += contrib

        @pl.when(i == nS - 1)
        def _():
            out_ref[0] = acc_scr[...].astype(BF)

        if ne:
            @pl.when((p == 1) & (j == nJ - 1) & (i == nS - 1))
            def _():
                _exchange_phase("wait", sendbufs, recvbufs, sems)

    return pl.pallas_call(
        body, name=name, grid=(2, nJ, nS),
        out_shape=[jax.ShapeDtypeStruct((2, F, D), BF)] + _like(exchange),
        in_specs=[pl.BlockSpec((T, D), lambda p, j, i: (i, 0)),
                  pl.BlockSpec((1, T, FC), lambda p, j, i: (p, i, j))] + [HBM] * ne,
        out_specs=[pl.BlockSpec((1, FC, D), lambda p, j, i: (p, j, 0))] + [HBM] * ne,
        scratch_shapes=[pltpu.VMEM((FC, D), F32)] + (_comm_sems(ne) if ne else []),
        compiler_params=_cparams(56, ("arbitrary", "arbitrary", "arbitrary")),
    )(h, dgu, *exchange)


def _pool_counts(pos0, T):
    pos = pos0 + lax.broadcasted_iota(jnp.int32, (T, 1), 0)
    return [jnp.minimum(pos + 1, w).astype(F32) for w in WINDOWS]


def _pool_fwd(xa, halo, ext_scr, cnts, T):
    ext_scr[0:HALO, :] = halo
    ext_scr[HALO:HALO + T, :] = xa
    out = []
    for gi, w in enumerate(WINDOWS):
        cols = slice(128 * gi, 128 * gi + 128)
        acc = xa[:, cols]
        for k in range(1, w):
            acc = acc + ext_scr[HALO - k:HALO - k + T, cols]
        out.append(acc / cnts[gi] - xa[:, cols])
    return out


def _sgu_fwd(vnb, ws_ref, sv_scr, T):
    lane = lax.broadcasted_iota(jnp.int32, (CHUNK, 128), 1)
    for n in range(T // CHUNK):
        rows = slice(n * CHUNK, (n + 1) * CHUNK)
        for b in range(DG // 128):
            cols = slice(128 * b, 128 * b + 128)
            vb = vnb[rows, cols]
            sv_scr[rows, cols] = jnp.where(lane < 64, _dot(ws_ref[2 * b], vb), _dot(ws_ref[2 * b + 1], vb))


def _mix_fwd(x, mod, gn, gn_next, wmi, wmo, wp, ps, lg, lb, ws, bias, name):
    S = x.shape[0]
    T = min(T_MIX_FWD, S)

    def body(x_ref, mod_ref, gn_ref, gnn_ref, wmi_ref, wmo_ref, wp_ref, ps_ref, lg_ref, lb_ref, ws_ref, bias_ref,
             xo_ref, hn_ref, carry_scr, ext_scr, sv_scr, ycat_scr):
        i = pl.program_id(0)

        @pl.when(i == 0)
        def _():
            carry_scr[...] = jnp.zeros_like(carry_scr)

        x = x_ref[...]
        _, _, _, h = _rms_mod(x, gn_ref[...], mod_ref[3:4, :], mod_ref[4:5, :])
        proj = _dot_nt(h.astype(BF), wmi_ref[...])
        xa = proj[:, 0:DP]
        p = _pool_fwd(xa, carry_scr[...], ext_scr, _pool_counts(i * T, T), T)
        carry_scr[...] = xa[T - HALO:T, :]
        for gi in range(4):
            cols = slice(128 * gi, 128 * gi + 128)
            ycat_scr[:, cols] = (_dot(p[gi].astype(BF), wp_ref[gi]) * ps_ref[:, cols]).astype(BF)
        u, _ = _gelu(proj[:, DP:DP + DG])
        v, _ = _gelu(proj[:, DP + DG:DPROJ])
        mu = jnp.mean(v, axis=-1, keepdims=True)
        vc = v - mu
        rstd = lax.rsqrt(jnp.mean(vc * vc, axis=-1, keepdims=True) + EPS)
        vn = vc * rstd * lg_ref[...] + lb_ref[...]
        _sgu_fwd(vn.astype(BF), ws_ref, sv_scr, T)
        for n in range(T // CHUNK):
            rows = slice(n * CHUNK, (n + 1) * CHUNK)
            ycat_scr[rows, DP:D] = (u[rows, :] * (sv_scr[rows, :] + bias_ref[...])).astype(BF)
        xo = x + mod_ref[5:6, :] * _dot(ycat_scr[...], wmo_ref[...])
        xo_ref[...] = xo
        _, _, _, hn = _rms_mod(xo, gnn_ref[...], mod_ref[6:7, :], mod_ref[7:8, :])
        hn_ref[...] = hn.astype(BF)

    full = lambda shape: pl.BlockSpec(shape, lambda i: (0,) * len(shape))
    tile = pl.BlockSpec((T, D), lambda i: (i, 0))
    return pl.pallas_call(
        body, name=name, grid=(S // T,),
        out_shape=[jax.ShapeDtypeStruct((S, D), F32), jax.ShapeDtypeStruct((S, D), BF)],
        in_specs=[tile, full((9, D)), full((1, D)), full((1, D)), full((DPROJ, D)), full((D, D)),
                  full((4, 128, 128)), full((1, DP)), full((1, DG)), full((1, DG)), full((8, CHUNK, CHUNK)),
                  full((CHUNK, DG))],
        out_specs=[tile, tile],
        scratch_shapes=[pltpu.VMEM((HALO, DP), F32), pltpu.VMEM((T + HALO, DP), F32), pltpu.VMEM((T, DG), F32),
                        pltpu.VMEM((T, D), BF)],
        compiler_params=_cparams(48, ("arbitrary",)),
    )(x, mod, gn, gn_next, wmi, wmo, wp, ps, lg, lb, ws, bias)


def _mix_bwd(x, dxo, mod, gn, wmi, wmo, wp, ps, lg, lb, ws, bias, name, exchange=()):
    S = x.shape[0]
    T = min(T_MIX, S)
    nS = S // T
    hb = T // HALO
    ne = len(exchange)

    def body(*refs):
        (x_ref, xh_ref, dxo_ref, mod_ref, gn_ref, wmi_ref, wmo_ref, wp_ref, ps_ref, lg_ref, lb_ref, ws_ref,
         bias_ref) = refs[:13]
        sendbufs = refs[13:13 + ne]
        dxi_ref, gwmi_out, gwmo_out, gwp_ref, gws_ref, st_ref, vec_ref, dbias_ref = refs[13 + ne:21 + ne]
        recvbufs = refs[21 + ne:21 + 2 * ne]
        (carry_scr, ext_scr, qext_scr, sv_scr, dvn_scr, ycat_scr, dproj_scr, gwmi_ref,
         gwmo_ref) = refs[21 + 2 * ne:30 + 2 * ne]
        sems = refs[30 + 2 * ne:]
        i = pl.program_id(0)
        t = nS - 1 - i
        gn = gn_ref[...]
        shift, scale, gate = mod_ref[3:4, :], mod_ref[4:5, :], mod_ref[5:6, :]

        @pl.when(i == 0)
        def _():
            if ne:
                _exchange_phase("start", sendbufs, recvbufs, sems)
            carry_scr[...] = jnp.zeros_like(carry_scr)
            gwmi_ref[...] = jnp.zeros_like(gwmi_ref)
            gwmo_ref[...] = jnp.zeros_like(gwmo_ref)
            gwp_ref[...] = jnp.zeros_like(gwp_ref)
            gws_ref[...] = jnp.zeros_like(gws_ref)
            st_ref[...] = jnp.zeros_like(st_ref)
            vec_ref[...] = jnp.zeros_like(vec_ref)
            dbias_ref[...] = jnp.zeros_like(dbias_ref)

        x = x_ref[...]
        dxo = dxo_ref[...]
        r, xhat, n, h = _rms_mod(x, gn, shift, scale)
        hbf = h.astype(BF)
        proj = _dot_nt(hbf, wmi_ref[...])
        xa = proj[:, 0:DP]
        zu = proj[:, DP:DP + DG]
        zv = proj[:, DP + DG:DPROJ]
        _, _, _, hh = _rms_mod(xh_ref[...], gn, shift, scale)
        halo = _dot_nt(hh.astype(BF), wmi_ref[0:DP, :])
        halo = jnp.where(t == 0, 0.0, halo)
        cnts = _pool_counts(t * T, T)
        p = _pool_fwd(xa, halo, ext_scr, cnts, T)
        m = []
        for gi in range(4):
            cols = slice(128 * gi, 128 * gi + 128)
            m.append(_dot(p[gi].astype(BF), wp_ref[gi]))
            ycat_scr[:, cols] = (m[gi] * ps_ref[:, cols]).astype(BF)
        u, tu = _gelu(zu)
        v, tv = _gelu(zv)
        mu = jnp.mean(v, axis=-1, keepdims=True)
        vc = v - mu
        rstd = lax.rsqrt(jnp.mean(vc * vc, axis=-1, keepdims=True) + EPS)
        vhat = vc * rstd
        lg = lg_ref[...]
        vnb = (vhat * lg + lb_ref[...]).astype(BF)
        _sgu_fwd(vnb, ws_ref, sv_scr, T)
        for nck in range(T // CHUNK):
            rows = slice(nck * CHUNK, (nck + 1) * CHUNK)
            sv_scr[rows, :] = sv_scr[rows, :] + bias_ref[...]
        sv = sv_scr[...]
        ycat_scr[:, DP:D] = (u * sv).astype(BF)

        gwmo_ref[...] += _dot_tn(ycat_scr[...], dxo.astype(BF))
        dyc = _dot_nt((dxo * gate).astype(BF), wmo_ref[...])
        dya = dyc[:, 0:DP]
        dyb = dyc[:, DP:D]

        dps = []
        dp = []
        for gi in range(4):
            cols = slice(128 * gi, 128 * gi + 128)
            dps.append(jnp.sum(dya[:, cols] * m[gi], axis=0, keepdims=True))
            dm = (dya[:, cols] * ps_ref[:, cols]).astype(BF)
            gwp_ref[gi] += _dot_tn(p[gi].astype(BF), dm)
            dp.append(_dot_nt(dm, wp_ref[gi]))
            qext_scr[0:T, cols] = dp[gi] / cnts[gi]
        qext_scr[T:T + HALO, :] = carry_scr[...]
        for gi, w in enumerate(WINDOWS):
            cols = slice(128 * gi, 128 * gi + 128)
            acc = qext_scr[0:T, cols]
            for k in range(1, w):
                acc = acc + qext_scr[k:k + T, cols]
            dproj_scr[:, cols] = (acc - dp[gi]).astype(BF)
        carry_scr[...] = qext_scr[0:HALO, :]

        du = dyb * sv
        dsv = dyb * u
        lane = lax.broadcasted_iota(jnp.int32, (CHUNK, 128), 1)
        dbias = jnp.zeros((CHUNK, DG), F32)
        for nck in range(T // CHUNK):
            rows = slice(nck * CHUNK, (nck + 1) * CHUNK)
            dbias = dbias + dsv[rows, :]
            for b in range(DG // 128):
                cols = slice(128 * b, 128 * b + 128)
                dsvb = dsv[rows, cols]
                vb = vnb[rows, cols]
                gws_ref[2 * b] += _dot_nt(jnp.where(lane < 64, dsvb, 0.0).astype(BF), vb)
                gws_ref[2 * b + 1] += _dot_nt(jnp.where(lane < 64, 0.0, dsvb).astype(BF), vb)
                dsvbb = dsvb.astype(BF)
                dvn_scr[rows, cols] = jnp.where(lane < 64, _dot_tn(ws_ref[2 * b], dsvbb),
                                                _dot_tn(ws_ref[2 * b + 1], dsvbb))
        dbias_ref[...] += dbias
        dvn = dvn_scr[...]
        dlg = jnp.sum(dvn * vhat, axis=0, keepdims=True)
        dlb = jnp.sum(dvn, axis=0, keepdims=True)
        dvhat = dvn * lg
        dv = rstd * (dvhat - jnp.mean(dvhat, axis=-1, keepdims=True)
                     - vhat * jnp.mean(dvhat * vhat, axis=-1, keepdims=True))
        dproj_scr[:, DP:DP + DG] = (du * _gelu_grad(zu, tu)).astype(BF)
        dproj_scr[:, DP + DG:DPROJ] = (dv * _gelu_grad(zv, tv)).astype(BF)
        vec_ref[...] += _rows3(jnp.concatenate(dps, axis=1), dlg, dlb, DP)

        dproj = dproj_scr[...]
        gwmi_ref[...] += _dot_tn(dproj, hbf)
        dh = _dot(dproj, wmi_ref[...])
        dxi, dshift, dscale, dgn = _rms_mod_bwd(dh, dxo, r, xhat, n, gn, scale)
        dxi_ref[...] = dxi
        st_ref[...] += _rows3(dshift, dscale, dgn, D)

        @pl.when(i == nS - 1)
        def _():
            acc = gwmo_ref[...]
            dgate = jnp.sum(acc * wmo_ref[...].astype(F32), axis=0, keepdims=True)
            row = lax.broadcasted_iota(jnp.int32, (8, D), 0)
            st_ref[...] += jnp.where(row == 3, dgate, 0.0)
            gwmo_out[...] = (acc * gate).astype(BF)
            gwmi_out[...] = gwmi_ref[...].astype(BF)
            tt = lax.broadcasted_iota(jnp.int32, (CHUNK, CHUNK), 0)
            ss = lax.broadcasted_iota(jnp.int32, (CHUNK, CHUNK), 1)
            for hd in range(8):
                gws_ref[hd] = jnp.where(tt >= ss, gws_ref[hd], 0.0)
            if ne:
                _exchange_phase("wait", sendbufs, recvbufs, sems)

    full = lambda shape: pl.BlockSpec(shape, lambda i: (0,) * len(shape))
    return pl.pallas_call(
        body, name=name, grid=(nS,),
        out_shape=[jax.ShapeDtypeStruct((S, D), F32), jax.ShapeDtypeStruct((DPROJ, D), BF),
                   jax.ShapeDtypeStruct((D, D), BF), jax.ShapeDtypeStruct((4, 128, 128), F32),
                   jax.ShapeDtypeStruct((8, CHUNK, CHUNK), F32), jax.ShapeDtypeStruct((8, D), F32),
                   jax.ShapeDtypeStruct((8, DP), F32), jax.ShapeDtypeStruct((CHUNK, DG), F32)] + _like(exchange),
        in_specs=[pl.BlockSpec((T, D), lambda i: (nS - 1 - i, 0)),
                  pl.BlockSpec((HALO, D), lambda i: (jnp.maximum((nS - 1 - i) * hb - 1, 0), 0)),
                  pl.BlockSpec((T, D), lambda i: (nS - 1 - i, 0)),
                  full((9, D)), full((1, D)), full((DPROJ, D)), full((D, D)),
                  full((4, 128, 128)), full((1, DP)), full((1, DG)), full((1, DG)), full((8, CHUNK, CHUNK)),
                  full((CHUNK, DG))] + [HBM] * ne,
        out_specs=[pl.BlockSpec((T, D), lambda i: (nS - 1 - i, 0)), full((DPROJ, D)), full((D, D)),
                   full((4, 128, 128)), full((8, CHUNK, CHUNK)), full((8, D)), full((8, DP)), full((CHUNK, DG))]
                  + [HBM] * ne,
        scratch_shapes=[pltpu.VMEM((HALO, DP), F32), pltpu.VMEM((T + HALO, DP), F32),
                        pltpu.VMEM((T + HALO, DP), F32), pltpu.VMEM((T, DG), F32), pltpu.VMEM((T, DG), F32),
                        pltpu.VMEM((T, D), BF), pltpu.VMEM((T, DPROJ), BF), pltpu.VMEM((DPROJ, D), F32),
                        pltpu.VMEM((D, D), F32)] + (_comm_sems(ne) if ne else []),
        compiler_params=_cparams(56, ("arbitrary",)),
    )(x, x, dxo, mod, gn, wmi, wmo, wp, ps, lg, lb, ws, bias, *exchange)


def kernel(x, c, w_ada, b_ada, norm_ffn1_g, ffn1_w_in, ffn1_w_out, norm_mix_g, w_mix_in, w_pool, pool_scale, gmlp_ln_g, gmlp_ln_b, w_spatial, b_spatial, w_mix_out, norm_ffn2_g, ffn2_w_in, ffn2_w_out, norm_final_g, loss_target, m_w_ada, m_b_ada, m_norm_ffn1_g, m_ffn1_w_in, m_ffn1_w_out, m_norm_mix_g, m_w_mix_in, m_w_pool, m_pool_scale, m_gmlp_ln_g, m_gmlp_ln_b, m_w_spatial, m_b_spatial, m_w_mix_out, m_norm_ffn2_g, m_ffn2_w_in, m_ffn2_w_out, m_norm_final_g, v_w_ada, v_b_ada, v_norm_ffn1_g, v_ffn1_w_in, v_ffn1_w_out, v_norm_mix_g, v_w_mix_in, v_w_pool, v_pool_scale, v_gmlp_ln_g, v_gmlp_ln_b, v_w_spatial, v_b_spatial, v_w_mix_out, v_norm_ffn2_g, v_ffn2_w_in, v_ffn2_w_out, v_norm_final_g):
    weights = dict(w_ada=w_ada, b_ada=b_ada, norm_ffn1_g=norm_ffn1_g, ffn1_w_in=ffn1_w_in, ffn1_w_out=ffn1_w_out,
                   norm_mix_g=norm_mix_g, w_mix_in=w_mix_in, w_pool=w_pool, pool_scale=pool_scale,
                   gmlp_ln_g=gmlp_ln_g, gmlp_ln_b=gmlp_ln_b, w_spatial=w_spatial, b_spatial=b_spatial,
                   w_mix_out=w_mix_out, norm_ffn2_g=norm_ffn2_g, ffn2_w_in=ffn2_w_in, ffn2_w_out=ffn2_w_out,
                   norm_final_g=norm_final_g)
    mom1 = dict(w_ada=m_w_ada, b_ada=m_b_ada, norm_ffn1_g=m_norm_ffn1_g, ffn1_w_in=m_ffn1_w_in,
                ffn1_w_out=m_ffn1_w_out, norm_mix_g=m_norm_mix_g, w_mix_in=m_w_mix_in, w_pool=m_w_pool,
                pool_scale=m_pool_scale, gmlp_ln_g=m_gmlp_ln_g, gmlp_ln_b=m_gmlp_ln_b, w_spatial=m_w_spatial,
                b_spatial=m_b_spatial, w_mix_out=m_w_mix_out, norm_ffn2_g=m_norm_ffn2_g, ffn2_w_in=m_ffn2_w_in,
                ffn2_w_out=m_ffn2_w_out, norm_final_g=m_norm_final_g)
    mom2 = dict(w_ada=v_w_ada, b_ada=v_b_ada, norm_ffn1_g=v_norm_ffn1_g, ffn1_w_in=v_ffn1_w_in,
                ffn1_w_out=v_ffn1_w_out, norm_mix_g=v_norm_mix_g, w_mix_in=v_w_mix_in, w_pool=v_w_pool,
                pool_scale=v_pool_scale, gmlp_ln_g=v_gmlp_ln_g, gmlp_ln_b=v_gmlp_ln_b, w_spatial=v_w_spatial,
                b_spatial=v_b_spatial, w_mix_out=v_w_mix_out, norm_ffn2_g=v_norm_ffn2_g, ffn2_w_in=v_ffn2_w_in,
                ffn2_w_out=v_ffn2_w_out, norm_final_g=v_norm_final_g)
    order = list(weights)
    xs = x[0]
    target = loss_target[0]
    transposed = ("ffn1_w_in", "w_mix_in", "ffn2_w_in")
    big = ("ffn1_w_in", "ffn1_w_out", "w_mix_in", "w_mix_out", "ffn2_w_in", "ffn2_w_out")
    local = lambda a, k: a[0].T if k in transposed else a[0]
    wc = w_ada.shape[2]

    shard = dict(zip(big, _cast_shards([local(weights[k], k) for k in big])))
    mod, cact_all, h1, g_w1_in, g_w1_out = _ada_forward(
        jnp.broadcast_to(c, (8, D)), w_ada[0], b_ada.reshape(NDEV, wc), [shard["ffn1_w_in"], shard["ffn1_w_out"]],
        xs, norm_ffn1_g)
    w1_in = g_w1_in.reshape(2, F, D)
    w1_out = g_w1_out.reshape(F, D)

    x1, gu1, h1, g_wmi, g_wmo, g_w2_out, g_w2_in = _ffn_fwd(
        xs, mod, norm_ffn1_g, w1_in, w1_out, 0, "ffn1_fwd", h=h1,
        gather=[shard["w_mix_in"], shard["w_mix_out"], shard["ffn2_w_out"], shard["ffn2_w_in"]])
    wmi = g_wmi.reshape(DPROJ, D)
    wmo = g_wmo.reshape(D, D)
    w2_in = g_w2_in.reshape(2, F, D)
    w2_out = g_w2_out.reshape(F, D)
    tril = jnp.tril(jnp.ones((CHUNK, CHUNK), dtype=bool))
    ws_b = jnp.where(tril[None], w_spatial[0], 0.0).astype(BF)
    wp_b = w_pool[0].astype(BF)
    bias = jnp.repeat(b_spatial[0].T, DG // 8, axis=1)
    mix_args = (wmi, wmo, wp_b, pool_scale, gmlp_ln_g, gmlp_ln_b, ws_b, bias)
    x2, h3 = _mix_fwd(x1, mod, norm_mix_g, norm_ffn2_g, *mix_args, "mix_fwd")
    dx3, gu3, h3, st_f = _ffn_fwd(x2, mod, norm_ffn2_g, w2_in, w2_out, 2, "ffn2_fwd", h=h3,
                                  loss=(norm_final_g.reshape(1, D), target))

    slots = lambda a: a.reshape(NDEV, a.size // (NDEV * D), D)
    dgu3, d_w2_out, dgate3 = _ffn_bwd_hidden(dx3, mod, gu3, w2_out, 2, "ffn2_bwd_hidden")
    d_w2_in = _ffn_bwd_win(h3, dgu3, "ffn2_bwd_win")[0]
    dx2, st3 = _ffn_bwd_input(dgu3, w2_in, x2, dx3, mod, norm_ffn2_g, 2, "ffn2_bwd_input")
    dx1, d_wmi, d_wmo, d_wp, d_ws, st2, vec2, dbias, r_w2_in, r_w2_out = _mix_bwd(
        x1, dx2, mod, norm_mix_g, *mix_args, "mix_bwd", exchange=[slots(d_w2_in), slots(d_w2_out)])
    dgu1, d_w1_out, dgate1, r_wmi, r_wmo = _ffn_bwd_hidden(
        dx1, mod, gu1, w1_out, 0, "ffn1_bwd_hidden", exchange=[slots(d_wmi), slots(d_wmo)])
    d_w1_in, r_w1_out = _ffn_bwd_win(h1, dgu1, "ffn1_bwd_win", exchange=[slots(d_w1_out)])
    dx0, st1, r_w1_in = _ffn_bwd_input(dgu1, w1_in, xs, dx1, mod, norm_ffn1_g, 0, "ffn1_bwd_input",
                                       exchange=[slots(d_w1_in)])

    received = dict(ffn1_w_in=r_w1_in, ffn1_w_out=r_w1_out, w_mix_in=r_wmi, w_mix_out=r_wmo,
                    ffn2_w_in=r_w2_in, ffn2_w_out=r_w2_out)
    tiles = dict(ffn1_w_in=176, ffn1_w_out=176, w_mix_in=96, w_mix_out=128, ffn2_w_in=176, ffn2_w_out=176)
    result = {}
    for k, recv in received.items():
        res = _sum_adamw(recv, local(weights[k], k), local(mom1[k], k), local(mom2[k], k), tiles[k], "update_" + k)
        result[k] = tuple((a.T if k in transposed else a)[None] for a in res)
    row = lambda a: a.reshape(1, D)
    params = {k: (weights[k], mom1[k], mom2[k]) for k in SMALL}
    params["norm_final_g"] = (row(norm_final_g), row(m_norm_final_g), row(v_norm_final_g))
    tot, rsum, dmine = _small_reduce(d_ws, d_wp, dbias, st1, st2, st3, st_f, vec2, dgate1, dgate3)
    small, loss_row = _small_update(tot, rsum, params)
    result.update(small)
    result["norm_final_g"] = tuple(a.reshape(D) for a in small["norm_final_g"])
    result["w_ada"] = tuple(a[None] for a in _ada_update(cact_all, dmine, w_ada[0], m_w_ada[0], v_w_ada[0], 256))

    return (loss_row[0, 0], dx0[None], *[result[k][0] for k in order], *[result[k][1] for k in order],
            *[result[k][2] for k in order], *[result[k][3] for k in order])
```

```python
import math

import jax
import jax.numpy as jnp
from jax import lax
from jax.experimental import pallas as pl
from jax.experimental.pallas import tpu as pltpu

D = 1024
F = 2816
DP = 512
DG = 512
DPROJ = DP + 2 * DG
CHUNK = 128
WINDOWS = (2, 4, 8, 16)
HALO = 16
NDEV = 8
T_FFN = 512
T_MIX = 256
T_MIX_FWD = 512
T_WIN = 2048
EPS = 1e-6
LR, B1, B2, AEPS, WD, STEP = 0.001, 0.9, 0.999, 1e-08, 0.01, 10
BC1 = 1.0 - B1 ** STEP
BC2 = 1.0 - B2 ** STEP
GELU_C = math.sqrt(2.0 / math.pi)
GELU_A = 0.044715

BF = jnp.bfloat16
F32 = jnp.float32
MESH = pl.DeviceIdType.MESH
HBM = pl.BlockSpec(memory_space=pltpu.HBM)


def _whole(a):
    return pl.BlockSpec(a.shape, lambda i: (0,) * len(a.shape))


NT = (((1,), (1,)), ((), ()))
TN = (((0,), (0,)), ((), ()))


def _dot(a, b):
    return jnp.dot(a, b, preferred_element_type=F32)


def _dot_nt(a, b):
    return lax.dot_general(a, b, NT, preferred_element_type=F32)


def _dot_tn(a, b):
    return lax.dot_general(a, b, TN, preferred_element_type=F32)


def _cparams(vmem_mb, sem=None):
    kw = dict(vmem_limit_bytes=vmem_mb * 1024 * 1024)
    if sem is not None:
        kw["dimension_semantics"] = sem
    return pltpu.CompilerParams(**kw)


def _position():
    return lax.axis_index("x"), lax.axis_index("y"), lax.axis_index("c")


def _slot(p):
    return 4 * p[0] + 2 * p[1] + p[2]


def _flip(me, d):
    x, y, c = me
    return (1 - x if d & 4 else x, 1 - y if d & 2 else y, 1 - c if d & 1 else c)


def _remote(src, dst, send_sem, recv_sem, to):
    return pltpu.make_async_remote_copy(src_ref=src, dst_ref=dst, send_sem=send_sem, recv_sem=recv_sem,
                                        device_id=to, device_id_type=MESH)


def _comm_sems(n):
    return [pltpu.SemaphoreType.DMA((n, 7)), pltpu.SemaphoreType.DMA((n, 7)), pltpu.SemaphoreType.DMA((n,))]


def _gather_phase(phase, xs, outs, sems):
    send_sems, recv_sems, local_sems = sems
    n = len(xs)
    me = _position()
    x, y, c = me
    sibling = (x, y, 1 - c)
    xn, yn, diag = (1 - x, y), (x, 1 - y), (1 - x, 1 - y)
    relay_from = (x + c * (1 - 2 * x), y + (1 - c) * (1 - 2 * y))
    relay_to = (x + (1 - c) * (1 - 2 * x), y + c * (1 - 2 * y))

    def copy(a, k, block, to, src=None):
        dst = outs[a].at[_slot(block)]
        return _remote(dst if src is None else src, dst, send_sems.at[a, k], recv_sems.at[a, k], to)

    def mine(a):
        return pltpu.make_async_copy(xs[a], outs[a].at[_slot(me)], local_sems.at[a])

    def first(a):
        return [copy(a, 0, me, sibling, src=xs[a]), copy(a, 1, me, (*xn, c), src=xs[a]),
                copy(a, 2, me, (*yn, c), src=xs[a])]

    def second(a):
        return [copy(a, 3, (*relay_from, c), (*relay_to, c)), copy(a, 4, (*xn, c), sibling),
                copy(a, 5, (*yn, c), sibling)]

    def third(a):
        return copy(a, 6, (*diag, c), sibling)

    if phase == "start":
        for a in range(n):
            mine(a).start()
            for cp in first(a):
                cp.start()
    elif phase == "forward":
        for a in range(n):
            copy(a, 1, (*xn, c), me).wait_recv()
            copy(a, 2, (*yn, c), me).wait_recv()
            for cp in second(a):
                cp.start()
    else:
        for a in range(n):
            copy(a, 3, (*diag, c), me).wait_recv()
            third(a).start()
        for a in range(n):
            copy(a, 0, sibling, me).wait_recv()
            for k, chip in ((4, xn), (5, yn), (6, diag)):
                copy(a, k, (*chip, 1 - c), me).wait_recv()
        for a in range(n):
            for cp in first(a) + second(a) + [third(a)]:
                cp.wait_send()
            mine(a).wait()


def _exchange_phase(phase, xs, outs, sems):
    send_sems, recv_sems, local_sems = sems
    me = _position()
    for a in range(len(xs)):
        copies = [pltpu.make_async_copy(xs[a].at[_slot(me)], outs[a].at[_slot(me)], local_sems.at[a])]
        for d in range(1, NDEV):
            to = _flip(me, d)
            copies.append(_remote(xs[a].at[_slot(to)], outs[a].at[_slot(me)],
                                  send_sems.at[a, d - 1], recv_sems.at[a, d - 1], to))
        for cp in copies:
            if phase == "start":
                cp.start()
            else:
                cp.wait()


def _like(bufs):
    return [jax.ShapeDtypeStruct(b.shape, b.dtype) for b in bufs]


def _rms_mod(x, gn, shift, scale):
    ms = jnp.mean(x * x, axis=-1, keepdims=True)
    r = lax.rsqrt(ms + EPS)
    xhat = x * r
    n = xhat * gn
    h = n * (1.0 + scale) + shift
    return r, xhat, n, h


def _rms_mod_bwd(dh, dres, r, xhat, n, gn, scale):
    dshift = jnp.sum(dh, axis=0, keepdims=True)
    dscale = jnp.sum(dh * n, axis=0, keepdims=True)
    dn = dh * (1.0 + scale)
    dgn = jnp.sum(dn * xhat, axis=0, keepdims=True)
    dxhat = dn * gn
    dx = dres + r * (dxhat - xhat * jnp.mean(dxhat * xhat, axis=-1, keepdims=True))
    return dx, dshift, dscale, dgn


def _final_norm_loss(x, gf, target):
    r = lax.rsqrt(jnp.mean(x * x, axis=-1, keepdims=True) + EPS)
    xhat = x * r
    e = xhat * gf - target
    part = 0.5 * jnp.sum(jnp.sum(e * e, axis=-1, keepdims=True), axis=0, keepdims=True) / D
    dy = e / D
    dgf = jnp.sum(dy * xhat, axis=0, keepdims=True)
    dxhat = dy * gf
    dx = r * (dxhat - xhat * jnp.mean(dxhat * xhat, axis=-1, keepdims=True))
    return dx, dgf, part


def _rows3(a, b, c, width):
    row = lax.broadcasted_iota(jnp.int32, (8, width), 0)
    z = jnp.zeros((8, width), F32)
    return jnp.where(row == 0, a, z) + jnp.where(row == 1, b, z) + jnp.where(row == 2, c, z)


def _sigmoid(x):
    return 0.5 * jnp.tanh(0.5 * x) + 0.5


def _gelu(x):
    t = jnp.tanh(GELU_C * (x + GELU_A * x * x * x))
    return 0.5 * x * (1.0 + t), t


def _gelu_grad(x, t):
    return 0.5 * (1.0 + t) + 0.5 * x * (1.0 - t * t) * GELU_C * (1.0 + 3.0 * GELU_A * x * x)


def _adamw(w, g, m, v):
    m = B1 * m + (1.0 - B1) * g
    v = B2 * v + (1.0 - B2) * (g * g)
    m_hat = m / BC1
    v_hat = v / BC2
    delta = -LR * (m_hat / (jnp.sqrt(v_hat) + AEPS) + WD * w)
    return delta, m, v


def _cast_shards(shards):
    n = len(shards)

    def body(*refs):
        for a in range(n):
            refs[n + a][...] = refs[a][...].astype(BF)

    resident = pl.BlockSpec(memory_space=pltpu.VMEM)
    return pl.pallas_call(
        body, name="cast_shards", out_shape=[jax.ShapeDtypeStruct(s.shape, BF) for s in shards],
        in_specs=[resident] * n, out_specs=[resident] * n, compiler_params=_cparams(40),
    )(*shards)


def _ada_forward(c8, w_ada, b8, shards, x, gn):
    wc = w_ada.shape[1]
    n = len(shards)
    S = x.shape[0]
    T = min(T_FFN, S)
    nS = S // T

    def ada(c8_ref, w_ref, b8_ref, xs, mod_ref, cact_ref, gathered, call_ref, mall_ref, modp_ref, send_sems,
            recv_sems, gsems):
        me = _position()
        my = _slot(me)
        row = lax.broadcasted_iota(jnp.int32, (8, 1), 0)
        call_ref[my] = c8_ref[...]
        sends = []
        for d in range(1, NDEV):
            to = _flip(me, d)
            sends.append(_remote(call_ref.at[my], call_ref.at[my], send_sems.at[0, d - 1], recv_sems.at[0, d - 1], to))
        for cp in sends:
            cp.start()
        _gather_phase("start", xs, gathered, gsems)
        for cp in sends:
            cp.wait()
        c_all = jnp.zeros((8, D), F32)
        for k in range(NDEV):
            c_all = c_all + jnp.where(row == k, call_ref[k], 0.0)
        cact = c_all * jax.nn.sigmoid(c_all)
        cact_ref[...] = cact
        part = _dot(cact.astype(BF), w_ref[...].astype(BF))
        mall_ref[my] = part
        sends = []
        for d in range(1, NDEV):
            to = _flip(me, d)
            sends.append(_remote(mall_ref.at[my], mall_ref.at[my], send_sems.at[1, d - 1], recv_sems.at[1, d - 1], to))
        for cp in sends:
            cp.start()
        _gather_phase("forward", xs, gathered, gsems)
        for cp in sends:
            cp.wait()
        out = jnp.zeros((8, wc), F32)
        for k in range(NDEV):
            piece = jnp.sum(jnp.where(row == my, mall_ref[k], 0.0), axis=0, keepdims=True)
            out = out + jnp.where(row == k, piece, 0.0)
        modp_ref[...] = out + b8_ref[...]
        for q in range(9 * NDEV):
            mod_ref[q // 8:q // 8 + 1, 128 * (q % 8):128 * (q % 8 + 1)] = \
                modp_ref[q // 9:q // 9 + 1, 128 * (q % 9):128 * (q % 9 + 1)]

    def body(*refs):
        c8_ref, w_ref, b8_ref, gn_ref, x_ref = refs[:5]
        xs = refs[5:5 + n]
        mod_ref, cact_ref, h_ref = refs[5 + n:8 + n]
        gathered = refs[8 + n:8 + 2 * n]
        call_ref, mall_ref, modp_ref, send_sems, recv_sems = refs[8 + 2 * n:13 + 2 * n]
        gsems = refs[13 + 2 * n:]
        i = pl.program_id(0)

        @pl.when(i == 0)
        def _():
            ada(c8_ref, w_ref, b8_ref, xs, mod_ref, cact_ref, gathered, call_ref, mall_ref, modp_ref, send_sems,
                recv_sems, gsems)

        _, _, _, h = _rms_mod(x_ref[...], gn_ref[...], mod_ref[0:1, :], mod_ref[1:2, :])
        h_ref[...] = h.astype(BF)

        @pl.when(i == nS - 1)
        def _():
            _gather_phase("finish", xs, gathered, gsems)

    outs = [jax.ShapeDtypeStruct((9, D), F32), jax.ShapeDtypeStruct((8, D), F32)]
    tile = pl.BlockSpec((T, D), lambda i: (i, 0))
    return pl.pallas_call(
        body, name="ada_forward", grid=(nS,),
        out_shape=outs + [jax.ShapeDtypeStruct((S, D), BF)]
                  + [jax.ShapeDtypeStruct((NDEV,) + s.shape, s.dtype) for s in shards],
        in_specs=[_whole(a) for a in (c8, w_ada, b8, gn)] + [tile] + [HBM] * n,
        out_specs=[_whole(a) for a in outs] + [tile] + [HBM] * n,
        scratch_shapes=[pltpu.VMEM((NDEV, 8, D), F32), pltpu.VMEM((NDEV, 8, wc), F32), pltpu.VMEM((8, wc), F32),
                        pltpu.SemaphoreType.DMA((2, 7)), pltpu.SemaphoreType.DMA((2, 7))] + _comm_sems(n),
        compiler_params=_cparams(40, ("arbitrary",)),
    )(c8, w_ada, b8, gn, x, *shards)


MATS = ("w_spatial", "w_pool", "b_spatial")
VECS = ("norm_ffn1_g", "norm_mix_g", "norm_ffn2_g", "norm_final_g", "pool_scale", "gmlp_ln_g", "gmlp_ln_b", "b_ada")
VEC_WIDTH = dict(norm_ffn1_g=D, norm_mix_g=D, norm_ffn2_g=D, norm_final_g=D, pool_scale=DP, gmlp_ln_g=DG,
                 gmlp_ln_b=DG, b_ada=9 * D)
MAT_ROWS = 1600
MAT_SLICE = MAT_ROWS // NDEV
VEC_LANES = sum(VEC_WIDTH.values()) + 128
DMOD_AT = VEC_LANES - 128 - 9 * D
SMALL = MATS + VECS


def _small_reduce(g_ws, g_wp, dbias, st1, st2, st3, st_f, vec2, dgate1, dgate3):
    wc = 9 * D // NDEV

    def body(g_ws_ref, g_wp_ref, dbias_ref, st1_ref, st2_ref, st3_ref, stf_ref, vec2_ref, dg1_ref, dg3_ref,
             tot_ref, rsum_ref, dmine_ref,
             pack_ref, rs_ref, ag_ref, rv_ref, dmp_ref, dw_ref, send_sems, recv_sems):
        me = _position()
        my = _slot(me)

        pack_ref[0:1024, :] = g_ws_ref[...].reshape(1024, 128)
        pack_ref[1024:1536, :] = g_wp_ref[...].reshape(512, 128)
        ch = lax.broadcasted_iota(jnp.int32, (DG, 128), 0)
        hd = lax.broadcasted_iota(jnp.int32, (DG, 128), 1)
        sel = jnp.where(ch // 64 == hd, 1.0, 0.0).astype(F32)
        heads = jnp.dot(dbias_ref[...], sel, preferred_element_type=F32, precision=lax.Precision.HIGHEST)
        pack_ref[1536:1544, :] = heads.T[0:8, :]
        pack_ref[1544:MAT_ROWS, :] = jnp.zeros((MAT_ROWS - 1544, 128), F32)
        dgate1 = dg1_ref[0:1, :] + dg1_ref[8:9, :]
        dgate3 = dg3_ref[0:1, :] + dg3_ref[8:9, :]
        row = jnp.concatenate(
            [st1_ref[2:3, :], st2_ref[2:3, :], st3_ref[2:3, :], stf_ref[0:1, :],
             vec2_ref[0:1, :], vec2_ref[1:2, :], vec2_ref[2:3, :],
             st1_ref[0:1, :], st1_ref[1:2, :], dgate1, st2_ref[0:1, :], st2_ref[1:2, :], st2_ref[3:4, :],
             st3_ref[0:1, :], st3_ref[1:2, :], dgate3, stf_ref[1:2, 0:128]], axis=1)
        rv_ref[my] = row
        for k in range(NDEV):
            dmp_ref[k] = row[:, DMOD_AT + wc * k:DMOD_AT + wc * (k + 1)]
        dw_ref[my] = dmp_ref[my]
        rs_ref[my] = pack_ref[pl.ds(pl.multiple_of(my * MAT_SLICE, 8), MAT_SLICE), :]

        first = []
        for d in range(1, NDEV):
            to = _flip(me, d)
            theirs = pl.ds(pl.multiple_of(_slot(to) * MAT_SLICE, 8), MAT_SLICE)
            first.append(_remote(pack_ref.at[theirs, :], rs_ref.at[my], send_sems.at[0, d - 1], recv_sems.at[0, d - 1], to))
            first.append(_remote(dmp_ref.at[_slot(to)], dw_ref.at[my], send_sems.at[1, d - 1], recv_sems.at[1, d - 1], to))
            first.append(_remote(rv_ref.at[my], rv_ref.at[my], send_sems.at[2, d - 1], recv_sems.at[2, d - 1], to))
        for cp in first:
            cp.start()
        for cp in first:
            cp.wait()
        red = rs_ref[0]
        for k in range(1, NDEV):
            red = red + rs_ref[k]
        ag_ref[my] = red
        second = []
        for d in range(1, NDEV):
            to = _flip(me, d)
            second.append(_remote(ag_ref.at[my], ag_ref.at[my], send_sems.at[3, d - 1], recv_sems.at[3, d - 1], to))
        for cp in second:
            cp.start()

        rsum = rv_ref[0]
        for k in range(1, NDEV):
            rsum = rsum + rv_ref[k]
        rsum_ref[...] = rsum
        r8 = lax.broadcasted_iota(jnp.int32, (8, 1), 0)
        dmine = jnp.zeros((8, wc), F32)
        for k in range(NDEV):
            dmine = dmine + jnp.where(r8 == k, dw_ref[k], 0.0)
        dmine_ref[...] = dmine

        for cp in second:
            cp.wait()
        for k in range(NDEV):
            tot_ref[k * MAT_SLICE:(k + 1) * MAT_SLICE, :] = ag_ref[k]

    ins = (g_ws, g_wp, dbias, st1, st2, st3, st_f, vec2, dgate1, dgate3)
    outs = [jax.ShapeDtypeStruct((MAT_ROWS, 128), F32), jax.ShapeDtypeStruct((1, VEC_LANES), F32),
            jax.ShapeDtypeStruct((8, wc), F32)]
    return pl.pallas_call(
        body, name="small_reduce", grid=(1,), out_shape=outs,
        in_specs=[_whole(a) for a in ins], out_specs=[_whole(a) for a in outs],
        scratch_shapes=[pltpu.VMEM((MAT_ROWS, 128), F32), pltpu.VMEM((NDEV, MAT_SLICE, 128), F32),
                        pltpu.VMEM((NDEV, MAT_SLICE, 128), F32),
                        pltpu.VMEM((NDEV, 1, VEC_LANES), F32), pltpu.VMEM((NDEV, 1, wc), F32),
                        pltpu.VMEM((NDEV, 1, wc), F32),
                        pltpu.SemaphoreType.DMA((4, 7)), pltpu.SemaphoreType.DMA((4, 7))],
        compiler_params=_cparams(32, ("arbitrary",)),
    )(*ins)


def _small_update(tot, rsum, params):
    flat = [a for k in SMALL for a in params[k]]
    n_in = 2 + len(flat)

    def body(*refs):
        tot_ref, rsum_ref = refs[:2]
        p_hbm = refs[2:n_in]
        o_refs = refs[n_in:n_in + 4 * len(SMALL)]
        loss_ref = refs[n_in + 4 * len(SMALL)]
        p_refs = refs[n_in + 4 * len(SMALL) + 1:-1]
        sem = refs[-1]
        fetch = [pltpu.make_async_copy(p_hbm[k], p_refs[k], sem.at[k]) for k in range(len(flat))]
        for cp in fetch:
            cp.start()
        for cp in fetch:
            cp.wait()
        loss_ref[...] = rsum_ref[:, VEC_LANES - 128:VEC_LANES]

        def update(idx, g):
            w_ref, m_ref, v_ref = p_refs[3 * idx:3 * idx + 3]
            g_out, d_out, m_out, v_out = o_refs[4 * idx:4 * idx + 4]
            g = g.reshape(w_ref.shape)
            g_out[...] = g
            d_out[...], m_out[...], v_out[...] = _adamw(w_ref[...], g, m_ref[...], v_ref[...])

        update(0, tot_ref[0:1024, :])
        update(1, tot_ref[1024:1536, :])
        update(2, tot_ref[1536:1544, :])
        at = 0
        for idx, k in enumerate(VECS):
            update(3 + idx, rsum_ref[:, at:at + VEC_WIDTH[k]])
            at += VEC_WIDTH[k]

    outs = []
    for k in SMALL:
        outs += [jax.ShapeDtypeStruct(params[k][0].shape, F32)] * 4
    outs += [jax.ShapeDtypeStruct((1, 128), F32)]
    res = pl.pallas_call(
        body, name="small_update", grid=(1,), out_shape=outs,
        in_specs=[_whole(tot), _whole(rsum)] + [HBM] * len(flat), out_specs=[_whole(a) for a in outs],
        scratch_shapes=[pltpu.VMEM(a.shape, F32) for a in flat] + [pltpu.SemaphoreType.DMA((len(flat),))],
        compiler_params=_cparams(32, ("arbitrary",)),
    )(tot, rsum, *[pltpu.with_memory_space_constraint(a, pltpu.HBM) for a in flat])
    return {k: tuple(res[4 * i:4 * i + 4]) for i, k in enumerate(SMALL)}, res[-1]


def _sum_adamw(recv, w, m, v, tr, name, own=None):
    R, C = w.shape

    def body(*refs):
        r_ref = refs[0]
        o_ref = refs[1] if own is not None else None
        w_ref, m_ref, v_ref, g_ref, d_ref, nm_ref, nv_ref = refs[-7:]
        my = _slot(_position()) if own is not None else None

        def part(k):
            if own is None:
                return r_ref[k].astype(F32)
            return jnp.where(my == k, o_ref[k], r_ref[k]).astype(F32)

        g = part(0)
        for k in range(1, NDEV):
            g = g + part(k)
        g_ref[...] = g
        d_ref[...], nm_ref[...], nv_ref[...] = _adamw(w_ref[...], g, m_ref[...], v_ref[...])

    blk = pl.BlockSpec((tr, C), lambda i: (i, 0))
    slots = pl.BlockSpec((NDEV, tr, C), lambda i: (0, i, 0))
    out = jax.ShapeDtypeStruct((R, C), F32)
    bufs = (recv,) if own is None else (recv, own)
    return pl.pallas_call(
        body, name=name, grid=(R // tr,), out_shape=[out] * 4,
        in_specs=[slots] * len(bufs) + [blk, blk, blk], out_specs=[blk] * 4,
        compiler_params=_cparams(48, ("arbitrary",)),
    )(*bufs, w, m, v)


SEM = pl.BlockSpec(memory_space=pltpu.SEMAPHORE)
EFFECT = pltpu.SideEffectType.DATAFLOW_SIDE_EFFECTING


def _exchange_start(buf, name):
    def body(src_ref, land_ref, send_sems, recv_sems, src_thru, land_thru, token):
        me = _position()
        for d in range(1, NDEV):
            to = _flip(me, d)
            _remote(src_ref.at[_slot(to)], land_ref.at[_slot(me)], send_sems.at[d - 1], recv_sems.at[d - 1], to).start()
        token[...] = jnp.zeros_like(token)

    like = pltpu.HBM(buf.shape, buf.dtype)
    return pl.pallas_call(
        body, name=name,
        out_shape=(pltpu.SemaphoreType.DMA((NDEV - 1,)), pltpu.SemaphoreType.DMA((NDEV - 1,)), like, like,
                   jax.ShapeDtypeStruct((8, 128), F32)),
        in_specs=(HBM, HBM), out_specs=(SEM, SEM, HBM, HBM, pl.BlockSpec(memory_space=pltpu.VMEM)),
        input_output_aliases={0: 2, 1: 3},
        compiler_params=pltpu.CompilerParams(has_side_effects=EFFECT),
    )(pltpu.with_memory_space_constraint(buf, pltpu.HBM),
      pltpu.with_memory_space_constraint(lax.empty(buf.shape, buf.dtype), pltpu.HBM))


def _exchange_wait(send_sems, recv_sems, src_thru, land_thru, after, name):
    def body(src_ref, land_ref, send_sems, recv_sems, after_ref, src_out, land_out):
        me = _position()
        for d in range(1, NDEV):
            to = _flip(me, d)
            cp = _remote(src_ref.at[_slot(to)], land_ref.at[_slot(me)], send_sems.at[d - 1], recv_sems.at[d - 1], to)
            cp.wait_send()
            cp.wait_recv()

    like = pltpu.HBM(src_thru.shape, src_thru.dtype)
    return pl.pallas_call(
        body, name=name, out_shape=(like, like),
        in_specs=(HBM, HBM, SEM, SEM, pl.BlockSpec(memory_space=pl.ANY)), out_specs=(HBM, HBM),
        input_output_aliases={0: 0, 1: 1},
        compiler_params=pltpu.CompilerParams(has_side_effects=EFFECT),
    )(src_thru, land_thru, send_sems, recv_sems, after)


def _ada_update(cact_all, dmine, w, m, v, tr):
    R, C = w.shape

    def body(c_ref, dm_ref, w_ref, m_ref, v_ref, g_ref, d_ref, nm_ref, nv_ref):
        g = _dot_tn(c_ref[...].astype(BF), dm_ref[...].astype(BF))
        g_ref[...] = g
        d_ref[...], nm_ref[...], nv_ref[...] = _adamw(w_ref[...], g, m_ref[...], v_ref[...])

    blk = pl.BlockSpec((tr, C), lambda i: (i, 0))
    out = jax.ShapeDtypeStruct((R, C), F32)
    return pl.pallas_call(
        body, name="update_w_ada", grid=(R // tr,), out_shape=[out] * 4,
        in_specs=[pl.BlockSpec((8, tr), lambda i: (0, i)), pl.BlockSpec((8, C), lambda i: (0, 0)), blk, blk, blk],
        out_specs=[blk] * 4,
        compiler_params=_cparams(48, ("arbitrary",)),
    )(cact_all, dmine, w, m, v)


FC = F // 2


def _ffn_fwd(x, mod, gn, w_in_t, w_out, sub, name, gather=(), loss=None, h=None):
    S = x.shape[0]
    T = min(T_FFN, S)
    nS, nJ = S // T, F // FC
    ng = len(gather)
    nl = 2 if loss else 0
    nh = 0 if h is None else 1
    forward_step = nS // 2

    def body(*refs):
        x_ref, mod_ref, gn_ref, wg_ref, wu_ref, wo_ref = refs[:6]
        gf_ref, t_ref = refs[6 + nh:6 + nh + nl] if loss else (None, None)
        shards = refs[6 + nh + nl:6 + nh + nl + ng]
        at = 6 + nh + nl + ng
        xo_ref, gu_ref = refs[at:at + 2]
        h_ref = refs[6] if nh else refs[at + 2]
        at += 3 - nh
        gathered = refs[at:at + ng]
        at += ng
        st_ref = refs[at] if loss else None
        at += nl // 2
        acc_scr = refs[at]
        sems = refs[at + 1:]
        i, j = pl.program_id(0), pl.program_id(1)

        if ng:
            @pl.when((i == 0) & (j == 0))
            def _():
                _gather_phase("start", shards, gathered, sems)

            @pl.when((i == forward_step) & (j == 0))
            def _():
                _gather_phase("forward", shards, gathered, sems)

        @pl.when(j == 0)
        def _():
            if not nh:
                _, _, _, hh = _rms_mod(x_ref[...], gn_ref[...], mod_ref[3 * sub:3 * sub + 1, :],
                                       mod_ref[3 * sub + 1:3 * sub + 2, :])
                h_ref[...] = hh.astype(BF)
            acc_scr[...] = jnp.zeros_like(acc_scr)

        hb = h_ref[...]
        g = _dot_nt(hb, wg_ref[0])
        u = _dot_nt(hb, wu_ref[0])
        gu_ref[0] = g.astype(BF)
        gu_ref[1] = u.astype(BF)
        a = (g * _sigmoid(g) * u).astype(BF)
        acc_scr[...] += _dot(a, wo_ref[...])

        @pl.when(j == nJ - 1)
        def _():
            xo = x_ref[...] + (0.5 * mod_ref[3 * sub + 2:3 * sub + 3, :]) * acc_scr[...]
            if not loss:
                xo_ref[...] = xo
            else:
                dx, dgf, part = _final_norm_loss(xo, gf_ref[...], t_ref[...])
                xo_ref[...] = dx
                upd = _rows3(dgf, jnp.broadcast_to(part, (1, D)), jnp.zeros((1, D), F32), D)

                @pl.when(i == 0)
                def _():
                    st_ref[...] = upd

                @pl.when(i > 0)
                def _():
                    st_ref[...] += upd

        if ng:
            @pl.when((i == nS - 1) & (j == nJ - 1))
            def _():
                _gather_phase("finish", shards, gathered, sems)

    tile = pl.BlockSpec((T, D), lambda i, j: (i, 0))
    res = pl.pallas_call(
        body, name=name, grid=(nS, nJ),
        out_shape=[jax.ShapeDtypeStruct((S, D), F32), jax.ShapeDtypeStruct((2, S, F), BF)]
                  + ([] if nh else [jax.ShapeDtypeStruct((S, D), BF)])
                  + [jax.ShapeDtypeStruct((NDEV,) + s.shape, s.dtype) for s in gather]
                  + ([jax.ShapeDtypeStruct((8, D), F32)] if loss else []),
        in_specs=[tile,
                  pl.BlockSpec((9, D), lambda i, j: (0, 0)),
                  pl.BlockSpec((1, D), lambda i, j: (0, 0)),
                  pl.BlockSpec((1, FC, D), lambda i, j: (0, j, 0)),
                  pl.BlockSpec((1, FC, D), lambda i, j: (1, j, 0)),
                  pl.BlockSpec((FC, D), lambda i, j: (j, 0))] + [tile] * nh
                 + ([pl.BlockSpec((1, D), lambda i, j: (0, 0)), tile] if loss else []) + [HBM] * ng,
        out_specs=[tile, pl.BlockSpec((2, T, FC), lambda i, j: (0, i, j))] + [tile] * (1 - nh) + [HBM] * ng
                  + ([pl.BlockSpec((8, D), lambda i, j: (0, 0))] if loss else []),
        scratch_shapes=[pltpu.VMEM((T, D), F32)] + (_comm_sems(ng) if ng else []),
        compiler_params=_cparams(56, ("arbitrary", "arbitrary")),
    )(x, mod, gn, w_in_t, w_in_t, w_out, *(() if h is None else (h,)), *(loss or ()), *gather)
    return res if h is None else [res[0], res[1], h, *res[2:]]


def _ffn_bwd_hidden(dx, mod, gu, w_out, sub, name, exchange=()):
    S = dx.shape[0]
    T = min(T_FFN, S)
    nS, nJ = S // T, F // FC
    ne = len(exchange)

    def body(*refs):
        dx_ref, mod_ref, gu_ref, wo_ref = refs[:4]
        sendbufs = refs[4:4 + ne]
        dgu_ref, gw_ref, dgate_ref = refs[4 + ne:7 + ne]
        recvbufs = refs[7 + ne:7 + 2 * ne]
        acc_scr = refs[7 + 2 * ne]
        sems = refs[8 + 2 * ne:]
        j, i = pl.program_id(0), pl.program_id(1)

        if ne:
            @pl.when((i == 0) & (j == 0))
            def _():
                _exchange_phase("start", sendbufs, recvbufs, sems)

        gate = mod_ref[3 * sub + 2:3 * sub + 3, :]
        dx = dx_ref[...]
        da = _dot_nt((dx * (0.5 * gate)).astype(BF), wo_ref[...])
        g = gu_ref[0].astype(F32)
        u = gu_ref[1].astype(F32)
        sg = _sigmoid(g)
        s = g * sg
        dgu_ref[0] = (da * u * (sg * (1.0 + g * (1.0 - sg)))).astype(BF)
        dgu_ref[1] = (da * s).astype(BF)
        contrib = _dot_tn((s * u).astype(BF), dx.astype(BF))

        @pl.when(i == 0)
        def _():
            acc_scr[...] = contrib

        @pl.when(i > 0)
        def _():
            acc_scr[...] += contrib

        @pl.when(i == nS - 1)
        def _():
            acc = acc_scr[...]
            dgate = 0.5 * jnp.sum(acc * wo_ref[...].astype(F32), axis=0, keepdims=True)
            dgate_ref[...] = jnp.broadcast_to(dgate, (8, D))
            gw_ref[...] = (acc * (0.5 * gate)).astype(BF)

        if ne:
            @pl.when((i == nS - 1) & (j == nJ - 1))
            def _():
                _exchange_phase("wait", sendbufs, recvbufs, sems)

    return pl.pallas_call(
        body, name=name, grid=(nJ, nS),
        out_shape=[jax.ShapeDtypeStruct((2, S, F), BF), jax.ShapeDtypeStruct((F, D), BF),
                   jax.ShapeDtypeStruct((8 * nJ, D), F32)] + _like(exchange),
        in_specs=[pl.BlockSpec((T, D), lambda j, i: (i, 0)),
                  pl.BlockSpec((9, D), lambda j, i: (0, 0)),
                  pl.BlockSpec((2, T, FC), lambda j, i: (0, i, j)),
                  pl.BlockSpec((FC, D), lambda j, i: (j, 0))] + [HBM] * ne,
        out_specs=[pl.BlockSpec((2, T, FC), lambda j, i: (0, i, j)),
                   pl.BlockSpec((FC, D), lambda j, i: (j, 0)),
                   pl.BlockSpec((8, D), lambda j, i: (j, 0))] + [HBM] * ne,
        scratch_shapes=[pltpu.VMEM((FC, D), F32)] + (_comm_sems(ne) if ne else []),
        compiler_params=_cparams(56, ("arbitrary", "arbitrary")),
    )(dx, mod, gu, w_out, *exchange)


def _ffn_bwd_input(dgu, w_in_t, x, dx, mod, gn, sub, name, exchange=()):
    S = x.shape[0]
    T = min(T_FFN, S)
    nS = S // T
    ne = len(exchange)
    NC = 256
    chunks = [slice(k * NC, (k + 1) * NC) for k in range(D // NC)]

    def body(*refs):
        dgu_ref, w_ref, x_ref, dx_ref, mod_ref, gn_ref = refs[:6]
        sendbufs = refs[6:6 + ne]
        dxin_ref, st_ref = refs[6 + ne:8 + ne]
        recvbufs = refs[8 + ne:8 + 2 * ne]
        dxh_scr = refs[8 + 2 * ne]
        sems = refs[9 + 2 * ne:]
        i = pl.program_id(0)

        if ne:
            @pl.when(i == 0)
            def _():
                _exchange_phase("start", sendbufs, recvbufs, sems)

        gn = gn_ref[...]
        scale = mod_ref[3 * sub + 1:3 * sub + 2, :]
        r, xhat, n, _ = _rms_mod(x_ref[...], gn, mod_ref[3 * sub:3 * sub + 1, :], scale)
        dg = dgu_ref[0]
        du = dgu_ref[1]
        rowsum = jnp.zeros((T, 1), F32)
        dshift, dscale, dgn = [], [], []
        for cols in chunks:
            dh = _dot(dg, w_ref[0, :, cols]) + _dot(du, w_ref[1, :, cols])
            dshift.append(jnp.sum(dh, axis=0, keepdims=True))
            dscale.append(jnp.sum(dh * n[:, cols], axis=0, keepdims=True))
            dn = dh * (1.0 + scale[:, cols])
            dgn.append(jnp.sum(dn * xhat[:, cols], axis=0, keepdims=True))
            dxhat = dn * gn[:, cols]
            rowsum = rowsum + jnp.sum(dxhat * xhat[:, cols], axis=-1, keepdims=True)
            dxh_scr[:, cols] = dxhat
        dxin_ref[...] = dx_ref[...] + r * (dxh_scr[...] - xhat * (rowsum / D))
        cat = lambda parts: jnp.concatenate(parts, axis=1)
        upd = _rows3(cat(dshift), cat(dscale), cat(dgn), D)

        @pl.when(i == 0)
        def _():
            st_ref[...] = upd

        @pl.when(i > 0)
        def _():
            st_ref[...] += upd

        if ne:
            @pl.when(i == nS - 1)
            def _():
                _exchange_phase("wait", sendbufs, recvbufs, sems)

    tile = pl.BlockSpec((T, D), lambda i: (i, 0))
    return pl.pallas_call(
        body, name=name, grid=(nS,),
        out_shape=[jax.ShapeDtypeStruct((S, D), F32), jax.ShapeDtypeStruct((8, D), F32)] + _like(exchange),
        in_specs=[pl.BlockSpec((2, T, F), lambda i: (0, i, 0)),
                  pl.BlockSpec((2, F, D), lambda i: (0, 0, 0), pipeline_mode=pl.Buffered(1)),
                  tile, tile,
                  pl.BlockSpec((9, D), lambda i: (0, 0)),
                  pl.BlockSpec((1, D), lambda i: (0, 0))] + [HBM] * ne,
        out_specs=[tile, pl.BlockSpec((8, D), lambda i: (0, 0))] + [HBM] * ne,
        scratch_shapes=[pltpu.VMEM((T, D), F32)] + (_comm_sems(ne) if ne else []),
        compiler_params=_cparams(60, ("arbitrary",)),
    )(dgu, w_in_t, x, dx, mod, gn, *exchange)


def _ffn_bwd_win(h, dgu, name, exchange=()):
    S = h.shape[0]
    T = min(T_WIN, S)
    nS, nJ = S // T, F // FC
    ne = len(exchange)

    def body(*refs):
        h_ref, dgu_ref = refs[:2]
        sendbufs = refs[2:2 + ne]
        out_ref = refs[2 + ne]
        recvbufs = refs[3 + ne:3 + 2 * ne]
        acc_scr = refs[3 + 2 * ne]
        sems = refs[4 + 2 * ne:]
        p, j, i = pl.program_id(0), pl.program_id(1), pl.program_id(2)

        if ne:
            @pl.when((p == 0) & (j == 0) & (i == 0))
            def _():
                _exchange_phase("start", sendbufs, recvbufs, sems)

        contrib = _dot_tn(dgu_ref[0], h_ref[...])

        @pl.when(i == 0)
        def _():
            acc_scr[...] = contrib

        @pl.when(i > 0)
        def _():
            acc_scr[...] += contrib

        @pl.when(i == nS - 1)
        def _():
            out_ref[0] = acc_scr[...].astype(BF)

        if ne:
            @pl.when((p == 1) & (j == nJ - 1) & (i == nS - 1))
            def _():
                _exchange_phase("wait", sendbufs, recvbufs, sems)

    return pl.pallas_call(
        body, name=name, grid=(2, nJ, nS),
        out_shape=[jax.ShapeDtypeStruct((2, F, D), BF)] + _like(exchange),
        in_specs=[pl.BlockSpec((T, D), lambda p, j, i: (i, 0)),
                  pl.BlockSpec((1, T, FC), lambda p, j, i: (p, i, j))] + [HBM] * ne,
        out_specs=[pl.BlockSpec((1, FC, D), lambda p, j, i: (p, j, 0))] + [HBM] * ne,
        scratch_shapes=[pltpu.VMEM((FC, D), F32)] + (_comm_sems(ne) if ne else []),
        compiler_params=_cparams(56, ("arbitrary", "arbitrary", "arbitrary")),
    )(h, dgu, *exchange)


def _pool_counts(pos0, T):
    pos = pos0 + lax.broadcasted_iota(jnp.int32, (T, 1), 0)
    return [jnp.minimum(pos + 1, w).astype(F32) for w in WINDOWS]


def _pool_fwd(xa, halo, ext_scr, cnts, T):
    ext_scr[0:HALO, :] = halo
    ext_scr[HALO:HALO + T, :] = xa
    out = []
    for gi, w in enumerate(WINDOWS):
        cols = slice(128 * gi, 128 * gi + 128)
        acc = xa[:, cols]
        for k in range(1, w):
            acc = acc + ext_scr[HALO - k:HALO - k + T, cols]
        out.append(acc / cnts[gi] - xa[:, cols])
    return out


def _sgu_fwd(vnb, ws_ref, sv_scr, T):
    lane = lax.broadcasted_iota(jnp.int32, (CHUNK, 128), 1)
    for n in range(T // CHUNK):
        rows = slice(n * CHUNK, (n + 1) * CHUNK)
        for b in range(DG // 128):
            cols = slice(128 * b, 128 * b + 128)
            vb = vnb[rows, cols]
            sv_scr[rows, cols] = jnp.where(lane < 64, _dot(ws_ref[2 * b], vb), _dot(ws_ref[2 * b + 1], vb))


def _mix_fwd(x, mod, gn, gn_next, wmi, wmo, wp, ps, lg, lb, ws, bias, name):
    S = x.shape[0]
    T = min(T_MIX_FWD, S)

    def body(x_ref, mod_ref, gn_ref, gnn_ref, wmi_ref, wmo_ref, wp_ref, ps_ref, lg_ref, lb_ref, ws_ref, bias_ref,
             xo_ref, hn_ref, carry_scr, ext_scr, sv_scr, ycat_scr):
        i = pl.program_id(0)

        @pl.when(i == 0)
        def _():
            carry_scr[...] = jnp.zeros_like(carry_scr)

        x = x_ref[...]
        _, _, _, h = _rms_mod(x, gn_ref[...], mod_ref[3:4, :], mod_ref[4:5, :])
        proj = _dot_nt(h.astype(BF), wmi_ref[...])
        xa = proj[:, 0:DP]
        p = _pool_fwd(xa, carry_scr[...], ext_scr, _pool_counts(i * T, T), T)
        carry_scr[...] = xa[T - HALO:T, :]
        for gi in range(4):
            cols = slice(128 * gi, 128 * gi + 128)
            ycat_scr[:, cols] = (_dot(p[gi].astype(BF), wp_ref[gi]) * ps_ref[:, cols]).astype(BF)
        u, _ = _gelu(proj[:, DP:DP + DG])
        v, _ = _gelu(proj[:, DP + DG:DPROJ])
        mu = jnp.mean(v, axis=-1, keepdims=True)
        vc = v - mu
        rstd = lax.rsqrt(jnp.mean(vc * vc, axis=-1, keepdims=True) + EPS)
        vn = vc * rstd * lg_ref[...] + lb_ref[...]
        _sgu_fwd(vn.astype(BF), ws_ref, sv_scr, T)
        for n in range(T // CHUNK):
            rows = slice(n * CHUNK, (n + 1) * CHUNK)
            ycat_scr[rows, DP:D] = (u[rows, :] * (sv_scr[rows, :] + bias_ref[...])).astype(BF)
        xo = x + mod_ref[5:6, :] * _dot(ycat_scr[...], wmo_ref[...])
        xo_ref[...] = xo
        _, _, _, hn = _rms_mod(xo, gnn_ref[...], mod_ref[6:7, :], mod_ref[7:8, :])
        hn_ref[...] = hn.astype(BF)

    full = lambda shape: pl.BlockSpec(shape, lambda i: (0,) * len(shape))
    tile = pl.BlockSpec((T, D), lambda i: (i, 0))
    return pl.pallas_call(
        body, name=name, grid=(S // T,),
        out_shape=[jax.ShapeDtypeStruct((S, D), F32), jax.ShapeDtypeStruct((S, D), BF)],
        in_specs=[tile, full((9, D)), full((1, D)), full((1, D)), full((DPROJ, D)), full((D, D)),
                  full((4, 128, 128)), full((1, DP)), full((1, DG)), full((1, DG)), full((8, CHUNK, CHUNK)),
                  full((CHUNK, DG))],
        out_specs=[tile, tile],
        scratch_shapes=[pltpu.VMEM((HALO, DP), F32), pltpu.VMEM((T + HALO, DP), F32), pltpu.VMEM((T, DG), F32),
                        pltpu.VMEM((T, D), BF)],
        compiler_params=_cparams(48, ("arbitrary",)),
    )(x, mod, gn, gn_next, wmi, wmo, wp, ps, lg, lb, ws, bias)


def _mix_bwd(x, dxo, mod, gn, wmi, wmo, wp, ps, lg, lb, ws, bias, name, exchange=()):
    S = x.shape[0]
    T = min(T_MIX, S)
    nS = S // T
    hb = T // HALO
    ne = len(exchange)

    def body(*refs):
        (x_ref, xh_ref, dxo_ref, mod_ref, gn_ref, wmi_ref, wmo_ref, wp_ref, ps_ref, lg_ref, lb_ref, ws_ref,
         bias_ref) = refs[:13]
        sendbufs = refs[13:13 + ne]
        dxi_ref, gwmi_out, gwmo_out, gwp_ref, gws_ref, st_ref, vec_ref, dbias_ref = refs[13 + ne:21 + ne]
        recvbufs = refs[21 + ne:21 + 2 * ne]
        (carry_scr, ext_scr, qext_scr, sv_scr, dvn_scr, ycat_scr, dproj_scr, gwmi_ref,
         gwmo_ref) = refs[21 + 2 * ne:30 + 2 * ne]
        sems = refs[30 + 2 * ne:]
        i = pl.program_id(0)
        t = nS - 1 - i
        gn = gn_ref[...]
        shift, scale, gate = mod_ref[3:4, :], mod_ref[4:5, :], mod_ref[5:6, :]

        @pl.when(i == 0)
        def _():
            if ne:
                _exchange_phase("start", sendbufs, recvbufs, sems)
            carry_scr[...] = jnp.zeros_like(carry_scr)
            gwmi_ref[...] = jnp.zeros_like(gwmi_ref)
            gwmo_ref[...] = jnp.zeros_like(gwmo_ref)
            gwp_ref[...] = jnp.zeros_like(gwp_ref)
            gws_ref[...] = jnp.zeros_like(gws_ref)
            st_ref[...] = jnp.zeros_like(st_ref)
            vec_ref[...] = jnp.zeros_like(vec_ref)
            dbias_ref[...] = jnp.zeros_like(dbias_ref)

        x = x_ref[...]
        dxo = dxo_ref[...]
        r, xhat, n, h = _rms_mod(x, gn, shift, scale)
        hbf = h.astype(BF)
        proj = _dot_nt(hbf, wmi_ref[...])
        xa = proj[:, 0:DP]
        zu = proj[:, DP:DP + DG]
        zv = proj[:, DP + DG:DPROJ]
        _, _, _, hh = _rms_mod(xh_ref[...], gn, shift, scale)
        halo = _dot_nt(hh.astype(BF), wmi_ref[0:DP, :])
        halo = jnp.where(t == 0, 0.0, halo)
        cnts = _pool_counts(t * T, T)
        p = _pool_fwd(xa, halo, ext_scr, cnts, T)
        m = []
        for gi in range(4):
            cols = slice(128 * gi, 128 * gi + 128)
            m.append(_dot(p[gi].astype(BF), wp_ref[gi]))
            ycat_scr[:, cols] = (m[gi] * ps_ref[:, cols]).astype(BF)
        u, tu = _gelu(zu)
        v, tv = _gelu(zv)
        mu = jnp.mean(v, axis=-1, keepdims=True)
        vc = v - mu
        rstd = lax.rsqrt(jnp.mean(vc * vc, axis=-1, keepdims=True) + EPS)
        vhat = vc * rstd
        lg = lg_ref[...]
        vnb = (vhat * lg + lb_ref[...]).astype(BF)
        _sgu_fwd(vnb, ws_ref, sv_scr, T)
        for nck in range(T // CHUNK):
            rows = slice(nck * CHUNK, (nck + 1) * CHUNK)
            sv_scr[rows, :] = sv_scr[rows, :] + bias_ref[...]
        sv = sv_scr[...]
        ycat_scr[:, DP:D] = (u * sv).astype(BF)

        gwmo_ref[...] += _dot_tn(ycat_scr[...], dxo.astype(BF))
        dyc = _dot_nt((dxo * gate).astype(BF), wmo_ref[...])
        dya = dyc[:, 0:DP]
        dyb = dyc[:, DP:D]

        dps = []
        dp = []
        for gi in range(4):
            cols = slice(128 * gi, 128 * gi + 128)
            dps.append(jnp.sum(dya[:, cols] * m[gi], axis=0, keepdims=True))
            dm = (dya[:, cols] * ps_ref[:, cols]).astype(BF)
            gwp_ref[gi] += _dot_tn(p[gi].astype(BF), dm)
            dp.append(_dot_nt(dm, wp_ref[gi]))
            qext_scr[0:T, cols] = dp[gi] / cnts[gi]
        qext_scr[T:T + HALO, :] = carry_scr[...]
        for gi, w in enumerate(WINDOWS):
            cols = slice(128 * gi, 128 * gi + 128)
            acc = qext_scr[0:T, cols]
            for k in range(1, w):
                acc = acc + qext_scr[k:k + T, cols]
            dproj_scr[:, cols] = (acc - dp[gi]).astype(BF)
        carry_scr[...] = qext_scr[0:HALO, :]

        du = dyb * sv
        dsv = dyb * u
        lane = lax.broadcasted_iota(jnp.int32, (CHUNK, 128), 1)
        dbias = jnp.zeros((CHUNK, DG), F32)
        for nck in range(T // CHUNK):
            rows = slice(nck * CHUNK, (nck + 1) * CHUNK)
            dbias = dbias + dsv[rows, :]
            for b in range(DG // 128):
                cols = slice(128 * b, 128 * b + 128)
                dsvb = dsv[rows, cols]
                vb = vnb[rows, cols]
                gws_ref[2 * b] += _dot_nt(jnp.where(lane < 64, dsvb, 0.0).astype(BF), vb)
                gws_ref[2 * b + 1] += _dot_nt(jnp.where(lane < 64, 0.0, dsvb).astype(BF), vb)
                dsvbb = dsvb.astype(BF)
                dvn_scr[rows, cols] = jnp.where(lane < 64, _dot_tn(ws_ref[2 * b], dsvbb),
                                                _dot_tn(ws_ref[2 * b + 1], dsvbb))
        dbias_ref[...] += dbias
        dvn = dvn_scr[...]
        dlg = jnp.sum(dvn * vhat, axis=0, keepdims=True)
        dlb = jnp.sum(dvn, axis=0, keepdims=True)
        dvhat = dvn * lg
        dv = rstd * (dvhat - jnp.mean(dvhat, axis=-1, keepdims=True)
                     - vhat * jnp.mean(dvhat * vhat, axis=-1, keepdims=True))
        dproj_scr[:, DP:DP + DG] = (du * _gelu_grad(zu, tu)).astype(BF)
        dproj_scr[:, DP + DG:DPROJ] = (dv * _gelu_grad(zv, tv)).astype(BF)
        vec_ref[...] += _rows3(jnp.concatenate(dps, axis=1), dlg, dlb, DP)

        dproj = dproj_scr[...]
        gwmi_ref[...] += _dot_tn(dproj, hbf)
        dh = _dot(dproj, wmi_ref[...])
        dxi, dshift, dscale, dgn = _rms_mod_bwd(dh, dxo, r, xhat, n, gn, scale)
        dxi_ref[...] = dxi
        st_ref[...] += _rows3(dshift, dscale, dgn, D)

        @pl.when(i == nS - 1)
        def _():
            acc = gwmo_ref[...]
            dgate = jnp.sum(acc * wmo_ref[...].astype(F32), axis=0, keepdims=True)
            row = lax.broadcasted_iota(jnp.int32, (8, D), 0)
            st_ref[...] += jnp.where(row == 3, dgate, 0.0)
            gwmo_out[...] = (acc * gate).astype(BF)
            gwmi_out[...] = gwmi_ref[...].astype(BF)
            tt = lax.broadcasted_iota(jnp.int32, (CHUNK, CHUNK), 0)
            ss = lax.broadcasted_iota(jnp.int32, (CHUNK, CHUNK), 1)
            for hd in range(8):
                gws_ref[hd] = jnp.where(tt >= ss, gws_ref[hd], 0.0)
            if ne:
                _exchange_phase("wait", sendbufs, recvbufs, sems)

    full = lambda shape: pl.BlockSpec(shape, lambda i: (0,) * len(shape))
    return pl.pallas_call(
        body, name=name, grid=(nS,),
        out_shape=[jax.ShapeDtypeStruct((S, D), F32), jax.ShapeDtypeStruct((DPROJ, D), BF),
                   jax.ShapeDtypeStruct((D, D), BF), jax.ShapeDtypeStruct((4, 128, 128), F32),
                   jax.ShapeDtypeStruct((8, CHUNK, CHUNK), F32), jax.ShapeDtypeStruct((8, D), F32),
                   jax.ShapeDtypeStruct((8, DP), F32), jax.ShapeDtypeStruct((CHUNK, DG), F32)] + _like(exchange),
        in_specs=[pl.BlockSpec((T, D), lambda i: (nS - 1 - i, 0)),
                  pl.BlockSpec((HALO, D), lambda i: (jnp.maximum((nS - 1 - i) * hb - 1, 0), 0)),
                  pl.BlockSpec((T, D), lambda i: (nS - 1 - i, 0)),
                  full((9, D)), full((1, D)), full((DPROJ, D)), full((D, D)),
                  full((4, 128, 128)), full((1, DP)), full((1, DG)), full((1, DG)), full((8, CHUNK, CHUNK)),
                  full((CHUNK, DG))] + [HBM] * ne,
        out_specs=[pl.BlockSpec((T, D), lambda i: (nS - 1 - i, 0)), full((DPROJ, D)), full((D, D)),
                   full((4, 128, 128)), full((8, CHUNK, CHUNK)), full((8, D)), full((8, DP)), full((CHUNK, DG))]
                  + [HBM] * ne,
        scratch_shapes=[pltpu.VMEM((HALO, DP), F32), pltpu.VMEM((T + HALO, DP), F32),
                        pltpu.VMEM((T + HALO, DP), F32), pltpu.VMEM((T, DG), F32), pltpu.VMEM((T, DG), F32),
                        pltpu.VMEM((T, D), BF), pltpu.VMEM((T, DPROJ), BF), pltpu.VMEM((DPROJ, D), F32),
                        pltpu.VMEM((D, D), F32)] + (_comm_sems(ne) if ne else []),
        compiler_params=_cparams(56, ("arbitrary",)),
    )(x, x, dxo, mod, gn, wmi, wmo, wp, ps, lg, lb, ws, bias, *exchange)


def kernel(x, c, w_ada, b_ada, norm_ffn1_g, ffn1_w_in, ffn1_w_out, norm_mix_g, w_mix_in, w_pool, pool_scale, gmlp_ln_g, gmlp_ln_b, w_spatial, b_spatial, w_mix_out, norm_ffn2_g, ffn2_w_in, ffn2_w_out, norm_final_g, loss_target, m_w_ada, m_b_ada, m_norm_ffn1_g, m_ffn1_w_in, m_ffn1_w_out, m_norm_mix_g, m_w_mix_in, m_w_pool, m_pool_scale, m_gmlp_ln_g, m_gmlp_ln_b, m_w_spatial, m_b_spatial, m_w_mix_out, m_norm_ffn2_g, m_ffn2_w_in, m_ffn2_w_out, m_norm_final_g, v_w_ada, v_b_ada, v_norm_ffn1_g, v_ffn1_w_in, v_ffn1_w_out, v_norm_mix_g, v_w_mix_in, v_w_pool, v_pool_scale, v_gmlp_ln_g, v_gmlp_ln_b, v_w_spatial, v_b_spatial, v_w_mix_out, v_norm_ffn2_g, v_ffn2_w_in, v_ffn2_w_out, v_norm_final_g):
    weights = dict(w_ada=w_ada, b_ada=b_ada, norm_ffn1_g=norm_ffn1_g, ffn1_w_in=ffn1_w_in, ffn1_w_out=ffn1_w_out,
                   norm_mix_g=norm_mix_g, w_mix_in=w_mix_in, w_pool=w_pool, pool_scale=pool_scale,
                   gmlp_ln_g=gmlp_ln_g, gmlp_ln_b=gmlp_ln_b, w_spatial=w_spatial, b_spatial=b_spatial,
                   w_mix_out=w_mix_out, norm_ffn2_g=norm_ffn2_g, ffn2_w_in=ffn2_w_in, ffn2_w_out=ffn2_w_out,
                   norm_final_g=norm_final_g)
    mom1 = dict(w_ada=m_w_ada, b_ada=m_b_ada, norm_ffn1_g=m_norm_ffn1_g, ffn1_w_in=m_ffn1_w_in,
                ffn1_w_out=m_ffn1_w_out, norm_mix_g=m_norm_mix_g, w_mix_in=m_w_mix_in, w_pool=m_w_pool,
                pool_scale=m_pool_scale, gmlp_ln_g=m_gmlp_ln_g, gmlp_ln_b=m_gmlp_ln_b, w_spatial=m_w_spatial,
                b_spatial=m_b_spatial, w_mix_out=m_w_mix_out, norm_ffn2_g=m_norm_ffn2_g, ffn2_w_in=m_ffn2_w_in,
                ffn2_w_out=m_ffn2_w_out, norm_final_g=m_norm_final_g)
    mom2 = dict(w_ada=v_w_ada, b_ada=v_b_ada, norm_ffn1_g=v_norm_ffn1_g, ffn1_w_in=v_ffn1_w_in,
                ffn1_w_out=v_ffn1_w_out, norm_mix_g=v_norm_mix_g, w_mix_in=v_w_mix_in, w_pool=v_w_pool,
                pool_scale=v_pool_scale, gmlp_ln_g=v_gmlp_ln_g, gmlp_ln_b=v_gmlp_ln_b, w_spatial=v_w_spatial,
                b_spatial=v_b_spatial, w_mix_out=v_w_mix_out, norm_ffn2_g=v_norm_ffn2_g, ffn2_w_in=v_ffn2_w_in,
                ffn2_w_out=v_ffn2_w_out, norm_final_g=v_norm_final_g)
    order = list(weights)
    xs = x[0]
    target = loss_target[0]
    transposed = ("ffn1_w_in", "w_mix_in", "ffn2_w_in")
    big = ("ffn1_w_in", "ffn1_w_out", "w_mix_in", "w_mix_out", "ffn2_w_in", "ffn2_w_out")
    local = lambda a, k: a[0].T if k in transposed else a[0]
    wc = w_ada.shape[2]

    shard = dict(zip(big, _cast_shards([local(weights[k], k) for k in big])))
    mod, cact_all, h1, g_w1_in, g_w1_out = _ada_forward(
        jnp.broadcast_to(c, (8, D)), w_ada[0], b_ada.reshape(NDEV, wc), [shard["ffn1_w_in"], shard["ffn1_w_out"]],
        xs, norm_ffn1_g)
    w1_in = g_w1_in.reshape(2, F, D)
    w1_out = g_w1_out.reshape(F, D)

    x1, gu1, h1, g_wmi, g_wmo, g_w2_out, g_w2_in = _ffn_fwd(
        xs, mod, norm_ffn1_g, w1_in, w1_out, 0, "ffn1_fwd", h=h1,
        gather=[shard["w_mix_in"], shard["w_mix_out"], shard["ffn2_w_out"], shard["ffn2_w_in"]])
    wmi = g_wmi.reshape(DPROJ, D)
    wmo = g_wmo.reshape(D, D)
    w2_in = g_w2_in.reshape(2, F, D)
    w2_out = g_w2_out.reshape(F, D)
    tril = jnp.tril(jnp.ones((CHUNK, CHUNK), dtype=bool))
    ws_b = jnp.where(tril[None], w_spatial[0], 0.0).astype(BF)
    wp_b = w_pool[0].astype(BF)
    bias = jnp.repeat(b_spatial[0].T, DG // 8, axis=1)
    mix_args = (wmi, wmo, wp_b, pool_scale, gmlp_ln_g, gmlp_ln_b, ws_b, bias)
    x2, h3 = _mix_fwd(x1, mod, norm_mix_g, norm_ffn2_g, *mix_args, "mix_fwd")
    dx3, gu3, h3, st_f = _ffn_fwd(x2, mod, norm_ffn2_g, w2_in, w2_out, 2, "ffn2_fwd", h=h3,
                                  loss=(norm_final_g.reshape(1, D), target))

    slots = lambda a: a.reshape(NDEV, a.size // (NDEV * D), D)
    dgu3, d_w2_out, dgate3 = _ffn_bwd_hidden(dx3, mod, gu3, w2_out, 2, "ffn2_bwd_hidden")
    d_w2_in = _ffn_bwd_win(h3, dgu3, "ffn2_bwd_win")[0]
    dx2, st3 = _ffn_bwd_input(dgu3, w2_in, x2, dx3, mod, norm_ffn2_g, 2, "ffn2_bwd_input")
    dx1, d_wmi, d_wmo, d_wp, d_ws, st2, vec2, dbias, r_w2_in, r_w2_out = _mix_bwd(
        x1, dx2, mod, norm_mix_g, *mix_args, "mix_bwd", exchange=[slots(d_w2_in), slots(d_w2_out)])
    dgu1, d_w1_out, dgate1, r_wmi, r_wmo = _ffn_bwd_hidden(
        dx1, mod, gu1, w1_out, 0, "ffn1_bwd_hidden", exchange=[slots(d_wmi), slots(d_wmo)])
    d_w1_in, r_w1_out = _ffn_bwd_win(h1, dgu1, "ffn1_bwd_win", exchange=[slots(d_w1_out)])
    send_sems, recv_sems, sent, landing, token = _exchange_start(slots(d_w1_in), "w_in_grad_start")
    dx0, st1 = _ffn_bwd_input(dgu1, w1_in, xs, dx1, mod + token[0, 0], norm_ffn1_g, 0, "ffn1_bwd_input")

    received = dict(ffn1_w_out=r_w1_out, w_mix_in=r_wmi, w_mix_out=r_wmo, ffn2_w_in=r_w2_in, ffn2_w_out=r_w2_out)
    tiles = dict(ffn1_w_in=176, ffn1_w_out=176, w_mix_in=96, w_mix_out=128, ffn2_w_in=176, ffn2_w_out=176)
    result = {}

    def update(k, recv, own=None):
        res = _sum_adamw(recv, local(weights[k], k), local(mom1[k], k), local(mom2[k], k), tiles[k], "update_" + k,
                         own=own)
        result[k] = tuple((a.T if k in transposed else a)[None] for a in res)

    for k, recv in received.items():
        update(k, recv)
    row = lambda a: a.reshape(1, D)
    params = {k: (weights[k], mom1[k], mom2[k]) for k in SMALL}
    params["norm_final_g"] = (row(norm_final_g), row(m_norm_final_g), row(v_norm_final_g))
    tot, rsum, dmine = _small_reduce(d_ws, d_wp, dbias, st1, st2, st3, st_f, vec2, dgate1, dgate3)
    sent, landing = _exchange_wait(send_sems, recv_sems, sent, landing, rsum, "w_in_grad_wait")
    update("ffn1_w_in", landing, own=sent)
    small, loss_row = _small_update(tot, rsum, params)
    result.update(small)
    result["norm_final_g"] = tuple(a.reshape(D) for a in small["norm_final_g"])
    result["w_ada"] = tuple(a[None] for a in _ada_update(cact_all, dmine, w_ada[0], m_w_ada[0], v_w_ada[0], 256))

    return (loss_row[0, 0], dx0[None], *[result[k][0] for k in order], *[result[k][1] for k in order],
            *[result[k][2] for k in order], *[result[k][3] for k in order])
```

```python
import math

import jax
import jax.numpy as jnp
from jax import lax
from jax.experimental import pallas as pl
from jax.experimental.pallas import tpu as pltpu

D = 1024
F = 2816
DP = 512
DG = 512
DPROJ = DP + 2 * DG
CHUNK = 128
WINDOWS = (2, 4, 8, 16)
HALO = 16
NDEV = 8
T_FFN = 512
T_MIX = 256
T_MIX_FWD = 512
T_WIN = 2048
EPS = 1e-6
LR, B1, B2, AEPS, WD, STEP = 0.001, 0.9, 0.999, 1e-08, 0.01, 10
BC1 = 1.0 - B1 ** STEP
BC2 = 1.0 - B2 ** STEP
GELU_C = math.sqrt(2.0 / math.pi)
GELU_A = 0.044715

BF = jnp.bfloat16
F32 = jnp.float32
MESH = pl.DeviceIdType.MESH
HBM = pl.BlockSpec(memory_space=pltpu.HBM)


def _whole(a):
    return pl.BlockSpec(a.shape, lambda i: (0,) * len(a.shape))


NT = (((1,), (1,)), ((), ()))
TN = (((0,), (0,)), ((), ()))


def _dot(a, b):
    return jnp.dot(a, b, preferred_element_type=F32)


def _dot_nt(a, b):
    return lax.dot_general(a, b, NT, preferred_element_type=F32)


def _dot_tn(a, b):
    return lax.dot_general(a, b, TN, preferred_element_type=F32)


def _cparams(vmem_mb, sem=None):
    kw = dict(vmem_limit_bytes=vmem_mb * 1024 * 1024)
    if sem is not None:
        kw["dimension_semantics"] = sem
    return pltpu.CompilerParams(**kw)


def _position():
    return lax.axis_index("x"), lax.axis_index("y"), lax.axis_index("c")


def _slot(p):
    return 4 * p[0] + 2 * p[1] + p[2]


def _flip(me, d):
    x, y, c = me
    return (1 - x if d & 4 else x, 1 - y if d & 2 else y, 1 - c if d & 1 else c)


def _remote(src, dst, send_sem, recv_sem, to):
    return pltpu.make_async_remote_copy(src_ref=src, dst_ref=dst, send_sem=send_sem, recv_sem=recv_sem,
                                        device_id=to, device_id_type=MESH)


def _comm_sems(n):
    return [pltpu.SemaphoreType.DMA((n, 7)), pltpu.SemaphoreType.DMA((n, 7)), pltpu.SemaphoreType.DMA((n,))]


def _gather_phase(phase, xs, outs, sems):
    send_sems, recv_sems, local_sems = sems
    n = len(xs)
    me = _position()
    x, y, c = me
    sibling = (x, y, 1 - c)
    xn, yn, diag = (1 - x, y), (x, 1 - y), (1 - x, 1 - y)
    relay_from = (x + c * (1 - 2 * x), y + (1 - c) * (1 - 2 * y))
    relay_to = (x + (1 - c) * (1 - 2 * x), y + c * (1 - 2 * y))

    def copy(a, k, block, to, src=None):
        dst = outs[a].at[_slot(block)]
        return _remote(dst if src is None else src, dst, send_sems.at[a, k], recv_sems.at[a, k], to)

    def mine(a):
        return pltpu.make_async_copy(xs[a], outs[a].at[_slot(me)], local_sems.at[a])

    def first(a):
        return [copy(a, 0, me, sibling, src=xs[a]), copy(a, 1, me, (*xn, c), src=xs[a]),
                copy(a, 2, me, (*yn, c), src=xs[a])]

    def second(a):
        return [copy(a, 3, (*relay_from, c), (*relay_to, c)), copy(a, 4, (*xn, c), sibling),
                copy(a, 5, (*yn, c), sibling)]

    def third(a):
        return copy(a, 6, (*diag, c), sibling)

    if phase == "start":
        for a in range(n):
            mine(a).start()
            for cp in first(a):
                cp.start()
    elif phase == "forward":
        for a in range(n):
            copy(a, 1, (*xn, c), me).wait_recv()
            copy(a, 2, (*yn, c), me).wait_recv()
            for cp in second(a):
                cp.start()
    else:
        for a in range(n):
            copy(a, 3, (*diag, c), me).wait_recv()
            third(a).start()
        for a in range(n):
            copy(a, 0, sibling, me).wait_recv()
            for k, chip in ((4, xn), (5, yn), (6, diag)):
                copy(a, k, (*chip, 1 - c), me).wait_recv()
        for a in range(n):
            for cp in first(a) + second(a) + [third(a)]:
                cp.wait_send()
            mine(a).wait()


def _exchange_phase(phase, xs, outs, sems):
    send_sems, recv_sems, local_sems = sems
    me = _position()
    for a in range(len(xs)):
        copies = [pltpu.make_async_copy(xs[a].at[_slot(me)], outs[a].at[_slot(me)], local_sems.at[a])]
        for d in range(1, NDEV):
            to = _flip(me, d)
            copies.append(_remote(xs[a].at[_slot(to)], outs[a].at[_slot(me)],
                                  send_sems.at[a, d - 1], recv_sems.at[a, d - 1], to))
        for cp in copies:
            if phase == "start":
                cp.start()
            else:
                cp.wait()


def _like(bufs):
    return [jax.ShapeDtypeStruct(b.shape, b.dtype) for b in bufs]


def _rms_mod(x, gn, shift, scale):
    ms = jnp.mean(x * x, axis=-1, keepdims=True)
    r = lax.rsqrt(ms + EPS)
    xhat = x * r
    n = xhat * gn
    h = n * (1.0 + scale) + shift
    return r, xhat, n, h


def _rms_mod_bwd(dh, dres, r, xhat, n, gn, scale):
    dshift = jnp.sum(dh, axis=0, keepdims=True)
    dscale = jnp.sum(dh * n, axis=0, keepdims=True)
    dn = dh * (1.0 + scale)
    dgn = jnp.sum(dn * xhat, axis=0, keepdims=True)
    dxhat = dn * gn
    dx = dres + r * (dxhat - xhat * jnp.mean(dxhat * xhat, axis=-1, keepdims=True))
    return dx, dshift, dscale, dgn


def _final_norm_loss(x, gf, target):
    r = lax.rsqrt(jnp.mean(x * x, axis=-1, keepdims=True) + EPS)
    xhat = x * r
    e = xhat * gf - target
    part = 0.5 * jnp.sum(jnp.sum(e * e, axis=-1, keepdims=True), axis=0, keepdims=True) / D
    dy = e / D
    dgf = jnp.sum(dy * xhat, axis=0, keepdims=True)
    dxhat = dy * gf
    dx = r * (dxhat - xhat * jnp.mean(dxhat * xhat, axis=-1, keepdims=True))
    return dx, dgf, part


def _rows3(a, b, c, width):
    row = lax.broadcasted_iota(jnp.int32, (8, width), 0)
    z = jnp.zeros((8, width), F32)
    return jnp.where(row == 0, a, z) + jnp.where(row == 1, b, z) + jnp.where(row == 2, c, z)


def _sigmoid(x):
    return 0.5 * jnp.tanh(0.5 * x) + 0.5


def _gelu(x):
    t = jnp.tanh(GELU_C * (x + GELU_A * x * x * x))
    return 0.5 * x * (1.0 + t), t


def _gelu_grad(x, t):
    return 0.5 * (1.0 + t) + 0.5 * x * (1.0 - t * t) * GELU_C * (1.0 + 3.0 * GELU_A * x * x)


def _adamw(w, g, m, v):
    m = B1 * m + (1.0 - B1) * g
    v = B2 * v + (1.0 - B2) * (g * g)
    m_hat = m / BC1
    v_hat = v / BC2
    delta = -LR * (m_hat / (jnp.sqrt(v_hat) + AEPS) + WD * w)
    return delta, m, v


def _cast_shards(shards):
    n = len(shards)

    def body(*refs):
        for a in range(n):
            refs[n + a][...] = refs[a][...].astype(BF)

    resident = pl.BlockSpec(memory_space=pltpu.VMEM)
    return pl.pallas_call(
        body, name="cast_shards", out_shape=[jax.ShapeDtypeStruct(s.shape, BF) for s in shards],
        in_specs=[resident] * n, out_specs=[resident] * n, compiler_params=_cparams(40),
    )(*shards)


def _ada_forward(c8, w_ada, b8, shards, x, gn):
    wc = w_ada.shape[1]
    n = len(shards)
    S = x.shape[0]
    T = min(T_FFN, S)
    nS = S // T

    def ada(c8_ref, w_ref, b8_ref, xs, mod_ref, cact_ref, gathered, call_ref, mall_ref, modp_ref, send_sems,
            recv_sems, gsems):
        me = _position()
        my = _slot(me)
        row = lax.broadcasted_iota(jnp.int32, (8, 1), 0)
        call_ref[my] = c8_ref[...]
        sends = []
        for d in range(1, NDEV):
            to = _flip(me, d)
            sends.append(_remote(call_ref.at[my], call_ref.at[my], send_sems.at[0, d - 1], recv_sems.at[0, d - 1], to))
        for cp in sends:
            cp.start()
        _gather_phase("start", xs, gathered, gsems)
        for cp in sends:
            cp.wait()
        c_all = jnp.zeros((8, D), F32)
        for k in range(NDEV):
            c_all = c_all + jnp.where(row == k, call_ref[k], 0.0)
        cact = c_all * jax.nn.sigmoid(c_all)
        cact_ref[...] = cact
        part = _dot(cact.astype(BF), w_ref[...].astype(BF))
        mall_ref[my] = part
        sends = []
        for d in range(1, NDEV):
            to = _flip(me, d)
            sends.append(_remote(mall_ref.at[my], mall_ref.at[my], send_sems.at[1, d - 1], recv_sems.at[1, d - 1], to))
        for cp in sends:
            cp.start()
        _gather_phase("forward", xs, gathered, gsems)
        for cp in sends:
            cp.wait()
        out = jnp.zeros((8, wc), F32)
        for k in range(NDEV):
            piece = jnp.sum(jnp.where(row == my, mall_ref[k], 0.0), axis=0, keepdims=True)
            out = out + jnp.where(row == k, piece, 0.0)
        modp_ref[...] = out + b8_ref[...]
        for q in range(9 * NDEV):
            mod_ref[q // 8:q // 8 + 1, 128 * (q % 8):128 * (q % 8 + 1)] = \
                modp_ref[q // 9:q // 9 + 1, 128 * (q % 9):128 * (q % 9 + 1)]

    def body(*refs):
        c8_ref, w_ref, b8_ref, gn_ref, x_ref = refs[:5]
        xs = refs[5:5 + n]
        mod_ref, cact_ref, h_ref = refs[5 + n:8 + n]
        gathered = refs[8 + n:8 + 2 * n]
        call_ref, mall_ref, modp_ref, send_sems, recv_sems = refs[8 + 2 * n:13 + 2 * n]
        gsems = refs[13 + 2 * n:]
        i = pl.program_id(0)

        @pl.when(i == 0)
        def _():
            ada(c8_ref, w_ref, b8_ref, xs, mod_ref, cact_ref, gathered, call_ref, mall_ref, modp_ref, send_sems,
                recv_sems, gsems)

        _, _, _, h = _rms_mod(x_ref[...], gn_ref[...], mod_ref[0:1, :], mod_ref[1:2, :])
        h_ref[...] = h.astype(BF)

        @pl.when(i == nS - 1)
        def _():
            _gather_phase("finish", xs, gathered, gsems)

    outs = [jax.ShapeDtypeStruct((9, D), F32), jax.ShapeDtypeStruct((8, D), F32)]
    tile = pl.BlockSpec((T, D), lambda i: (i, 0))
    return pl.pallas_call(
        body, name="ada_forward", grid=(nS,),
        out_shape=outs + [jax.ShapeDtypeStruct((S, D), BF)]
                  + [jax.ShapeDtypeStruct((NDEV,) + s.shape, s.dtype) for s in shards],
        in_specs=[_whole(a) for a in (c8, w_ada, b8, gn)] + [tile] + [HBM] * n,
        out_specs=[_whole(a) for a in outs] + [tile] + [HBM] * n,
        scratch_shapes=[pltpu.VMEM((NDEV, 8, D), F32), pltpu.VMEM((NDEV, 8, wc), F32), pltpu.VMEM((8, wc), F32),
                        pltpu.SemaphoreType.DMA((2, 7)), pltpu.SemaphoreType.DMA((2, 7))] + _comm_sems(n),
        compiler_params=_cparams(40, ("arbitrary",)),
    )(c8, w_ada, b8, gn, x, *shards)


MATS = ("w_spatial", "w_pool", "b_spatial")
VECS = ("norm_ffn1_g", "norm_mix_g", "norm_ffn2_g", "norm_final_g", "pool_scale", "gmlp_ln_g", "gmlp_ln_b", "b_ada")
VEC_WIDTH = dict(norm_ffn1_g=D, norm_mix_g=D, norm_ffn2_g=D, norm_final_g=D, pool_scale=DP, gmlp_ln_g=DG,
                 gmlp_ln_b=DG, b_ada=9 * D)
MAT_ROWS = 1600
MAT_SLICE = MAT_ROWS // NDEV
VEC_LANES = sum(VEC_WIDTH.values()) + 128
DMOD_AT = VEC_LANES - 128 - 9 * D
SMALL = MATS + VECS


def _small_reduce(g_ws, g_wp, dbias, st1, st2, st3, st_f, vec2, dgate1, dgate3):
    wc = 9 * D // NDEV

    def body(g_ws_ref, g_wp_ref, dbias_ref, st1_ref, st2_ref, st3_ref, stf_ref, vec2_ref, dg1_ref, dg3_ref,
             tot_ref, rsum_ref, dmine_ref,
             pack_ref, rs_ref, ag_ref, rv_ref, dmp_ref, dw_ref, send_sems, recv_sems):
        me = _position()
        my = _slot(me)

        pack_ref[0:1024, :] = g_ws_ref[...].reshape(1024, 128)
        pack_ref[1024:1536, :] = g_wp_ref[...].reshape(512, 128)
        ch = lax.broadcasted_iota(jnp.int32, (DG, 128), 0)
        hd = lax.broadcasted_iota(jnp.int32, (DG, 128), 1)
        sel = jnp.where(ch // 64 == hd, 1.0, 0.0).astype(F32)
        heads = jnp.dot(dbias_ref[...], sel, preferred_element_type=F32, precision=lax.Precision.HIGHEST)
        pack_ref[1536:1544, :] = heads.T[0:8, :]
        pack_ref[1544:MAT_ROWS, :] = jnp.zeros((MAT_ROWS - 1544, 128), F32)
        dgate1 = dg1_ref[0:1, :] + dg1_ref[8:9, :]
        dgate3 = dg3_ref[0:1, :] + dg3_ref[8:9, :]
        row = jnp.concatenate(
            [st1_ref[2:3, :], st2_ref[2:3, :], st3_ref[2:3, :], stf_ref[0:1, :],
             vec2_ref[0:1, :], vec2_ref[1:2, :], vec2_ref[2:3, :],
             st1_ref[0:1, :], st1_ref[1:2, :], dgate1, st2_ref[0:1, :], st2_ref[1:2, :], st2_ref[3:4, :],
             st3_ref[0:1, :], st3_ref[1:2, :], dgate3, stf_ref[1:2, 0:128]], axis=1)
        rv_ref[my] = row
        for k in range(NDEV):
            dmp_ref[k] = row[:, DMOD_AT + wc * k:DMOD_AT + wc * (k + 1)]
        dw_ref[my] = dmp_ref[my]
        rs_ref[my] = pack_ref[pl.ds(pl.multiple_of(my * MAT_SLICE, 8), MAT_SLICE), :]

        first = []
        for d in range(1, NDEV):
            to = _flip(me, d)
            theirs = pl.ds(pl.multiple_of(_slot(to) * MAT_SLICE, 8), MAT_SLICE)
            first.append(_remote(pack_ref.at[theirs, :], rs_ref.at[my], send_sems.at[0, d - 1], recv_sems.at[0, d - 1], to))
            first.append(_remote(dmp_ref.at[_slot(to)], dw_ref.at[my], send_sems.at[1, d - 1], recv_sems.at[1, d - 1], to))
            first.append(_remote(rv_ref.at[my], rv_ref.at[my], send_sems.at[2, d - 1], recv_sems.at[2, d - 1], to))
        for cp in first:
            cp.start()
        for cp in first:
            cp.wait()
        red = rs_ref[0]
        for k in range(1, NDEV):
            red = red + rs_ref[k]
        ag_ref[my] = red
        second = []
        for d in range(1, NDEV):
            to = _flip(me, d)
            second.append(_remote(ag_ref.at[my], ag_ref.at[my], send_sems.at[3, d - 1], recv_sems.at[3, d - 1], to))
        for cp in second:
            cp.start()

        rsum = rv_ref[0]
        for k in range(1, NDEV):
            rsum = rsum + rv_ref[k]
        rsum_ref[...] = rsum
        r8 = lax.broadcasted_iota(jnp.int32, (8, 1), 0)
        dmine = jnp.zeros((8, wc), F32)
        for k in range(NDEV):
            dmine = dmine + jnp.where(r8 == k, dw_ref[k], 0.0)
        dmine_ref[...] = dmine

        for cp in second:
            cp.wait()
        for k in range(NDEV):
            tot_ref[k * MAT_SLICE:(k + 1) * MAT_SLICE, :] = ag_ref[k]

    ins = (g_ws, g_wp, dbias, st1, st2, st3, st_f, vec2, dgate1, dgate3)
    outs = [jax.ShapeDtypeStruct((MAT_ROWS, 128), F32), jax.ShapeDtypeStruct((1, VEC_LANES), F32),
            jax.ShapeDtypeStruct((8, wc), F32)]
    return pl.pallas_call(
        body, name="small_reduce", grid=(1,), out_shape=outs,
        in_specs=[_whole(a) for a in ins], out_specs=[_whole(a) for a in outs],
        scratch_shapes=[pltpu.VMEM((MAT_ROWS, 128), F32), pltpu.VMEM((NDEV, MAT_SLICE, 128), F32),
                        pltpu.VMEM((NDEV, MAT_SLICE, 128), F32),
                        pltpu.VMEM((NDEV, 1, VEC_LANES), F32), pltpu.VMEM((NDEV, 1, wc), F32),
                        pltpu.VMEM((NDEV, 1, wc), F32),
                        pltpu.SemaphoreType.DMA((4, 7)), pltpu.SemaphoreType.DMA((4, 7))],
        compiler_params=_cparams(32, ("arbitrary",)),
    )(*ins)


def _small_update(tot, rsum, params):
    flat = [a for k in SMALL for a in params[k]]
    n_in = 2 + len(flat)

    def body(*refs):
        tot_ref, rsum_ref = refs[:2]
        p_hbm = refs[2:n_in]
        o_refs = refs[n_in:n_in + 4 * len(SMALL)]
        loss_ref = refs[n_in + 4 * len(SMALL)]
        p_refs = refs[n_in + 4 * len(SMALL) + 1:-1]
        sem = refs[-1]
        fetch = [pltpu.make_async_copy(p_hbm[k], p_refs[k], sem.at[k]) for k in range(len(flat))]
        for cp in fetch:
            cp.start()
        for cp in fetch:
            cp.wait()
        loss_ref[...] = rsum_ref[:, VEC_LANES - 128:VEC_LANES]

        def update(idx, g):
            w_ref, m_ref, v_ref = p_refs[3 * idx:3 * idx + 3]
            g_out, d_out, m_out, v_out = o_refs[4 * idx:4 * idx + 4]
            g = g.reshape(w_ref.shape)
            g_out[...] = g
            d_out[...], m_out[...], v_out[...] = _adamw(w_ref[...], g, m_ref[...], v_ref[...])

        update(0, tot_ref[0:1024, :])
        update(1, tot_ref[1024:1536, :])
        update(2, tot_ref[1536:1544, :])
        at = 0
        for idx, k in enumerate(VECS):
            update(3 + idx, rsum_ref[:, at:at + VEC_WIDTH[k]])
            at += VEC_WIDTH[k]

    outs = []
    for k in SMALL:
        outs += [jax.ShapeDtypeStruct(params[k][0].shape, F32)] * 4
    outs += [jax.ShapeDtypeStruct((1, 128), F32)]
    res = pl.pallas_call(
        body, name="small_update", grid=(1,), out_shape=outs,
        in_specs=[_whole(tot), _whole(rsum)] + [HBM] * len(flat), out_specs=[_whole(a) for a in outs],
        scratch_shapes=[pltpu.VMEM(a.shape, F32) for a in flat] + [pltpu.SemaphoreType.DMA((len(flat),))],
        compiler_params=_cparams(32, ("arbitrary",)),
    )(tot, rsum, *[pltpu.with_memory_space_constraint(a, pltpu.HBM) for a in flat])
    return {k: tuple(res[4 * i:4 * i + 4]) for i, k in enumerate(SMALL)}, res[-1]


def _sum_adamw(recv, w, m, v, tr, name, own=None):
    R, C = w.shape

    def body(*refs):
        r_ref = refs[0]
        o_ref = refs[1] if own is not None else None
        w_ref, m_ref, v_ref, g_ref, d_ref, nm_ref, nv_ref = refs[-7:]
        my = _slot(_position()) if own is not None else None

        def part(k):
            if own is None:
                return r_ref[k].astype(F32)
            return jnp.where(my == k, o_ref[k], r_ref[k]).astype(F32)

        g = part(0)
        for k in range(1, NDEV):
            g = g + part(k)
        g_ref[...] = g
        d_ref[...], nm_ref[...], nv_ref[...] = _adamw(w_ref[...], g, m_ref[...], v_ref[...])

    blk = pl.BlockSpec((tr, C), lambda i: (i, 0))
    slots = pl.BlockSpec((NDEV, tr, C), lambda i: (0, i, 0))
    out = jax.ShapeDtypeStruct((R, C), F32)
    bufs = (recv,) if own is None else (recv, own)
    return pl.pallas_call(
        body, name=name, grid=(R // tr,), out_shape=[out] * 4,
        in_specs=[slots] * len(bufs) + [blk, blk, blk], out_specs=[blk] * 4,
        compiler_params=_cparams(48, ("arbitrary",)),
    )(*bufs, w, m, v)


SEM = pl.BlockSpec(memory_space=pltpu.SEMAPHORE)
EFFECT = pltpu.SideEffectType.DATAFLOW_SIDE_EFFECTING


def _exchange_start(buf, name):
    def body(src_ref, land_ref, send_sems, recv_sems, src_thru, land_thru, token):
        me = _position()
        for d in range(1, NDEV):
            to = _flip(me, d)
            _remote(src_ref.at[_slot(to)], land_ref.at[_slot(me)], send_sems.at[d - 1], recv_sems.at[d - 1], to).start()
        token[...] = jnp.zeros_like(token)

    like = pltpu.HBM(buf.shape, buf.dtype)
    return pl.pallas_call(
        body, name=name,
        out_shape=(pltpu.SemaphoreType.DMA((NDEV - 1,)), pltpu.SemaphoreType.DMA((NDEV - 1,)), like, like,
                   jax.ShapeDtypeStruct((8, 128), F32)),
        in_specs=(HBM, HBM), out_specs=(SEM, SEM, HBM, HBM, pl.BlockSpec(memory_space=pltpu.VMEM)),
        input_output_aliases={0: 2, 1: 3},
        compiler_params=pltpu.CompilerParams(has_side_effects=EFFECT),
    )(pltpu.with_memory_space_constraint(buf, pltpu.HBM),
      pltpu.with_memory_space_constraint(lax.empty(buf.shape, buf.dtype), pltpu.HBM))


def _exchange_wait(send_sems, recv_sems, src_thru, land_thru, after, name):
    def body(src_ref, land_ref, send_sems, recv_sems, after_ref, src_out, land_out):
        me = _position()
        for d in range(1, NDEV):
            to = _flip(me, d)
            cp = _remote(src_ref.at[_slot(to)], land_ref.at[_slot(me)], send_sems.at[d - 1], recv_sems.at[d - 1], to)
            cp.wait_send()
            cp.wait_recv()

    like = pltpu.HBM(src_thru.shape, src_thru.dtype)
    return pl.pallas_call(
        body, name=name, out_shape=(like, like),
        in_specs=(HBM, HBM, SEM, SEM, pl.BlockSpec(memory_space=pl.ANY)), out_specs=(HBM, HBM),
        input_output_aliases={0: 0, 1: 1},
        compiler_params=pltpu.CompilerParams(has_side_effects=EFFECT),
    )(src_thru, land_thru, send_sems, recv_sems, after)


def _ada_update(cact_all, dmine, w, m, v, tr):
    R, C = w.shape

    def body(c_ref, dm_ref, w_ref, m_ref, v_ref, g_ref, d_ref, nm_ref, nv_ref):
        g = _dot_tn(c_ref[...].astype(BF), dm_ref[...].astype(BF))
        g_ref[...] = g
        d_ref[...], nm_ref[...], nv_ref[...] = _adamw(w_ref[...], g, m_ref[...], v_ref[...])

    blk = pl.BlockSpec((tr, C), lambda i: (i, 0))
    out = jax.ShapeDtypeStruct((R, C), F32)
    return pl.pallas_call(
        body, name="update_w_ada", grid=(R // tr,), out_shape=[out] * 4,
        in_specs=[pl.BlockSpec((8, tr), lambda i: (0, i)), pl.BlockSpec((8, C), lambda i: (0, 0)), blk, blk, blk],
        out_specs=[blk] * 4,
        compiler_params=_cparams(48, ("arbitrary",)),
    )(cact_all, dmine, w, m, v)


FC = F // 2


def _ffn_fwd(x, mod, gn, w_in_t, w_out, sub, name, gather=(), loss=None, h=None):
    S = x.shape[0]
    T = min(T_FFN, S)
    nS, nJ = S // T, F // FC
    ng = len(gather)
    nl = 2 if loss else 0
    nh = 0 if h is None else 1
    forward_step = nS // 2

    def body(*refs):
        x_ref, mod_ref, gn_ref, wg_ref, wu_ref, wo_ref = refs[:6]
        gf_ref, t_ref = refs[6 + nh:6 + nh + nl] if loss else (None, None)
        shards = refs[6 + nh + nl:6 + nh + nl + ng]
        at = 6 + nh + nl + ng
        xo_ref, gu_ref = refs[at:at + 2]
        h_ref = refs[6] if nh else refs[at + 2]
        at += 3 - nh
        gathered = refs[at:at + ng]
        at += ng
        st_ref = refs[at] if loss else None
        at += nl // 2
        acc_scr = refs[at]
        sems = refs[at + 1:]
        i, j = pl.program_id(0), pl.program_id(1)

        if ng:
            @pl.when((i == 0) & (j == 0))
            def _():
                _gather_phase("start", shards, gathered, sems)

            @pl.when((i == forward_step) & (j == 0))
            def _():
                _gather_phase("forward", shards, gathered, sems)

        @pl.when(j == 0)
        def _():
            if not nh:
                _, _, _, hh = _rms_mod(x_ref[...], gn_ref[...], mod_ref[3 * sub:3 * sub + 1, :],
                                       mod_ref[3 * sub + 1:3 * sub + 2, :])
                h_ref[...] = hh.astype(BF)
            acc_scr[...] = jnp.zeros_like(acc_scr)

        hb = h_ref[...]
        g = _dot_nt(hb, wg_ref[0])
        u = _dot_nt(hb, wu_ref[0])
        gu_ref[0] = g.astype(BF)
        gu_ref[1] = u.astype(BF)
        a = (g * _sigmoid(g) * u).astype(BF)
        acc_scr[...] += _dot(a, wo_ref[...])

        @pl.when(j == nJ - 1)
        def _():
            xo = x_ref[...] + (0.5 * mod_ref[3 * sub + 2:3 * sub + 3, :]) * acc_scr[...]
            if not loss:
                xo_ref[...] = xo
            else:
                dx, dgf, part = _final_norm_loss(xo, gf_ref[...], t_ref[...])
                xo_ref[...] = dx
                upd = _rows3(dgf, jnp.broadcast_to(part, (1, D)), jnp.zeros((1, D), F32), D)

                @pl.when(i == 0)
                def _():
                    st_ref[...] = upd

                @pl.when(i > 0)
                def _():
                    st_ref[...] += upd

        if ng:
            @pl.when((i == nS - 1) & (j == nJ - 1))
            def _():
                _gather_phase("finish", shards, gathered, sems)

    tile = pl.BlockSpec((T, D), lambda i, j: (i, 0))
    res = pl.pallas_call(
        body, name=name, grid=(nS, nJ),
        out_shape=[jax.ShapeDtypeStruct((S, D), F32), jax.ShapeDtypeStruct((2, S, F), BF)]
                  + ([] if nh else [jax.ShapeDtypeStruct((S, D), BF)])
                  + [jax.ShapeDtypeStruct((NDEV,) + s.shape, s.dtype) for s in gather]
                  + ([jax.ShapeDtypeStruct((8, D), F32)] if loss else []),
        in_specs=[tile,
                  pl.BlockSpec((9, D), lambda i, j: (0, 0)),
                  pl.BlockSpec((1, D), lambda i, j: (0, 0)),
                  pl.BlockSpec((1, FC, D), lambda i, j: (0, j, 0)),
                  pl.BlockSpec((1, FC, D), lambda i, j: (1, j, 0)),
                  pl.BlockSpec((FC, D), lambda i, j: (j, 0))] + [tile] * nh
                 + ([pl.BlockSpec((1, D), lambda i, j: (0, 0)), tile] if loss else []) + [HBM] * ng,
        out_specs=[tile, pl.BlockSpec((2, T, FC), lambda i, j: (0, i, j))] + [tile] * (1 - nh) + [HBM] * ng
                  + ([pl.BlockSpec((8, D), lambda i, j: (0, 0))] if loss else []),
        scratch_shapes=[pltpu.VMEM((T, D), F32)] + (_comm_sems(ng) if ng else []),
        compiler_params=_cparams(56, ("arbitrary", "arbitrary")),
    )(x, mod, gn, w_in_t, w_in_t, w_out, *(() if h is None else (h,)), *(loss or ()), *gather)
    return res if h is None else [res[0], res[1], h, *res[2:]]


def _ffn_bwd_hidden(dx, mod, gu, w_out, sub, name, exchange=()):
    S = dx.shape[0]
    T = min(T_FFN, S)
    nS, nJ = S // T, F // FC
    ne = len(exchange)

    def body(*refs):
        dx_ref, mod_ref, gu_ref, wo_ref = refs[:4]
        sendbufs = refs[4:4 + ne]
        dgu_ref, gw_ref, dgate_ref = refs[4 + ne:7 + ne]
        recvbufs = refs[7 + ne:7 + 2 * ne]
        acc_scr = refs[7 + 2 * ne]
        sems = refs[8 + 2 * ne:]
        j, i = pl.program_id(0), pl.program_id(1)

        if ne:
            @pl.when((i == 0) & (j == 0))
            def _():
                _exchange_phase("start", sendbufs, recvbufs, sems)

        gate = mod_ref[3 * sub + 2:3 * sub + 3, :]
        dx = dx_ref[...]
        da = _dot_nt((dx * (0.5 * gate)).astype(BF), wo_ref[...])
        g = gu_ref[0].astype(F32)
        u = gu_ref[1].astype(F32)
        sg = _sigmoid(g)
        s = g * sg
        dgu_ref[0] = (da * u * (sg * (1.0 + g * (1.0 - sg)))).astype(BF)
        dgu_ref[1] = (da * s).astype(BF)
        contrib = _dot_tn((s * u).astype(BF), dx.astype(BF))

        @pl.when(i == 0)
        def _():
            acc_scr[...] = contrib

        @pl.when(i > 0)
        def _():
            acc_scr[...] += contrib

        @pl.when(i == nS - 1)
        def _():
            acc = acc_scr[...]
            dgate = 0.5 * jnp.sum(acc * wo_ref[...].astype(F32), axis=0, keepdims=True)
            dgate_ref[...] = jnp.broadcast_to(dgate, (8, D))
            gw_ref[...] = (acc * (0.5 * gate)).astype(BF)

        if ne:
            @pl.when((i == nS - 1) & (j == nJ - 1))
            def _():
                _exchange_phase("wait", sendbufs, recvbufs, sems)

    return pl.pallas_call(
        body, name=name, grid=(nJ, nS),
        out_shape=[jax.ShapeDtypeStruct((2, S, F), BF), jax.ShapeDtypeStruct((F, D), BF),
                   jax.ShapeDtypeStruct((8 * nJ, D), F32)] + _like(exchange),
        in_specs=[pl.BlockSpec((T, D), lambda j, i: (i, 0)),
                  pl.BlockSpec((9, D), lambda j, i: (0, 0)),
                  pl.BlockSpec((2, T, FC), lambda j, i: (0, i, j)),
                  pl.BlockSpec((FC, D), lambda j, i: (j, 0))] + [HBM] * ne,
        out_specs=[pl.BlockSpec((2, T, FC), lambda j, i: (0, i, j)),
                   pl.BlockSpec((FC, D), lambda j, i: (j, 0)),
                   pl.BlockSpec((8, D), lambda j, i: (j, 0))] + [HBM] * ne,
        scratch_shapes=[pltpu.VMEM((FC, D), F32)] + (_comm_sems(ne) if ne else []),
        compiler_params=_cparams(56, ("arbitrary", "arbitrary")),
    )(dx, mod, gu, w_out, *exchange)


def _ffn_bwd_input(dgu, w_in_t, x, dx, mod, gn, sub, name, exchange=()):
    S = x.shape[0]
    T = min(T_FFN, S)
    nS = S // T
    ne = len(exchange)
    NC = 256
    chunks = [slice(k * NC, (k + 1) * NC) for k in range(D // NC)]

    def body(*refs):
        dgu_ref, w_ref, x_ref, dx_ref, mod_ref, gn_ref = refs[:6]
        sendbufs = refs[6:6 + ne]
        dxin_ref, st_ref = refs[6 + ne:8 + ne]
        recvbufs = refs[8 + ne:8 + 2 * ne]
        dxh_scr = refs[8 + 2 * ne]
        sems = refs[9 + 2 * ne:]
        i = pl.program_id(0)

        if ne:
            @pl.when(i == 0)
            def _():
                _exchange_phase("start", sendbufs, recvbufs, sems)

        gn = gn_ref[...]
        scale = mod_ref[3 * sub + 1:3 * sub + 2, :]
        r, xhat, n, _ = _rms_mod(x_ref[...], gn, mod_ref[3 * sub:3 * sub + 1, :], scale)
        dg = dgu_ref[0]
        du = dgu_ref[1]
        rowsum = jnp.zeros((T, 1), F32)
        dshift, dscale, dgn = [], [], []
        for cols in chunks:
            dh = _dot(dg, w_ref[0, :, cols]) + _dot(du, w_ref[1, :, cols])
            dshift.append(jnp.sum(dh, axis=0, keepdims=True))
            dscale.append(jnp.sum(dh * n[:, cols], axis=0, keepdims=True))
            dn = dh * (1.0 + scale[:, cols])
            dgn.append(jnp.sum(dn * xhat[:, cols], axis=0, keepdims=True))
            dxhat = dn * gn[:, cols]
            rowsum = rowsum + jnp.sum(dxhat * xhat[:, cols], axis=-1, keepdims=True)
            dxh_scr[:, cols] = dxhat
        dxin_ref[...] = dx_ref[...] + r * (dxh_scr[...] - xhat * (rowsum / D))
        cat = lambda parts: jnp.concatenate(parts, axis=1)
        upd = _rows3(cat(dshift), cat(dscale), cat(dgn), D)

        @pl.when(i == 0)
        def _():
            st_ref[...] = upd

        @pl.when(i > 0)
        def _():
            st_ref[...] += upd

        if ne:
            @pl.when(i == nS - 1)
            def _():
                _exchange_phase("wait", sendbufs, recvbufs, sems)

    tile = pl.BlockSpec((T, D), lambda i: (i, 0))
    return pl.pallas_call(
        body, name=name, grid=(nS,),
        out_shape=[jax.ShapeDtypeStruct((S, D), F32), jax.ShapeDtypeStruct((8, D), F32)] + _like(exchange),
        in_specs=[pl.BlockSpec((2, T, F), lambda i: (0, i, 0)),
                  pl.BlockSpec((2, F, D), lambda i: (0, 0, 0), pipeline_mode=pl.Buffered(1)),
                  tile, tile,
                  pl.BlockSpec((9, D), lambda i: (0, 0)),
                  pl.BlockSpec((1, D), lambda i: (0, 0))] + [HBM] * ne,
        out_specs=[tile, pl.BlockSpec((8, D), lambda i: (0, 0))] + [HBM] * ne,
        scratch_shapes=[pltpu.VMEM((T, D), F32)] + (_comm_sems(ne) if ne else []),
        compiler_params=_cparams(60, ("arbitrary",)),
    )(dgu, w_in_t, x, dx, mod, gn, *exchange)


def _ffn_bwd_win(h, dgu, name, exchange=()):
    S = h.shape[0]
    T = min(T_WIN, S)
    nS, nJ = S // T, F // FC
    ne = len(exchange)

    def body(*refs):
        h_ref, dgu_ref = refs[:2]
        sendbufs = refs[2:2 + ne]
        out_ref = refs[2 + ne]
        recvbufs = refs[3 + ne:3 + 2 * ne]
        acc_scr = refs[3 + 2 * ne]
        sems = refs[4 + 2 * ne:]
        p, j, i = pl.program_id(0), pl.program_id(1), pl.program_id(2)

        if ne:
            @pl.when((p == 0) & (j == 0) & (i == 0))
            def _():
                _exchange_phase("start", sendbufs, recvbufs, sems)

        contrib = _dot_tn(dgu_ref[0], h_ref[...])

        @pl.when(i == 0)
        def _():
            acc_scr[...] = contrib

        @pl.when(i > 0)
        def _():
            acc_scr[...] += contrib

        @pl.when(i == nS - 1)
        def _():
            out_ref[0] = acc_scr[...].astype(BF)

        if ne:
            @pl.when((p == 1) & (j == nJ - 1) & (i == nS - 1))
            def _():
                _exchange_phase("wait", sendbufs, recvbufs, sems)

    return pl.pallas_call(
        body, name=name, grid=(2, nJ, nS),
        out_shape=[jax.ShapeDtypeStruct((2, F, D), BF)] + _like(exchange),
        in_specs=[pl.BlockSpec((T, D), lambda p, j, i: (i, 0)),
                  pl.BlockSpec((1, T, FC), lambda p, j, i: (p, i, j))] + [HBM] * ne,
        out_specs=[pl.BlockSpec((1, FC, D), lambda p, j, i: (p, j, 0))] + [HBM] * ne,
        scratch_shapes=[pltpu.VMEM((FC, D), F32)] + (_comm_sems(ne) if ne else []),
        compiler_params=_cparams(56, ("arbitrary", "arbitrary", "arbitrary")),
    )(h, dgu, *exchange)


def _pool_counts(pos0, T):
    pos = pos0 + lax.broadcasted_iota(jnp.int32, (T, 1), 0)
    return [jnp.minimum(pos + 1, w).astype(F32) for w in WINDOWS]


def _pool_fwd(xa, halo, ext_scr, cnts, T):
    ext_scr[0:HALO, :] = halo
    ext_scr[HALO:HALO + T, :] = xa
    out = []
    for gi, w in enumerate(WINDOWS):
        cols = slice(128 * gi, 128 * gi + 128)
        acc = xa[:, cols]
        for k in range(1, w):
            acc = acc + ext_scr[HALO - k:HALO - k + T, cols]
        out.append(acc / cnts[gi] - xa[:, cols])
    return out


def _sgu_fwd(vnb, ws_ref, sv_scr, T):
    lane = lax.broadcasted_iota(jnp.int32, (CHUNK, 128), 1)
    for n in range(T // CHUNK):
        rows = slice(n * CHUNK, (n + 1) * CHUNK)
        for b in range(DG // 128):
            cols = slice(128 * b, 128 * b + 128)
            vb = vnb[rows, cols]
            sv_scr[rows, cols] = jnp.where(lane < 64, _dot(ws_ref[2 * b], vb), _dot(ws_ref[2 * b + 1], vb))


def _mix_fwd(x, mod, gn, gn_next, wmi, wmo, wp, ps, lg, lb, ws, bias, name):
    S = x.shape[0]
    T = min(T_MIX_FWD, S)

    def body(x_ref, mod_ref, gn_ref, gnn_ref, wmi_ref, wmo_ref, wp_ref, ps_ref, lg_ref, lb_ref, ws_ref, bias_ref,
             xo_ref, hn_ref, carry_scr, ext_scr, sv_scr, ycat_scr):
        i = pl.program_id(0)

        @pl.when(i == 0)
        def _():
            carry_scr[...] = jnp.zeros_like(carry_scr)

        x = x_ref[...]
        _, _, _, h = _rms_mod(x, gn_ref[...], mod_ref[3:4, :], mod_ref[4:5, :])
        proj = _dot_nt(h.astype(BF), wmi_ref[...])
        xa = proj[:, 0:DP]
        p = _pool_fwd(xa, carry_scr[...], ext_scr, _pool_counts(i * T, T), T)
        carry_scr[...] = xa[T - HALO:T, :]
        for gi in range(4):
            cols = slice(128 * gi, 128 * gi + 128)
            ycat_scr[:, cols] = (_dot(p[gi].astype(BF), wp_ref[gi]) * ps_ref[:, cols]).astype(BF)
        u, _ = _gelu(proj[:, DP:DP + DG])
        v, _ = _gelu(proj[:, DP + DG:DPROJ])
        mu = jnp.mean(v, axis=-1, keepdims=True)
        vc = v - mu
        rstd = lax.rsqrt(jnp.mean(vc * vc, axis=-1, keepdims=True) + EPS)
        vn = vc * rstd * lg_ref[...] + lb_ref[...]
        _sgu_fwd(vn.astype(BF), ws_ref, sv_scr, T)
        for n in range(T // CHUNK):
            rows = slice(n * CHUNK, (n + 1) * CHUNK)
            ycat_scr[rows, DP:D] = (u[rows, :] * (sv_scr[rows, :] + bias_ref[...])).astype(BF)
        xo = x + mod_ref[5:6, :] * _dot(ycat_scr[...], wmo_ref[...])
        xo_ref[...] = xo
        _, _, _, hn = _rms_mod(xo, gnn_ref[...], mod_ref[6:7, :], mod_ref[7:8, :])
        hn_ref[...] = hn.astype(BF)

    full = lambda shape: pl.BlockSpec(shape, lambda i: (0,) * len(shape))
    tile = pl.BlockSpec((T, D), lambda i: (i, 0))
    return pl.pallas_call(
        body, name=name, grid=(S // T,),
        out_shape=[jax.ShapeDtypeStruct((S, D), F32), jax.ShapeDtypeStruct((S, D), BF)],
        in_specs=[tile, full((9, D)), full((1, D)), full((1, D)), full((DPROJ, D)), full((D, D)),
                  full((4, 128, 128)), full((1, DP)), full((1, DG)), full((1, DG)), full((8, CHUNK, CHUNK)),
                  full((CHUNK, DG))],
        out_specs=[tile, tile],
        scratch_shapes=[pltpu.VMEM((HALO, DP), F32), pltpu.VMEM((T + HALO, DP), F32), pltpu.VMEM((T, DG), F32),
                        pltpu.VMEM((T, D), BF)],
        compiler_params=_cparams(48, ("arbitrary",)),
    )(x, mod, gn, gn_next, wmi, wmo, wp, ps, lg, lb, ws, bias)


def _mix_bwd(x, dxo, mod, gn, wmi, wmo, wp, ps, lg, lb, ws, bias, name, exchange=()):
    S = x.shape[0]
    T = min(T_MIX, S)
    nS = S // T
    hb = T // HALO
    ne = len(exchange)

    def body(*refs):
        (x_ref, xh_ref, dxo_ref, mod_ref, gn_ref, wmi_ref, wmo_ref, wp_ref, ps_ref, lg_ref, lb_ref, ws_ref,
         bias_ref) = refs[:13]
        sendbufs = refs[13:13 + ne]
        dxi_ref, gwmi_out, gwmo_out, gwp_ref, gws_ref, st_ref, vec_ref, dbias_ref = refs[13 + ne:21 + ne]
        recvbufs = refs[21 + ne:21 + 2 * ne]
        (carry_scr, ext_scr, qext_scr, sv_scr, dvn_scr, ycat_scr, dproj_scr, gwmi_ref,
         gwmo_ref) = refs[21 + 2 * ne:30 + 2 * ne]
        sems = refs[30 + 2 * ne:]
        i = pl.program_id(0)
        t = nS - 1 - i
        gn = gn_ref[...]
        shift, scale, gate = mod_ref[3:4, :], mod_ref[4:5, :], mod_ref[5:6, :]

        @pl.when(i == 0)
        def _():
            if ne:
                _exchange_phase("start", sendbufs, recvbufs, sems)
            carry_scr[...] = jnp.zeros_like(carry_scr)
            gwmi_ref[...] = jnp.zeros_like(gwmi_ref)
            gwmo_ref[...] = jnp.zeros_like(gwmo_ref)
            gwp_ref[...] = jnp.zeros_like(gwp_ref)
            gws_ref[...] = jnp.zeros_like(gws_ref)
            st_ref[...] = jnp.zeros_like(st_ref)
            vec_ref[...] = jnp.zeros_like(vec_ref)
            dbias_ref[...] = jnp.zeros_like(dbias_ref)

        x = x_ref[...]
        dxo = dxo_ref[...]
        r, xhat, n, h = _rms_mod(x, gn, shift, scale)
        hbf = h.astype(BF)
        proj = _dot_nt(hbf, wmi_ref[...])
        xa = proj[:, 0:DP]
        zu = proj[:, DP:DP + DG]
        zv = proj[:, DP + DG:DPROJ]
        _, _, _, hh = _rms_mod(xh_ref[...], gn, shift, scale)
        halo = _dot_nt(hh.astype(BF), wmi_ref[0:DP, :])
        halo = jnp.where(t == 0, 0.0, halo)
        cnts = _pool_counts(t * T, T)
        p = _pool_fwd(xa, halo, ext_scr, cnts, T)
        m = []
        for gi in range(4):
            cols = slice(128 * gi, 128 * gi + 128)
            m.append(_dot(p[gi].astype(BF), wp_ref[gi]))
            ycat_scr[:, cols] = (m[gi] * ps_ref[:, cols]).astype(BF)
        u, tu = _gelu(zu)
        v, tv = _gelu(zv)
        mu = jnp.mean(v, axis=-1, keepdims=True)
        vc = v - mu
        rstd = lax.rsqrt(jnp.mean(vc * vc, axis=-1, keepdims=True) + EPS)
        vhat = vc * rstd
        lg = lg_ref[...]
        vnb = (vhat * lg + lb_ref[...]).astype(BF)
        _sgu_fwd(vnb, ws_ref, sv_scr, T)
        for nck in range(T // CHUNK):
            rows = slice(nck * CHUNK, (nck + 1) * CHUNK)
            sv_scr[rows, :] = sv_scr[rows, :] + bias_ref[...]
        sv = sv_scr[...]
        ycat_scr[:, DP:D] = (u * sv).astype(BF)

        gwmo_ref[...] += _dot_tn(ycat_scr[...], dxo.astype(BF))
        dyc = _dot_nt((dxo * gate).astype(BF), wmo_ref[...])
        dya = dyc[:, 0:DP]
        dyb = dyc[:, DP:D]

        dps = []
        dp = []
        for gi in range(4):
            cols = slice(128 * gi, 128 * gi + 128)
            dps.append(jnp.sum(dya[:, cols] * m[gi], axis=0, keepdims=True))
            dm = (dya[:, cols] * ps_ref[:, cols]).astype(BF)
            gwp_ref[gi] += _dot_tn(p[gi].astype(BF), dm)
            dp.append(_dot_nt(dm, wp_ref[gi]))
            qext_scr[0:T, cols] = dp[gi] / cnts[gi]
        qext_scr[T:T + HALO, :] = carry_scr[...]
        for gi, w in enumerate(WINDOWS):
            cols = slice(128 * gi, 128 * gi + 128)
            acc = qext_scr[0:T, cols]
            for k in range(1, w):
                acc = acc + qext_scr[k:k + T, cols]
            dproj_scr[:, cols] = (acc - dp[gi]).astype(BF)
        carry_scr[...] = qext_scr[0:HALO, :]

        du = dyb * sv
        dsv = dyb * u
        lane = lax.broadcasted_iota(jnp.int32, (CHUNK, 128), 1)
        dbias = jnp.zeros((CHUNK, DG), F32)
        for nck in range(T // CHUNK):
            rows = slice(nck * CHUNK, (nck + 1) * CHUNK)
            dbias = dbias + dsv[rows, :]
            for b in range(DG // 128):
                cols = slice(128 * b, 128 * b + 128)
                dsvb = dsv[rows, cols]
                vb = vnb[rows, cols]
                gws_ref[2 * b] += _dot_nt(jnp.where(lane < 64, dsvb, 0.0).astype(BF), vb)
                gws_ref[2 * b + 1] += _dot_nt(jnp.where(lane < 64, 0.0, dsvb).astype(BF), vb)
                dsvbb = dsvb.astype(BF)
                dvn_scr[rows, cols] = jnp.where(lane < 64, _dot_tn(ws_ref[2 * b], dsvbb),
                                                _dot_tn(ws_ref[2 * b + 1], dsvbb))
        dbias_ref[...] += dbias
        dvn = dvn_scr[...]
        dlg = jnp.sum(dvn * vhat, axis=0, keepdims=True)
        dlb = jnp.sum(dvn, axis=0, keepdims=True)
        dvhat = dvn * lg
        dv = rstd * (dvhat - jnp.mean(dvhat, axis=-1, keepdims=True)
                     - vhat * jnp.mean(dvhat * vhat, axis=-1, keepdims=True))
        dproj_scr[:, DP:DP + DG] = (du * _gelu_grad(zu, tu)).astype(BF)
        dproj_scr[:, DP + DG:DPROJ] = (dv * _gelu_grad(zv, tv)).astype(BF)
        vec_ref[...] += _rows3(jnp.concatenate(dps, axis=1), dlg, dlb, DP)

        dproj = dproj_scr[...]
        gwmi_ref[...] += _dot_tn(dproj, hbf)
        dh = _dot(dproj, wmi_ref[...])
        dxi, dshift, dscale, dgn = _rms_mod_bwd(dh, dxo, r, xhat, n, gn, scale)
        dxi_ref[...] = dxi
        st_ref[...] += _rows3(dshift, dscale, dgn, D)

        @pl.when(i == nS - 1)
        def _():
            acc = gwmo_ref[...]
            dgate = jnp.sum(acc * wmo_ref[...].astype(F32), axis=0, keepdims=True)
            row = lax.broadcasted_iota(jnp.int32, (8, D), 0)
            st_ref[...] += jnp.where(row == 3, dgate, 0.0)
            gwmo_out[...] = (acc * gate).astype(BF)
            gwmi_out[...] = gwmi_ref[...].astype(BF)
            tt = lax.broadcasted_iota(jnp.int32, (CHUNK, CHUNK), 0)
            ss = lax.broadcasted_iota(jnp.int32, (CHUNK, CHUNK), 1)
            for hd in range(8):
                gws_ref[hd] = jnp.where(tt >= ss, gws_ref[hd], 0.0)
            if ne:
                _exchange_phase("wait", sendbufs, recvbufs, sems)

    full = lambda shape: pl.BlockSpec(shape, lambda i: (0,) * len(shape))
    return pl.pallas_call(
        body, name=name, grid=(nS,),
        out_shape=[jax.ShapeDtypeStruct((S, D), F32), jax.ShapeDtypeStruct((DPROJ, D), BF),
                   jax.ShapeDtypeStruct((D, D), BF), jax.ShapeDtypeStruct((4, 128, 128), F32),
                   jax.ShapeDtypeStruct((8, CHUNK, CHUNK), F32), jax.ShapeDtypeStruct((8, D), F32),
                   jax.ShapeDtypeStruct((8, DP), F32), jax.ShapeDtypeStruct((CHUNK, DG), F32)] + _like(exchange),
        in_specs=[pl.BlockSpec((T, D), lambda i: (nS - 1 - i, 0)),
                  pl.BlockSpec((HALO, D), lambda i: (jnp.maximum((nS - 1 - i) * hb - 1, 0), 0)),
                  pl.BlockSpec((T, D), lambda i: (nS - 1 - i, 0)),
                  full((9, D)), full((1, D)), full((DPROJ, D)), full((D, D)),
                  full((4, 128, 128)), full((1, DP)), full((1, DG)), full((1, DG)), full((8, CHUNK, CHUNK)),
                  full((CHUNK, DG))] + [HBM] * ne,
        out_specs=[pl.BlockSpec((T, D), lambda i: (nS - 1 - i, 0)), full((DPROJ, D)), full((D, D)),
                   full((4, 128, 128)), full((8, CHUNK, CHUNK)), full((8, D)), full((8, DP)), full((CHUNK, DG))]
                  + [HBM] * ne,
        scratch_shapes=[pltpu.VMEM((HALO, DP), F32), pltpu.VMEM((T + HALO, DP), F32),
                        pltpu.VMEM((T + HALO, DP), F32), pltpu.VMEM((T, DG), F32), pltpu.VMEM((T, DG), F32),
                        pltpu.VMEM((T, D), BF), pltpu.VMEM((T, DPROJ), BF), pltpu.VMEM((DPROJ, D), F32),
                        pltpu.VMEM((D, D), F32)] + (_comm_sems(ne) if ne else []),
        compiler_params=_cparams(56, ("arbitrary",)),
    )(x, x, dxo, mod, gn, wmi, wmo, wp, ps, lg, lb, ws, bias, *exchange)


def kernel(x, c, w_ada, b_ada, norm_ffn1_g, ffn1_w_in, ffn1_w_out, norm_mix_g, w_mix_in, w_pool, pool_scale, gmlp_ln_g, gmlp_ln_b, w_spatial, b_spatial, w_mix_out, norm_ffn2_g, ffn2_w_in, ffn2_w_out, norm_final_g, loss_target, m_w_ada, m_b_ada, m_norm_ffn1_g, m_ffn1_w_in, m_ffn1_w_out, m_norm_mix_g, m_w_mix_in, m_w_pool, m_pool_scale, m_gmlp_ln_g, m_gmlp_ln_b, m_w_spatial, m_b_spatial, m_w_mix_out, m_norm_ffn2_g, m_ffn2_w_in, m_ffn2_w_out, m_norm_final_g, v_w_ada, v_b_ada, v_norm_ffn1_g, v_ffn1_w_in, v_ffn1_w_out, v_norm_mix_g, v_w_mix_in, v_w_pool, v_pool_scale, v_gmlp_ln_g, v_gmlp_ln_b, v_w_spatial, v_b_spatial, v_w_mix_out, v_norm_ffn2_g, v_ffn2_w_in, v_ffn2_w_out, v_norm_final_g):
    weights = dict(w_ada=w_ada, b_ada=b_ada, norm_ffn1_g=norm_ffn1_g, ffn1_w_in=ffn1_w_in, ffn1_w_out=ffn1_w_out,
                   norm_mix_g=norm_mix_g, w_mix_in=w_mix_in, w_pool=w_pool, pool_scale=pool_scale,
                   gmlp_ln_g=gmlp_ln_g, gmlp_ln_b=gmlp_ln_b, w_spatial=w_spatial, b_spatial=b_spatial,
                   w_mix_out=w_mix_out, norm_ffn2_g=norm_ffn2_g, ffn2_w_in=ffn2_w_in, ffn2_w_out=ffn2_w_out,
                   norm_final_g=norm_final_g)
    mom1 = dict(w_ada=m_w_ada, b_ada=m_b_ada, norm_ffn1_g=m_norm_ffn1_g, ffn1_w_in=m_ffn1_w_in,
                ffn1_w_out=m_ffn1_w_out, norm_mix_g=m_norm_mix_g, w_mix_in=m_w_mix_in, w_pool=m_w_pool,
                pool_scale=m_pool_scale, gmlp_ln_g=m_gmlp_ln_g, gmlp_ln_b=m_gmlp_ln_b, w_spatial=m_w_spatial,
                b_spatial=m_b_spatial, w_mix_out=m_w_mix_out, norm_ffn2_g=m_norm_ffn2_g, ffn2_w_in=m_ffn2_w_in,
                ffn2_w_out=m_ffn2_w_out, norm_final_g=m_norm_final_g)
    mom2 = dict(w_ada=v_w_ada, b_ada=v_b_ada, norm_ffn1_g=v_norm_ffn1_g, ffn1_w_in=v_ffn1_w_in,
                ffn1_w_out=v_ffn1_w_out, norm_mix_g=v_norm_mix_g, w_mix_in=v_w_mix_in, w_pool=v_w_pool,
                pool_scale=v_pool_scale, gmlp_ln_g=v_gmlp_ln_g, gmlp_ln_b=v_gmlp_ln_b, w_spatial=v_w_spatial,
                b_spatial=v_b_spatial, w_mix_out=v_w_mix_out, norm_ffn2_g=v_norm_ffn2_g, ffn2_w_in=v_ffn2_w_in,
                ffn2_w_out=v_ffn2_w_out, norm_final_g=v_norm_final_g)
    order = list(weights)
    xs = x[0]
    target = loss_target[0]
    transposed = ("ffn1_w_in", "w_mix_in", "ffn2_w_in")
    big = ("ffn1_w_in", "ffn1_w_out", "w_mix_in", "w_mix_out", "ffn2_w_in", "ffn2_w_out")
    local = lambda a, k: a[0].T if k in transposed else a[0]
    wc = w_ada.shape[2]

    shard = dict(zip(big, _cast_shards([local(weights[k], k) for k in big])))
    mod, cact_all, h1, g_w1_in, g_w1_out = _ada_forward(
        jnp.broadcast_to(c, (8, D)), w_ada[0], b_ada.reshape(NDEV, wc), [shard["ffn1_w_in"], shard["ffn1_w_out"]],
        xs, norm_ffn1_g)
    w1_in = g_w1_in.reshape(2, F, D)
    w1_out = g_w1_out.reshape(F, D)

    x1, gu1, h1, g_wmi, g_wmo, g_w2_out, g_w2_in = _ffn_fwd(
        xs, mod, norm_ffn1_g, w1_in, w1_out, 0, "ffn1_fwd", h=h1,
        gather=[shard["w_mix_in"], shard["w_mix_out"], shard["ffn2_w_out"], shard["ffn2_w_in"]])
    wmi = g_wmi.reshape(DPROJ, D)
    wmo = g_wmo.reshape(D, D)
    w2_in = g_w2_in.reshape(2, F, D)
    w2_out = g_w2_out.reshape(F, D)
    tril = jnp.tril(jnp.ones((CHUNK, CHUNK), dtype=bool))
    ws_b = jnp.where(tril[None], w_spatial[0], 0.0).astype(BF)
    wp_b = w_pool[0].astype(BF)
    bias = jnp.repeat(b_spatial[0].T, DG // 8, axis=1)
    mix_args = (wmi, wmo, wp_b, pool_scale, gmlp_ln_g, gmlp_ln_b, ws_b, bias)
    x2, h3 = _mix_fwd(x1, mod, norm_mix_g, norm_ffn2_g, *mix_args, "mix_fwd")
    dx3, gu3, h3, st_f = _ffn_fwd(x2, mod, norm_ffn2_g, w2_in, w2_out, 2, "ffn2_fwd", h=h3,
                                  loss=(norm_final_g.reshape(1, D), target))

    slots = lambda a: a.reshape(NDEV, a.size // (NDEV * D), D)
    dgu3, d_w2_out, dgate3 = _ffn_bwd_hidden(dx3, mod, gu3, w2_out, 2, "ffn2_bwd_hidden")
    d_w2_in = _ffn_bwd_win(h3, dgu3, "ffn2_bwd_win")[0]
    flight2 = _exchange_start(slots(d_w2_in), "ffn2_w_in_grad_start")
    flight3 = _exchange_start(slots(d_w2_out), "ffn2_w_out_grad_start")
    dx2, st3 = _ffn_bwd_input(dgu3, w2_in, x2, dx3, mod + (flight2[4][0, 0] + flight3[4][0, 0]), norm_ffn2_g, 2,
                              "ffn2_bwd_input")
    dx1, d_wmi, d_wmo, d_wp, d_ws, st2, vec2, dbias = _mix_bwd(x1, dx2, mod, norm_mix_g, *mix_args, "mix_bwd")
    dgu1, d_w1_out, dgate1, r_wmi, r_wmo = _ffn_bwd_hidden(
        dx1, mod, gu1, w1_out, 0, "ffn1_bwd_hidden", exchange=[slots(d_wmi), slots(d_wmo)])
    d_w1_in, r_w1_out = _ffn_bwd_win(h1, dgu1, "ffn1_bwd_win", exchange=[slots(d_w1_out)])
    send_sems, recv_sems, sent, landing, token = _exchange_start(slots(d_w1_in), "w_in_grad_start")
    dx0, st1 = _ffn_bwd_input(dgu1, w1_in, xs, dx1, mod + token[0, 0], norm_ffn1_g, 0, "ffn1_bwd_input")

    received = dict(ffn1_w_out=r_w1_out, w_mix_in=r_wmi, w_mix_out=r_wmo)
    tiles = dict(ffn1_w_in=176, ffn1_w_out=176, w_mix_in=96, w_mix_out=128, ffn2_w_in=176, ffn2_w_out=176)
    result = {}

    def update(k, recv, own=None):
        res = _sum_adamw(recv, local(weights[k], k), local(mom1[k], k), local(mom2[k], k), tiles[k], "update_" + k,
                         own=own)
        result[k] = tuple((a.T if k in transposed else a)[None] for a in res)

    for k, recv in received.items():
        update(k, recv)
    for k, flight in (("ffn2_w_in", flight2), ("ffn2_w_out", flight3)):
        sent2, landing2 = _exchange_wait(*flight[:4], st1, k + "_grad_wait")
        update(k, landing2, own=sent2)
    row = lambda a: a.reshape(1, D)
    params = {k: (weights[k], mom1[k], mom2[k]) for k in SMALL}
    params["norm_final_g"] = (row(norm_final_g), row(m_norm_final_g), row(v_norm_final_g))
    tot, rsum, dmine = _small_reduce(d_ws, d_wp, dbias, st1, st2, st3, st_f, vec2, dgate1, dgate3)
    sent, landing = _exchange_wait(send_sems, recv_sems, sent, landing, rsum, "w_in_grad_wait")
    update("ffn1_w_in", landing, own=sent)
    small, loss_row = _small_update(tot, rsum, params)
    result.update(small)
    result["norm_final_g"] = tuple(a.reshape(D) for a in small["norm_final_g"])
    result["w_ada"] = tuple(a[None] for a in _ada_update(cact_all, dmine, w_ada[0], m_w_ada[0], v_w_ada[0], 256))

    return (loss_row[0, 0], dx0[None], *[result[k][0] for k in order], *[result[k][1] for k in order],
            *[result[k][2] for k in order], *[result[k][3] for k in order])
```

```python
import math

import jax
import jax.numpy as jnp
from jax import lax
from jax.experimental import pallas as pl
from jax.experimental.pallas import tpu as pltpu

D = 1024
F = 2816
DP = 512
DG = 512
DPROJ = DP + 2 * DG
CHUNK = 128
WINDOWS = (2, 4, 8, 16)
HALO = 16
NDEV = 8
T_FFN = 512
T_MIX = 256
T_MIX_FWD = 512
T_WIN = 2048
EPS = 1e-6
LR, B1, B2, AEPS, WD, STEP = 0.001, 0.9, 0.999, 1e-08, 0.01, 10
BC1 = 1.0 - B1 ** STEP
BC2 = 1.0 - B2 ** STEP
GELU_C = math.sqrt(2.0 / math.pi)
GELU_A = 0.044715

BF = jnp.bfloat16
F32 = jnp.float32
MESH = pl.DeviceIdType.MESH
HBM = pl.BlockSpec(memory_space=pltpu.HBM)


def _whole(a):
    return pl.BlockSpec(a.shape, lambda i: (0,) * len(a.shape))


NT = (((1,), (1,)), ((), ()))
TN = (((0,), (0,)), ((), ()))


def _dot(a, b):
    return jnp.dot(a, b, preferred_element_type=F32)


def _dot_nt(a, b):
    return lax.dot_general(a, b, NT, preferred_element_type=F32)


def _dot_tn(a, b):
    return lax.dot_general(a, b, TN, preferred_element_type=F32)


def _cparams(vmem_mb, sem=None):
    kw = dict(vmem_limit_bytes=vmem_mb * 1024 * 1024)
    if sem is not None:
        kw["dimension_semantics"] = sem
    return pltpu.CompilerParams(**kw)


def _position():
    return lax.axis_index("x"), lax.axis_index("y"), lax.axis_index("c")


def _slot(p):
    return 4 * p[0] + 2 * p[1] + p[2]


def _flip(me, d):
    x, y, c = me
    return (1 - x if d & 4 else x, 1 - y if d & 2 else y, 1 - c if d & 1 else c)


def _remote(src, dst, send_sem, recv_sem, to):
    return pltpu.make_async_remote_copy(src_ref=src, dst_ref=dst, send_sem=send_sem, recv_sem=recv_sem,
                                        device_id=to, device_id_type=MESH)


def _comm_sems(n):
    return [pltpu.SemaphoreType.DMA((n, 7)), pltpu.SemaphoreType.DMA((n, 7)), pltpu.SemaphoreType.DMA((n,))]


def _gather_phase(phase, xs, outs, sems):
    send_sems, recv_sems, local_sems = sems
    n = len(xs)
    me = _position()
    x, y, c = me
    sibling = (x, y, 1 - c)
    xn, yn, diag = (1 - x, y), (x, 1 - y), (1 - x, 1 - y)
    relay_from = (x + c * (1 - 2 * x), y + (1 - c) * (1 - 2 * y))
    relay_to = (x + (1 - c) * (1 - 2 * x), y + c * (1 - 2 * y))

    def copy(a, k, block, to, src=None):
        dst = outs[a].at[_slot(block)]
        return _remote(dst if src is None else src, dst, send_sems.at[a, k], recv_sems.at[a, k], to)

    def mine(a):
        return pltpu.make_async_copy(xs[a], outs[a].at[_slot(me)], local_sems.at[a])

    def first(a):
        return [copy(a, 0, me, sibling, src=xs[a]), copy(a, 1, me, (*xn, c), src=xs[a]),
                copy(a, 2, me, (*yn, c), src=xs[a])]

    def second(a):
        return [copy(a, 3, (*relay_from, c), (*relay_to, c)), copy(a, 4, (*xn, c), sibling),
                copy(a, 5, (*yn, c), sibling)]

    def third(a):
        return copy(a, 6, (*diag, c), sibling)

    if phase == "start":
        for a in range(n):
            mine(a).start()
            for cp in first(a):
                cp.start()
    elif phase == "forward":
        for a in range(n):
            copy(a, 1, (*xn, c), me).wait_recv()
            copy(a, 2, (*yn, c), me).wait_recv()
            for cp in second(a):
                cp.start()
    else:
        for a in range(n):
            copy(a, 3, (*diag, c), me).wait_recv()
            third(a).start()
        for a in range(n):
            copy(a, 0, sibling, me).wait_recv()
            for k, chip in ((4, xn), (5, yn), (6, diag)):
                copy(a, k, (*chip, 1 - c), me).wait_recv()
        for a in range(n):
            for cp in first(a) + second(a) + [third(a)]:
                cp.wait_send()
            mine(a).wait()


def _exchange_phase(phase, xs, outs, sems):
    send_sems, recv_sems, local_sems = sems
    me = _position()
    for a in range(len(xs)):
        copies = [pltpu.make_async_copy(xs[a].at[_slot(me)], outs[a].at[_slot(me)], local_sems.at[a])]
        for d in range(1, NDEV):
            to = _flip(me, d)
            copies.append(_remote(xs[a].at[_slot(to)], outs[a].at[_slot(me)],
                                  send_sems.at[a, d - 1], recv_sems.at[a, d - 1], to))
        for cp in copies:
            if phase == "start":
                cp.start()
            else:
                cp.wait()


def _like(bufs):
    return [jax.ShapeDtypeStruct(b.shape, b.dtype) for b in bufs]


def _rms_mod(x, gn, shift, scale):
    ms = jnp.mean(x * x, axis=-1, keepdims=True)
    r = lax.rsqrt(ms + EPS)
    xhat = x * r
    n = xhat * gn
    h = n * (1.0 + scale) + shift
    return r, xhat, n, h


def _rms_mod_bwd(dh, dres, r, xhat, n, gn, scale):
    dshift = jnp.sum(dh, axis=0, keepdims=True)
    dscale = jnp.sum(dh * n, axis=0, keepdims=True)
    dn = dh * (1.0 + scale)
    dgn = jnp.sum(dn * xhat, axis=0, keepdims=True)
    dxhat = dn * gn
    dx = dres + r * (dxhat - xhat * jnp.mean(dxhat * xhat, axis=-1, keepdims=True))
    return dx, dshift, dscale, dgn


def _final_norm_loss(x, gf, target):
    r = lax.rsqrt(jnp.mean(x * x, axis=-1, keepdims=True) + EPS)
    xhat = x * r
    e = xhat * gf - target
    part = 0.5 * jnp.sum(jnp.sum(e * e, axis=-1, keepdims=True), axis=0, keepdims=True) / D
    dy = e / D
    dgf = jnp.sum(dy * xhat, axis=0, keepdims=True)
    dxhat = dy * gf
    dx = r * (dxhat - xhat * jnp.mean(dxhat * xhat, axis=-1, keepdims=True))
    return dx, dgf, part


def _rows3(a, b, c, width):
    row = lax.broadcasted_iota(jnp.int32, (8, width), 0)
    z = jnp.zeros((8, width), F32)
    return jnp.where(row == 0, a, z) + jnp.where(row == 1, b, z) + jnp.where(row == 2, c, z)


def _sigmoid(x):
    return 0.5 * jnp.tanh(0.5 * x) + 0.5


def _gelu(x):
    t = jnp.tanh(GELU_C * (x + GELU_A * x * x * x))
    return 0.5 * x * (1.0 + t), t


def _gelu_grad(x, t):
    return 0.5 * (1.0 + t) + 0.5 * x * (1.0 - t * t) * GELU_C * (1.0 + 3.0 * GELU_A * x * x)


def _adamw(w, g, m, v):
    m = B1 * m + (1.0 - B1) * g
    v = B2 * v + (1.0 - B2) * (g * g)
    m_hat = m / BC1
    v_hat = v / BC2
    delta = -LR * (m_hat / (jnp.sqrt(v_hat) + AEPS) + WD * w)
    return delta, m, v


def _cast_shards(shards):
    n = len(shards)

    def body(*refs):
        for a in range(n):
            refs[n + a][...] = refs[a][...].astype(BF)

    resident = pl.BlockSpec(memory_space=pltpu.VMEM)
    return pl.pallas_call(
        body, name="cast_shards", out_shape=[jax.ShapeDtypeStruct(s.shape, BF) for s in shards],
        in_specs=[resident] * n, out_specs=[resident] * n, compiler_params=_cparams(40),
    )(*shards)


def _ada_forward(c8, w_ada, b8, shards, x, gn):
    wc = w_ada.shape[1]
    n = len(shards)
    S = x.shape[0]
    T = min(T_FFN, S)
    nS = S // T

    def ada(c8_ref, w_ref, b8_ref, xs, mod_ref, cact_ref, gathered, call_ref, mall_ref, modp_ref, send_sems,
            recv_sems, gsems):
        me = _position()
        my = _slot(me)
        row = lax.broadcasted_iota(jnp.int32, (8, 1), 0)
        call_ref[my] = c8_ref[...]
        sends = []
        for d in range(1, NDEV):
            to = _flip(me, d)
            sends.append(_remote(call_ref.at[my], call_ref.at[my], send_sems.at[0, d - 1], recv_sems.at[0, d - 1], to))
        for cp in sends:
            cp.start()
        _gather_phase("start", xs, gathered, gsems)
        for cp in sends:
            cp.wait()
        c_all = jnp.zeros((8, D), F32)
        for k in range(NDEV):
            c_all = c_all + jnp.where(row == k, call_ref[k], 0.0)
        cact = c_all * jax.nn.sigmoid(c_all)
        cact_ref[...] = cact
        part = _dot(cact.astype(BF), w_ref[...].astype(BF))
        mall_ref[my] = part
        sends = []
        for d in range(1, NDEV):
            to = _flip(me, d)
            sends.append(_remote(mall_ref.at[my], mall_ref.at[my], send_sems.at[1, d - 1], recv_sems.at[1, d - 1], to))
        for cp in sends:
            cp.start()
        _gather_phase("forward", xs, gathered, gsems)
        for cp in sends:
            cp.wait()
        out = jnp.zeros((8, wc), F32)
        for k in range(NDEV):
            piece = jnp.sum(jnp.where(row == my, mall_ref[k], 0.0), axis=0, keepdims=True)
            out = out + jnp.where(row == k, piece, 0.0)
        modp_ref[...] = out + b8_ref[...]
        for q in range(9 * NDEV):
            mod_ref[q // 8:q // 8 + 1, 128 * (q % 8):128 * (q % 8 + 1)] = \
                modp_ref[q // 9:q // 9 + 1, 128 * (q % 9):128 * (q % 9 + 1)]

    def body(*refs):
        c8_ref, w_ref, b8_ref, gn_ref, x_ref = refs[:5]
        xs = refs[5:5 + n]
        mod_ref, cact_ref, h_ref = refs[5 + n:8 + n]
        gathered = refs[8 + n:8 + 2 * n]
        call_ref, mall_ref, modp_ref, send_sems, recv_sems = refs[8 + 2 * n:13 + 2 * n]
        gsems = refs[13 + 2 * n:]
        i = pl.program_id(0)

        @pl.when(i == 0)
        def _():
            ada(c8_ref, w_ref, b8_ref, xs, mod_ref, cact_ref, gathered, call_ref, mall_ref, modp_ref, send_sems,
                recv_sems, gsems)

        _, _, _, h = _rms_mod(x_ref[...], gn_ref[...], mod_ref[0:1, :], mod_ref[1:2, :])
        h_ref[...] = h.astype(BF)

        @pl.when(i == nS - 1)
        def _():
            _gather_phase("finish", xs, gathered, gsems)

    outs = [jax.ShapeDtypeStruct((9, D), F32), jax.ShapeDtypeStruct((8, D), F32)]
    tile = pl.BlockSpec((T, D), lambda i: (i, 0))
    return pl.pallas_call(
        body, name="ada_forward", grid=(nS,),
        out_shape=outs + [jax.ShapeDtypeStruct((S, D), BF)]
                  + [jax.ShapeDtypeStruct((NDEV,) + s.shape, s.dtype) for s in shards],
        in_specs=[_whole(a) for a in (c8, w_ada, b8, gn)] + [tile] + [HBM] * n,
        out_specs=[_whole(a) for a in outs] + [tile] + [HBM] * n,
        scratch_shapes=[pltpu.VMEM((NDEV, 8, D), F32), pltpu.VMEM((NDEV, 8, wc), F32), pltpu.VMEM((8, wc), F32),
                        pltpu.SemaphoreType.DMA((2, 7)), pltpu.SemaphoreType.DMA((2, 7))] + _comm_sems(n),
        compiler_params=_cparams(40, ("arbitrary",)),
    )(c8, w_ada, b8, gn, x, *shards)


MATS = ("w_spatial", "w_pool", "b_spatial")
VECS = ("norm_ffn1_g", "norm_mix_g", "norm_ffn2_g", "norm_final_g", "pool_scale", "gmlp_ln_g", "gmlp_ln_b", "b_ada")
VEC_WIDTH = dict(norm_ffn1_g=D, norm_mix_g=D, norm_ffn2_g=D, norm_final_g=D, pool_scale=DP, gmlp_ln_g=DG,
                 gmlp_ln_b=DG, b_ada=9 * D)
MAT_ROWS = 1600
MAT_SLICE = MAT_ROWS // NDEV
VEC_LANES = sum(VEC_WIDTH.values()) + 128
DMOD_AT = VEC_LANES - 128 - 9 * D
SMALL = MATS + VECS


def _small_reduce(g_ws, g_wp, dbias, st1, st2, st3, st_f, vec2, dgate1, dgate3):
    wc = 9 * D // NDEV

    def body(g_ws_ref, g_wp_ref, dbias_ref, st1_ref, st2_ref, st3_ref, stf_ref, vec2_ref, dg1_ref, dg3_ref,
             tot_ref, rsum_ref, dmine_ref,
             pack_ref, rs_ref, ag_ref, rv_ref, dmp_ref, dw_ref, send_sems, recv_sems):
        me = _position()
        my = _slot(me)

        pack_ref[0:1024, :] = g_ws_ref[...].reshape(1024, 128)
        pack_ref[1024:1536, :] = g_wp_ref[...].reshape(512, 128)
        ch = lax.broadcasted_iota(jnp.int32, (DG, 128), 0)
        hd = lax.broadcasted_iota(jnp.int32, (DG, 128), 1)
        sel = jnp.where(ch // 64 == hd, 1.0, 0.0).astype(F32)
        heads = jnp.dot(dbias_ref[...], sel, preferred_element_type=F32, precision=lax.Precision.HIGHEST)
        pack_ref[1536:1544, :] = heads.T[0:8, :]
        pack_ref[1544:MAT_ROWS, :] = jnp.zeros((MAT_ROWS - 1544, 128), F32)
        dgate1 = dg1_ref[0:1, :] + dg1_ref[8:9, :]
        dgate3 = dg3_ref[0:1, :] + dg3_ref[8:9, :]
        row = jnp.concatenate(
            [st1_ref[2:3, :], st2_ref[2:3, :], st3_ref[2:3, :], stf_ref[0:1, :],
             vec2_ref[0:1, :], vec2_ref[1:2, :], vec2_ref[2:3, :],
             st1_ref[0:1, :], st1_ref[1:2, :], dgate1, st2_ref[0:1, :], st2_ref[1:2, :], st2_ref[3:4, :],
             st3_ref[0:1, :], st3_ref[1:2, :], dgate3, stf_ref[1:2, 0:128]], axis=1)
        rv_ref[my] = row
        for k in range(NDEV):
            dmp_ref[k] = row[:, DMOD_AT + wc * k:DMOD_AT + wc * (k + 1)]
        dw_ref[my] = dmp_ref[my]
        rs_ref[my] = pack_ref[pl.ds(pl.multiple_of(my * MAT_SLICE, 8), MAT_SLICE), :]

        first = []
        for d in range(1, NDEV):
            to = _flip(me, d)
            theirs = pl.ds(pl.multiple_of(_slot(to) * MAT_SLICE, 8), MAT_SLICE)
            first.append(_remote(pack_ref.at[theirs, :], rs_ref.at[my], send_sems.at[0, d - 1], recv_sems.at[0, d - 1], to))
            first.append(_remote(dmp_ref.at[_slot(to)], dw_ref.at[my], send_sems.at[1, d - 1], recv_sems.at[1, d - 1], to))
            first.append(_remote(rv_ref.at[my], rv_ref.at[my], send_sems.at[2, d - 1], recv_sems.at[2, d - 1], to))
        for cp in first:
            cp.start()
        for cp in first:
            cp.wait()
        red = rs_ref[0]
        for k in range(1, NDEV):
            red = red + rs_ref[k]
        ag_ref[my] = red
        second = []
        for d in range(1, NDEV):
            to = _flip(me, d)
            second.append(_remote(ag_ref.at[my], ag_ref.at[my], send_sems.at[3, d - 1], recv_sems.at[3, d - 1], to))
        for cp in second:
            cp.start()

        rsum = rv_ref[0]
        for k in range(1, NDEV):
            rsum = rsum + rv_ref[k]
        rsum_ref[...] = rsum
        r8 = lax.broadcasted_iota(jnp.int32, (8, 1), 0)
        dmine = jnp.zeros((8, wc), F32)
        for k in range(NDEV):
            dmine = dmine + jnp.where(r8 == k, dw_ref[k], 0.0)
        dmine_ref[...] = dmine

        for cp in second:
            cp.wait()
        for k in range(NDEV):
            tot_ref[k * MAT_SLICE:(k + 1) * MAT_SLICE, :] = ag_ref[k]

    ins = (g_ws, g_wp, dbias, st1, st2, st3, st_f, vec2, dgate1, dgate3)
    outs = [jax.ShapeDtypeStruct((MAT_ROWS, 128), F32), jax.ShapeDtypeStruct((1, VEC_LANES), F32),
            jax.ShapeDtypeStruct((8, wc), F32)]
    return pl.pallas_call(
        body, name="small_reduce", grid=(1,), out_shape=outs,
        in_specs=[_whole(a) for a in ins], out_specs=[_whole(a) for a in outs],
        scratch_shapes=[pltpu.VMEM((MAT_ROWS, 128), F32), pltpu.VMEM((NDEV, MAT_SLICE, 128), F32),
                        pltpu.VMEM((NDEV, MAT_SLICE, 128), F32),
                        pltpu.VMEM((NDEV, 1, VEC_LANES), F32), pltpu.VMEM((NDEV, 1, wc), F32),
                        pltpu.VMEM((NDEV, 1, wc), F32),
                        pltpu.SemaphoreType.DMA((4, 7)), pltpu.SemaphoreType.DMA((4, 7))],
        compiler_params=_cparams(32, ("arbitrary",)),
    )(*ins)


def _small_update(tot, rsum, params):
    flat = [a for k in SMALL for a in params[k]]
    n_in = 2 + len(flat)

    def body(*refs):
        tot_ref, rsum_ref = refs[:2]
        p_hbm = refs[2:n_in]
        o_refs = refs[n_in:n_in + 4 * len(SMALL)]
        loss_ref = refs[n_in + 4 * len(SMALL)]
        p_refs = refs[n_in + 4 * len(SMALL) + 1:-1]
        sem = refs[-1]
        fetch = [pltpu.make_async_copy(p_hbm[k], p_refs[k], sem.at[k]) for k in range(len(flat))]
        for cp in fetch:
            cp.start()
        for cp in fetch:
            cp.wait()
        loss_ref[...] = rsum_ref[:, VEC_LANES - 128:VEC_LANES]

        def update(idx, g):
            w_ref, m_ref, v_ref = p_refs[3 * idx:3 * idx + 3]
            g_out, d_out, m_out, v_out = o_refs[4 * idx:4 * idx + 4]
            g = g.reshape(w_ref.shape)
            g_out[...] = g
            d_out[...], m_out[...], v_out[...] = _adamw(w_ref[...], g, m_ref[...], v_ref[...])

        update(0, tot_ref[0:1024, :])
        update(1, tot_ref[1024:1536, :])
        update(2, tot_ref[1536:1544, :])
        at = 0
        for idx, k in enumerate(VECS):
            update(3 + idx, rsum_ref[:, at:at + VEC_WIDTH[k]])
            at += VEC_WIDTH[k]

    outs = []
    for k in SMALL:
        outs += [jax.ShapeDtypeStruct(params[k][0].shape, F32)] * 4
    outs += [jax.ShapeDtypeStruct((1, 128), F32)]
    res = pl.pallas_call(
        body, name="small_update", grid=(1,), out_shape=outs,
        in_specs=[_whole(tot), _whole(rsum)] + [HBM] * len(flat), out_specs=[_whole(a) for a in outs],
        scratch_shapes=[pltpu.VMEM(a.shape, F32) for a in flat] + [pltpu.SemaphoreType.DMA((len(flat),))],
        compiler_params=_cparams(32, ("arbitrary",)),
    )(tot, rsum, *[pltpu.with_memory_space_constraint(a, pltpu.HBM) for a in flat])
    return {k: tuple(res[4 * i:4 * i + 4]) for i, k in enumerate(SMALL)}, res[-1]


def _sum_adamw(recv, w, m, v, tr, name, own=None):
    R, C = w.shape

    def body(*refs):
        r_ref = refs[0]
        o_ref = refs[1] if own is not None else None
        w_ref, m_ref, v_ref, g_ref, d_ref, nm_ref, nv_ref = refs[-7:]
        my = _slot(_position()) if own is not None else None

        def part(k):
            if own is None:
                return r_ref[k].astype(F32)
            return jnp.where(my == k, o_ref[k], r_ref[k]).astype(F32)

        g = part(0)
        for k in range(1, NDEV):
            g = g + part(k)
        g_ref[...] = g
        d_ref[...], nm_ref[...], nv_ref[...] = _adamw(w_ref[...], g, m_ref[...], v_ref[...])

    blk = pl.BlockSpec((tr, C), lambda i: (i, 0))
    slots = pl.BlockSpec((NDEV, tr, C), lambda i: (0, i, 0))
    out = jax.ShapeDtypeStruct((R, C), F32)
    bufs = (recv,) if own is None else (recv, own)
    return pl.pallas_call(
        body, name=name, grid=(R // tr,), out_shape=[out] * 4,
        in_specs=[slots] * len(bufs) + [blk, blk, blk], out_specs=[blk] * 4,
        compiler_params=_cparams(48, ("arbitrary",)),
    )(*bufs, w, m, v)


SEM = pl.BlockSpec(memory_space=pltpu.SEMAPHORE)
EFFECT = pltpu.SideEffectType.DATAFLOW_SIDE_EFFECTING


def _exchange_start(buf, name):
    def body(src_ref, land_ref, send_sems, recv_sems, src_thru, land_thru, token):
        me = _position()
        for d in range(1, NDEV):
            to = _flip(me, d)
            _remote(src_ref.at[_slot(to)], land_ref.at[_slot(me)], send_sems.at[d - 1], recv_sems.at[d - 1], to).start()
        token[...] = jnp.zeros_like(token)

    like = pltpu.HBM(buf.shape, buf.dtype)
    return pl.pallas_call(
        body, name=name,
        out_shape=(pltpu.SemaphoreType.DMA((NDEV - 1,)), pltpu.SemaphoreType.DMA((NDEV - 1,)), like, like,
                   jax.ShapeDtypeStruct((8, 128), F32)),
        in_specs=(HBM, HBM), out_specs=(SEM, SEM, HBM, HBM, pl.BlockSpec(memory_space=pltpu.VMEM)),
        input_output_aliases={0: 2, 1: 3},
        compiler_params=pltpu.CompilerParams(has_side_effects=EFFECT),
    )(pltpu.with_memory_space_constraint(buf, pltpu.HBM),
      pltpu.with_memory_space_constraint(lax.empty(buf.shape, buf.dtype), pltpu.HBM))


def _exchange_wait(send_sems, recv_sems, src_thru, land_thru, after, name):
    def body(src_ref, land_ref, send_sems, recv_sems, after_ref, src_out, land_out):
        me = _position()
        for d in range(1, NDEV):
            to = _flip(me, d)
            cp = _remote(src_ref.at[_slot(to)], land_ref.at[_slot(me)], send_sems.at[d - 1], recv_sems.at[d - 1], to)
            cp.wait_send()
            cp.wait_recv()

    like = pltpu.HBM(src_thru.shape, src_thru.dtype)
    return pl.pallas_call(
        body, name=name, out_shape=(like, like),
        in_specs=(HBM, HBM, SEM, SEM, pl.BlockSpec(memory_space=pl.ANY)), out_specs=(HBM, HBM),
        input_output_aliases={0: 0, 1: 1},
        compiler_params=pltpu.CompilerParams(has_side_effects=EFFECT),
    )(src_thru, land_thru, send_sems, recv_sems, after)


def _ada_update(cact_all, dmine, w, m, v, tr):
    R, C = w.shape

    def body(c_ref, dm_ref, w_ref, m_ref, v_ref, g_ref, d_ref, nm_ref, nv_ref):
        g = _dot_tn(c_ref[...].astype(BF), dm_ref[...].astype(BF))
        g_ref[...] = g
        d_ref[...], nm_ref[...], nv_ref[...] = _adamw(w_ref[...], g, m_ref[...], v_ref[...])

    blk = pl.BlockSpec((tr, C), lambda i: (i, 0))
    out = jax.ShapeDtypeStruct((R, C), F32)
    return pl.pallas_call(
        body, name="update_w_ada", grid=(R // tr,), out_shape=[out] * 4,
        in_specs=[pl.BlockSpec((8, tr), lambda i: (0, i)), pl.BlockSpec((8, C), lambda i: (0, 0)), blk, blk, blk],
        out_specs=[blk] * 4,
        compiler_params=_cparams(48, ("arbitrary",)),
    )(cact_all, dmine, w, m, v)


FC = F // 2
FC_FWD = F


def _ffn_fwd(x, mod, gn, w_in_t, w_out, sub, name, gather=(), loss=None, h=None):
    S = x.shape[0]
    T = min(T_FFN, S)
    FC = FC_FWD
    nS, nJ = S // T, F // FC
    ng = len(gather)
    nl = 2 if loss else 0
    nh = 0 if h is None else 1
    forward_step = nS // 2

    def body(*refs):
        x_ref, mod_ref, gn_ref, wg_ref, wu_ref, wo_ref = refs[:6]
        gf_ref, t_ref = refs[6 + nh:6 + nh + nl] if loss else (None, None)
        shards = refs[6 + nh + nl:6 + nh + nl + ng]
        at = 6 + nh + nl + ng
        xo_ref, gu_ref = refs[at:at + 2]
        h_ref = refs[6] if nh else refs[at + 2]
        at += 3 - nh
        gathered = refs[at:at + ng]
        at += ng
        st_ref = refs[at] if loss else None
        at += nl // 2
        acc_scr = refs[at]
        sems = refs[at + 1:]
        i, j = pl.program_id(0), pl.program_id(1)

        if ng:
            @pl.when((i == 0) & (j == 0))
            def _():
                _gather_phase("start", shards, gathered, sems)

            @pl.when((i == forward_step) & (j == 0))
            def _():
                _gather_phase("forward", shards, gathered, sems)

        @pl.when(j == 0)
        def _():
            if not nh:
                _, _, _, hh = _rms_mod(x_ref[...], gn_ref[...], mod_ref[3 * sub:3 * sub + 1, :],
                                       mod_ref[3 * sub + 1:3 * sub + 2, :])
                h_ref[...] = hh.astype(BF)
            acc_scr[...] = jnp.zeros_like(acc_scr)

        hb = h_ref[...]
        g = _dot_nt(hb, wg_ref[0])
        u = _dot_nt(hb, wu_ref[0])
        gu_ref[0] = g.astype(BF)
        gu_ref[1] = u.astype(BF)
        a = (g * _sigmoid(g) * u).astype(BF)
        acc_scr[...] += _dot(a, wo_ref[...])

        @pl.when(j == nJ - 1)
        def _():
            xo = x_ref[...] + (0.5 * mod_ref[3 * sub + 2:3 * sub + 3, :]) * acc_scr[...]
            if not loss:
                xo_ref[...] = xo
            else:
                dx, dgf, part = _final_norm_loss(xo, gf_ref[...], t_ref[...])
                xo_ref[...] = dx
                upd = _rows3(dgf, jnp.broadcast_to(part, (1, D)), jnp.zeros((1, D), F32), D)

                @pl.when(i == 0)
                def _():
                    st_ref[...] = upd

                @pl.when(i > 0)
                def _():
                    st_ref[...] += upd

        if ng:
            @pl.when((i == nS - 1) & (j == nJ - 1))
            def _():
                _gather_phase("finish", shards, gathered, sems)

    tile = pl.BlockSpec((T, D), lambda i, j: (i, 0))
    res = pl.pallas_call(
        body, name=name, grid=(nS, nJ),
        out_shape=[jax.ShapeDtypeStruct((S, D), F32), jax.ShapeDtypeStruct((2, S, F), BF)]
                  + ([] if nh else [jax.ShapeDtypeStruct((S, D), BF)])
                  + [jax.ShapeDtypeStruct((NDEV,) + s.shape, s.dtype) for s in gather]
                  + ([jax.ShapeDtypeStruct((8, D), F32)] if loss else []),
        in_specs=[tile,
                  pl.BlockSpec((9, D), lambda i, j: (0, 0)),
                  pl.BlockSpec((1, D), lambda i, j: (0, 0)),
                  pl.BlockSpec((1, FC, D), lambda i, j: (0, j, 0)),
                  pl.BlockSpec((1, FC, D), lambda i, j: (1, j, 0)),
                  pl.BlockSpec((FC, D), lambda i, j: (j, 0))] + [tile] * nh
                 + ([pl.BlockSpec((1, D), lambda i, j: (0, 0)), tile] if loss else []) + [HBM] * ng,
        out_specs=[tile, pl.BlockSpec((2, T, FC), lambda i, j: (0, i, j))] + [tile] * (1 - nh) + [HBM] * ng
                  + ([pl.BlockSpec((8, D), lambda i, j: (0, 0))] if loss else []),
        scratch_shapes=[pltpu.VMEM((T, D), F32)] + (_comm_sems(ng) if ng else []),
        compiler_params=_cparams(60, ("arbitrary", "arbitrary")),
    )(x, mod, gn, w_in_t, w_in_t, w_out, *(() if h is None else (h,)), *(loss or ()), *gather)
    return res if h is None else [res[0], res[1], h, *res[2:]]


def _ffn_bwd_hidden(dx, mod, gu, w_out, sub, name, exchange=()):
    S = dx.shape[0]
    T = min(T_FFN, S)
    nS, nJ = S // T, F // FC
    ne = len(exchange)

    def body(*refs):
        dx_ref, mod_ref, gu_ref, wo_ref = refs[:4]
        sendbufs = refs[4:4 + ne]
        dgu_ref, gw_ref, dgate_ref = refs[4 + ne:7 + ne]
        recvbufs = refs[7 + ne:7 + 2 * ne]
        acc_scr = refs[7 + 2 * ne]
        sems = refs[8 + 2 * ne:]
        j, i = pl.program_id(0), pl.program_id(1)

        if ne:
            @pl.when((i == 0) & (j == 0))
            def _():
                _exchange_phase("start", sendbufs, recvbufs, sems)

        gate = mod_ref[3 * sub + 2:3 * sub + 3, :]
        dx = dx_ref[...]
        da = _dot_nt((dx * (0.5 * gate)).astype(BF), wo_ref[...])
        g = gu_ref[0].astype(F32)
        u = gu_ref[1].astype(F32)
        sg = _sigmoid(g)
        s = g * sg
        dgu_ref[0] = (da * u * (sg * (1.0 + g * (1.0 - sg)))).astype(BF)
        dgu_ref[1] = (da * s).astype(BF)
        contrib = _dot_tn((s * u).astype(BF), dx.astype(BF))

        @pl.when(i == 0)
        def _():
            acc_scr[...] = contrib

        @pl.when(i > 0)
        def _():
            acc_scr[...] += contrib

        @pl.when(i == nS - 1)
        def _():
            acc = acc_scr[...]
            dgate = 0.5 * jnp.sum(acc * wo_ref[...].astype(F32), axis=0, keepdims=True)
            dgate_ref[...] = jnp.broadcast_to(dgate, (8, D))
            gw_ref[...] = (acc * (0.5 * gate)).astype(BF)

        if ne:
            @pl.when((i == nS - 1) & (j == nJ - 1))
            def _():
                _exchange_phase("wait", sendbufs, recvbufs, sems)

    return pl.pallas_call(
        body, name=name, grid=(nJ, nS),
        out_shape=[jax.ShapeDtypeStruct((2, S, F), BF), jax.ShapeDtypeStruct((F, D), BF),
                   jax.ShapeDtypeStruct((8 * nJ, D), F32)] + _like(exchange),
        in_specs=[pl.BlockSpec((T, D), lambda j, i: (i, 0)),
                  pl.BlockSpec((9, D), lambda j, i: (0, 0)),
                  pl.BlockSpec((2, T, FC), lambda j, i: (0, i, j)),
                  pl.BlockSpec((FC, D), lambda j, i: (j, 0))] + [HBM] * ne,
        out_specs=[pl.BlockSpec((2, T, FC), lambda j, i: (0, i, j)),
                   pl.BlockSpec((FC, D), lambda j, i: (j, 0)),
                   pl.BlockSpec((8, D), lambda j, i: (j, 0))] + [HBM] * ne,
        scratch_shapes=[pltpu.VMEM((FC, D), F32)] + (_comm_sems(ne) if ne else []),
        compiler_params=_cparams(56, ("arbitrary", "arbitrary")),
    )(dx, mod, gu, w_out, *exchange)


def _ffn_bwd_input(dgu, w_in_t, x, dx, mod, gn, sub, name, exchange=()):
    S = x.shape[0]
    T = min(T_FFN, S)
    nS = S // T
    ne = len(exchange)
    NC = 256
    chunks = [slice(k * NC, (k + 1) * NC) for k in range(D // NC)]

    def body(*refs):
        dgu_ref, w_ref, x_ref, dx_ref, mod_ref, gn_ref = refs[:6]
        sendbufs = refs[6:6 + ne]
        dxin_ref, st_ref = refs[6 + ne:8 + ne]
        recvbufs = refs[8 + ne:8 + 2 * ne]
        dxh_scr = refs[8 + 2 * ne]
        sems = refs[9 + 2 * ne:]
        i = pl.program_id(0)

        if ne:
            @pl.when(i == 0)
            def _():
                _exchange_phase("start", sendbufs, recvbufs, sems)

        gn = gn_ref[...]
        scale = mod_ref[3 * sub + 1:3 * sub + 2, :]
        r, xhat, n, _ = _rms_mod(x_ref[...], gn, mod_ref[3 * sub:3 * sub + 1, :], scale)
        dg = dgu_ref[0]
        du = dgu_ref[1]
        rowsum = jnp.zeros((T, 1), F32)
        dshift, dscale, dgn = [], [], []
        for cols in chunks:
            dh = _dot(dg, w_ref[0, :, cols]) + _dot(du, w_ref[1, :, cols])
            dshift.append(jnp.sum(dh, axis=0, keepdims=True))
            dscale.append(jnp.sum(dh * n[:, cols], axis=0, keepdims=True))
            dn = dh * (1.0 + scale[:, cols])
            dgn.append(jnp.sum(dn * xhat[:, cols], axis=0, keepdims=True))
            dxhat = dn * gn[:, cols]
            rowsum = rowsum + jnp.sum(dxhat * xhat[:, cols], axis=-1, keepdims=True)
            dxh_scr[:, cols] = dxhat
        dxin_ref[...] = dx_ref[...] + r * (dxh_scr[...] - xhat * (rowsum / D))
        cat = lambda parts: jnp.concatenate(parts, axis=1)
        upd = _rows3(cat(dshift), cat(dscale), cat(dgn), D)

        @pl.when(i == 0)
        def _():
            st_ref[...] = upd

        @pl.when(i > 0)
        def _():
            st_ref[...] += upd

        if ne:
            @pl.when(i == nS - 1)
            def _():
                _exchange_phase("wait", sendbufs, recvbufs, sems)

    tile = pl.BlockSpec((T, D), lambda i: (i, 0))
    return pl.pallas_call(
        body, name=name, grid=(nS,),
        out_shape=[jax.ShapeDtypeStruct((S, D), F32), jax.ShapeDtypeStruct((8, D), F32)] + _like(exchange),
        in_specs=[pl.BlockSpec((2, T, F), lambda i: (0, i, 0)),
                  pl.BlockSpec((2, F, D), lambda i: (0, 0, 0), pipeline_mode=pl.Buffered(1)),
                  tile, tile,
                  pl.BlockSpec((9, D), lambda i: (0, 0)),
                  pl.BlockSpec((1, D), lambda i: (0, 0))] + [HBM] * ne,
        out_specs=[tile, pl.BlockSpec((8, D), lambda i: (0, 0))] + [HBM] * ne,
        scratch_shapes=[pltpu.VMEM((T, D), F32)] + (_comm_sems(ne) if ne else []),
        compiler_params=_cparams(60, ("arbitrary",)),
    )(dgu, w_in_t, x, dx, mod, gn, *exchange)


def _ffn_bwd_win(h, dgu, name, exchange=()):
    S = h.shape[0]
    T = min(T_WIN, S)
    nS, nJ = S // T, F // FC
    ne = len(exchange)

    def body(*refs):
        h_ref, dgu_ref = refs[:2]
        sendbufs = refs[2:2 + ne]
        out_ref = refs[2 + ne]
        recvbufs = refs[3 + ne:3 + 2 * ne]
        acc_scr = refs[3 + 2 * ne]
        sems = refs[4 + 2 * ne:]
        p, j, i = pl.program_id(0), pl.program_id(1), pl.program_id(2)

        if ne:
            @pl.when((p == 0) & (j == 0) & (i == 0))
            def _():
                _exchange_phase("start", sendbufs, recvbufs, sems)

        contrib = _dot_tn(dgu_ref[0], h_ref[...])

        @pl.when(i == 0)
        def _():
            acc_scr[...] = contrib

        @pl.when(i > 0)
        def _():
            acc_scr[...] += contrib

        @pl.when(i == nS - 1)
        def _():
            out_ref[0] = acc_scr[...].astype(BF)

        if ne:
            @pl.when((p == 1) & (j == nJ - 1) & (i == nS - 1))
            def _():
                _exchange_phase("wait", sendbufs, recvbufs, sems)

    return pl.pallas_call(
        body, name=name, grid=(2, nJ, nS),
        out_shape=[jax.ShapeDtypeStruct((2, F, D), BF)] + _like(exchange),
        in_specs=[pl.BlockSpec((T, D), lambda p, j, i: (i, 0)),
                  pl.BlockSpec((1, T, FC), lambda p, j, i: (p, i, j))] + [HBM] * ne,
        out_specs=[pl.BlockSpec((1, FC, D), lambda p, j, i: (p, j, 0))] + [HBM] * ne,
        scratch_shapes=[pltpu.VMEM((FC, D), F32)] + (_comm_sems(ne) if ne else []),
        compiler_params=_cparams(56, ("arbitrary", "arbitrary", "arbitrary")),
    )(h, dgu, *exchange)


def _pool_counts(pos0, T):
    pos = pos0 + lax.broadcasted_iota(jnp.int32, (T, 1), 0)
    return [jnp.minimum(pos + 1, w).astype(F32) for w in WINDOWS]


def _pool_fwd(xa, halo, ext_scr, cnts, T):
    ext_scr[0:HALO, :] = halo
    ext_scr[HALO:HALO + T, :] = xa
    out = []
    for gi, w in enumerate(WINDOWS):
        cols = slice(128 * gi, 128 * gi + 128)
        acc = xa[:, cols]
        for k in range(1, w):
            acc = acc + ext_scr[HALO - k:HALO - k + T, cols]
        out.append(acc / cnts[gi] - xa[:, cols])
    return out


def _sgu_fwd(vnb, ws_ref, sv_scr, T):
    lane = lax.broadcasted_iota(jnp.int32, (CHUNK, 128), 1)
    for n in range(T // CHUNK):
        rows = slice(n * CHUNK, (n + 1) * CHUNK)
        for b in range(DG // 128):
            cols = slice(128 * b, 128 * b + 128)
            vb = vnb[rows, cols]
            sv_scr[rows, cols] = jnp.where(lane < 64, _dot(ws_ref[2 * b], vb), _dot(ws_ref[2 * b + 1], vb))


def _mix_fwd(x, mod, gn, gn_next, wmi, wmo, wp, ps, lg, lb, ws, bias, name):
    S = x.shape[0]
    T = min(T_MIX_FWD, S)

    def body(x_ref, mod_ref, gn_ref, gnn_ref, wmi_ref, wmo_ref, wp_ref, ps_ref, lg_ref, lb_ref, ws_ref, bias_ref,
             xo_ref, hn_ref, carry_scr, ext_scr, sv_scr, ycat_scr):
        i = pl.program_id(0)

        @pl.when(i == 0)
        def _():
            carry_scr[...] = jnp.zeros_like(carry_scr)

        x = x_ref[...]
        _, _, _, h = _rms_mod(x, gn_ref[...], mod_ref[3:4, :], mod_ref[4:5, :])
        proj = _dot_nt(h.astype(BF), wmi_ref[...])
        xa = proj[:, 0:DP]
        p = _pool_fwd(xa, carry_scr[...], ext_scr, _pool_counts(i * T, T), T)
        carry_scr[...] = xa[T - HALO:T, :]
        for gi in range(4):
            cols = slice(128 * gi, 128 * gi + 128)
            ycat_scr[:, cols] = (_dot(p[gi].astype(BF), wp_ref[gi]) * ps_ref[:, cols]).astype(BF)
        u, _ = _gelu(proj[:, DP:DP + DG])
        v, _ = _gelu(proj[:, DP + DG:DPROJ])
        mu = jnp.mean(v, axis=-1, keepdims=True)
        vc = v - mu
        rstd = lax.rsqrt(jnp.mean(vc * vc, axis=-1, keepdims=True) + EPS)
        vn = vc * rstd * lg_ref[...] + lb_ref[...]
        _sgu_fwd(vn.astype(BF), ws_ref, sv_scr, T)
        for n in range(T // CHUNK):
            rows = slice(n * CHUNK, (n + 1) * CHUNK)
            ycat_scr[rows, DP:D] = (u[rows, :] * (sv_scr[rows, :] + bias_ref[...])).astype(BF)
        xo = x + mod_ref[5:6, :] * _dot(ycat_scr[...], wmo_ref[...])
        xo_ref[...] = xo
        _, _, _, hn = _rms_mod(xo, gnn_ref[...], mod_ref[6:7, :], mod_ref[7:8, :])
        hn_ref[...] = hn.astype(BF)

    full = lambda shape: pl.BlockSpec(shape, lambda i: (0,) * len(shape))
    tile = pl.BlockSpec((T, D), lambda i: (i, 0))
    return pl.pallas_call(
        body, name=name, grid=(S // T,),
        out_shape=[jax.ShapeDtypeStruct((S, D), F32), jax.ShapeDtypeStruct((S, D), BF)],
        in_specs=[tile, full((9, D)), full((1, D)), full((1, D)), full((DPROJ, D)), full((D, D)),
                  full((4, 128, 128)), full((1, DP)), full((1, DG)), full((1, DG)), full((8, CHUNK, CHUNK)),
                  full((CHUNK, DG))],
        out_specs=[tile, tile],
        scratch_shapes=[pltpu.VMEM((HALO, DP), F32), pltpu.VMEM((T + HALO, DP), F32), pltpu.VMEM((T, DG), F32),
                        pltpu.VMEM((T, D), BF)],
        compiler_params=_cparams(48, ("arbitrary",)),
    )(x, mod, gn, gn_next, wmi, wmo, wp, ps, lg, lb, ws, bias)


def _mix_bwd(x, dxo, mod, gn, wmi, wmo, wp, ps, lg, lb, ws, bias, name, exchange=()):
    S = x.shape[0]
    T = min(T_MIX, S)
    nS = S // T
    hb = T // HALO
    ne = len(exchange)

    def body(*refs):
        (x_ref, xh_ref, dxo_ref, mod_ref, gn_ref, wmi_ref, wmo_ref, wp_ref, ps_ref, lg_ref, lb_ref, ws_ref,
         bias_ref) = refs[:13]
        sendbufs = refs[13:13 + ne]
        dxi_ref, gwmi_out, gwmo_out, gwp_ref, gws_ref, st_ref, vec_ref, dbias_ref = refs[13 + ne:21 + ne]
        recvbufs = refs[21 + ne:21 + 2 * ne]
        (carry_scr, ext_scr, qext_scr, sv_scr, dvn_scr, ycat_scr, dproj_scr, gwmi_ref,
         gwmo_ref) = refs[21 + 2 * ne:30 + 2 * ne]
        sems = refs[30 + 2 * ne:]
        i = pl.program_id(0)
        t = nS - 1 - i
        gn = gn_ref[...]
        shift, scale, gate = mod_ref[3:4, :], mod_ref[4:5, :], mod_ref[5:6, :]

        @pl.when(i == 0)
        def _():
            if ne:
                _exchange_phase("start", sendbufs, recvbufs, sems)
            carry_scr[...] = jnp.zeros_like(carry_scr)
            gwmi_ref[...] = jnp.zeros_like(gwmi_ref)
            gwmo_ref[...] = jnp.zeros_like(gwmo_ref)
            gwp_ref[...] = jnp.zeros_like(gwp_ref)
            gws_ref[...] = jnp.zeros_like(gws_ref)
            st_ref[...] = jnp.zeros_like(st_ref)
            vec_ref[...] = jnp.zeros_like(vec_ref)
            dbias_ref[...] = jnp.zeros_like(dbias_ref)

        x = x_ref[...]
        dxo = dxo_ref[...]
        r, xhat, n, h = _rms_mod(x, gn, shift, scale)
        hbf = h.astype(BF)
        proj = _dot_nt(hbf, wmi_ref[...])
        xa = proj[:, 0:DP]
        zu = proj[:, DP:DP + DG]
        zv = proj[:, DP + DG:DPROJ]
        _, _, _, hh = _rms_mod(xh_ref[...], gn, shift, scale)
        halo = _dot_nt(hh.astype(BF), wmi_ref[0:DP, :])
        halo = jnp.where(t == 0, 0.0, halo)
        cnts = _pool_counts(t * T, T)
        p = _pool_fwd(xa, halo, ext_scr, cnts, T)
        m = []
        for gi in range(4):
            cols = slice(128 * gi, 128 * gi + 128)
            m.append(_dot(p[gi].astype(BF), wp_ref[gi]))
            ycat_scr[:, cols] = (m[gi] * ps_ref[:, cols]).astype(BF)
        u, tu = _gelu(zu)
        v, tv = _gelu(zv)
        mu = jnp.mean(v, axis=-1, keepdims=True)
        vc = v - mu
        rstd = lax.rsqrt(jnp.mean(vc * vc, axis=-1, keepdims=True) + EPS)
        vhat = vc * rstd
        lg = lg_ref[...]
        vnb = (vhat * lg + lb_ref[...]).astype(BF)
        _sgu_fwd(vnb, ws_ref, sv_scr, T)
        for nck in range(T // CHUNK):
            rows = slice(nck * CHUNK, (nck + 1) * CHUNK)
            sv_scr[rows, :] = sv_scr[rows, :] + bias_ref[...]
        sv = sv_scr[...]
        ycat_scr[:, DP:D] = (u * sv).astype(BF)

        gwmo_ref[...] += _dot_tn(ycat_scr[...], dxo.astype(BF))
        dyc = _dot_nt((dxo * gate).astype(BF), wmo_ref[...])
        dya = dyc[:, 0:DP]
        dyb = dyc[:, DP:D]

        dps = []
        dp = []
        for gi in range(4):
            cols = slice(128 * gi, 128 * gi + 128)
            dps.append(jnp.sum(dya[:, cols] * m[gi], axis=0, keepdims=True))
            dm = (dya[:, cols] * ps_ref[:, cols]).astype(BF)
            gwp_ref[gi] += _dot_tn(p[gi].astype(BF), dm)
            dp.append(_dot_nt(dm, wp_ref[gi]))
            qext_scr[0:T, cols] = dp[gi] / cnts[gi]
        qext_scr[T:T + HALO, :] = carry_scr[...]
        for gi, w in enumerate(WINDOWS):
            cols = slice(128 * gi, 128 * gi + 128)
            acc = qext_scr[0:T, cols]
            for k in range(1, w):
                acc = acc + qext_scr[k:k + T, cols]
            dproj_scr[:, cols] = (acc - dp[gi]).astype(BF)
        carry_scr[...] = qext_scr[0:HALO, :]

        du = dyb * sv
        dsv = dyb * u
        lane = lax.broadcasted_iota(jnp.int32, (CHUNK, 128), 1)
        dbias = jnp.zeros((CHUNK, DG), F32)
        for nck in range(T // CHUNK):
            rows = slice(nck * CHUNK, (nck + 1) * CHUNK)
            dbias = dbias + dsv[rows, :]
            for b in range(DG // 128):
                cols = slice(128 * b, 128 * b + 128)
                dsvb = dsv[rows, cols]
                vb = vnb[rows, cols]
                gws_ref[2 * b] += _dot_nt(jnp.where(lane < 64, dsvb, 0.0).astype(BF), vb)
                gws_ref[2 * b + 1] += _dot_nt(jnp.where(lane < 64, 0.0, dsvb).astype(BF), vb)
                dsvbb = dsvb.astype(BF)
                dvn_scr[rows, cols] = jnp.where(lane < 64, _dot_tn(ws_ref[2 * b], dsvbb),
                                                _dot_tn(ws_ref[2 * b + 1], dsvbb))
        dbias_ref[...] += dbias
        dvn = dvn_scr[...]
        dlg = jnp.sum(dvn * vhat, axis=0, keepdims=True)
        dlb = jnp.sum(dvn, axis=0, keepdims=True)
        dvhat = dvn * lg
        dv = rstd * (dvhat - jnp.mean(dvhat, axis=-1, keepdims=True)
                     - vhat * jnp.mean(dvhat * vhat, axis=-1, keepdims=True))
        dproj_scr[:, DP:DP + DG] = (du * _gelu_grad(zu, tu)).astype(BF)
        dproj_scr[:, DP + DG:DPROJ] = (dv * _gelu_grad(zv, tv)).astype(BF)
        vec_ref[...] += _rows3(jnp.concatenate(dps, axis=1), dlg, dlb, DP)

        dproj = dproj_scr[...]
        gwmi_ref[...] += _dot_tn(dproj, hbf)
        dh = _dot(dproj, wmi_ref[...])
        dxi, dshift, dscale, dgn = _rms_mod_bwd(dh, dxo, r, xhat, n, gn, scale)
        dxi_ref[...] = dxi
        st_ref[...] += _rows3(dshift, dscale, dgn, D)

        @pl.when(i == nS - 1)
        def _():
            acc = gwmo_ref[...]
            dgate = jnp.sum(acc * wmo_ref[...].astype(F32), axis=0, keepdims=True)
            row = lax.broadcasted_iota(jnp.int32, (8, D), 0)
            st_ref[...] += jnp.where(row == 3, dgate, 0.0)
            gwmo_out[...] = (acc * gate).astype(BF)
            gwmi_out[...] = gwmi_ref[...].astype(BF)
            tt = lax.broadcasted_iota(jnp.int32, (CHUNK, CHUNK), 0)
            ss = lax.broadcasted_iota(jnp.int32, (CHUNK, CHUNK), 1)
            for hd in range(8):
                gws_ref[hd] = jnp.where(tt >= ss, gws_ref[hd], 0.0)
            if ne:
                _exchange_phase("wait", sendbufs, recvbufs, sems)

    full = lambda shape: pl.BlockSpec(shape, lambda i: (0,) * len(shape))
    return pl.pallas_call(
        body, name=name, grid=(nS,),
        out_shape=[jax.ShapeDtypeStruct((S, D), F32), jax.ShapeDtypeStruct((DPROJ, D), BF),
                   jax.ShapeDtypeStruct((D, D), BF), jax.ShapeDtypeStruct((4, 128, 128), F32),
                   jax.ShapeDtypeStruct((8, CHUNK, CHUNK), F32), jax.ShapeDtypeStruct((8, D), F32),
                   jax.ShapeDtypeStruct((8, DP), F32), jax.ShapeDtypeStruct((CHUNK, DG), F32)] + _like(exchange),
        in_specs=[pl.BlockSpec((T, D), lambda i: (nS - 1 - i, 0)),
                  pl.BlockSpec((HALO, D), lambda i: (jnp.maximum((nS - 1 - i) * hb - 1, 0), 0)),
                  pl.BlockSpec((T, D), lambda i: (nS - 1 - i, 0)),
                  full((9, D)), full((1, D)), full((DPROJ, D)), full((D, D)),
                  full((4, 128, 128)), full((1, DP)), full((1, DG)), full((1, DG)), full((8, CHUNK, CHUNK)),
                  full((CHUNK, DG))] + [HBM] * ne,
        out_specs=[pl.BlockSpec((T, D), lambda i: (nS - 1 - i, 0)), full((DPROJ, D)), full((D, D)),
                   full((4, 128, 128)), full((8, CHUNK, CHUNK)), full((8, D)), full((8, DP)), full((CHUNK, DG))]
                  + [HBM] * ne,
        scratch_shapes=[pltpu.VMEM((HALO, DP), F32), pltpu.VMEM((T + HALO, DP), F32),
                        pltpu.VMEM((T + HALO, DP), F32), pltpu.VMEM((T, DG), F32), pltpu.VMEM((T, DG), F32),
                        pltpu.VMEM((T, D), BF), pltpu.VMEM((T, DPROJ), BF), pltpu.VMEM((DPROJ, D), F32),
                        pltpu.VMEM((D, D), F32)] + (_comm_sems(ne) if ne else []),
        compiler_params=_cparams(56, ("arbitrary",)),
    )(x, x, dxo, mod, gn, wmi, wmo, wp, ps, lg, lb, ws, bias, *exchange)


def kernel(x, c, w_ada, b_ada, norm_ffn1_g, ffn1_w_in, ffn1_w_out, norm_mix_g, w_mix_in, w_pool, pool_scale, gmlp_ln_g, gmlp_ln_b, w_spatial, b_spatial, w_mix_out, norm_ffn2_g, ffn2_w_in, ffn2_w_out, norm_final_g, loss_target, m_w_ada, m_b_ada, m_norm_ffn1_g, m_ffn1_w_in, m_ffn1_w_out, m_norm_mix_g, m_w_mix_in, m_w_pool, m_pool_scale, m_gmlp_ln_g, m_gmlp_ln_b, m_w_spatial, m_b_spatial, m_w_mix_out, m_norm_ffn2_g, m_ffn2_w_in, m_ffn2_w_out, m_norm_final_g, v_w_ada, v_b_ada, v_norm_ffn1_g, v_ffn1_w_in, v_ffn1_w_out, v_norm_mix_g, v_w_mix_in, v_w_pool, v_pool_scale, v_gmlp_ln_g, v_gmlp_ln_b, v_w_spatial, v_b_spatial, v_w_mix_out, v_norm_ffn2_g, v_ffn2_w_in, v_ffn2_w_out, v_norm_final_g):
    weights = dict(w_ada=w_ada, b_ada=b_ada, norm_ffn1_g=norm_ffn1_g, ffn1_w_in=ffn1_w_in, ffn1_w_out=ffn1_w_out,
                   norm_mix_g=norm_mix_g, w_mix_in=w_mix_in, w_pool=w_pool, pool_scale=pool_scale,
                   gmlp_ln_g=gmlp_ln_g, gmlp_ln_b=gmlp_ln_b, w_spatial=w_spatial, b_spatial=b_spatial,
                   w_mix_out=w_mix_out, norm_ffn2_g=norm_ffn2_g, ffn2_w_in=ffn2_w_in, ffn2_w_out=ffn2_w_out,
                   norm_final_g=norm_final_g)
    mom1 = dict(w_ada=m_w_ada, b_ada=m_b_ada, norm_ffn1_g=m_norm_ffn1_g, ffn1_w_in=m_ffn1_w_in,
                ffn1_w_out=m_ffn1_w_out, norm_mix_g=m_norm_mix_g, w_mix_in=m_w_mix_in, w_pool=m_w_pool,
                pool_scale=m_pool_scale, gmlp_ln_g=m_gmlp_ln_g, gmlp_ln_b=m_gmlp_ln_b, w_spatial=m_w_spatial,
                b_spatial=m_b_spatial, w_mix_out=m_w_mix_out, norm_ffn2_g=m_norm_ffn2_g, ffn2_w_in=m_ffn2_w_in,
                ffn2_w_out=m_ffn2_w_out, norm_final_g=m_norm_final_g)
    mom2 = dict(w_ada=v_w_ada, b_ada=v_b_ada, norm_ffn1_g=v_norm_ffn1_g, ffn1_w_in=v_ffn1_w_in,
                ffn1_w_out=v_ffn1_w_out, norm_mix_g=v_norm_mix_g, w_mix_in=v_w_mix_in, w_pool=v_w_pool,
                pool_scale=v_pool_scale, gmlp_ln_g=v_gmlp_ln_g, gmlp_ln_b=v_gmlp_ln_b, w_spatial=v_w_spatial,
                b_spatial=v_b_spatial, w_mix_out=v_w_mix_out, norm_ffn2_g=v_norm_ffn2_g, ffn2_w_in=v_ffn2_w_in,
                ffn2_w_out=v_ffn2_w_out, norm_final_g=v_norm_final_g)
    order = list(weights)
    xs = x[0]
    target = loss_target[0]
    transposed = ("ffn1_w_in", "w_mix_in", "ffn2_w_in")
    big = ("ffn1_w_in", "ffn1_w_out", "w_mix_in", "w_mix_out", "ffn2_w_in", "ffn2_w_out")
    local = lambda a, k: a[0].T if k in transposed else a[0]
    wc = w_ada.shape[2]

    shard = dict(zip(big, _cast_shards([local(weights[k], k) for k in big])))
    mod, cact_all, h1, g_w1_in, g_w1_out = _ada_forward(
        jnp.broadcast_to(c, (8, D)), w_ada[0], b_ada.reshape(NDEV, wc), [shard["ffn1_w_in"], shard["ffn1_w_out"]],
        xs, norm_ffn1_g)
    w1_in = g_w1_in.reshape(2, F, D)
    w1_out = g_w1_out.reshape(F, D)

    x1, gu1, h1, g_wmi, g_wmo, g_w2_out, g_w2_in = _ffn_fwd(
        xs, mod, norm_ffn1_g, w1_in, w1_out, 0, "ffn1_fwd", h=h1,
        gather=[shard["w_mix_in"], shard["w_mix_out"], shard["ffn2_w_out"], shard["ffn2_w_in"]])
    wmi = g_wmi.reshape(DPROJ, D)
    wmo = g_wmo.reshape(D, D)
    w2_in = g_w2_in.reshape(2, F, D)
    w2_out = g_w2_out.reshape(F, D)
    tril = jnp.tril(jnp.ones((CHUNK, CHUNK), dtype=bool))
    ws_b = jnp.where(tril[None], w_spatial[0], 0.0).astype(BF)
    wp_b = w_pool[0].astype(BF)
    bias = jnp.repeat(b_spatial[0].T, DG // 8, axis=1)
    mix_args = (wmi, wmo, wp_b, pool_scale, gmlp_ln_g, gmlp_ln_b, ws_b, bias)
    x2, h3 = _mix_fwd(x1, mod, norm_mix_g, norm_ffn2_g, *mix_args, "mix_fwd")
    dx3, gu3, h3, st_f = _ffn_fwd(x2, mod, norm_ffn2_g, w2_in, w2_out, 2, "ffn2_fwd", h=h3,
                                  loss=(norm_final_g.reshape(1, D), target))

    slots = lambda a: a.reshape(NDEV, a.size // (NDEV * D), D)
    dgu3, d_w2_out, dgate3 = _ffn_bwd_hidden(dx3, mod, gu3, w2_out, 2, "ffn2_bwd_hidden")
    d_w2_in = _ffn_bwd_win(h3, dgu3, "ffn2_bwd_win")[0]
    dx2, st3 = _ffn_bwd_input(dgu3, w2_in, x2, dx3, mod, norm_ffn2_g, 2, "ffn2_bwd_input")
    dx1, d_wmi, d_wmo, d_wp, d_ws, st2, vec2, dbias, r_w2_in, r_w2_out = _mix_bwd(
        x1, dx2, mod, norm_mix_g, *mix_args, "mix_bwd", exchange=[slots(d_w2_in), slots(d_w2_out)])
    dgu1, d_w1_out, dgate1, r_wmi, r_wmo = _ffn_bwd_hidden(
        dx1, mod, gu1, w1_out, 0, "ffn1_bwd_hidden", exchange=[slots(d_wmi), slots(d_wmo)])
    d_w1_in, r_w1_out = _ffn_bwd_win(h1, dgu1, "ffn1_bwd_win", exchange=[slots(d_w1_out)])
    send_sems, recv_sems, sent, landing, token = _exchange_start(slots(d_w1_in), "w_in_grad_start")
    dx0, st1 = _ffn_bwd_input(dgu1, w1_in, xs, dx1, mod + token[0, 0], norm_ffn1_g, 0, "ffn1_bwd_input")

    received = dict(ffn1_w_out=r_w1_out, w_mix_in=r_wmi, w_mix_out=r_wmo, ffn2_w_in=r_w2_in, ffn2_w_out=r_w2_out)
    tiles = dict(ffn1_w_in=176, ffn1_w_out=176, w_mix_in=96, w_mix_out=128, ffn2_w_in=176, ffn2_w_out=176)
    result = {}

    def update(k, recv, own=None):
        res = _sum_adamw(recv, local(weights[k], k), local(mom1[k], k), local(mom2[k], k), tiles[k], "update_" + k,
                         own=own)
        result[k] = tuple((a.T if k in transposed else a)[None] for a in res)

    for k, recv in received.items():
        update(k, recv)
    row = lambda a: a.reshape(1, D)
    params = {k: (weights[k], mom1[k], mom2[k]) for k in SMALL}
    params["norm_final_g"] = (row(norm_final_g), row(m_norm_final_g), row(v_norm_final_g))
    tot, rsum, dmine = _small_reduce(d_ws, d_wp, dbias, st1, st2, st3, st_f, vec2, dgate1, dgate3)
    sent, landing = _exchange_wait(send_sems, recv_sems, sent, landing, rsum, "w_in_grad_wait")
    update("ffn1_w_in", landing, own=sent)
    small, loss_row = _small_update(tot, rsum, params)
    result.update(small)
    result["norm_final_g"] = tuple(a.reshape(D) for a in small["norm_final_g"])
    result["w_ada"] = tuple(a[None] for a in _ada_update(cact_all, dmine, w_ada[0], m_w_ada[0], v_w_ada[0], 256))

    return (loss_row[0, 0], dx0[None], *[result[k][0] for k in order], *[result[k][1] for k in order],
            *[result[k][2] for k in order], *[result[k][3] for k in order])
```

```python
import math

import jax
import jax.numpy as jnp
from jax import lax
from jax.experimental import pallas as pl
from jax.experimental.pallas import tpu as pltpu

D = 1024
F = 2816
DP = 512
DG = 512
DPROJ = DP + 2 * DG
CHUNK = 128
WINDOWS = (2, 4, 8, 16)
HALO = 16
NDEV = 8
T_FFN = 512
T_MIX = 256
T_MIX_FWD = 512
T_WIN = 2048
EPS = 1e-6
LR, B1, B2, AEPS, WD, STEP = 0.001, 0.9, 0.999, 1e-08, 0.01, 10
BC1 = 1.0 - B1 ** STEP
BC2 = 1.0 - B2 ** STEP
GELU_C = math.sqrt(2.0 / math.pi)
GELU_A = 0.044715

BF = jnp.bfloat16
F32 = jnp.float32
MESH = pl.DeviceIdType.MESH
HBM = pl.BlockSpec(memory_space=pltpu.HBM)


def _whole(a):
    return pl.BlockSpec(a.shape, lambda i: (0,) * len(a.shape))


NT = (((1,), (1,)), ((), ()))
TN = (((0,), (0,)), ((), ()))


def _dot(a, b):
    return jnp.dot(a, b, preferred_element_type=F32)


def _dot_nt(a, b):
    return lax.dot_general(a, b, NT, preferred_element_type=F32)


def _dot_tn(a, b):
    return lax.dot_general(a, b, TN, preferred_element_type=F32)


def _cparams(vmem_mb, sem=None):
    kw = dict(vmem_limit_bytes=vmem_mb * 1024 * 1024)
    if sem is not None:
        kw["dimension_semantics"] = sem
    return pltpu.CompilerParams(**kw)


def _position():
    return lax.axis_index("x"), lax.axis_index("y"), lax.axis_index("c")


def _slot(p):
    return 4 * p[0] + 2 * p[1] + p[2]


def _flip(me, d):
    x, y, c = me
    return (1 - x if d & 4 else x, 1 - y if d & 2 else y, 1 - c if d & 1 else c)


def _remote(src, dst, send_sem, recv_sem, to):
    return pltpu.make_async_remote_copy(src_ref=src, dst_ref=dst, send_sem=send_sem, recv_sem=recv_sem,
                                        device_id=to, device_id_type=MESH)


def _comm_sems(n):
    return [pltpu.SemaphoreType.DMA((n, 7)), pltpu.SemaphoreType.DMA((n, 7)), pltpu.SemaphoreType.DMA((n,))]


def _gather_phase(phase, xs, outs, sems):
    send_sems, recv_sems, local_sems = sems
    n = len(xs)
    me = _position()
    x, y, c = me
    sibling = (x, y, 1 - c)
    xn, yn, diag = (1 - x, y), (x, 1 - y), (1 - x, 1 - y)
    relay_from = (x + c * (1 - 2 * x), y + (1 - c) * (1 - 2 * y))
    relay_to = (x + (1 - c) * (1 - 2 * x), y + c * (1 - 2 * y))

    def copy(a, k, block, to, src=None):
        dst = outs[a].at[_slot(block)]
        return _remote(dst if src is None else src, dst, send_sems.at[a, k], recv_sems.at[a, k], to)

    def mine(a):
        return pltpu.make_async_copy(xs[a], outs[a].at[_slot(me)], local_sems.at[a])

    def first(a):
        return [copy(a, 0, me, sibling, src=xs[a]), copy(a, 1, me, (*xn, c), src=xs[a]),
                copy(a, 2, me, (*yn, c), src=xs[a])]

    def second(a):
        return [copy(a, 3, (*relay_from, c), (*relay_to, c)), copy(a, 4, (*xn, c), sibling),
                copy(a, 5, (*yn, c), sibling)]

    def third(a):
        return copy(a, 6, (*diag, c), sibling)

    if phase == "start":
        for a in range(n):
            mine(a).start()
            for cp in first(a):
                cp.start()
    elif phase == "forward":
        for a in range(n):
            copy(a, 1, (*xn, c), me).wait_recv()
            copy(a, 2, (*yn, c), me).wait_recv()
            for cp in second(a):
                cp.start()
    else:
        for a in range(n):
            copy(a, 3, (*diag, c), me).wait_recv()
            third(a).start()
        for a in range(n):
            copy(a, 0, sibling, me).wait_recv()
            for k, chip in ((4, xn), (5, yn), (6, diag)):
                copy(a, k, (*chip, 1 - c), me).wait_recv()
        for a in range(n):
            for cp in first(a) + second(a) + [third(a)]:
                cp.wait_send()
            mine(a).wait()


def _exchange_phase(phase, xs, outs, sems):
    send_sems, recv_sems, local_sems = sems
    me = _position()
    for a in range(len(xs)):
        copies = [pltpu.make_async_copy(xs[a].at[_slot(me)], outs[a].at[_slot(me)], local_sems.at[a])]
        for d in range(1, NDEV):
            to = _flip(me, d)
            copies.append(_remote(xs[a].at[_slot(to)], outs[a].at[_slot(me)],
                                  send_sems.at[a, d - 1], recv_sems.at[a, d - 1], to))
        for cp in copies:
            if phase == "start":
                cp.start()
            else:
                cp.wait()


def _like(bufs):
    return [jax.ShapeDtypeStruct(b.shape, b.dtype) for b in bufs]


def _rms_mod(x, gn, shift, scale):
    ms = jnp.mean(x * x, axis=-1, keepdims=True)
    r = lax.rsqrt(ms + EPS)
    xhat = x * r
    n = xhat * gn
    h = n * (1.0 + scale) + shift
    return r, xhat, n, h


def _rms_mod_bwd(dh, dres, r, xhat, n, gn, scale):
    dshift = jnp.sum(dh, axis=0, keepdims=True)
    dscale = jnp.sum(dh * n, axis=0, keepdims=True)
    dn = dh * (1.0 + scale)
    dgn = jnp.sum(dn * xhat, axis=0, keepdims=True)
    dxhat = dn * gn
    dx = dres + r * (dxhat - xhat * jnp.mean(dxhat * xhat, axis=-1, keepdims=True))
    return dx, dshift, dscale, dgn


def _final_norm_loss(x, gf, target):
    r = lax.rsqrt(jnp.mean(x * x, axis=-1, keepdims=True) + EPS)
    xhat = x * r
    e = xhat * gf - target
    part = 0.5 * jnp.sum(jnp.sum(e * e, axis=-1, keepdims=True), axis=0, keepdims=True) / D
    dy = e / D
    dgf = jnp.sum(dy * xhat, axis=0, keepdims=True)
    dxhat = dy * gf
    dx = r * (dxhat - xhat * jnp.mean(dxhat * xhat, axis=-1, keepdims=True))
    return dx, dgf, part


def _rows3(a, b, c, width):
    row = lax.broadcasted_iota(jnp.int32, (8, width), 0)
    z = jnp.zeros((8, width), F32)
    return jnp.where(row == 0, a, z) + jnp.where(row == 1, b, z) + jnp.where(row == 2, c, z)


def _sigmoid(x):
    return 0.5 * jnp.tanh(0.5 * x) + 0.5


def _gelu(x):
    t = jnp.tanh(GELU_C * (x + GELU_A * x * x * x))
    return 0.5 * x * (1.0 + t), t


def _gelu_grad(x, t):
    return 0.5 * (1.0 + t) + 0.5 * x * (1.0 - t * t) * GELU_C * (1.0 + 3.0 * GELU_A * x * x)


def _adamw(w, g, m, v):
    m = B1 * m + (1.0 - B1) * g
    v = B2 * v + (1.0 - B2) * (g * g)
    m_hat = m / BC1
    v_hat = v / BC2
    delta = -LR * (m_hat / (jnp.sqrt(v_hat) + AEPS) + WD * w)
    return delta, m, v


def _cast_shards(shards):
    n = len(shards)

    def body(*refs):
        for a in range(n):
            refs[n + a][...] = refs[a][...].astype(BF)

    resident = pl.BlockSpec(memory_space=pltpu.VMEM)
    return pl.pallas_call(
        body, name="cast_shards", out_shape=[jax.ShapeDtypeStruct(s.shape, BF) for s in shards],
        in_specs=[resident] * n, out_specs=[resident] * n, compiler_params=_cparams(40),
    )(*shards)


def _ada_forward(c8, w_ada, b8, shards, x, gn):
    wc = w_ada.shape[1]
    n = len(shards)
    S = x.shape[0]
    T = min(T_FFN, S)
    nS = S // T

    def ada(c8_ref, w_ref, b8_ref, xs, mod_ref, cact_ref, gathered, call_ref, mall_ref, modp_ref, send_sems,
            recv_sems, gsems):
        me = _position()
        my = _slot(me)
        row = lax.broadcasted_iota(jnp.int32, (8, 1), 0)
        call_ref[my] = c8_ref[...]
        sends = []
        for d in range(1, NDEV):
            to = _flip(me, d)
            sends.append(_remote(call_ref.at[my], call_ref.at[my], send_sems.at[0, d - 1], recv_sems.at[0, d - 1], to))
        for cp in sends:
            cp.start()
        _gather_phase("start", xs, gathered, gsems)
        for cp in sends:
            cp.wait()
        c_all = jnp.zeros((8, D), F32)
        for k in range(NDEV):
            c_all = c_all + jnp.where(row == k, call_ref[k], 0.0)
        cact = c_all * jax.nn.sigmoid(c_all)
        cact_ref[...] = cact
        part = _dot(cact.astype(BF), w_ref[...].astype(BF))
        mall_ref[my] = part
        sends = []
        for d in range(1, NDEV):
            to = _flip(me, d)
            sends.append(_remote(mall_ref.at[my], mall_ref.at[my], send_sems.at[1, d - 1], recv_sems.at[1, d - 1], to))
        for cp in sends:
            cp.start()
        _gather_phase("forward", xs, gathered, gsems)
        for cp in sends:
            cp.wait()
        out = jnp.zeros((8, wc), F32)
        for k in range(NDEV):
            piece = jnp.sum(jnp.where(row == my, mall_ref[k], 0.0), axis=0, keepdims=True)
            out = out + jnp.where(row == k, piece, 0.0)
        modp_ref[...] = out + b8_ref[...]
        for q in range(9 * NDEV):
            mod_ref[q // 8:q // 8 + 1, 128 * (q % 8):128 * (q % 8 + 1)] = \
                modp_ref[q // 9:q // 9 + 1, 128 * (q % 9):128 * (q % 9 + 1)]

    def body(*refs):
        c8_ref, w_ref, b8_ref, gn_ref, x_ref = refs[:5]
        xs = refs[5:5 + n]
        mod_ref, cact_ref, h_ref = refs[5 + n:8 + n]
        gathered = refs[8 + n:8 + 2 * n]
        call_ref, mall_ref, modp_ref, send_sems, recv_sems = refs[8 + 2 * n:13 + 2 * n]
        gsems = refs[13 + 2 * n:]
        i = pl.program_id(0)

        @pl.when(i == 0)
        def _():
            ada(c8_ref, w_ref, b8_ref, xs, mod_ref, cact_ref, gathered, call_ref, mall_ref, modp_ref, send_sems,
                recv_sems, gsems)

        _, _, _, h = _rms_mod(x_ref[...], gn_ref[...], mod_ref[0:1, :], mod_ref[1:2, :])
        h_ref[...] = h.astype(BF)

        @pl.when(i == nS - 1)
        def _():
            _gather_phase("finish", xs, gathered, gsems)

    outs = [jax.ShapeDtypeStruct((9, D), F32), jax.ShapeDtypeStruct((8, D), F32)]
    tile = pl.BlockSpec((T, D), lambda i: (i, 0))
    return pl.pallas_call(
        body, name="ada_forward", grid=(nS,),
        out_shape=outs + [jax.ShapeDtypeStruct((S, D), BF)]
                  + [jax.ShapeDtypeStruct((NDEV,) + s.shape, s.dtype) for s in shards],
        in_specs=[_whole(a) for a in (c8, w_ada, b8, gn)] + [tile] + [HBM] * n,
        out_specs=[_whole(a) for a in outs] + [tile] + [HBM] * n,
        scratch_shapes=[pltpu.VMEM((NDEV, 8, D), F32), pltpu.VMEM((NDEV, 8, wc), F32), pltpu.VMEM((8, wc), F32),
                        pltpu.SemaphoreType.DMA((2, 7)), pltpu.SemaphoreType.DMA((2, 7))] + _comm_sems(n),
        compiler_params=_cparams(40, ("arbitrary",)),
    )(c8, w_ada, b8, gn, x, *shards)


MATS = ("w_spatial", "w_pool", "b_spatial")
VECS = ("norm_ffn1_g", "norm_mix_g", "norm_ffn2_g", "norm_final_g", "pool_scale", "gmlp_ln_g", "gmlp_ln_b", "b_ada")
VEC_WIDTH = dict(norm_ffn1_g=D, norm_mix_g=D, norm_ffn2_g=D, norm_final_g=D, pool_scale=DP, gmlp_ln_g=DG,
                 gmlp_ln_b=DG, b_ada=9 * D)
MAT_ROWS = 1600
MAT_SLICE = MAT_ROWS // NDEV
VEC_LANES = sum(VEC_WIDTH.values()) + 128
DMOD_AT = VEC_LANES - 128 - 9 * D
SMALL = MATS + VECS


def _small_reduce(g_ws, g_wp, dbias, st1, st2, st3, st_f, vec2, dgate1, dgate3):
    wc = 9 * D // NDEV

    def body(g_ws_ref, g_wp_ref, dbias_ref, st1_ref, st2_ref, st3_ref, stf_ref, vec2_ref, dg1_ref, dg3_ref,
             tot_ref, rsum_ref, dmine_ref,
             pack_ref, rs_ref, ag_ref, rv_ref, dmp_ref, dw_ref, send_sems, recv_sems):
        me = _position()
        my = _slot(me)

        pack_ref[0:1024, :] = g_ws_ref[...].reshape(1024, 128)
        pack_ref[1024:1536, :] = g_wp_ref[...].reshape(512, 128)
        ch = lax.broadcasted_iota(jnp.int32, (DG, 128), 0)
        hd = lax.broadcasted_iota(jnp.int32, (DG, 128), 1)
        sel = jnp.where(ch // 64 == hd, 1.0, 0.0).astype(F32)
        heads = jnp.dot(dbias_ref[...], sel, preferred_element_type=F32, precision=lax.Precision.HIGHEST)
        pack_ref[1536:1544, :] = heads.T[0:8, :]
        pack_ref[1544:MAT_ROWS, :] = jnp.zeros((MAT_ROWS - 1544, 128), F32)
        dgate1 = dg1_ref[0:1, :] + dg1_ref[8:9, :]
        dgate3 = dg3_ref[0:1, :] + dg3_ref[8:9, :]
        row = jnp.concatenate(
            [st1_ref[2:3, :], st2_ref[2:3, :], st3_ref[2:3, :], stf_ref[0:1, :],
             vec2_ref[0:1, :], vec2_ref[1:2, :], vec2_ref[2:3, :],
             st1_ref[0:1, :], st1_ref[1:2, :], dgate1, st2_ref[0:1, :], st2_ref[1:2, :], st2_ref[3:4, :],
             st3_ref[0:1, :], st3_ref[1:2, :], dgate3, stf_ref[1:2, 0:128]], axis=1)
        rv_ref[my] = row
        for k in range(NDEV):
            dmp_ref[k] = row[:, DMOD_AT + wc * k:DMOD_AT + wc * (k + 1)]
        dw_ref[my] = dmp_ref[my]
        rs_ref[my] = pack_ref[pl.ds(pl.multiple_of(my * MAT_SLICE, 8), MAT_SLICE), :]

        first = []
        for d in range(1, NDEV):
            to = _flip(me, d)
            theirs = pl.ds(pl.multiple_of(_slot(to) * MAT_SLICE, 8), MAT_SLICE)
            first.append(_remote(pack_ref.at[theirs, :], rs_ref.at[my], send_sems.at[0, d - 1], recv_sems.at[0, d - 1], to))
            first.append(_remote(dmp_ref.at[_slot(to)], dw_ref.at[my], send_sems.at[1, d - 1], recv_sems.at[1, d - 1], to))
            first.append(_remote(rv_ref.at[my], rv_ref.at[my], send_sems.at[2, d - 1], recv_sems.at[2, d - 1], to))
        for cp in first:
            cp.start()
        for cp in first:
            cp.wait()
        red = rs_ref[0]
        for k in range(1, NDEV):
            red = red + rs_ref[k]
        ag_ref[my] = red
        second = []
        for d in range(1, NDEV):
            to = _flip(me, d)
            second.append(_remote(ag_ref.at[my], ag_ref.at[my], send_sems.at[3, d - 1], recv_sems.at[3, d - 1], to))
        for cp in second:
            cp.start()

        rsum = rv_ref[0]
        for k in range(1, NDEV):
            rsum = rsum + rv_ref[k]
        rsum_ref[...] = rsum
        r8 = lax.broadcasted_iota(jnp.int32, (8, 1), 0)
        dmine = jnp.zeros((8, wc), F32)
        for k in range(NDEV):
            dmine = dmine + jnp.where(r8 == k, dw_ref[k], 0.0)
        dmine_ref[...] = dmine

        for cp in second:
            cp.wait()
        for k in range(NDEV):
            tot_ref[k * MAT_SLICE:(k + 1) * MAT_SLICE, :] = ag_ref[k]

    ins = (g_ws, g_wp, dbias, st1, st2, st3, st_f, vec2, dgate1, dgate3)
    outs = [jax.ShapeDtypeStruct((MAT_ROWS, 128), F32), jax.ShapeDtypeStruct((1, VEC_LANES), F32),
            jax.ShapeDtypeStruct((8, wc), F32)]
    return pl.pallas_call(
        body, name="small_reduce", grid=(1,), out_shape=outs,
        in_specs=[_whole(a) for a in ins], out_specs=[_whole(a) for a in outs],
        scratch_shapes=[pltpu.VMEM((MAT_ROWS, 128), F32), pltpu.VMEM((NDEV, MAT_SLICE, 128), F32),
                        pltpu.VMEM((NDEV, MAT_SLICE, 128), F32),
                        pltpu.VMEM((NDEV, 1, VEC_LANES), F32), pltpu.VMEM((NDEV, 1, wc), F32),
                        pltpu.VMEM((NDEV, 1, wc), F32),
                        pltpu.SemaphoreType.DMA((4, 7)), pltpu.SemaphoreType.DMA((4, 7))],
        compiler_params=_cparams(32, ("arbitrary",)),
    )(*ins)


def _small_update(tot, rsum, params):
    flat = [a for k in SMALL for a in params[k]]
    n_in = 2 + len(flat)

    def body(*refs):
        tot_ref, rsum_ref = refs[:2]
        p_hbm = refs[2:n_in]
        o_refs = refs[n_in:n_in + 4 * len(SMALL)]
        loss_ref = refs[n_in + 4 * len(SMALL)]
        p_refs = refs[n_in + 4 * len(SMALL) + 1:-1]
        sem = refs[-1]
        fetch = [pltpu.make_async_copy(p_hbm[k], p_refs[k], sem.at[k]) for k in range(len(flat))]
        for cp in fetch:
            cp.start()
        for cp in fetch:
            cp.wait()
        loss_ref[...] = rsum_ref[:, VEC_LANES - 128:VEC_LANES]

        def update(idx, g):
            w_ref, m_ref, v_ref = p_refs[3 * idx:3 * idx + 3]
            g_out, d_out, m_out, v_out = o_refs[4 * idx:4 * idx + 4]
            g = g.reshape(w_ref.shape)
            g_out[...] = g
            d_out[...], m_out[...], v_out[...] = _adamw(w_ref[...], g, m_ref[...], v_ref[...])

        update(0, tot_ref[0:1024, :])
        update(1, tot_ref[1024:1536, :])
        update(2, tot_ref[1536:1544, :])
        at = 0
        for idx, k in enumerate(VECS):
            update(3 + idx, rsum_ref[:, at:at + VEC_WIDTH[k]])
            at += VEC_WIDTH[k]

    outs = []
    for k in SMALL:
        outs += [jax.ShapeDtypeStruct(params[k][0].shape, F32)] * 4
    outs += [jax.ShapeDtypeStruct((1, 128), F32)]
    res = pl.pallas_call(
        body, name="small_update", grid=(1,), out_shape=outs,
        in_specs=[_whole(tot), _whole(rsum)] + [HBM] * len(flat), out_specs=[_whole(a) for a in outs],
        scratch_shapes=[pltpu.VMEM(a.shape, F32) for a in flat] + [pltpu.SemaphoreType.DMA((len(flat),))],
        compiler_params=_cparams(32, ("arbitrary",)),
    )(tot, rsum, *[pltpu.with_memory_space_constraint(a, pltpu.HBM) for a in flat])
    return {k: tuple(res[4 * i:4 * i + 4]) for i, k in enumerate(SMALL)}, res[-1]


def _sum_adamw(recv, w, m, v, tr, name, own=None):
    R, C = w.shape

    def body(*refs):
        r_ref = refs[0]
        o_ref = refs[1] if own is not None else None
        w_ref, m_ref, v_ref, g_ref, d_ref, nm_ref, nv_ref = refs[-7:]
        my = _slot(_position()) if own is not None else None

        def part(k):
            if own is None:
                return r_ref[k].astype(F32)
            return jnp.where(my == k, o_ref[k], r_ref[k]).astype(F32)

        g = part(0)
        for k in range(1, NDEV):
            g = g + part(k)
        g_ref[...] = g
        d_ref[...], nm_ref[...], nv_ref[...] = _adamw(w_ref[...], g, m_ref[...], v_ref[...])

    blk = pl.BlockSpec((tr, C), lambda i: (i, 0))
    slots = pl.BlockSpec((NDEV, tr, C), lambda i: (0, i, 0))
    out = jax.ShapeDtypeStruct((R, C), F32)
    bufs = (recv,) if own is None else (recv, own)
    return pl.pallas_call(
        body, name=name, grid=(R // tr,), out_shape=[out] * 4,
        in_specs=[slots] * len(bufs) + [blk, blk, blk], out_specs=[blk] * 4,
        compiler_params=_cparams(48, ("arbitrary",)),
    )(*bufs, w, m, v)


SEM = pl.BlockSpec(memory_space=pltpu.SEMAPHORE)
EFFECT = pltpu.SideEffectType.DATAFLOW_SIDE_EFFECTING


def _exchange_start(buf, name):
    def body(src_ref, land_ref, send_sems, recv_sems, src_thru, land_thru, token):
        me = _position()
        for d in range(1, NDEV):
            to = _flip(me, d)
            _remote(src_ref.at[_slot(to)], land_ref.at[_slot(me)], send_sems.at[d - 1], recv_sems.at[d - 1], to).start()
        token[...] = jnp.zeros_like(token)

    like = pltpu.HBM(buf.shape, buf.dtype)
    return pl.pallas_call(
        body, name=name,
        out_shape=(pltpu.SemaphoreType.DMA((NDEV - 1,)), pltpu.SemaphoreType.DMA((NDEV - 1,)), like, like,
                   jax.ShapeDtypeStruct((8, 128), F32)),
        in_specs=(HBM, HBM), out_specs=(SEM, SEM, HBM, HBM, pl.BlockSpec(memory_space=pltpu.VMEM)),
        input_output_aliases={0: 2, 1: 3},
        compiler_params=pltpu.CompilerParams(has_side_effects=EFFECT),
    )(pltpu.with_memory_space_constraint(buf, pltpu.HBM),
      pltpu.with_memory_space_constraint(lax.empty(buf.shape, buf.dtype), pltpu.HBM))


def _exchange_wait(send_sems, recv_sems, src_thru, land_thru, after, name):
    def body(src_ref, land_ref, send_sems, recv_sems, after_ref, src_out, land_out):
        me = _position()
        for d in range(1, NDEV):
            to = _flip(me, d)
            cp = _remote(src_ref.at[_slot(to)], land_ref.at[_slot(me)], send_sems.at[d - 1], recv_sems.at[d - 1], to)
            cp.wait_send()
            cp.wait_recv()

    like = pltpu.HBM(src_thru.shape, src_thru.dtype)
    return pl.pallas_call(
        body, name=name, out_shape=(like, like),
        in_specs=(HBM, HBM, SEM, SEM, pl.BlockSpec(memory_space=pl.ANY)), out_specs=(HBM, HBM),
        input_output_aliases={0: 0, 1: 1},
        compiler_params=pltpu.CompilerParams(has_side_effects=EFFECT),
    )(src_thru, land_thru, send_sems, recv_sems, after)


def _ada_update(cact_all, dmine, w, m, v, tr):
    R, C = w.shape

    def body(c_ref, dm_ref, w_ref, m_ref, v_ref, g_ref, d_ref, nm_ref, nv_ref):
        g = _dot_tn(c_ref[...].astype(BF), dm_ref[...].astype(BF))
        g_ref[...] = g
        d_ref[...], nm_ref[...], nv_ref[...] = _adamw(w_ref[...], g, m_ref[...], v_ref[...])

    blk = pl.BlockSpec((tr, C), lambda i: (i, 0))
    out = jax.ShapeDtypeStruct((R, C), F32)
    return pl.pallas_call(
        body, name="update_w_ada", grid=(R // tr,), out_shape=[out] * 4,
        in_specs=[pl.BlockSpec((8, tr), lambda i: (0, i)), pl.BlockSpec((8, C), lambda i: (0, 0)), blk, blk, blk],
        out_specs=[blk] * 4,
        compiler_params=_cparams(48, ("arbitrary",)),
    )(cact_all, dmine, w, m, v)


FC = F // 2
FC_FWD = F


def _ffn_fwd(x, mod, gn, w_in_t, w_out, sub, name, gather=(), loss=None, h=None):
    S = x.shape[0]
    T = min(T_FFN, S)
    FC = FC_FWD
    nS, nJ = S // T, F // FC
    ng = len(gather)
    nl = 2 if loss else 0
    nh = 0 if h is None else 1
    forward_step = nS // 2

    def body(*refs):
        x_ref, mod_ref, gn_ref, wg_ref, wu_ref, wo_ref = refs[:6]
        gf_ref, t_ref = refs[6 + nh:6 + nh + nl] if loss else (None, None)
        shards = refs[6 + nh + nl:6 + nh + nl + ng]
        at = 6 + nh + nl + ng
        xo_ref, gu_ref = refs[at:at + 2]
        h_ref = refs[6] if nh else refs[at + 2]
        at += 3 - nh
        gathered = refs[at:at + ng]
        at += ng
        st_ref = refs[at] if loss else None
        at += nl // 2
        acc_scr = refs[at]
        sems = refs[at + 1:]
        i, j = pl.program_id(0), pl.program_id(1)

        if ng:
            @pl.when((i == 0) & (j == 0))
            def _():
                _gather_phase("start", shards, gathered, sems)

            @pl.when((i == forward_step) & (j == 0))
            def _():
                _gather_phase("forward", shards, gathered, sems)

        @pl.when(j == 0)
        def _():
            if not nh:
                _, _, _, hh = _rms_mod(x_ref[...], gn_ref[...], mod_ref[3 * sub:3 * sub + 1, :],
                                       mod_ref[3 * sub + 1:3 * sub + 2, :])
                h_ref[...] = hh.astype(BF)
            acc_scr[...] = jnp.zeros_like(acc_scr)

        hb = h_ref[...]
        g = _dot_nt(hb, wg_ref[0])
        u = _dot_nt(hb, wu_ref[0])
        gu_ref[0] = g.astype(BF)
        gu_ref[1] = u.astype(BF)
        a = (g * _sigmoid(g) * u).astype(BF)
        acc_scr[...] += _dot(a, wo_ref[...])

        @pl.when(j == nJ - 1)
        def _():
            xo = x_ref[...] + (0.5 * mod_ref[3 * sub + 2:3 * sub + 3, :]) * acc_scr[...]
            if not loss:
                xo_ref[...] = xo
            else:
                dx, dgf, part = _final_norm_loss(xo, gf_ref[...], t_ref[...])
                xo_ref[...] = dx
                upd = _rows3(dgf, jnp.broadcast_to(part, (1, D)), jnp.zeros((1, D), F32), D)

                @pl.when(i == 0)
                def _():
                    st_ref[...] = upd

                @pl.when(i > 0)
                def _():
                    st_ref[...] += upd

        if ng:
            @pl.when((i == nS - 1) & (j == nJ - 1))
            def _():
                _gather_phase("finish", shards, gathered, sems)

    tile = pl.BlockSpec((T, D), lambda i, j: (i, 0))
    res = pl.pallas_call(
        body, name=name, grid=(nS, nJ),
        out_shape=[jax.ShapeDtypeStruct((S, D), F32), jax.ShapeDtypeStruct((2, S, F), BF)]
                  + ([] if nh else [jax.ShapeDtypeStruct((S, D), BF)])
                  + [jax.ShapeDtypeStruct((NDEV,) + s.shape, s.dtype) for s in gather]
                  + ([jax.ShapeDtypeStruct((8, D), F32)] if loss else []),
        in_specs=[tile,
                  pl.BlockSpec((9, D), lambda i, j: (0, 0)),
                  pl.BlockSpec((1, D), lambda i, j: (0, 0)),
                  pl.BlockSpec((1, FC, D), lambda i, j: (0, j, 0)),
                  pl.BlockSpec((1, FC, D), lambda i, j: (1, j, 0)),
                  pl.BlockSpec((FC, D), lambda i, j: (j, 0))] + [tile] * nh
                 + ([pl.BlockSpec((1, D), lambda i, j: (0, 0)), tile] if loss else []) + [HBM] * ng,
        out_specs=[tile, pl.BlockSpec((2, T, FC), lambda i, j: (0, i, j))] + [tile] * (1 - nh) + [HBM] * ng
                  + ([pl.BlockSpec((8, D), lambda i, j: (0, 0))] if loss else []),
        scratch_shapes=[pltpu.VMEM((T, D), F32)] + (_comm_sems(ng) if ng else []),
        compiler_params=_cparams(60, ("arbitrary", "arbitrary")),
    )(x, mod, gn, w_in_t, w_in_t, w_out, *(() if h is None else (h,)), *(loss or ()), *gather)
    return res if h is None else [res[0], res[1], h, *res[2:]]


def _ffn_bwd_hidden(dx, mod, gu, w_out, sub, name, exchange=()):
    S = dx.shape[0]
    T = min(T_FFN, S)
    nS, nJ = S // T, F // FC
    ne = len(exchange)

    def body(*refs):
        dx_ref, mod_ref, gu_ref, wo_ref = refs[:4]
        sendbufs = refs[4:4 + ne]
        dgu_ref, gw_ref, dgate_ref = refs[4 + ne:7 + ne]
        recvbufs = refs[7 + ne:7 + 2 * ne]
        acc_scr = refs[7 + 2 * ne]
        sems = refs[8 + 2 * ne:]
        j, i = pl.program_id(0), pl.program_id(1)

        if ne:
            @pl.when((i == 0) & (j == 0))
            def _():
                _exchange_phase("start", sendbufs, recvbufs, sems)

        gate = mod_ref[3 * sub + 2:3 * sub + 3, :]
        dx = dx_ref[...]
        da = _dot_nt((dx * (0.5 * gate)).astype(BF), wo_ref[...])
        g = gu_ref[0].astype(F32)
        u = gu_ref[1].astype(F32)
        sg = _sigmoid(g)
        s = g * sg
        dgu_ref[0] = (da * u * (sg * (1.0 + g * (1.0 - sg)))).astype(BF)
        dgu_ref[1] = (da * s).astype(BF)
        contrib = _dot_tn((s * u).astype(BF), dx.astype(BF))

        @pl.when(i == 0)
        def _():
            acc_scr[...] = contrib

        @pl.when(i > 0)
        def _():
            acc_scr[...] += contrib

        @pl.when(i == nS - 1)
        def _():
            acc = acc_scr[...]
            dgate = 0.5 * jnp.sum(acc * wo_ref[...].astype(F32), axis=0, keepdims=True)
            dgate_ref[...] = jnp.broadcast_to(dgate, (8, D))
            gw_ref[...] = (acc * (0.5 * gate)).astype(BF)

        if ne:
            @pl.when((i == nS - 1) & (j == nJ - 1))
            def _():
                _exchange_phase("wait", sendbufs, recvbufs, sems)

    return pl.pallas_call(
        body, name=name, grid=(nJ, nS),
        out_shape=[jax.ShapeDtypeStruct((2, S, F), BF), jax.ShapeDtypeStruct((F, D), BF),
                   jax.ShapeDtypeStruct((8 * nJ, D), F32)] + _like(exchange),
        in_specs=[pl.BlockSpec((T, D), lambda j, i: (i, 0)),
                  pl.BlockSpec((9, D), lambda j, i: (0, 0)),
                  pl.BlockSpec((2, T, FC), lambda j, i: (0, i, j)),
                  pl.BlockSpec((FC, D), lambda j, i: (j, 0))] + [HBM] * ne,
        out_specs=[pl.BlockSpec((2, T, FC), lambda j, i: (0, i, j)),
                   pl.BlockSpec((FC, D), lambda j, i: (j, 0)),
                   pl.BlockSpec((8, D), lambda j, i: (j, 0))] + [HBM] * ne,
        scratch_shapes=[pltpu.VMEM((FC, D), F32)] + (_comm_sems(ne) if ne else []),
        compiler_params=_cparams(56, ("arbitrary", "arbitrary")),
    )(dx, mod, gu, w_out, *exchange)


def _ffn_bwd_hidden_wide(dx, mod, gu, w_out, sub, name, exchange=()):
    S = dx.shape[0]
    T = min(T_FFN, S)
    nS = S // T
    ne = len(exchange)

    def body(*refs):
        dx_ref, mod_ref, gu_ref, wo_ref = refs[:4]
        sendbufs = refs[4:4 + ne]
        dgu_ref, a_ref = refs[4 + ne:6 + ne]
        recvbufs = refs[6 + ne:6 + 2 * ne]
        sems = refs[6 + 2 * ne:]
        i = pl.program_id(0)

        if ne:
            @pl.when(i == 0)
            def _():
                _exchange_phase("start", sendbufs, recvbufs, sems)

        gate = mod_ref[3 * sub + 2:3 * sub + 3, :]
        da = _dot_nt((dx_ref[...] * (0.5 * gate)).astype(BF), wo_ref[...])
        g = gu_ref[0].astype(F32)
        u = gu_ref[1].astype(F32)
        sg = _sigmoid(g)
        s = g * sg
        dgu_ref[0] = (da * u * (sg * (1.0 + g * (1.0 - sg)))).astype(BF)
        dgu_ref[1] = (da * s).astype(BF)
        a_ref[...] = (s * u).astype(BF)

        if ne:
            @pl.when(i == nS - 1)
            def _():
                _exchange_phase("wait", sendbufs, recvbufs, sems)

    return pl.pallas_call(
        body, name=name, grid=(nS,),
        out_shape=[jax.ShapeDtypeStruct((2, S, F), BF), jax.ShapeDtypeStruct((S, F), BF)] + _like(exchange),
        in_specs=[pl.BlockSpec((T, D), lambda i: (i, 0)),
                  pl.BlockSpec((9, D), lambda i: (0, 0)),
                  pl.BlockSpec((2, T, F), lambda i: (0, i, 0)),
                  pl.BlockSpec((F, D), lambda i: (0, 0))] + [HBM] * ne,
        out_specs=[pl.BlockSpec((2, T, F), lambda i: (0, i, 0)),
                   pl.BlockSpec((T, F), lambda i: (i, 0))] + [HBM] * ne,
        scratch_shapes=_comm_sems(ne) if ne else [],
        compiler_params=_cparams(60, ("arbitrary",)),
    )(dx, mod, gu, w_out, *exchange)


def _ffn_bwd_wout(a, dx, mod, w_out, sub, name):
    S = dx.shape[0]
    T = min(T_WIN, S)
    nS, nJ = S // T, F // FC

    def body(a_ref, dx_ref, mod_ref, wo_ref, gw_ref, dgate_ref, acc_scr):
        i = pl.program_id(1)
        contrib = _dot_tn(a_ref[...], dx_ref[...].astype(BF))

        @pl.when(i == 0)
        def _():
            acc_scr[...] = contrib

        @pl.when(i > 0)
        def _():
            acc_scr[...] += contrib

        @pl.when(i == nS - 1)
        def _():
            acc = acc_scr[...]
            dgate = 0.5 * jnp.sum(acc * wo_ref[...].astype(F32), axis=0, keepdims=True)
            dgate_ref[...] = jnp.broadcast_to(dgate, (8, D))
            gw_ref[...] = (acc * (0.5 * mod_ref[3 * sub + 2:3 * sub + 3, :])).astype(BF)

    return pl.pallas_call(
        body, name=name, grid=(nJ, nS),
        out_shape=[jax.ShapeDtypeStruct((F, D), BF), jax.ShapeDtypeStruct((8 * nJ, D), F32)],
        in_specs=[pl.BlockSpec((T, FC), lambda j, i: (i, j)),
                  pl.BlockSpec((T, D), lambda j, i: (i, 0)),
                  pl.BlockSpec((9, D), lambda j, i: (0, 0)),
                  pl.BlockSpec((FC, D), lambda j, i: (j, 0))],
        out_specs=[pl.BlockSpec((FC, D), lambda j, i: (j, 0)),
                   pl.BlockSpec((8, D), lambda j, i: (j, 0))],
        scratch_shapes=[pltpu.VMEM((FC, D), F32)],
        compiler_params=_cparams(56, ("arbitrary", "arbitrary")),
    )(a, dx, mod, w_out)


def _ffn_bwd_input(dgu, w_in_t, x, dx, mod, gn, sub, name, exchange=()):
    S = x.shape[0]
    T = min(T_FFN, S)
    nS = S // T
    ne = len(exchange)
    NC = 256
    chunks = [slice(k * NC, (k + 1) * NC) for k in range(D // NC)]

    def body(*refs):
        dgu_ref, w_ref, x_ref, dx_ref, mod_ref, gn_ref = refs[:6]
        sendbufs = refs[6:6 + ne]
        dxin_ref, st_ref = refs[6 + ne:8 + ne]
        recvbufs = refs[8 + ne:8 + 2 * ne]
        dxh_scr = refs[8 + 2 * ne]
        sems = refs[9 + 2 * ne:]
        i = pl.program_id(0)

        if ne:
            @pl.when(i == 0)
            def _():
                _exchange_phase("start", sendbufs, recvbufs, sems)

        gn = gn_ref[...]
        scale = mod_ref[3 * sub + 1:3 * sub + 2, :]
        r, xhat, n, _ = _rms_mod(x_ref[...], gn, mod_ref[3 * sub:3 * sub + 1, :], scale)
        dg = dgu_ref[0]
        du = dgu_ref[1]
        rowsum = jnp.zeros((T, 1), F32)
        dshift, dscale, dgn = [], [], []
        for cols in chunks:
            dh = _dot(dg, w_ref[0, :, cols]) + _dot(du, w_ref[1, :, cols])
            dshift.append(jnp.sum(dh, axis=0, keepdims=True))
            dscale.append(jnp.sum(dh * n[:, cols], axis=0, keepdims=True))
            dn = dh * (1.0 + scale[:, cols])
            dgn.append(jnp.sum(dn * xhat[:, cols], axis=0, keepdims=True))
            dxhat = dn * gn[:, cols]
            rowsum = rowsum + jnp.sum(dxhat * xhat[:, cols], axis=-1, keepdims=True)
            dxh_scr[:, cols] = dxhat
        dxin_ref[...] = dx_ref[...] + r * (dxh_scr[...] - xhat * (rowsum / D))
        cat = lambda parts: jnp.concatenate(parts, axis=1)
        upd = _rows3(cat(dshift), cat(dscale), cat(dgn), D)

        @pl.when(i == 0)
        def _():
            st_ref[...] = upd

        @pl.when(i > 0)
        def _():
            st_ref[...] += upd

        if ne:
            @pl.when(i == nS - 1)
            def _():
                _exchange_phase("wait", sendbufs, recvbufs, sems)

    tile = pl.BlockSpec((T, D), lambda i: (i, 0))
    return pl.pallas_call(
        body, name=name, grid=(nS,),
        out_shape=[jax.ShapeDtypeStruct((S, D), F32), jax.ShapeDtypeStruct((8, D), F32)] + _like(exchange),
        in_specs=[pl.BlockSpec((2, T, F), lambda i: (0, i, 0)),
                  pl.BlockSpec((2, F, D), lambda i: (0, 0, 0), pipeline_mode=pl.Buffered(1)),
                  tile, tile,
                  pl.BlockSpec((9, D), lambda i: (0, 0)),
                  pl.BlockSpec((1, D), lambda i: (0, 0))] + [HBM] * ne,
        out_specs=[tile, pl.BlockSpec((8, D), lambda i: (0, 0))] + [HBM] * ne,
        scratch_shapes=[pltpu.VMEM((T, D), F32)] + (_comm_sems(ne) if ne else []),
        compiler_params=_cparams(60, ("arbitrary",)),
    )(dgu, w_in_t, x, dx, mod, gn, *exchange)


def _ffn_bwd_win(h, dgu, name, exchange=()):
    S = h.shape[0]
    T = min(T_WIN, S)
    nS, nJ = S // T, F // FC
    ne = len(exchange)

    def body(*refs):
        h_ref, dgu_ref = refs[:2]
        sendbufs = refs[2:2 + ne]
        out_ref = refs[2 + ne]
        recvbufs = refs[3 + ne:3 + 2 * ne]
        acc_scr = refs[3 + 2 * ne]
        sems = refs[4 + 2 * ne:]
        p, j, i = pl.program_id(0), pl.program_id(1), pl.program_id(2)

        if ne:
            @pl.when((p == 0) & (j == 0) & (i == 0))
            def _():
                _exchange_phase("start", sendbufs, recvbufs, sems)

        contrib = _dot_tn(dgu_ref[0], h_ref[...])

        @pl.when(i == 0)
        def _():
            acc_scr[...] = contrib

        @pl.when(i > 0)
        def _():
            acc_scr[...] += contrib

        @pl.when(i == nS - 1)
        def _():
            out_ref[0] = acc_scr[...].astype(BF)

        if ne:
            @pl.when((p == 1) & (j == nJ - 1) & (i == nS - 1))
            def _():
                _exchange_phase("wait", sendbufs, recvbufs, sems)

    return pl.pallas_call(
        body, name=name, grid=(2, nJ, nS),
        out_shape=[jax.ShapeDtypeStruct((2, F, D), BF)] + _like(exchange),
        in_specs=[pl.BlockSpec((T, D), lambda p, j, i: (i, 0)),
                  pl.BlockSpec((1, T, FC), lambda p, j, i: (p, i, j))] + [HBM] * ne,
        out_specs=[pl.BlockSpec((1, FC, D), lambda p, j, i: (p, j, 0))] + [HBM] * ne,
        scratch_shapes=[pltpu.VMEM((FC, D), F32)] + (_comm_sems(ne) if ne else []),
        compiler_params=_cparams(56, ("arbitrary", "arbitrary", "arbitrary")),
    )(h, dgu, *exchange)


def _pool_counts(pos0, T):
    pos = pos0 + lax.broadcasted_iota(jnp.int32, (T, 1), 0)
    return [jnp.minimum(pos + 1, w).astype(F32) for w in WINDOWS]


def _pool_fwd(xa, halo, ext_scr, cnts, T):
    ext_scr[0:HALO, :] = halo
    ext_scr[HALO:HALO + T, :] = xa
    out = []
    for gi, w in enumerate(WINDOWS):
        cols = slice(128 * gi, 128 * gi + 128)
        acc = xa[:, cols]
        for k in range(1, w):
            acc = acc + ext_scr[HALO - k:HALO - k + T, cols]
        out.append(acc / cnts[gi] - xa[:, cols])
    return out


def _sgu_fwd(vnb, ws_ref, sv_scr, T):
    lane = lax.broadcasted_iota(jnp.int32, (CHUNK, 128), 1)
    for n in range(T // CHUNK):
        rows = slice(n * CHUNK, (n + 1) * CHUNK)
        for b in range(DG // 128):
            cols = slice(128 * b, 128 * b + 128)
            vb = vnb[rows, cols]
            sv_scr[rows, cols] = jnp.where(lane < 64, _dot(ws_ref[2 * b], vb), _dot(ws_ref[2 * b + 1], vb))


def _mix_fwd(x, mod, gn, gn_next, wmi, wmo, wp, ps, lg, lb, ws, bias, name):
    S = x.shape[0]
    T = min(T_MIX_FWD, S)

    def body(x_ref, mod_ref, gn_ref, gnn_ref, wmi_ref, wmo_ref, wp_ref, ps_ref, lg_ref, lb_ref, ws_ref, bias_ref,
             xo_ref, hn_ref, carry_scr, ext_scr, sv_scr, ycat_scr):
        i = pl.program_id(0)

        @pl.when(i == 0)
        def _():
            carry_scr[...] = jnp.zeros_like(carry_scr)

        x = x_ref[...]
        _, _, _, h = _rms_mod(x, gn_ref[...], mod_ref[3:4, :], mod_ref[4:5, :])
        proj = _dot_nt(h.astype(BF), wmi_ref[...])
        xa = proj[:, 0:DP]
        p = _pool_fwd(xa, carry_scr[...], ext_scr, _pool_counts(i * T, T), T)
        carry_scr[...] = xa[T - HALO:T, :]
        for gi in range(4):
            cols = slice(128 * gi, 128 * gi + 128)
            ycat_scr[:, cols] = (_dot(p[gi].astype(BF), wp_ref[gi]) * ps_ref[:, cols]).astype(BF)
        u, _ = _gelu(proj[:, DP:DP + DG])
        v, _ = _gelu(proj[:, DP + DG:DPROJ])
        mu = jnp.mean(v, axis=-1, keepdims=True)
        vc = v - mu
        rstd = lax.rsqrt(jnp.mean(vc * vc, axis=-1, keepdims=True) + EPS)
        vn = vc * rstd * lg_ref[...] + lb_ref[...]
        _sgu_fwd(vn.astype(BF), ws_ref, sv_scr, T)
        for n in range(T // CHUNK):
            rows = slice(n * CHUNK, (n + 1) * CHUNK)
            ycat_scr[rows, DP:D] = (u[rows, :] * (sv_scr[rows, :] + bias_ref[...])).astype(BF)
        xo = x + mod_ref[5:6, :] * _dot(ycat_scr[...], wmo_ref[...])
        xo_ref[...] = xo
        _, _, _, hn = _rms_mod(xo, gnn_ref[...], mod_ref[6:7, :], mod_ref[7:8, :])
        hn_ref[...] = hn.astype(BF)

    full = lambda shape: pl.BlockSpec(shape, lambda i: (0,) * len(shape))
    tile = pl.BlockSpec((T, D), lambda i: (i, 0))
    return pl.pallas_call(
        body, name=name, grid=(S // T,),
        out_shape=[jax.ShapeDtypeStruct((S, D), F32), jax.ShapeDtypeStruct((S, D), BF)],
        in_specs=[tile, full((9, D)), full((1, D)), full((1, D)), full((DPROJ, D)), full((D, D)),
                  full((4, 128, 128)), full((1, DP)), full((1, DG)), full((1, DG)), full((8, CHUNK, CHUNK)),
                  full((CHUNK, DG))],
        out_specs=[tile, tile],
        scratch_shapes=[pltpu.VMEM((HALO, DP), F32), pltpu.VMEM((T + HALO, DP), F32), pltpu.VMEM((T, DG), F32),
                        pltpu.VMEM((T, D), BF)],
        compiler_params=_cparams(48, ("arbitrary",)),
    )(x, mod, gn, gn_next, wmi, wmo, wp, ps, lg, lb, ws, bias)


def _mix_bwd(x, dxo, mod, gn, wmi, wmo, wp, ps, lg, lb, ws, bias, name, exchange=()):
    S = x.shape[0]
    T = min(T_MIX, S)
    nS = S // T
    hb = T // HALO
    ne = len(exchange)

    def body(*refs):
        (x_ref, xh_ref, dxo_ref, mod_ref, gn_ref, wmi_ref, wmo_ref, wp_ref, ps_ref, lg_ref, lb_ref, ws_ref,
         bias_ref) = refs[:13]
        sendbufs = refs[13:13 + ne]
        dxi_ref, gwmi_out, gwmo_out, gwp_ref, gws_ref, st_ref, vec_ref, dbias_ref = refs[13 + ne:21 + ne]
        recvbufs = refs[21 + ne:21 + 2 * ne]
        (carry_scr, ext_scr, qext_scr, sv_scr, dvn_scr, ycat_scr, dproj_scr, gwmi_ref,
         gwmo_ref) = refs[21 + 2 * ne:30 + 2 * ne]
        sems = refs[30 + 2 * ne:]
        i = pl.program_id(0)
        t = nS - 1 - i
        gn = gn_ref[...]
        shift, scale, gate = mod_ref[3:4, :], mod_ref[4:5, :], mod_ref[5:6, :]

        @pl.when(i == 0)
        def _():
            if ne:
                _exchange_phase("start", sendbufs, recvbufs, sems)
            carry_scr[...] = jnp.zeros_like(carry_scr)
            gwmi_ref[...] = jnp.zeros_like(gwmi_ref)
            gwmo_ref[...] = jnp.zeros_like(gwmo_ref)
            gwp_ref[...] = jnp.zeros_like(gwp_ref)
            gws_ref[...] = jnp.zeros_like(gws_ref)
            st_ref[...] = jnp.zeros_like(st_ref)
            vec_ref[...] = jnp.zeros_like(vec_ref)
            dbias_ref[...] = jnp.zeros_like(dbias_ref)

        x = x_ref[...]
        dxo = dxo_ref[...]
        r, xhat, n, h = _rms_mod(x, gn, shift, scale)
        hbf = h.astype(BF)
        proj = _dot_nt(hbf, wmi_ref[...])
        xa = proj[:, 0:DP]
        zu = proj[:, DP:DP + DG]
        zv = proj[:, DP + DG:DPROJ]
        _, _, _, hh = _rms_mod(xh_ref[...], gn, shift, scale)
        halo = _dot_nt(hh.astype(BF), wmi_ref[0:DP, :])
        halo = jnp.where(t == 0, 0.0, halo)
        cnts = _pool_counts(t * T, T)
        p = _pool_fwd(xa, halo, ext_scr, cnts, T)
        m = []
        for gi in range(4):
            cols = slice(128 * gi, 128 * gi + 128)
            m.append(_dot(p[gi].astype(BF), wp_ref[gi]))
            ycat_scr[:, cols] = (m[gi] * ps_ref[:, cols]).astype(BF)
        u, tu = _gelu(zu)
        v, tv = _gelu(zv)
        mu = jnp.mean(v, axis=-1, keepdims=True)
        vc = v - mu
        rstd = lax.rsqrt(jnp.mean(vc * vc, axis=-1, keepdims=True) + EPS)
        vhat = vc * rstd
        lg = lg_ref[...]
        vnb = (vhat * lg + lb_ref[...]).astype(BF)
        _sgu_fwd(vnb, ws_ref, sv_scr, T)
        for nck in range(T // CHUNK):
            rows = slice(nck * CHUNK, (nck + 1) * CHUNK)
            sv_scr[rows, :] = sv_scr[rows, :] + bias_ref[...]
        sv = sv_scr[...]
        ycat_scr[:, DP:D] = (u * sv).astype(BF)

        gwmo_ref[...] += _dot_tn(ycat_scr[...], dxo.astype(BF))
        dyc = _dot_nt((dxo * gate).astype(BF), wmo_ref[...])
        dya = dyc[:, 0:DP]
        dyb = dyc[:, DP:D]

        dps = []
        dp = []
        for gi in range(4):
            cols = slice(128 * gi, 128 * gi + 128)
            dps.append(jnp.sum(dya[:, cols] * m[gi], axis=0, keepdims=True))
            dm = (dya[:, cols] * ps_ref[:, cols]).astype(BF)
            gwp_ref[gi] += _dot_tn(p[gi].astype(BF), dm)
            dp.append(_dot_nt(dm, wp_ref[gi]))
            qext_scr[0:T, cols] = dp[gi] / cnts[gi]
        qext_scr[T:T + HALO, :] = carry_scr[...]
        for gi, w in enumerate(WINDOWS):
            cols = slice(128 * gi, 128 * gi + 128)
            acc = qext_scr[0:T, cols]
            for k in range(1, w):
                acc = acc + qext_scr[k:k + T, cols]
            dproj_scr[:, cols] = (acc - dp[gi]).astype(BF)
        carry_scr[...] = qext_scr[0:HALO, :]

        du = dyb * sv
        dsv = dyb * u
        lane = lax.broadcasted_iota(jnp.int32, (CHUNK, 128), 1)
        dbias = jnp.zeros((CHUNK, DG), F32)
        for nck in range(T // CHUNK):
            rows = slice(nck * CHUNK, (nck + 1) * CHUNK)
            dbias = dbias + dsv[rows, :]
            for b in range(DG // 128):
                cols = slice(128 * b, 128 * b + 128)
                dsvb = dsv[rows, cols]
                vb = vnb[rows, cols]
                gws_ref[2 * b] += _dot_nt(jnp.where(lane < 64, dsvb, 0.0).astype(BF), vb)
                gws_ref[2 * b + 1] += _dot_nt(jnp.where(lane < 64, 0.0, dsvb).astype(BF), vb)
                dsvbb = dsvb.astype(BF)
                dvn_scr[rows, cols] = jnp.where(lane < 64, _dot_tn(ws_ref[2 * b], dsvbb),
                                                _dot_tn(ws_ref[2 * b + 1], dsvbb))
        dbias_ref[...] += dbias
        dvn = dvn_scr[...]
        dlg = jnp.sum(dvn * vhat, axis=0, keepdims=True)
        dlb = jnp.sum(dvn, axis=0, keepdims=True)
        dvhat = dvn * lg
        dv = rstd * (dvhat - jnp.mean(dvhat, axis=-1, keepdims=True)
                     - vhat * jnp.mean(dvhat * vhat, axis=-1, keepdims=True))
        dproj_scr[:, DP:DP + DG] = (du * _gelu_grad(zu, tu)).astype(BF)
        dproj_scr[:, DP + DG:DPROJ] = (dv * _gelu_grad(zv, tv)).astype(BF)
        vec_ref[...] += _rows3(jnp.concatenate(dps, axis=1), dlg, dlb, DP)

        dproj = dproj_scr[...]
        gwmi_ref[...] += _dot_tn(dproj, hbf)
        dh = _dot(dproj, wmi_ref[...])
        dxi, dshift, dscale, dgn = _rms_mod_bwd(dh, dxo, r, xhat, n, gn, scale)
        dxi_ref[...] = dxi
        st_ref[...] += _rows3(dshift, dscale, dgn, D)

        @pl.when(i == nS - 1)
        def _():
            acc = gwmo_ref[...]
            dgate = jnp.sum(acc * wmo_ref[...].astype(F32), axis=0, keepdims=True)
            row = lax.broadcasted_iota(jnp.int32, (8, D), 0)
            st_ref[...] += jnp.where(row == 3, dgate, 0.0)
            gwmo_out[...] = (acc * gate).astype(BF)
            gwmi_out[...] = gwmi_ref[...].astype(BF)
            tt = lax.broadcasted_iota(jnp.int32, (CHUNK, CHUNK), 0)
            ss = lax.broadcasted_iota(jnp.int32, (CHUNK, CHUNK), 1)
            for hd in range(8):
                gws_ref[hd] = jnp.where(tt >= ss, gws_ref[hd], 0.0)
            if ne:
                _exchange_phase("wait", sendbufs, recvbufs, sems)

    full = lambda shape: pl.BlockSpec(shape, lambda i: (0,) * len(shape))
    return pl.pallas_call(
        body, name=name, grid=(nS,),
        out_shape=[jax.ShapeDtypeStruct((S, D), F32), jax.ShapeDtypeStruct((DPROJ, D), BF),
                   jax.ShapeDtypeStruct((D, D), BF), jax.ShapeDtypeStruct((4, 128, 128), F32),
                   jax.ShapeDtypeStruct((8, CHUNK, CHUNK), F32), jax.ShapeDtypeStruct((8, D), F32),
                   jax.ShapeDtypeStruct((8, DP), F32), jax.ShapeDtypeStruct((CHUNK, DG), F32)] + _like(exchange),
        in_specs=[pl.BlockSpec((T, D), lambda i: (nS - 1 - i, 0)),
                  pl.BlockSpec((HALO, D), lambda i: (jnp.maximum((nS - 1 - i) * hb - 1, 0), 0)),
                  pl.BlockSpec((T, D), lambda i: (nS - 1 - i, 0)),
                  full((9, D)), full((1, D)), full((DPROJ, D)), full((D, D)),
                  full((4, 128, 128)), full((1, DP)), full((1, DG)), full((1, DG)), full((8, CHUNK, CHUNK)),
                  full((CHUNK, DG))] + [HBM] * ne,
        out_specs=[pl.BlockSpec((T, D), lambda i: (nS - 1 - i, 0)), full((DPROJ, D)), full((D, D)),
                   full((4, 128, 128)), full((8, CHUNK, CHUNK)), full((8, D)), full((8, DP)), full((CHUNK, DG))]
                  + [HBM] * ne,
        scratch_shapes=[pltpu.VMEM((HALO, DP), F32), pltpu.VMEM((T + HALO, DP), F32),
                        pltpu.VMEM((T + HALO, DP), F32), pltpu.VMEM((T, DG), F32), pltpu.VMEM((T, DG), F32),
                        pltpu.VMEM((T, D), BF), pltpu.VMEM((T, DPROJ), BF), pltpu.VMEM((DPROJ, D), F32),
                        pltpu.VMEM((D, D), F32)] + (_comm_sems(ne) if ne else []),
        compiler_params=_cparams(56, ("arbitrary",)),
    )(x, x, dxo, mod, gn, wmi, wmo, wp, ps, lg, lb, ws, bias, *exchange)


def kernel(x, c, w_ada, b_ada, norm_ffn1_g, ffn1_w_in, ffn1_w_out, norm_mix_g, w_mix_in, w_pool, pool_scale, gmlp_ln_g, gmlp_ln_b, w_spatial, b_spatial, w_mix_out, norm_ffn2_g, ffn2_w_in, ffn2_w_out, norm_final_g, loss_target, m_w_ada, m_b_ada, m_norm_ffn1_g, m_ffn1_w_in, m_ffn1_w_out, m_norm_mix_g, m_w_mix_in, m_w_pool, m_pool_scale, m_gmlp_ln_g, m_gmlp_ln_b, m_w_spatial, m_b_spatial, m_w_mix_out, m_norm_ffn2_g, m_ffn2_w_in, m_ffn2_w_out, m_norm_final_g, v_w_ada, v_b_ada, v_norm_ffn1_g, v_ffn1_w_in, v_ffn1_w_out, v_norm_mix_g, v_w_mix_in, v_w_pool, v_pool_scale, v_gmlp_ln_g, v_gmlp_ln_b, v_w_spatial, v_b_spatial, v_w_mix_out, v_norm_ffn2_g, v_ffn2_w_in, v_ffn2_w_out, v_norm_final_g):
    weights = dict(w_ada=w_ada, b_ada=b_ada, norm_ffn1_g=norm_ffn1_g, ffn1_w_in=ffn1_w_in, ffn1_w_out=ffn1_w_out,
                   norm_mix_g=norm_mix_g, w_mix_in=w_mix_in, w_pool=w_pool, pool_scale=pool_scale,
                   gmlp_ln_g=gmlp_ln_g, gmlp_ln_b=gmlp_ln_b, w_spatial=w_spatial, b_spatial=b_spatial,
                   w_mix_out=w_mix_out, norm_ffn2_g=norm_ffn2_g, ffn2_w_in=ffn2_w_in, ffn2_w_out=ffn2_w_out,
                   norm_final_g=norm_final_g)
    mom1 = dict(w_ada=m_w_ada, b_ada=m_b_ada, norm_ffn1_g=m_norm_ffn1_g, ffn1_w_in=m_ffn1_w_in,
                ffn1_w_out=m_ffn1_w_out, norm_mix_g=m_norm_mix_g, w_mix_in=m_w_mix_in, w_pool=m_w_pool,
                pool_scale=m_pool_scale, gmlp_ln_g=m_gmlp_ln_g, gmlp_ln_b=m_gmlp_ln_b, w_spatial=m_w_spatial,
                b_spatial=m_b_spatial, w_mix_out=m_w_mix_out, norm_ffn2_g=m_norm_ffn2_g, ffn2_w_in=m_ffn2_w_in,
                ffn2_w_out=m_ffn2_w_out, norm_final_g=m_norm_final_g)
    mom2 = dict(w_ada=v_w_ada, b_ada=v_b_ada, norm_ffn1_g=v_norm_ffn1_g, ffn1_w_in=v_ffn1_w_in,
                ffn1_w_out=v_ffn1_w_out, norm_mix_g=v_norm_mix_g, w_mix_in=v_w_mix_in, w_pool=v_w_pool,
                pool_scale=v_pool_scale, gmlp_ln_g=v_gmlp_ln_g, gmlp_ln_b=v_gmlp_ln_b, w_spatial=v_w_spatial,
                b_spatial=v_b_spatial, w_mix_out=v_w_mix_out, norm_ffn2_g=v_norm_ffn2_g, ffn2_w_in=v_ffn2_w_in,
                ffn2_w_out=v_ffn2_w_out, norm_final_g=v_norm_final_g)
    order = list(weights)
    xs = x[0]
    target = loss_target[0]
    transposed = ("ffn1_w_in", "w_mix_in", "ffn2_w_in")
    big = ("ffn1_w_in", "ffn1_w_out", "w_mix_in", "w_mix_out", "ffn2_w_in", "ffn2_w_out")
    local = lambda a, k: a[0].T if k in transposed else a[0]
    wc = w_ada.shape[2]

    shard = dict(zip(big, _cast_shards([local(weights[k], k) for k in big])))
    mod, cact_all, h1, g_w1_in, g_w1_out = _ada_forward(
        jnp.broadcast_to(c, (8, D)), w_ada[0], b_ada.reshape(NDEV, wc), [shard["ffn1_w_in"], shard["ffn1_w_out"]],
        xs, norm_ffn1_g)
    w1_in = g_w1_in.reshape(2, F, D)
    w1_out = g_w1_out.reshape(F, D)

    x1, gu1, h1, g_wmi, g_wmo, g_w2_out, g_w2_in = _ffn_fwd(
        xs, mod, norm_ffn1_g, w1_in, w1_out, 0, "ffn1_fwd", h=h1,
        gather=[shard["w_mix_in"], shard["w_mix_out"], shard["ffn2_w_out"], shard["ffn2_w_in"]])
    wmi = g_wmi.reshape(DPROJ, D)
    wmo = g_wmo.reshape(D, D)
    w2_in = g_w2_in.reshape(2, F, D)
    w2_out = g_w2_out.reshape(F, D)
    tril = jnp.tril(jnp.ones((CHUNK, CHUNK), dtype=bool))
    ws_b = jnp.where(tril[None], w_spatial[0], 0.0).astype(BF)
    wp_b = w_pool[0].astype(BF)
    bias = jnp.repeat(b_spatial[0].T, DG // 8, axis=1)
    mix_args = (wmi, wmo, wp_b, pool_scale, gmlp_ln_g, gmlp_ln_b, ws_b, bias)
    x2, h3 = _mix_fwd(x1, mod, norm_mix_g, norm_ffn2_g, *mix_args, "mix_fwd")
    dx3, gu3, h3, st_f = _ffn_fwd(x2, mod, norm_ffn2_g, w2_in, w2_out, 2, "ffn2_fwd", h=h3,
                                  loss=(norm_final_g.reshape(1, D), target))

    slots = lambda a: a.reshape(NDEV, a.size // (NDEV * D), D)
    dgu3, a3 = _ffn_bwd_hidden_wide(dx3, mod, gu3, w2_out, 2, "ffn2_bwd_hidden")
    d_w2_out, dgate3 = _ffn_bwd_wout(a3, dx3, mod, w2_out, 2, "ffn2_bwd_wout")
    d_w2_in = _ffn_bwd_win(h3, dgu3, "ffn2_bwd_win")[0]
    dx2, st3 = _ffn_bwd_input(dgu3, w2_in, x2, dx3, mod, norm_ffn2_g, 2, "ffn2_bwd_input")
    dx1, d_wmi, d_wmo, d_wp, d_ws, st2, vec2, dbias, r_w2_in, r_w2_out = _mix_bwd(
        x1, dx2, mod, norm_mix_g, *mix_args, "mix_bwd", exchange=[slots(d_w2_in), slots(d_w2_out)])
    dgu1, a1, r_wmi, r_wmo = _ffn_bwd_hidden_wide(
        dx1, mod, gu1, w1_out, 0, "ffn1_bwd_hidden", exchange=[slots(d_wmi), slots(d_wmo)])
    d_w1_out, dgate1 = _ffn_bwd_wout(a1, dx1, mod, w1_out, 0, "ffn1_bwd_wout")
    d_w1_in, r_w1_out = _ffn_bwd_win(h1, dgu1, "ffn1_bwd_win", exchange=[slots(d_w1_out)])
    send_sems, recv_sems, sent, landing, token = _exchange_start(slots(d_w1_in), "w_in_grad_start")
    dx0, st1 = _ffn_bwd_input(dgu1, w1_in, xs, dx1, mod + token[0, 0], norm_ffn1_g, 0, "ffn1_bwd_input")

    received = dict(ffn1_w_out=r_w1_out, w_mix_in=r_wmi, w_mix_out=r_wmo, ffn2_w_in=r_w2_in, ffn2_w_out=r_w2_out)
    tiles = dict(ffn1_w_in=176, ffn1_w_out=176, w_mix_in=96, w_mix_out=128, ffn2_w_in=176, ffn2_w_out=176)
    result = {}

    def update(k, recv, own=None):
        res = _sum_adamw(recv, local(weights[k], k), local(mom1[k], k), local(mom2[k], k), tiles[k], "update_" + k,
                         own=own)
        result[k] = tuple((a.T if k in transposed else a)[None] for a in res)

    for k, recv in received.items():
        update(k, recv)
    row = lambda a: a.reshape(1, D)
    params = {k: (weights[k], mom1[k], mom2[k]) for k in SMALL}
    params["norm_final_g"] = (row(norm_final_g), row(m_norm_final_g), row(v_norm_final_g))
    tot, rsum, dmine = _small_reduce(d_ws, d_wp, dbias, st1, st2, st3, st_f, vec2, dgate1, dgate3)
    sent, landing = _exchange_wait(send_sems, recv_sems, sent, landing, rsum, "w_in_grad_wait")
    update("ffn1_w_in", landing, own=sent)
    small, loss_row = _small_update(tot, rsum, params)
    result.update(small)
    result["norm_final_g"] = tuple(a.reshape(D) for a in small["norm_final_g"])
    result["w_ada"] = tuple(a[None] for a in _ada_update(cact_all, dmine, w_ada[0], m_w_ada[0], v_w_ada[0], 256))

    return (loss_row[0, 0], dx0[None], *[result[k][0] for k in order], *[result[k][1] for k in order],
            *[result[k][2] for k in order], *[result[k][3] for k in order])
```
